```python
import jax, jax.numpy as jnp
from jax import lax
import numpy as np

D_MODEL = 1024
BATCH = 16
SEQ = 256
DEPTH = 4
DEC_BATCH = 4
DEC_SEQ = 1024
PAST_LEN = 256

GRID_W = 64
HEAD_DIM = 64
CHUNK = 128
A_WIDTH = D_MODEL // 2
A_GROUPS = 4
A_GROUP_CH = A_WIDTH // A_GROUPS
B_HEADS = (D_MODEL // 2) // HEAD_DIM
B_KV_HEADS = B_HEADS // 4
C_HEADS = D_MODEL // HEAD_DIM
NA_MAX_ROWS = 8
NA_COLS = 16
D_FF = ((8 * D_MODEL // 3 + 255) // 256) * 256
ROPE_BASE = 10000.0
EPS = 1e-6
Q_BLOCK = 128
N_EVEN = (DEPTH + 1) // 2
N_ODD = DEPTH // 2
EVEN_SPLITS = (A_WIDTH, 2 * A_WIDTH, 2 * A_WIDTH + B_HEADS * HEAD_DIM, 2 * A_WIDTH + (B_HEADS + B_KV_HEADS) * HEAD_DIM)
EVEN_IN = 2 * A_WIDTH + (B_HEADS + 2 * B_KV_HEADS) * HEAD_DIM
ODD_IN = 3 * C_HEADS * HEAD_DIM
NEG = -1e30

kernel_name = 'hybrid_gmlp_gqa_natten_prefix_dit_step'


def rms_norm(x, g):
    xf = x.astype(jnp.float32)
    y = xf * lax.rsqrt(jnp.mean(xf * xf, axis=-1, keepdims=True) + EPS)
    return (y * g.astype(jnp.float32)).astype(x.dtype)


def layer_norm(x, g):
    xf = x.astype(jnp.float32)
    mu = jnp.mean(xf, axis=-1, keepdims=True)
    var = jnp.mean(jnp.square(xf - mu), axis=-1, keepdims=True)
    return ((xf - mu) * lax.rsqrt(var + EPS) * g.astype(jnp.float32)).astype(x.dtype)


def modulation(cond, w_mod, b_mod):
    m = jax.nn.silu(cond) @ w_mod + b_mod
    return jnp.split(m[:, None, :], 6, axis=-1)


def sublayer_in(x, g, shift, scale):
    return rms_norm(x, g) * (1 + scale) + shift


def sublayer_out(x, y, g, gate):
    return x + gate * rms_norm(y, g)


def axial_rope(x):
    s = x.shape[1]
    t = jnp.arange(s)
    half = HEAD_DIM // 2
    nf = half // 2
    freqs = ROPE_BASE ** (-jnp.arange(nf, dtype=jnp.float32) / nf)

    def rot(xa, pos):
        ang = pos.astype(jnp.float32)[:, None] * freqs[None, :]
        cos = jnp.cos(ang)[None, :, None, :]
        sin = jnp.sin(ang)[None, :, None, :]
        x1 = xa[..., :nf].astype(jnp.float32)
        x2 = xa[..., nf:].astype(jnp.float32)
        return jnp.concatenate([x1 * cos - x2 * sin, x2 * cos + x1 * sin], axis=-1)

    out = jnp.concatenate([rot(x[..., :half], t // GRID_W), rot(x[..., half:], t % GRID_W)], axis=-1)
    return out.astype(x.dtype)


def blocked_attention(q, k, v):
    b, sq, h, d = q.shape
    hk = k.shape[2]
    g = h // hk
    nb = sq // Q_BLOCK
    scale = d ** -0.5
    qb = q.reshape(b, nb, Q_BLOCK, hk, g, d).transpose(1, 0, 2, 3, 4, 5)

    def one_block(qblk):
        s = jnp.einsum('bqngd,bsnd->bngqs', qblk, k).astype(jnp.float32) * scale
        p = jax.nn.softmax(s, axis=-1).astype(v.dtype)
        return jnp.einsum('bngqs,bsnd->bqngd', p, v)

    o = lax.map(one_block, qb)
    return o.transpose(1, 0, 2, 3, 4, 5).reshape(b, sq, h, d)


def chunk_gmlp(u, v, sgu_w, sgu_b, sgu_norm):
    b, s, _ = u.shape
    n = s // CHUNK
    u = jax.nn.gelu(u)
    v = layer_norm(jax.nn.gelu(v), sgu_norm)
    vc = v.reshape(b, n, CHUNK, A_GROUPS, A_GROUP_CH)
    mixed = jnp.einsum('gpq,bnqgc->bnpgc', sgu_w, vc) + sgu_b.T[None, None, :, :, None]
    return u * mixed.reshape(b, s, A_WIDTH)


def even_mixer(h, w_in, w_out, sgu_w, sgu_b, sgu_norm, q_norm, k_norm, ctx_k=None, ctx_v=None):
    b, s, _ = h.shape
    z = h @ w_in
    u, v, q, k, vals = jnp.split(z, EVEN_SPLITS, axis=-1)
    a_out = chunk_gmlp(u, v, sgu_w, sgu_b, sgu_norm)
    q = rms_norm(q.reshape(b, s, B_HEADS, HEAD_DIM), q_norm)
    k = rms_norm(k.reshape(b, s, B_KV_HEADS, HEAD_DIM), k_norm)
    vals = vals.reshape(b, s, B_KV_HEADS, HEAD_DIM)
    if ctx_k is None:
        b_out = blocked_attention(q, k, vals)
    else:
        keys = jnp.concatenate([axial_rope(k), ctx_k.astype(k.dtype)], axis=1)
        values = jnp.concatenate([vals, ctx_v.astype(vals.dtype)], axis=1)
        b_out = blocked_attention(axial_rope(q), keys, values)
    y = jnp.concatenate([a_out, b_out.reshape(b, s, B_HEADS * HEAD_DIM)], axis=-1) @ w_out
    return y, k, vals


def neighbourhood_attention(q, k, v, ctx_k, ctx_v, rpb):
    b, s, h, d = q.shape
    rows = s // GRID_W
    wr = min(NA_MAX_ROWS, rows)
    scale = d ** -0.5
    qg = q.reshape(b, rows, GRID_W, h, d)
    kg = k.reshape(b, rows, GRID_W, h, d)
    vg = v.reshape(b, rows, GRID_W, h, d)
    cols = jnp.arange(GRID_W)
    col_start = jnp.clip(cols - NA_COLS // 2, 0, GRID_W - NA_COLS)
    col_in = (cols[None, :] >= col_start[:, None]) & (cols[None, :] < col_start[:, None] + NA_COLS)
    col_idx = jnp.clip(cols[None, :] - cols[:, None] + NA_COLS - 1, 0, 2 * NA_COLS - 2)
    rpb = rpb.astype(jnp.float32)

    def one_row(r):
        rs = jnp.clip(r - wr // 2, 0, rows - wr)
        qr = lax.dynamic_index_in_dim(qg, r, axis=1, keepdims=False)
        kb = lax.dynamic_slice_in_dim(kg, rs, wr, axis=1)
        vb = lax.dynamic_slice_in_dim(vg, rs, wr, axis=1)
        s_win = jnp.einsum('bqhd,brwhd->bhqrw', qr, kb).astype(jnp.float32) * scale
        row_idx = rs + jnp.arange(wr) - r + NA_MAX_ROWS - 1
        bias = rpb[:, row_idx][:, :, col_idx].transpose(0, 2, 1, 3)
        s_win = jnp.where(col_in[None, None, :, None, :], s_win + bias[None], NEG)
        s_ctx = jnp.einsum('bqhd,bchd->bhqc', qr, ctx_k).astype(jnp.float32) * scale
        scores = jnp.concatenate([s_win.reshape(b, h, GRID_W, wr * GRID_W), s_ctx], axis=-1)
        p = jax.nn.softmax(scores, axis=-1).astype(v.dtype)
        p_win = p[..., :wr * GRID_W].reshape(b, h, GRID_W, wr, GRID_W)
        p_ctx = p[..., wr * GRID_W:]
        return jnp.einsum('bhqrw,brwhd->bqhd', p_win, vb) + jnp.einsum('bhqc,bchd->bqhd', p_ctx, ctx_v)

    o = lax.map(one_row, jnp.arange(rows))
    return o.transpose(1, 0, 2, 3, 4).reshape(b, s, h, d)


def odd_mixer(h, w_in, w_out, rpb, ctx_k=None, ctx_v=None):
    b, s, _ = h.shape
    q, k, v = jnp.split(h @ w_in, 3, axis=-1)
    q = q.reshape(b, s, C_HEADS, HEAD_DIM)
    k = k.reshape(b, s, C_HEADS, HEAD_DIM)
    v = v.reshape(b, s, C_HEADS, HEAD_DIM)
    if ctx_k is None:
        o = blocked_attention(q, k, v)
    else:
        o = neighbourhood_attention(q, k, v, ctx_k.astype(k.dtype), ctx_v.astype(v.dtype), rpb)
    return o.reshape(b, s, C_HEADS * HEAD_DIM) @ w_out, k, v


def swiglu(h, w_in, w_out):
    g, u = jnp.split(h @ w_in, 2, axis=-1)
    return (jax.nn.silu(g) * u) @ w_out


def setup_inputs(seed: int = 0) -> dict:
    key = jax.random.key(seed)
    ks = iter(jax.random.split(key, 32))

    def nrm(shape, scale):
        return jax.random.normal(next(ks), shape, jnp.float32) * scale

    def gain(shape):
        return 1.0 + nrm(shape, 0.01)

    d = D_MODEL
    return {
        'x_prompt': nrm((BATCH, SEQ, d), 1.0),
        'x_sample': nrm((DEC_BATCH, DEC_SEQ, d), 1.0),
        'cache_attn_k': nrm((DEC_BATCH, N_EVEN, PAST_LEN, B_KV_HEADS, HEAD_DIM), 1.0),
        'cache_attn_v': nrm((DEC_BATCH, N_EVEN, PAST_LEN, B_KV_HEADS, HEAD_DIM), 1.0),
        'cache_na_k': nrm((DEC_BATCH, N_ODD, PAST_LEN, C_HEADS, HEAD_DIM), 1.0),
        'cache_na_v': nrm((DEC_BATCH, N_ODD, PAST_LEN, C_HEADS, HEAD_DIM), 1.0),
        'c': nrm((DEC_BATCH, d), 1.0),
        'c_ctx': nrm((d,), 1.0),
        'mod_w': nrm((DEPTH, d, 6 * d), 0.5 * d ** -0.5),
        'mod_b': nrm((DEPTH, 6 * d), 0.01),
        'norm_mix_pre': gain((DEPTH, d)),
        'norm_mix_post': gain((DEPTH, d)),
        'norm_ffn_pre': gain((DEPTH, d)),
        'norm_ffn_post': gain((DEPTH, d)),
        'even_w_in': nrm((N_EVEN, d, EVEN_IN), d ** -0.5),
        'even_w_out': nrm((N_EVEN, d, d), d ** -0.5),
        'sgu_w': nrm((N_EVEN, A_GROUPS, CHUNK, CHUNK), CHUNK ** -0.5),
        'sgu_b': gain((N_EVEN, A_GROUPS, CHUNK)),
        'sgu_norm': gain((N_EVEN, A_WIDTH)),
        'q_norm': gain((N_EVEN, HEAD_DIM)),
        'k_norm': gain((N_EVEN, HEAD_DIM)),
        'odd_w_in': nrm((N_ODD, d, ODD_IN), d ** -0.5),
        'odd_w_out': nrm((N_ODD, d, d), d ** -0.5),
        'na_rpb': nrm((N_ODD, C_HEADS, 2 * NA_MAX_ROWS - 1, 2 * NA_COLS - 1), 0.02),
        'ffn_w_in': nrm((DEPTH, d, 2 * D_FF), d ** -0.5),
        'ffn_w_out': nrm((DEPTH, D_FF, d), D_FF ** -0.5),
    }


def reference(x_prompt, x_sample, cache_attn_k, cache_attn_v, cache_na_k, cache_na_v, c, c_ctx,
              mod_w, mod_b, norm_mix_pre, norm_mix_post, norm_ffn_pre, norm_ffn_post,
              even_w_in, even_w_out, sgu_w, sgu_b, sgu_norm, q_norm, k_norm,
              odd_w_in, odd_w_out, na_rpb, ffn_w_in, ffn_w_out):
    xp, xs = x_prompt, x_sample
    attn_k, attn_v, na_k, na_v = [], [], [], []
    for l in range(DEPTH):
        sh_p, sc_p, gt_p, fsh_p, fsc_p, fgt_p = modulation(c_ctx[None, :], mod_w[l], mod_b[l])
        sh_s, sc_s, gt_s, fsh_s, fsc_s, fgt_s = modulation(c, mod_w[l], mod_b[l])
        hp = sublayer_in(xp, norm_mix_pre[l], sh_p, sc_p)
        hs = sublayer_in(xs, norm_mix_pre[l], sh_s, sc_s)
        if l % 2 == 0:
            e = l // 2
            yp, kp, vp = even_mixer(hp, even_w_in[e], even_w_out[e], sgu_w[e], sgu_b[e], sgu_norm[e], q_norm[e], k_norm[e])
            ys, _, _ = even_mixer(hs, even_w_in[e], even_w_out[e], sgu_w[e], sgu_b[e], sgu_norm[e], q_norm[e], k_norm[e],
                                  ctx_k=cache_attn_k[:, e], ctx_v=cache_attn_v[:, e])
            attn_k.append(kp)
            attn_v.append(vp)
        else:
            o = l // 2
            yp, kp, vp = odd_mixer(hp, odd_w_in[o], odd_w_out[o], na_rpb[o])
            ys, _, _ = odd_mixer(hs, odd_w_in[o], odd_w_out[o], na_rpb[o], ctx_k=cache_na_k[:, o], ctx_v=cache_na_v[:, o])
            na_k.append(kp)
            na_v.append(vp)
        xp = sublayer_out(xp, yp, norm_mix_post[l], gt_p)
        xs = sublayer_out(xs, ys, norm_mix_post[l], gt_s)
        fp = swiglu(sublayer_in(xp, norm_ffn_pre[l], fsh_p, fsc_p), ffn_w_in[l], ffn_w_out[l])
        fs = swiglu(sublayer_in(xs, norm_ffn_pre[l], fsh_s, fsc_s), ffn_w_in[l], ffn_w_out[l])
        xp = sublayer_out(xp, fp, norm_ffn_post[l], fgt_p)
        xs = sublayer_out(xs, fs, norm_ffn_post[l], fgt_s)
    new_attn_k = jnp.stack(attn_k, axis=1)
    new_attn_v = jnp.stack(attn_v, axis=1)
    new_na_k = jnp.stack(na_k, axis=1)
    new_na_v = jnp.stack(na_v, axis=1)
    return (xp, xs, new_attn_k, new_attn_v, new_na_k, new_na_v)
```

```python
import functools

import jax
import jax.numpy as jnp
from jax import lax
from jax.experimental import pallas as pl
from jax.experimental.pallas import tpu as pltpu

F32 = jnp.float32
BF16 = jnp.bfloat16

D = 1024
DEPTH = 4
HEAD_DIM = 64
GRID_W = 64
CHUNK = 128
A_WIDTH = D // 2
A_GROUPS = 4
B_HEADS = 8
B_KV_HEADS = 2
C_HEADS = 16
NA_ROWS = 8
NA_COLS = 16
D_FF = 2816
ROPE_BASE = 10000.0
EPS = 1e-6
NEG = -1e30
SEG = 1024
N_SEG = 8
LANES = 128
QSCALE = HEAD_DIM ** -0.5

TM_PROJ = 512
TM_FFN = 1024
TF_FFN = 256
TQ_ATTN = 256
NA_QROWS = 4
NA_KROWS = 12
VMEM_LIMIT = 56 * 1024 * 1024


def _cparams(n_axes):
    return pltpu.CompilerParams(dimension_semantics=("arbitrary",) * n_axes,
                                vmem_limit_bytes=VMEM_LIMIT)


def _rms(x):
    return x * lax.rsqrt(jnp.mean(x * x, axis=-1, keepdims=True) + EPS)


def _mod_norm(x, g, shift, scale):
    return _rms(x) * g * (1.0 + scale) + shift


def _mod_kernel(cond_ref, w_ref, b_ref, o_ref):
    s = cond_ref[...]
    s = s * jax.nn.sigmoid(s)
    o_ref[...] = jnp.dot(s.astype(BF16), w_ref[...].astype(BF16),
                         preferred_element_type=F32) + b_ref[...]


def _modulation(cond, mod_w, mod_b):
    b = mod_b.reshape(DEPTH, 6, 1, D)
    out = pl.pallas_call(
        _mod_kernel,
        grid=(DEPTH, 6),
        in_specs=[pl.BlockSpec((N_SEG, D), lambda l, j: (0, 0)),
                  pl.BlockSpec((None, D, D), lambda l, j: (l, 0, j)),
                  pl.BlockSpec((None, None, 1, D), lambda l, j: (l, j, 0, 0))],
        out_specs=pl.BlockSpec((None, None, N_SEG, D), lambda l, j: (l, j, 0, 0)),
        out_shape=jax.ShapeDtypeStruct((DEPTH, 6, N_SEG, D), F32),
        compiler_params=_cparams(2),
        name="modulation",
    )(cond, mod_w, b)
    return out.reshape(DEPTH, 6, N_SEG, 1, D)


def _mod_spec(layer, j, tm):
    return pl.BlockSpec((None, None, None, 1, D), lambda i: (layer, j, (i * tm) // SEG, 0, 0))


def _head_sumsq(y, bd_ref):
    sq = y * y
    hi = sq.astype(BF16)
    lo = (sq - hi.astype(F32)).astype(BF16)
    bd = bd_ref[...]
    return (jnp.dot(hi, bd, preferred_element_type=F32)
            + jnp.dot(lo, bd, preferred_element_type=F32))


def _rope(y, cos, sin):
    lane = lax.broadcasted_iota(jnp.int32, (1, LANES), 1)
    first = (lane & 16) == 0
    partner = jnp.where(first, pltpu.roll(y, LANES - 16, 1), pltpu.roll(y, 16, 1))
    return y * cos + partner * sin


def _even_in_kernel(x_ref, sh_ref, sc_ref, g_ref, w_ref, sguw_ref, sgub_ref, sgun_ref,
                    qn_ref, kn_ref, bdq_ref, bdk_ref, cos_ref, sin_ref,
                    a_ref, q_ref, kd_ref, vd_ref, kf_ref, vf_ref):
    tm = x_ref.shape[0]
    h = _mod_norm(x_ref[...], g_ref[...], sh_ref[...], sc_ref[...]).astype(BF16)

    u = jax.nn.gelu(jnp.dot(h, w_ref[:, 0:A_WIDTH], preferred_element_type=F32), approximate=True)
    v = jax.nn.gelu(jnp.dot(h, w_ref[:, A_WIDTH:2 * A_WIDTH], preferred_element_type=F32),
                    approximate=True)
    mu = jnp.mean(v, axis=-1, keepdims=True)
    vc = v - mu
    var = jnp.mean(vc * vc, axis=-1, keepdims=True)
    vn = (vc * lax.rsqrt(var + EPS) * sgun_ref[...]).astype(BF16)
    n_chunks = tm // CHUNK
    gch = A_WIDTH // A_GROUPS
    for g in range(A_GROUPS):
        rhs = jnp.concatenate([vn[n * CHUNK:(n + 1) * CHUNK, g * gch:(g + 1) * gch]
                               for n in range(n_chunks)], axis=1)
        mixed = jnp.dot(sguw_ref[g].astype(BF16), rhs, preferred_element_type=F32)
        bias = sgub_ref[g]
        for n in range(n_chunks):
            blk = (mixed[:, n * gch:(n + 1) * gch] + bias) * u[n * CHUNK:(n + 1) * CHUNK,
                                                              g * gch:(g + 1) * gch]
            a_ref[n * CHUNK:(n + 1) * CHUNK, g * gch:(g + 1) * gch] = blk.astype(BF16)

    c0 = 2 * A_WIDTH
    qw = B_HEADS * HEAD_DIM
    q = jnp.dot(h, w_ref[:, c0:c0 + qw], preferred_element_type=F32)
    q = q * lax.rsqrt(_head_sumsq(q, bdq_ref) * (1.0 / HEAD_DIM) + EPS) * qn_ref[...]
    cos = cos_ref[...]
    sin = sin_ref[...]
    for j in range(qw // LANES):
        qj = _rope(q[:, j * LANES:(j + 1) * LANES], cos, sin) * QSCALE
        q_ref[:, j * LANES:(j + 1) * LANES] = qj.astype(BF16)

    kw = B_KV_HEADS * HEAD_DIM
    k = jnp.dot(h, w_ref[:, c0 + qw:c0 + qw + kw], preferred_element_type=F32)
    k = k * lax.rsqrt(_head_sumsq(k, bdk_ref) * (1.0 / HEAD_DIM) + EPS) * kn_ref[...]
    kf_ref[...] = k
    vals = jnp.dot(h, w_ref[:, c0 + qw + kw:c0 + qw + 2 * kw], preferred_element_type=F32)
    vf_ref[...] = vals
    kr = _rope(k, cos, sin)
    lane = lax.broadcasted_iota(jnp.int32, (1, LANES), 1)
    low = lane < HEAD_DIM
    for src, dst in ((kr, kd_ref), (vals, vd_ref)):
        sw = pltpu.roll(src, HEAD_DIM, 1)
        dst[:, 0:LANES] = jnp.where(low, src, sw).astype(BF16)
        dst[:, LANES:2 * LANES] = jnp.where(low, sw, src).astype(BF16)


def _even_in(x, mods, layer, g_pre, w_bf, sgu_w, sgu_b, sgu_norm, q_norm, k_norm, bdq, bdk, cos, sin):
    T = x.shape[0]
    tm = TM_PROJ
    n_in = w_bf.shape[1]
    tiles_per_seg = SEG // tm
    n_prompt_tiles = T // 2 // tm
    full = lambda shape: pl.BlockSpec(shape, lambda i: (0,) * len(shape))
    tab_spec = pl.BlockSpec((None, tm, LANES), lambda i: (i // n_prompt_tiles, i % tiles_per_seg, 0))
    row = lambda w: pl.BlockSpec((tm, w), lambda i: (i, 0))
    return pl.pallas_call(
        _even_in_kernel,
        grid=(T // tm,),
        in_specs=[row(D), _mod_spec(layer, 0, tm), _mod_spec(layer, 1, tm), full((1, D)),
                  full((D, n_in)), full((A_GROUPS, CHUNK, CHUNK)), full((A_GROUPS, CHUNK, 1)),
                  full((1, A_WIDTH)), full((1, B_HEADS * HEAD_DIM)), full((1, LANES)),
                  full((B_HEADS * HEAD_DIM, B_HEADS * HEAD_DIM)), full((LANES, LANES)),
                  tab_spec, tab_spec],
        out_specs=[row(A_WIDTH), row(B_HEADS * HEAD_DIM), row(2 * LANES), row(2 * LANES),
                   row(LANES), row(LANES)],
        out_shape=[jax.ShapeDtypeStruct((T, A_WIDTH), BF16),
                   jax.ShapeDtypeStruct((T, B_HEADS * HEAD_DIM), BF16),
                   jax.ShapeDtypeStruct((T, 2 * LANES), BF16),
                   jax.ShapeDtypeStruct((T, 2 * LANES), BF16),
                   jax.ShapeDtypeStruct((T, LANES), F32),
                   jax.ShapeDtypeStruct((T, LANES), F32)],
        compiler_params=_cparams(1),
        name="even_in_proj",
    )(x, mods, mods, g_pre, w_bf, sgu_w, sgu_b, sgu_norm, q_norm, k_norm, bdq, bdk, cos, sin)


def _odd_in_kernel(x_ref, sh_ref, sc_ref, g_ref, w_ref, q_ref, k_ref, v_ref, kf_ref, vf_ref,
                   *, n_prompt_tiles):
    i = pl.program_id(0)
    hw = C_HEADS * HEAD_DIM
    h = _mod_norm(x_ref[...], g_ref[...], sh_ref[...], sc_ref[...]).astype(BF16)
    q = jnp.dot(h, w_ref[:, 0:hw], preferred_element_type=F32)
    q_ref[...] = (q * QSCALE).astype(BF16)
    k = jnp.dot(h, w_ref[:, hw:2 * hw], preferred_element_type=F32)
    k_ref[...] = k.astype(BF16)
    v = jnp.dot(h, w_ref[:, 2 * hw:3 * hw], preferred_element_type=F32)
    v_ref[...] = v.astype(BF16)

    @pl.when(i < n_prompt_tiles)
    def _():
        kf_ref[...] = k
        vf_ref[...] = v


def _odd_in(x, mods, layer, g_pre, w_bf):
    T = x.shape[0]
    tm = TM_PROJ
    hw = C_HEADS * HEAD_DIM
    n_prompt_tiles = T // 2 // tm
    full = lambda shape: pl.BlockSpec(shape, lambda i: (0,) * len(shape))
    row = pl.BlockSpec((tm, hw), lambda i: (i, 0))
    prow = pl.BlockSpec((tm, hw), lambda i: (jnp.minimum(i, n_prompt_tiles - 1), 0))
    return pl.pallas_call(
        functools.partial(_odd_in_kernel, n_prompt_tiles=n_prompt_tiles),
        grid=(T // tm,),
        in_specs=[pl.BlockSpec((tm, D), lambda i: (i, 0)), _mod_spec(layer, 0, tm),
                  _mod_spec(layer, 1, tm), full((1, D)), full((D, 3 * hw))],
        out_specs=[row, row, row, prow, prow],
        out_shape=[jax.ShapeDtypeStruct((T, hw), BF16)] * 3
                  + [jax.ShapeDtypeStruct((T // 2, hw), F32)] * 2,
        compiler_params=_cparams(1),
        name="odd_in_proj",
    )(x, mods, mods, g_pre, w_bf)


def _softmax_pv(pieces, mh):
    m = None
    for s, _ in pieces:
        ms = jnp.max(s, axis=-1, keepdims=True)
        m = ms if m is None else jnp.maximum(m, ms)
    den = None
    acc = None
    for s, val in pieces:
        p = jnp.exp(s - m)
        ls = jnp.sum(p, axis=-1, keepdims=True)
        den = ls if den is None else den + ls
        o = jnp.dot(p.astype(BF16), jnp.where(mh, val, jnp.zeros_like(val)),
                    preferred_element_type=F32)
        acc = o if acc is None else acc + o
    return acc / den


def _attn_kernel(*refs, has_ctx, tq):
    if has_ctx:
        q_ref, k_ref, v_ref, kc_ref, vc_ref, o_ref = refs
    else:
        q_ref, k_ref, v_ref, o_ref = refs
    lane = lax.broadcasted_iota(jnp.int32, (1, LANES), 1)
    k = k_ref[...]
    v = v_ref[...]
    if has_ctx:
        kv_head = pl.program_id(1) // 2
        sel = jnp.where(lane < HEAD_DIM, 0, 1) == kv_head
        kc = jnp.where(sel, kc_ref[...], 0.0)
        kc = (kc + pltpu.roll(kc, HEAD_DIM, 1)).astype(BF16)
        vc = jnp.where(sel, vc_ref[...], 0.0)
        vc = (vc + pltpu.roll(vc, HEAD_DIM, 1)).astype(BF16)
    nt = (((1,), (1,)), ((), ()))
    for t in range(q_ref.shape[0] // tq):
        q = q_ref[t * tq:(t + 1) * tq, :]
        out = None
        for hh in range(LANES // HEAD_DIM):
            mh = (lane < HEAD_DIM) if hh == 0 else (lane >= HEAD_DIM)
            qm = jnp.where(mh, q, jnp.zeros_like(q))
            pieces = [(lax.dot_general(qm, k, nt, preferred_element_type=F32), v)]
            if has_ctx:
                pieces.append((lax.dot_general(qm, kc, nt, preferred_element_type=F32), vc))
            o = _softmax_pv(pieces, mh)
            out = o if out is None else out + o
        o_ref[t * tq:(t + 1) * tq, :] = out.astype(BF16)


def _attention(q, k, v, *, n_batch, seq, row0, n_pairs, kv_col, ctx=None):
    blk0 = row0 // seq
    in_specs = [pl.BlockSpec((seq, LANES), lambda b, p: (blk0 + b, p)),
                pl.BlockSpec((seq, LANES), lambda b, p: (blk0 + b, kv_col(p))),
                pl.BlockSpec((seq, LANES), lambda b, p: (blk0 + b, kv_col(p)))]
    args = [q, k, v]
    if ctx is not None:
        kc, vc, e = ctx
        sc = kc.shape[2]
        cspec = pl.BlockSpec((None, None, sc, LANES), lambda b, p: (b, e, 0, 0))
        in_specs += [cspec, cspec]
        args += [kc, vc]
    return pl.pallas_call(
        functools.partial(_attn_kernel, has_ctx=ctx is not None, tq=min(TQ_ATTN, seq)),
        grid=(n_batch, n_pairs),
        in_specs=in_specs,
        out_specs=pl.BlockSpec((seq, LANES), lambda b, p: (b, p)),
        out_shape=jax.ShapeDtypeStruct((n_batch * seq, n_pairs * LANES), BF16),
        compiler_params=_cparams(2),
        name="attention",
    )(*args)


def _na_windows():
    rows = SEG // GRID_W
    starts = []
    for qb in range(rows // NA_QROWS):
        r_lo, r_hi = qb * NA_QROWS, qb * NA_QROWS + NA_QROWS - 1
        lo = min(max(r_lo - NA_ROWS // 2, 0), rows - NA_ROWS)
        hi = min(max(r_hi - NA_ROWS // 2, 0), rows - NA_ROWS) + NA_ROWS
        ws = min(lo, rows - NA_KROWS)
        assert ws <= lo and hi <= ws + NA_KROWS
        starts.append(ws)
    return starts


def _rpb_table_kernel(r_ref, o_ref):
    n = o_ref.shape[1]
    col = lax.broadcasted_iota(jnp.int32, (1, n), 1)
    qc = col >> 6
    kc = col & (GRID_W - 1)
    idx = lax.broadcasted_iota(jnp.int32, (r_ref.shape[1], n), 0)
    onehot = jnp.where(idx == (kc - qc + NA_COLS - 1), 1.0, 0.0).astype(BF16)
    r = r_ref[...]
    hi = r.astype(BF16)
    r1 = r - hi.astype(F32)
    mid = r1.astype(BF16)
    lo = (r1 - mid.astype(F32)).astype(BF16)
    val = (jnp.dot(lo, onehot, preferred_element_type=F32)
           + jnp.dot(mid, onehot, preferred_element_type=F32)
           + jnp.dot(hi, onehot, preferred_element_type=F32))
    start = jnp.clip(qc - NA_COLS // 2, 0, GRID_W - NA_COLS)
    inside = (kc >= start) & (kc < start + NA_COLS)
    o_ref[...] = jnp.where(inside, val, NEG)


def _na_bias(rpb):
    H = rpb.shape[0]
    nr = 2 * NA_ROWS - 1
    r2 = jnp.pad(rpb.reshape(H * nr, 2 * NA_COLS - 1), ((0, 0), (0, 1)))
    tab = pl.pallas_call(
        _rpb_table_kernel,
        out_shape=jax.ShapeDtypeStruct((H * nr, GRID_W * GRID_W), F32),
        name="rpb_table",
    )(r2).reshape(H, nr, GRID_W, GRID_W)
    rows = SEG // GRID_W
    neg = jnp.full((H, GRID_W, GRID_W), NEG, F32)
    blocks = []
    for qb, ws in enumerate(_na_windows()):
        qrows = []
        for i in range(NA_QROWS):
            r = qb * NA_QROWS + i
            rs = min(max(r - NA_ROWS // 2, 0), rows - NA_ROWS)
            tiles = [tab[:, kr - r + NA_ROWS - 1] if rs <= kr < rs + NA_ROWS else neg
                     for kr in range(ws, ws + NA_KROWS)]
            qrows.append(jnp.concatenate(tiles, axis=2))
        blocks.append(jnp.concatenate(qrows, axis=1))
    return jnp.stack(blocks, axis=1)


def _na_kernel(q_ref, k_ref, v_ref, kc_ref, vc_ref, b_ref, o_ref, *, starts):
    lane = lax.broadcasted_iota(jnp.int32, (1, LANES), 1)
    kc = kc_ref[...].astype(BF16)
    vc = vc_ref[...].astype(BF16)
    nt = (((1,), (1,)), ((), ()))
    nq = NA_QROWS * GRID_W
    nk = NA_KROWS * GRID_W
    for qb, ws in enumerate(starts):
        q = q_ref[qb * nq:(qb + 1) * nq, :]
        kw = k_ref[ws * GRID_W:ws * GRID_W + nk, :]
        vw = v_ref[ws * GRID_W:ws * GRID_W + nk, :]
        out = None
        for hh in range(LANES // HEAD_DIM):
            mh = (lane < HEAD_DIM) if hh == 0 else (lane >= HEAD_DIM)
            qm = jnp.where(mh, q, jnp.zeros_like(q))
            s_win = lax.dot_general(qm, kw, nt, preferred_element_type=F32) + b_ref[hh, qb]
            s_ctx = lax.dot_general(qm, kc, nt, preferred_element_type=F32)
            o = _softmax_pv([(s_win, vw), (s_ctx, vc)], mh)
            out = o if out is None else out + o
        o_ref[qb * nq:(qb + 1) * nq, :] = out.astype(BF16)


def _na_attention(q, k, v, kc, vc, o_idx, bias, *, n_batch, row0):
    blk0 = row0 // SEG
    n_pairs = q.shape[1] // LANES
    sc = kc.shape[2]
    n_qb = bias.shape[1]
    qkv = pl.BlockSpec((SEG, LANES), lambda p, b: (blk0 + b, p))
    cspec = pl.BlockSpec((None, None, sc, LANES), lambda p, b: (b, o_idx, 0, p))
    bspec = pl.BlockSpec((2, n_qb, NA_QROWS * GRID_W, NA_KROWS * GRID_W), lambda p, b: (p, 0, 0, 0))
    return pl.pallas_call(
        functools.partial(_na_kernel, starts=_na_windows()),
        grid=(n_pairs, n_batch),
        in_specs=[qkv, qkv, qkv, cspec, cspec, bspec],
        out_specs=pl.BlockSpec((SEG, LANES), lambda p, b: (b, p)),
        out_shape=jax.ShapeDtypeStruct((n_batch * SEG, n_pairs * LANES), BF16),
        compiler_params=_cparams(2),
        name="na_attention",
    )(q, k, v, kc, vc, bias)


def _out_kernel(a_ref, b_ref, x_ref, gt_ref, g_ref, w_ref, o_ref):
    half = a_ref.shape[1]
    y = (jnp.dot(a_ref[...], w_ref[0:half, :], preferred_element_type=F32)
         + jnp.dot(b_ref[...], w_ref[half:2 * half, :], preferred_element_type=F32))
    o_ref[...] = x_ref[...] + gt_ref[...] * (_rms(y) * g_ref[...])


def _out_proj(a, b, b_col, x, mods, layer, g_post, w_bf):
    T = x.shape[0]
    tm = TM_PROJ
    half = D // 2
    full = lambda shape: pl.BlockSpec(shape, lambda i: (0,) * len(shape))
    return pl.pallas_call(
        _out_kernel,
        grid=(T // tm,),
        in_specs=[pl.BlockSpec((tm, half), lambda i: (i, 0)),
                  pl.BlockSpec((tm, half), lambda i: (i, b_col)),
                  pl.BlockSpec((tm, D), lambda i: (i, 0)), _mod_spec(layer, 2, tm),
                  full((1, D)), full((D, D))],
        out_specs=pl.BlockSpec((tm, D), lambda i: (i, 0)),
        out_shape=jax.ShapeDtypeStruct((T, D), F32),
        compiler_params=_cparams(1),
        name="out_proj",
    )(a, b, x, mods, g_post, w_bf)


def _ffn_kernel(x_ref, sh_ref, sc_ref, gt_ref, gpre_ref, gpost_ref, wg_ref, wu_ref, wo_ref,
                o_ref, h_ref, acc_ref):
    j = pl.program_id(1)

    @pl.when(j == 0)
    def _():
        h_ref[...] = _mod_norm(x_ref[...], gpre_ref[...], sh_ref[...], sc_ref[...]).astype(BF16)
        acc_ref[...] = jnp.zeros_like(acc_ref)

    h = h_ref[...]
    g = jnp.dot(h, wg_ref[...], preferred_element_type=F32)
    u = jnp.dot(h, wu_ref[...], preferred_element_type=F32)
    act = (g * jax.nn.sigmoid(g) * u).astype(BF16)
    acc_ref[...] += jnp.dot(act, wo_ref[...], preferred_element_type=F32)

    @pl.when(j == pl.num_programs(1) - 1)
    def _():
        o_ref[...] = x_ref[...] + gt_ref[...] * (_rms(acc_ref[...]) * gpost_ref[...])


def _ffn(x, mods, layer, g_pre, g_post, w_in_bf, w_out_bf):
    T = x.shape[0]
    tm, tf = TM_FFN, TF_FFN
    nf = D_FF // tf
    mod = lambda j: pl.BlockSpec((None, None, None, 1, D),
                                 lambda i, f: (layer, j, (i * tm) // SEG, 0, 0))
    full = pl.BlockSpec((1, D), lambda i, f: (0, 0))
    xrow = pl.BlockSpec((tm, D), lambda i, f: (i, 0))
    return pl.pallas_call(
        _ffn_kernel,
        grid=(T // tm, nf),
        in_specs=[xrow, mod(3), mod(4), mod(5), full, full,
                  pl.BlockSpec((D, tf), lambda i, f: (0, f)),
                  pl.BlockSpec((D, tf), lambda i, f: (0, nf + f)),
                  pl.BlockSpec((tf, D), lambda i, f: (f, 0))],
        out_specs=xrow,
        out_shape=jax.ShapeDtypeStruct((T, D), F32),
        scratch_shapes=[pltpu.VMEM((tm, D), BF16), pltpu.VMEM((tm, D), F32)],
        compiler_params=_cparams(2),
        name="ffn",
    )(x, mods, mods, mods, g_pre, g_post, w_in_bf, w_in_bf, w_out_bf)


def _rope_tables():
    t = jnp.arange(SEG)
    nf = HEAD_DIM // 4
    freqs = ROPE_BASE ** (-jnp.arange(nf, dtype=F32) / nf)

    def cs(pos):
        ang = pos.astype(F32)[:, None] * freqs[None, :]
        return jnp.cos(ang), jnp.sin(ang)

    cr, sr = cs(t // GRID_W)
    cc, sn = cs(t % GRID_W)
    cos = jnp.concatenate([cr, cr, cc, cc], axis=1)
    sin = jnp.concatenate([-sr, sr, -sn, sn], axis=1)
    reps = LANES // HEAD_DIM
    cos = jnp.tile(cos, (1, reps))
    sin = jnp.tile(sin, (1, reps))
    return (jnp.stack([jnp.ones_like(cos), cos]), jnp.stack([jnp.zeros_like(sin), sin]))


def _block_diag_ones(n):
    i = jnp.arange(n) // HEAD_DIM
    return (i[:, None] == i[None, :]).astype(BF16)


def kernel(x_prompt, x_sample, cache_attn_k, cache_attn_v, cache_na_k, cache_na_v, c, c_ctx,
           mod_w, mod_b, norm_mix_pre, norm_mix_post, norm_ffn_pre, norm_ffn_post,
           even_w_in, even_w_out, sgu_w, sgu_b, sgu_norm, q_norm, k_norm,
           odd_w_in, odd_w_out, na_rpb, ffn_w_in, ffn_w_out):
    nb_p, seq_p, _ = x_prompt.shape
    nb_s, seq_s, _ = x_sample.shape
    tp = nb_p * seq_p
    assert seq_s == SEG and tp == nb_s * seq_s and tp % SEG == 0

    x = jnp.concatenate([x_prompt.reshape(tp, D), x_sample.reshape(nb_s * seq_s, D)], axis=0)
    cond = jnp.concatenate([jnp.broadcast_to(c_ctx[None, :], (tp // SEG, D)), c], axis=0)
    mods = _modulation(cond, mod_w, mod_b)

    cos, sin = _rope_tables()
    bdq = _block_diag_ones(B_HEADS * HEAD_DIM)
    bdk = _block_diag_ones(LANES)
    past = cache_attn_k.shape[2]
    ctx_ak = cache_attn_k.reshape(nb_s, -1, past, B_KV_HEADS * HEAD_DIM)
    ctx_av = cache_attn_v.reshape(nb_s, -1, past, B_KV_HEADS * HEAD_DIM)
    ctx_nk = cache_na_k.reshape(nb_s, -1, past, C_HEADS * HEAD_DIM)
    ctx_nv = cache_na_v.reshape(nb_s, -1, past, C_HEADS * HEAD_DIM)

    attn_k, attn_v, na_k, na_v = [], [], [], []
    for l in range(DEPTH):
        row = lambda a: a[l].reshape(1, -1)
        if l % 2 == 0:
            e = l // 2
            a_out, q, kd, vd, kf, vf = _even_in(
                x, mods, l, row(norm_mix_pre), even_w_in[e].astype(BF16), sgu_w[e],
                sgu_b[e].reshape(A_GROUPS, CHUNK, 1), sgu_norm[e].reshape(1, -1),
                jnp.tile(q_norm[e], B_HEADS).reshape(1, -1),
                jnp.tile(k_norm[e], B_KV_HEADS).reshape(1, -1), bdq, bdk, cos, sin)
            n_pairs = B_HEADS * HEAD_DIM // LANES
            pairs_per_kv = n_pairs // B_KV_HEADS
            kv_col = lambda p: p // pairs_per_kv
            bp = _attention(q, kd, vd, n_batch=nb_p, seq=seq_p, row0=0, n_pairs=n_pairs,
                            kv_col=kv_col)
            bs = _attention(q, kd, vd, n_batch=nb_s, seq=seq_s, row0=tp, n_pairs=n_pairs,
                            kv_col=kv_col, ctx=(ctx_ak, ctx_av, e))
            mix_a, mix_b, b_col = a_out, jnp.concatenate([bp, bs], axis=0), 0
            attn_k.append(kf[:tp].reshape(nb_p, seq_p, B_KV_HEADS, HEAD_DIM))
            attn_v.append(vf[:tp].reshape(nb_p, seq_p, B_KV_HEADS, HEAD_DIM))
            w_out = even_w_out[e]
        else:
            o = l // 2
            q, k, v, kf, vf = _odd_in(x, mods, l, row(norm_mix_pre), odd_w_in[o].astype(BF16))
            n_pairs = C_HEADS * HEAD_DIM // LANES
            op = _attention(q, k, v, n_batch=nb_p, seq=seq_p, row0=0, n_pairs=n_pairs,
                            kv_col=lambda p: p)
            os_ = _na_attention(q, k, v, ctx_nk, ctx_nv, o, _na_bias(na_rpb[o]),
                                n_batch=nb_s, row0=tp)
            mix = jnp.concatenate([op, os_], axis=0)
            mix_a, mix_b, b_col = mix, mix, 1
            na_k.append(kf.reshape(nb_p, seq_p, C_HEADS, HEAD_DIM))
            na_v.append(vf.reshape(nb_p, seq_p, C_HEADS, HEAD_DIM))
            w_out = odd_w_out[o]
        x = _out_proj(mix_a, mix_b, b_col, x, mods, l, row(norm_mix_post), w_out.astype(BF16))
        x = _ffn(x, mods, l, row(norm_ffn_pre), row(norm_ffn_post),
                 ffn_w_in[l].astype(BF16), ffn_w_out[l].astype(BF16))

    y_prompt = x[:tp].reshape(nb_p, seq_p, D)
    y_sample = x[tp:].reshape(nb_s, seq_s, D)
    return (y_prompt, y_sample, jnp.stack(attn_k, axis=1), jnp.stack(attn_v, axis=1),
            jnp.stack(na_k, axis=1), jnp.stack(na_v, axis=1))
```

```python
import functools

import jax
import jax.numpy as jnp
from jax import lax
from jax.experimental import pallas as pl
from jax.experimental.pallas import tpu as pltpu

F32 = jnp.float32
BF16 = jnp.bfloat16

D = 1024
DEPTH = 4
HEAD_DIM = 64
GRID_W = 64
CHUNK = 128
A_WIDTH = D // 2
A_GROUPS = 4
B_HEADS = 8
B_KV_HEADS = 2
C_HEADS = 16
NA_ROWS = 8
NA_COLS = 16
D_FF = 2816
ROPE_BASE = 10000.0
EPS = 1e-6
NEG = -1e30
SEG = 1024
N_SEG = 8
LANES = 128
LANE_SHIFT = LANES.bit_length() - 1
QSCALE = HEAD_DIM ** -0.5

TM_PROJ = 512
TM_FFN = 1024
TF_FFN = 256
TQ_ATTN = 256
NA_QROWS = 4
NA_KROWS = 12
VMEM_LIMIT = 56 * 1024 * 1024


def _cparams(n_axes):
    return pltpu.CompilerParams(dimension_semantics=("arbitrary",) * n_axes,
                                vmem_limit_bytes=VMEM_LIMIT)


def _rms(x):
    return x * lax.rsqrt(jnp.mean(x * x, axis=-1, keepdims=True) + EPS)


def _mod_norm(x, g, shift, scale):
    return _rms(x) * g * (1.0 + scale) + shift


def _const_spec(shape, *lead):
    block = (None,) * len(lead) + tuple(shape)
    return pl.BlockSpec(block, lambda *_: tuple(lead) + (0,) * len(shape))


def _weight_spec(shape, *lead):
    block = (None,) * len(lead) + tuple(shape)
    return pl.BlockSpec(block, lambda *_: tuple(lead) + (0,) * len(shape),
                        pipeline_mode=pl.Buffered(1))


def _mod_spec(layer, j, tm):
    return pl.BlockSpec((None, None, None, 1, D),
                        lambda i, *_: (layer, j, (i * tm) // SEG, 0, 0))


def _split_specs(tm, width, n_prompt_tiles, col=0):
    return [pl.BlockSpec((tm, width), lambda i, *_: (jnp.minimum(i, n_prompt_tiles - 1), col)),
            pl.BlockSpec((tm, width), lambda i, *_: (jnp.maximum(i - n_prompt_tiles, 0), col))]


def _pick(n_prompt_tiles, p_ref, s_ref):
    return jnp.where(pl.program_id(0) < n_prompt_tiles, p_ref[...], s_ref[...])


def _mod_kernel(cond_ref, w_ref, b_ref, o_ref):
    s = cond_ref[...]
    s = s * jax.nn.sigmoid(s)
    o_ref[...] = jnp.dot(s.astype(BF16), w_ref[...].astype(BF16),
                         preferred_element_type=F32) + b_ref[...]


def _modulation(cond, mod_w, mod_b):
    b = mod_b.reshape(DEPTH, 6, 1, D)
    out = pl.pallas_call(
        _mod_kernel,
        grid=(DEPTH, 6),
        in_specs=[pl.BlockSpec((N_SEG, D), lambda l, j: (0, 0)),
                  pl.BlockSpec((None, D, D), lambda l, j: (l, 0, j)),
                  pl.BlockSpec((None, None, 1, D), lambda l, j: (l, j, 0, 0))],
        out_specs=pl.BlockSpec((None, None, N_SEG, D), lambda l, j: (l, j, 0, 0)),
        out_shape=jax.ShapeDtypeStruct((DEPTH, 6, N_SEG, D), F32),
        compiler_params=_cparams(2),
        name="modulation",
    )(cond, mod_w, b)
    return out.reshape(DEPTH, 6, N_SEG, 1, D)


def _head_sumsq(y, bd_ref):
    sq = y * y
    hi = sq.astype(BF16)
    lo = (sq - hi.astype(F32)).astype(BF16)
    bd = bd_ref[...]
    return (jnp.dot(hi, bd, preferred_element_type=F32)
            + jnp.dot(lo, bd, preferred_element_type=F32))


def _rope(y, cos, sin):
    lane = lax.broadcasted_iota(jnp.int32, (1, LANES), 1)
    first = (lane & 16) == 0
    partner = jnp.where(first, pltpu.roll(y, LANES - 16, 1), pltpu.roll(y, 16, 1))
    return y * cos + partner * sin


def _even_in_kernel(*refs, n_prompt_tiles, split_x):
    refs = list(refs)
    x = _pick(n_prompt_tiles, refs.pop(0), refs.pop(0)) if split_x else refs.pop(0)[...]
    (sh_ref, sc_ref, g_ref, w_ref, sguw_ref, sgub_ref, sgun_ref, qn_ref, kn_ref, bdq_ref, bdk_ref,
     cos_ref, sin_ref, a_ref, q_ref, kd_ref, vd_ref, kf_ref, vf_ref, wbf_ref) = refs
    tm = a_ref.shape[0]

    @pl.when(pl.program_id(0) == 0)
    def _():
        wbf_ref[...] = w_ref[...].astype(BF16)

    h = _mod_norm(x, g_ref[...], sh_ref[...], sc_ref[...]).astype(BF16)

    u = jax.nn.gelu(jnp.dot(h, wbf_ref[:, 0:A_WIDTH], preferred_element_type=F32), approximate=True)
    v = jax.nn.gelu(jnp.dot(h, wbf_ref[:, A_WIDTH:2 * A_WIDTH], preferred_element_type=F32),
                    approximate=True)
    mu = jnp.mean(v, axis=-1, keepdims=True)
    vc = v - mu
    var = jnp.mean(vc * vc, axis=-1, keepdims=True)
    vn = (vc * lax.rsqrt(var + EPS) * sgun_ref[...]).astype(BF16)
    n_chunks = tm // CHUNK
    gch = A_WIDTH // A_GROUPS
    for g in range(A_GROUPS):
        rhs = jnp.concatenate([vn[n * CHUNK:(n + 1) * CHUNK, g * gch:(g + 1) * gch]
                               for n in range(n_chunks)], axis=1)
        mixed = jnp.dot(sguw_ref[g].astype(BF16), rhs, preferred_element_type=F32)
        bias = sgub_ref[g]
        for n in range(n_chunks):
            blk = (mixed[:, n * gch:(n + 1) * gch] + bias) * u[n * CHUNK:(n + 1) * CHUNK,
                                                              g * gch:(g + 1) * gch]
            a_ref[n * CHUNK:(n + 1) * CHUNK, g * gch:(g + 1) * gch] = blk.astype(BF16)

    c0 = 2 * A_WIDTH
    qw = B_HEADS * HEAD_DIM
    q = jnp.dot(h, wbf_ref[:, c0:c0 + qw], preferred_element_type=F32)
    q = q * lax.rsqrt(_head_sumsq(q, bdq_ref) * (1.0 / HEAD_DIM) + EPS) * qn_ref[...]
    cos = cos_ref[...]
    sin = sin_ref[...]
    for j in range(qw // LANES):
        qj = _rope(q[:, j * LANES:(j + 1) * LANES], cos, sin) * QSCALE
        q_ref[:, j * LANES:(j + 1) * LANES] = qj.astype(BF16)

    kw = B_KV_HEADS * HEAD_DIM
    k = jnp.dot(h, wbf_ref[:, c0 + qw:c0 + qw + kw], preferred_element_type=F32)
    k = k * lax.rsqrt(_head_sumsq(k, bdk_ref) * (1.0 / HEAD_DIM) + EPS) * kn_ref[...]
    kf_ref[...] = k
    vals = jnp.dot(h, wbf_ref[:, c0 + qw + kw:c0 + qw + 2 * kw], preferred_element_type=F32)
    vf_ref[...] = vals
    kr = _rope(k, cos, sin)
    lane = lax.broadcasted_iota(jnp.int32, (1, LANES), 1)
    low = lane < HEAD_DIM
    for src, dst in ((kr, kd_ref), (vals, vd_ref)):
        sw = pltpu.roll(src, HEAD_DIM, 1)
        dst[:, 0:LANES] = jnp.where(low, src, sw).astype(BF16)
        dst[:, LANES:2 * LANES] = jnp.where(low, sw, src).astype(BF16)


def _even_in(xs, mods, layer, e, T, g_pre, w_in, sgu_w, sgu_b, sgu_norm, q_norm, k_norm,
             bdq, bdk, cos, sin):
    tm = TM_PROJ
    n_in = w_in.shape[2]
    tiles_per_seg = SEG // tm
    n_prompt_tiles = T // 2 // tm
    split_x = len(xs) == 2
    tab_spec = pl.BlockSpec((None, tm, LANES),
                            lambda i: (i // n_prompt_tiles, i % tiles_per_seg, 0))
    row = lambda w: pl.BlockSpec((tm, w), lambda i: (i, 0))
    x_specs = _split_specs(tm, D, n_prompt_tiles) if split_x else [row(D)]
    qw = B_HEADS * HEAD_DIM
    return pl.pallas_call(
        functools.partial(_even_in_kernel, n_prompt_tiles=n_prompt_tiles, split_x=split_x),
        grid=(T // tm,),
        in_specs=x_specs + [
            _mod_spec(layer, 0, tm), _mod_spec(layer, 1, tm), _const_spec((1, D), layer),
            _weight_spec((D, n_in), e), _const_spec((A_GROUPS, CHUNK, CHUNK), e),
            _const_spec((A_GROUPS, CHUNK, 1), e), _const_spec((1, A_WIDTH), e),
            _const_spec((1, qw), e), _const_spec((1, LANES), e),
            _const_spec((qw, qw)), _const_spec((LANES, LANES)), tab_spec, tab_spec],
        out_specs=[row(A_WIDTH), row(qw), row(2 * LANES), row(2 * LANES), row(LANES), row(LANES)],
        out_shape=[jax.ShapeDtypeStruct((T, A_WIDTH), BF16),
                   jax.ShapeDtypeStruct((T, qw), BF16),
                   jax.ShapeDtypeStruct((T, 2 * LANES), BF16),
                   jax.ShapeDtypeStruct((T, 2 * LANES), BF16),
                   jax.ShapeDtypeStruct((T, LANES), F32),
                   jax.ShapeDtypeStruct((T, LANES), F32)],
        scratch_shapes=[pltpu.VMEM((D, n_in), BF16)],
        compiler_params=_cparams(1),
        name="even_in_proj",
    )(*xs, mods, mods, g_pre, w_in, sgu_w, sgu_b, sgu_norm, q_norm, k_norm, bdq, bdk, cos, sin)


def _odd_in_kernel(x_ref, sh_ref, sc_ref, g_ref, w_ref, q_ref, k_ref, v_ref, kf_ref, vf_ref,
                   wbf_ref, *, n_prompt_tiles):
    i = pl.program_id(0)
    hw = C_HEADS * HEAD_DIM

    @pl.when(i == 0)
    def _():
        for c in range(3):
            wbf_ref[:, c * hw:(c + 1) * hw] = w_ref[:, c * hw:(c + 1) * hw].astype(BF16)

    h = _mod_norm(x_ref[...], g_ref[...], sh_ref[...], sc_ref[...]).astype(BF16)
    q = jnp.dot(h, wbf_ref[:, 0:hw], preferred_element_type=F32)
    q_ref[...] = (q * QSCALE).astype(BF16)
    k = jnp.dot(h, wbf_ref[:, hw:2 * hw], preferred_element_type=F32)
    k_ref[...] = k.astype(BF16)
    v = jnp.dot(h, wbf_ref[:, 2 * hw:3 * hw], preferred_element_type=F32)
    v_ref[...] = v.astype(BF16)

    @pl.when(i < n_prompt_tiles)
    def _():
        kf_ref[...] = k
        vf_ref[...] = v


def _odd_in(x, mods, layer, o, g_pre, w_in):
    T = x.shape[0]
    tm = TM_PROJ
    hw = C_HEADS * HEAD_DIM
    n_prompt_tiles = T // 2 // tm
    row = pl.BlockSpec((tm, hw), lambda i: (i, 0))
    prow = pl.BlockSpec((tm, hw), lambda i: (jnp.minimum(i, n_prompt_tiles - 1), 0))
    return pl.pallas_call(
        functools.partial(_odd_in_kernel, n_prompt_tiles=n_prompt_tiles),
        grid=(T // tm,),
        in_specs=[pl.BlockSpec((tm, D), lambda i: (i, 0)), _mod_spec(layer, 0, tm),
                  _mod_spec(layer, 1, tm), _const_spec((1, D), layer),
                  _weight_spec((D, 3 * hw), o)],
        out_specs=[row, row, row, prow, prow],
        out_shape=[jax.ShapeDtypeStruct((T, hw), BF16)] * 3
                  + [jax.ShapeDtypeStruct((T // 2, hw), F32)] * 2,
        scratch_shapes=[pltpu.VMEM((D, 3 * hw), BF16)],
        compiler_params=_cparams(1),
        name="odd_in_proj",
    )(x, mods, mods, g_pre, w_in)


def _softmax_pv(pieces, mh):
    m = None
    for s, _ in pieces:
        ms = jnp.max(s, axis=-1, keepdims=True)
        m = ms if m is None else jnp.maximum(m, ms)
    den = None
    acc = None
    for s, val in pieces:
        p = jnp.exp(s - m)
        ls = jnp.sum(p, axis=-1, keepdims=True)
        den = ls if den is None else den + ls
        o = jnp.dot(p.astype(BF16), jnp.where(mh, val, jnp.zeros_like(val)),
                    preferred_element_type=F32)
        acc = o if acc is None else acc + o
    return acc / den


def _attn_kernel(*refs, has_ctx, tq):
    if has_ctx:
        q_ref, k_ref, v_ref, kc_ref, vc_ref, o_ref = refs
    else:
        q_ref, k_ref, v_ref, o_ref = refs
    lane = lax.broadcasted_iota(jnp.int32, (1, LANES), 1)
    k = k_ref[...]
    v = v_ref[...]
    if has_ctx:
        kv_head = pl.program_id(1) // 2
        sel = jnp.where(lane < HEAD_DIM, 0, 1) == kv_head
        kc = jnp.where(sel, kc_ref[...], 0.0)
        kc = (kc + pltpu.roll(kc, HEAD_DIM, 1)).astype(BF16)
        vc = jnp.where(sel, vc_ref[...], 0.0)
        vc = (vc + pltpu.roll(vc, HEAD_DIM, 1)).astype(BF16)
    nt = (((1,), (1,)), ((), ()))
    for t in range(q_ref.shape[0] // tq):
        q = q_ref[t * tq:(t + 1) * tq, :]
        out = None
        for hh in range(LANES // HEAD_DIM):
            mh = (lane < HEAD_DIM) if hh == 0 else (lane >= HEAD_DIM)
            qm = jnp.where(mh, q, jnp.zeros_like(q))
            pieces = [(lax.dot_general(qm, k, nt, preferred_element_type=F32), v)]
            if has_ctx:
                pieces.append((lax.dot_general(qm, kc, nt, preferred_element_type=F32), vc))
            o = _softmax_pv(pieces, mh)
            out = o if out is None else out + o
        o_ref[t * tq:(t + 1) * tq, :] = out.astype(BF16)


def _attention(q, k, v, *, n_batch, seq, row0, n_pairs, kv_col, ctx=None):
    blk0 = row0 // seq
    in_specs = [pl.BlockSpec((seq, LANES), lambda b, p: (blk0 + b, p)),
                pl.BlockSpec((seq, LANES), lambda b, p: (blk0 + b, kv_col(p))),
                pl.BlockSpec((seq, LANES), lambda b, p: (blk0 + b, kv_col(p)))]
    args = [q, k, v]
    if ctx is not None:
        kc, vc, e = ctx
        sc = kc.shape[2]
        cspec = pl.BlockSpec((None, None, sc, LANES), lambda b, p: (b, e, 0, 0))
        in_specs += [cspec, cspec]
        args += [kc, vc]
    return pl.pallas_call(
        functools.partial(_attn_kernel, has_ctx=ctx is not None, tq=min(TQ_ATTN, seq)),
        grid=(n_batch, n_pairs),
        in_specs=in_specs,
        out_specs=pl.BlockSpec((seq, LANES), lambda b, p: (b, p)),
        out_shape=jax.ShapeDtypeStruct((n_batch * seq, n_pairs * LANES), BF16),
        compiler_params=_cparams(2),
        name="attention",
    )(*args)


def _na_row_start(r):
    rows = SEG // GRID_W
    return min(max(r - NA_ROWS // 2, 0), rows - NA_ROWS)


def _na_windows():
    rows = SEG // GRID_W
    starts = []
    for qb in range(rows // NA_QROWS):
        lo = _na_row_start(qb * NA_QROWS)
        hi = _na_row_start(qb * NA_QROWS + NA_QROWS - 1) + NA_ROWS
        ws = min(lo, rows - NA_KROWS)
        assert ws <= lo and hi <= ws + NA_KROWS
        starts.append(ws)
    return starts


def _rpb_table_kernel(r_ref, o_ref):
    n = o_ref.shape[1]
    col = lax.broadcasted_iota(jnp.int32, (1, n), 1)
    qc = col >> LANE_SHIFT
    kc = col & (GRID_W - 1)
    idx = lax.broadcasted_iota(jnp.int32, (r_ref.shape[1], n), 0)
    onehot = jnp.where(idx == (kc - qc + NA_COLS - 1), 1.0, 0.0).astype(BF16)
    r = r_ref[...]
    hi = r.astype(BF16)
    r1 = r - hi.astype(F32)
    mid = r1.astype(BF16)
    lo = (r1 - mid.astype(F32)).astype(BF16)
    val = (jnp.dot(lo, onehot, preferred_element_type=F32)
           + jnp.dot(mid, onehot, preferred_element_type=F32)
           + jnp.dot(hi, onehot, preferred_element_type=F32))
    start = jnp.clip(qc - NA_COLS // 2, 0, GRID_W - NA_COLS)
    inside = (kc >= start) & (kc < start + NA_COLS)
    o_ref[...] = jnp.where(inside, val, NEG)


def _rpb_table(rpb):
    H = rpb.shape[0]
    nr = 2 * NA_ROWS - 1
    r2 = jnp.pad(rpb.reshape(H * nr, 2 * NA_COLS - 1), ((0, 0), (0, 1)))
    return pl.pallas_call(
        _rpb_table_kernel,
        out_shape=jax.ShapeDtypeStruct((H * nr, GRID_W * LANES), F32),
        name="rpb_table",
    )(r2).reshape(H, nr, GRID_W, LANES)


def _na_kernel(q_ref, k_ref, v_ref, kc_ref, vc_ref, t_ref, o_ref, bias_ref, *, starts):
    lane = lax.broadcasted_iota(jnp.int32, (1, LANES), 1)
    low = lane < HEAD_DIM

    @pl.when(pl.program_id(1) == 0)
    def _():
        neg = jnp.full((GRID_W, LANES), NEG, F32)
        for hh in range(LANES // HEAD_DIM):
            for qb, ws in enumerate(starts):
                for i in range(NA_QROWS):
                    r = qb * NA_QROWS + i
                    rs = _na_row_start(r)
                    for jp in range(NA_KROWS // 2):
                        kr = ws + 2 * jp
                        ok = [rs <= kr + d < rs + NA_ROWS for d in (0, 1)]
                        if not any(ok):
                            blk = neg
                        else:
                            t0 = t_ref[hh, kr - r + NA_ROWS - 1] if ok[0] else neg
                            t1 = t_ref[hh, kr + 1 - r + NA_ROWS - 1] if ok[1] else neg
                            blk = jnp.where(low, t0, t1)
                        bias_ref[hh, qb, i * GRID_W:(i + 1) * GRID_W,
                                 jp * LANES:(jp + 1) * LANES] = blk

    kc = kc_ref[...].astype(BF16)
    vc = vc_ref[...].astype(BF16)
    nt = (((1,), (1,)), ((), ()))
    nq = NA_QROWS * GRID_W
    nk = NA_KROWS * GRID_W
    for qb, ws in enumerate(starts):
        q = q_ref[qb * nq:(qb + 1) * nq, :]
        kw = k_ref[ws * GRID_W:ws * GRID_W + nk, :]
        vw = v_ref[ws * GRID_W:ws * GRID_W + nk, :]
        out = None
        for hh in range(LANES // HEAD_DIM):
            mh = low if hh == 0 else jnp.logical_not(low)
            qm = jnp.where(mh, q, jnp.zeros_like(q))
            s_win = lax.dot_general(qm, kw, nt, preferred_element_type=F32) + bias_ref[hh, qb]
            s_ctx = lax.dot_general(qm, kc, nt, preferred_element_type=F32)
            o = _softmax_pv([(s_win, vw), (s_ctx, vc)], mh)
            out = o if out is None else out + o
        o_ref[qb * nq:(qb + 1) * nq, :] = out.astype(BF16)


def _na_attention(q, k, v, kc, vc, o_idx, table, *, n_batch, row0):
    blk0 = row0 // SEG
    n_pairs = q.shape[1] // LANES
    sc = kc.shape[2]
    starts = _na_windows()
    heads = LANES // HEAD_DIM
    qkv = pl.BlockSpec((SEG, LANES), lambda p, b: (blk0 + b, p))
    cspec = pl.BlockSpec((None, None, sc, LANES), lambda p, b: (b, o_idx, 0, p))
    tspec = pl.BlockSpec((heads,) + table.shape[1:], lambda p, b: (p, 0, 0, 0))
    return pl.pallas_call(
        functools.partial(_na_kernel, starts=starts),
        grid=(n_pairs, n_batch),
        in_specs=[qkv, qkv, qkv, cspec, cspec, tspec],
        out_specs=pl.BlockSpec((SEG, LANES), lambda p, b: (b, p)),
        out_shape=jax.ShapeDtypeStruct((n_batch * SEG, n_pairs * LANES), BF16),
        scratch_shapes=[pltpu.VMEM((heads, len(starts), NA_QROWS * GRID_W, NA_KROWS * GRID_W), F32)],
        compiler_params=_cparams(2),
        name="na_attention",
    )(q, k, v, kc, vc, table)


def _out_kernel(*refs, n_prompt_tiles, split_x, has_a):
    refs = list(refs)
    x = _pick(n_prompt_tiles, refs.pop(0), refs.pop(0)) if split_x else refs.pop(0)[...]
    gt_ref, g_ref, w_ref = refs[:3]
    a_ref = refs[3] if has_a else None
    bp_ref, bs_ref, o_ref, wbf_ref = refs[3 + has_a:]

    @pl.when(pl.program_id(0) == 0)
    def _():
        wbf_ref[...] = w_ref[...].astype(BF16)

    b = _pick(n_prompt_tiles, bp_ref, bs_ref)
    if has_a:
        half = a_ref.shape[1]
        y = (jnp.dot(a_ref[...], wbf_ref[0:half, :], preferred_element_type=F32)
             + jnp.dot(b, wbf_ref[half:2 * half, :], preferred_element_type=F32))
    else:
        y = jnp.dot(b, wbf_ref[...], preferred_element_type=F32)
    o_ref[...] = x + gt_ref[...] * (_rms(y) * g_ref[...])


def _out_proj(xs, a, bp, bs, mods, layer, li, T, g_post, w_out):
    tm = TM_PROJ
    n_prompt_tiles = T // 2 // tm
    split_x = len(xs) == 2
    has_a = a is not None
    row = lambda w: pl.BlockSpec((tm, w), lambda i: (i, 0))
    x_specs = _split_specs(tm, D, n_prompt_tiles) if split_x else [row(D)]
    a_specs = [row(a.shape[1])] if has_a else []
    return pl.pallas_call(
        functools.partial(_out_kernel, n_prompt_tiles=n_prompt_tiles, split_x=split_x, has_a=has_a),
        grid=(T // tm,),
        in_specs=x_specs + [_mod_spec(layer, 2, tm), _const_spec((1, D), layer),
                            _weight_spec((D, D), li)] + a_specs
                 + _split_specs(tm, bp.shape[1], n_prompt_tiles),
        out_specs=row(D),
        out_shape=jax.ShapeDtypeStruct((T, D), F32),
        scratch_shapes=[pltpu.VMEM((D, D), BF16)],
        compiler_params=_cparams(1),
        name="out_proj",
    )(*xs, mods, g_post, w_out, *([a] if has_a else []), bp, bs)


def _ffn_kernel(*refs, n_prompt_tiles, split_out):
    (x_ref, sh_ref, sc_ref, gt_ref, gpre_ref, gpost_ref, wg_ref, wu_ref, wo_ref) = refs[:9]
    out_refs = refs[9:-2]
    h_ref, acc_ref = refs[-2:]
    i = pl.program_id(0)
    j = pl.program_id(1)

    @pl.when(j == 0)
    def _():
        h_ref[...] = _mod_norm(x_ref[...], gpre_ref[...], sh_ref[...], sc_ref[...]).astype(BF16)
        acc_ref[...] = jnp.zeros_like(acc_ref)

    h = h_ref[...]
    g = jnp.dot(h, wg_ref[...].astype(BF16), preferred_element_type=F32)
    u = jnp.dot(h, wu_ref[...].astype(BF16), preferred_element_type=F32)
    act = (g * jax.nn.sigmoid(g) * u).astype(BF16)
    acc_ref[...] += jnp.dot(act, wo_ref[...].astype(BF16), preferred_element_type=F32)

    last = j == pl.num_programs(1) - 1

    def result():
        return x_ref[...] + gt_ref[...] * (_rms(acc_ref[...]) * gpost_ref[...])

    if split_out:
        @pl.when(last & (i < n_prompt_tiles))
        def _():
            out_refs[0][...] = result()

        @pl.when(last & (i >= n_prompt_tiles))
        def _():
            out_refs[1][...] = result()
    else:
        @pl.when(last)
        def _():
            out_refs[0][...] = result()


def _ffn(x, mods, layer, g_pre, g_post, w_in, w_out, split_out):
    T = x.shape[0]
    tm, tf = TM_FFN, TF_FFN
    nf = D_FF // tf
    n_prompt_tiles = T // 2 // tm
    xrow = pl.BlockSpec((tm, D), lambda i, f: (i, 0))
    if split_out:
        out_specs = _split_specs(tm, D, n_prompt_tiles)
        out_shape = [jax.ShapeDtypeStruct((T // 2, D), F32)] * 2
    else:
        out_specs = [xrow]
        out_shape = [jax.ShapeDtypeStruct((T, D), F32)]
    return pl.pallas_call(
        functools.partial(_ffn_kernel, n_prompt_tiles=n_prompt_tiles, split_out=split_out),
        grid=(T // tm, nf),
        in_specs=[xrow, _mod_spec(layer, 3, tm), _mod_spec(layer, 4, tm), _mod_spec(layer, 5, tm),
                  _const_spec((1, D), layer), _const_spec((1, D), layer),
                  pl.BlockSpec((None, D, tf), lambda i, f: (layer, 0, f)),
                  pl.BlockSpec((None, D, tf), lambda i, f: (layer, 0, nf + f)),
                  pl.BlockSpec((None, tf, D), lambda i, f: (layer, f, 0))],
        out_specs=out_specs,
        out_shape=out_shape,
        scratch_shapes=[pltpu.VMEM((tm, D), BF16), pltpu.VMEM((tm, D), F32)],
        compiler_params=_cparams(2),
        name="ffn",
    )(x, mods, mods, mods, g_pre, g_post, w_in, w_in, w_out)


def _rope_tables():
    t = jnp.arange(SEG)
    nf = HEAD_DIM // 4
    freqs = ROPE_BASE ** (-jnp.arange(nf, dtype=F32) / nf)

    def cs(pos):
        ang = pos.astype(F32)[:, None] * freqs[None, :]
        return jnp.cos(ang), jnp.sin(ang)

    cr, sr = cs(t // GRID_W)
    cc, sn = cs(t % GRID_W)
    cos = jnp.concatenate([cr, cr, cc, cc], axis=1)
    sin = jnp.concatenate([-sr, sr, -sn, sn], axis=1)
    reps = LANES // HEAD_DIM
    cos = jnp.tile(cos, (1, reps))
    sin = jnp.tile(sin, (1, reps))
    return (jnp.stack([jnp.ones_like(cos), cos]), jnp.stack([jnp.zeros_like(sin), sin]))


def _block_diag_ones(n):
    i = jnp.arange(n) // HEAD_DIM
    return (i[:, None] == i[None, :]).astype(BF16)


def kernel(x_prompt, x_sample, cache_attn_k, cache_attn_v, cache_na_k, cache_na_v, c, c_ctx,
           mod_w, mod_b, norm_mix_pre, norm_mix_post, norm_ffn_pre, norm_ffn_post,
           even_w_in, even_w_out, sgu_w, sgu_b, sgu_norm, q_norm, k_norm,
           odd_w_in, odd_w_out, na_rpb, ffn_w_in, ffn_w_out):
    nb_p, seq_p, _ = x_prompt.shape
    nb_s, seq_s, _ = x_sample.shape
    tp = nb_p * seq_p
    T = 2 * tp
    assert seq_s == SEG and tp == nb_s * seq_s and tp % SEG == 0

    cond = jnp.concatenate([jnp.broadcast_to(c_ctx[None, :], (tp // SEG, D)), c], axis=0)
    mods = _modulation(cond, mod_w, mod_b)

    cos, sin = _rope_tables()
    qw = B_HEADS * HEAD_DIM
    bdq = _block_diag_ones(qw)
    bdk = _block_diag_ones(LANES)
    past = cache_attn_k.shape[2]
    ctx_ak = cache_attn_k.reshape(nb_s, -1, past, B_KV_HEADS * HEAD_DIM)
    ctx_av = cache_attn_v.reshape(nb_s, -1, past, B_KV_HEADS * HEAD_DIM)
    ctx_nk = cache_na_k.reshape(nb_s, -1, past, C_HEADS * HEAD_DIM)
    ctx_nv = cache_na_v.reshape(nb_s, -1, past, C_HEADS * HEAD_DIM)
    g_mix_pre, g_mix_post, g_ffn_pre, g_ffn_post = (
        a.reshape(DEPTH, 1, D) for a in (norm_mix_pre, norm_mix_post, norm_ffn_pre, norm_ffn_post))
    n_even = even_w_in.shape[0]
    sgu_b3 = sgu_b.reshape(n_even, A_GROUPS, CHUNK, 1)
    sgu_n3 = sgu_norm.reshape(n_even, 1, A_WIDTH)
    qn3 = jnp.tile(q_norm, (1, B_HEADS)).reshape(n_even, 1, qw)
    kn3 = jnp.tile(k_norm, (1, B_KV_HEADS)).reshape(n_even, 1, LANES)

    xs = (x_prompt.reshape(tp, D), x_sample.reshape(nb_s * seq_s, D))
    attn_k, attn_v, na_k, na_v = [], [], [], []
    for l in range(DEPTH):
        if l % 2 == 0:
            e = l // 2
            a_out, q, kd, vd, kf, vf = _even_in(xs, mods, l, e, T, g_mix_pre, even_w_in, sgu_w,
                                                sgu_b3, sgu_n3, qn3, kn3, bdq, bdk, cos, sin)
            n_pairs = qw // LANES
            pairs_per_kv = n_pairs // B_KV_HEADS
            kv_col = lambda p: p // pairs_per_kv
            mix_p = _attention(q, kd, vd, n_batch=nb_p, seq=seq_p, row0=0, n_pairs=n_pairs,
                               kv_col=kv_col)
            mix_s = _attention(q, kd, vd, n_batch=nb_s, seq=seq_s, row0=tp, n_pairs=n_pairs,
                               kv_col=kv_col, ctx=(ctx_ak, ctx_av, e))
            attn_k.append(kf[:tp].reshape(nb_p, seq_p, B_KV_HEADS, HEAD_DIM))
            attn_v.append(vf[:tp].reshape(nb_p, seq_p, B_KV_HEADS, HEAD_DIM))
            w_out, li = even_w_out, e
        else:
            o = l // 2
            a_out = None
            q, k, v, kf, vf = _odd_in(xs[0], mods, l, o, g_mix_pre, odd_w_in)
            n_pairs = C_HEADS * HEAD_DIM // LANES
            mix_p = _attention(q, k, v, n_batch=nb_p, seq=seq_p, row0=0, n_pairs=n_pairs,
                               kv_col=lambda p: p)
            mix_s = _na_attention(q, k, v, ctx_nk, ctx_nv, o, _rpb_table(na_rpb[o]),
                                  n_batch=nb_s, row0=tp)
            na_k.append(kf.reshape(nb_p, seq_p, C_HEADS, HEAD_DIM))
            na_v.append(vf.reshape(nb_p, seq_p, C_HEADS, HEAD_DIM))
            w_out, li = odd_w_out, o
        x = _out_proj(xs, a_out, mix_p, mix_s, mods, l, li, T, g_mix_post, w_out)
        xs = tuple(_ffn(x, mods, l, g_ffn_pre, g_ffn_post, ffn_w_in, ffn_w_out,
                        split_out=(l == DEPTH - 1)))

    y_prompt = xs[0].reshape(nb_p, seq_p, D)
    y_sample = xs[1].reshape(nb_s, seq_s, D)
    return (y_prompt, y_sample, jnp.stack(attn_k, axis=1), jnp.stack(attn_v, axis=1),
            jnp.stack(na_k, axis=1), jnp.stack(na_v, axis=1))
```

```python
import functools

import jax
import jax.numpy as jnp
from jax import lax
from jax.experimental import pallas as pl
from jax.experimental.pallas import tpu as pltpu

F32 = jnp.float32
BF16 = jnp.bfloat16

D = 1024
DEPTH = 4
HEAD_DIM = 64
GRID_W = 64
CHUNK = 128
A_WIDTH = D // 2
A_GROUPS = 4
B_HEADS = 8
B_KV_HEADS = 2
C_HEADS = 16
NA_ROWS = 8
NA_COLS = 16
D_FF = 2816
ROPE_BASE = 10000.0
EPS = 1e-6
NEG = -1e30
SEG = 1024
N_SEG = 8
LANES = 128
QSCALE = HEAD_DIM ** -0.5

TM_PROJ = 512
TM_FFN = 1024
TF_FFN = 256
TQ_ATTN = 256
PAIRS_PER_STEP = 4
NA_QROWS = 4
NA_KROWS = 12
VMEM_LIMIT = 56 * 1024 * 1024


def _cparams(n_axes):
    return pltpu.CompilerParams(dimension_semantics=("arbitrary",) * n_axes,
                                vmem_limit_bytes=VMEM_LIMIT)


def _rms(x):
    return x * lax.rsqrt(jnp.mean(x * x, axis=-1, keepdims=True) + EPS)


def _mod_norm(x, g, shift, scale):
    return _rms(x) * (g * (1.0 + scale)) + shift


def _gated_residual(x, y, g, gate):
    return x + _rms(y) * (gate * g)


def _const_spec(shape, *lead):
    block = (None,) * len(lead) + tuple(shape)
    return pl.BlockSpec(block, lambda *_: tuple(lead) + (0,) * len(shape))


def _weight_spec(shape, *lead):
    block = (None,) * len(lead) + tuple(shape)
    return pl.BlockSpec(block, lambda *_: tuple(lead) + (0,) * len(shape),
                        pipeline_mode=pl.Buffered(1))


def _mod_spec(layer, j, tm):
    return pl.BlockSpec((None, None, None, 1, D),
                        lambda i, *_: (layer, j, (i * tm) // SEG, 0, 0))


def _split_specs(tm, width, n_prompt_tiles, col=0):
    return [pl.BlockSpec((tm, width), lambda i, *_: (jnp.minimum(i, n_prompt_tiles - 1), col)),
            pl.BlockSpec((tm, width), lambda i, *_: (jnp.maximum(i - n_prompt_tiles, 0), col))]


def _pick(n_prompt_tiles, p_ref, s_ref):
    return jnp.where(pl.program_id(0) < n_prompt_tiles, p_ref[...], s_ref[...])


def _mod_kernel(cond_ref, w_ref, b_ref, o_ref):
    s = cond_ref[...]
    s = s * jax.nn.sigmoid(s)
    o_ref[...] = jnp.dot(s.astype(BF16), w_ref[...].astype(BF16),
                         preferred_element_type=F32) + b_ref[...]


def _modulation(cond, mod_w, mod_b):
    b = mod_b.reshape(DEPTH, 6, 1, D)
    out = pl.pallas_call(
        _mod_kernel,
        grid=(DEPTH, 6),
        in_specs=[pl.BlockSpec((N_SEG, D), lambda l, j: (0, 0)),
                  pl.BlockSpec((None, D, D), lambda l, j: (l, 0, j)),
                  pl.BlockSpec((None, None, 1, D), lambda l, j: (l, j, 0, 0))],
        out_specs=pl.BlockSpec((None, None, N_SEG, D), lambda l, j: (l, j, 0, 0)),
        out_shape=jax.ShapeDtypeStruct((DEPTH, 6, N_SEG, D), F32),
        compiler_params=_cparams(2),
        name="modulation",
    )(cond, mod_w, b)
    return out.reshape(DEPTH, 6, N_SEG, 1, D)


def _head_sumsq(y, bd_ref):
    sq = y * y
    hi = sq.astype(BF16)
    lo = (sq - hi.astype(F32)).astype(BF16)
    bd = bd_ref[...]
    return (jnp.dot(hi, bd, preferred_element_type=F32)
            + jnp.dot(lo, bd, preferred_element_type=F32))


def _rope(y, cos, sin):
    lane = lax.broadcasted_iota(jnp.int32, (1, LANES), 1)
    first = (lane & 16) == 0
    partner = jnp.where(first, pltpu.roll(y, LANES - 16, 1), pltpu.roll(y, 16, 1))
    return y * cos + partner * sin


def _even_in_kernel(*refs, n_prompt_tiles, split_x):
    refs = list(refs)
    x = _pick(n_prompt_tiles, refs.pop(0), refs.pop(0)) if split_x else refs.pop(0)[...]
    (sh_ref, sc_ref, g_ref, w_ref, sguw_ref, sgub_ref, sgun_ref, qn_ref, kn_ref, bdq_ref, bdk_ref,
     cos_ref, sin_ref, a_ref, q_ref, kd_ref, vd_ref, kf_ref, vf_ref, wbf_ref) = refs
    tm = a_ref.shape[0]

    @pl.when(pl.program_id(0) == 0)
    def _():
        wbf_ref[...] = w_ref[...].astype(BF16)

    h = _mod_norm(x, g_ref[...], sh_ref[...], sc_ref[...]).astype(BF16)

    u = jax.nn.gelu(jnp.dot(h, wbf_ref[:, 0:A_WIDTH], preferred_element_type=F32), approximate=True)
    v = jax.nn.gelu(jnp.dot(h, wbf_ref[:, A_WIDTH:2 * A_WIDTH], preferred_element_type=F32),
                    approximate=True)
    mu = jnp.mean(v, axis=-1, keepdims=True)
    vc = v - mu
    var = jnp.mean(vc * vc, axis=-1, keepdims=True)
    vn = (vc * lax.rsqrt(var + EPS) * sgun_ref[...]).astype(BF16)
    n_chunks = tm // CHUNK
    gch = A_WIDTH // A_GROUPS
    for g in range(A_GROUPS):
        rhs = jnp.concatenate([vn[n * CHUNK:(n + 1) * CHUNK, g * gch:(g + 1) * gch]
                               for n in range(n_chunks)], axis=1)
        mixed = jnp.dot(sguw_ref[g].astype(BF16), rhs, preferred_element_type=F32)
        bias = sgub_ref[g]
        for n in range(n_chunks):
            blk = (mixed[:, n * gch:(n + 1) * gch] + bias) * u[n * CHUNK:(n + 1) * CHUNK,
                                                              g * gch:(g + 1) * gch]
            a_ref[n * CHUNK:(n + 1) * CHUNK, g * gch:(g + 1) * gch] = blk.astype(BF16)

    c0 = 2 * A_WIDTH
    qw = B_HEADS * HEAD_DIM
    q = jnp.dot(h, wbf_ref[:, c0:c0 + qw], preferred_element_type=F32)
    q = q * lax.rsqrt(_head_sumsq(q, bdq_ref) * (1.0 / HEAD_DIM) + EPS) * qn_ref[...]
    cos = cos_ref[...]
    sin = sin_ref[...]
    for j in range(qw // LANES):
        qj = _rope(q[:, j * LANES:(j + 1) * LANES], cos, sin) * QSCALE
        q_ref[:, j * LANES:(j + 1) * LANES] = qj.astype(BF16)

    kw = B_KV_HEADS * HEAD_DIM
    k = jnp.dot(h, wbf_ref[:, c0 + qw:c0 + qw + kw], preferred_element_type=F32)
    k = k * lax.rsqrt(_head_sumsq(k, bdk_ref) * (1.0 / HEAD_DIM) + EPS) * kn_ref[...]
    kf_ref[...] = k
    vals = jnp.dot(h, wbf_ref[:, c0 + qw + kw:c0 + qw + 2 * kw], preferred_element_type=F32)
    vf_ref[...] = vals
    kr = _rope(k, cos, sin)
    lane = lax.broadcasted_iota(jnp.int32, (1, LANES), 1)
    low = lane < HEAD_DIM
    for src, dst in ((kr, kd_ref), (vals, vd_ref)):
        sw = pltpu.roll(src, HEAD_DIM, 1)
        dst[:, 0:LANES] = jnp.where(low, src, sw).astype(BF16)
        dst[:, LANES:2 * LANES] = jnp.where(low, sw, src).astype(BF16)


def _even_in(xs, mods, layer, e, T, g_pre, w_in, sgu_w, sgu_b, sgu_norm, q_norm, k_norm,
             bdq, bdk, cos, sin):
    tm = TM_PROJ
    n_in = w_in.shape[2]
    tiles_per_seg = SEG // tm
    n_prompt_tiles = T // 2 // tm
    split_x = len(xs) == 2
    tab_spec = pl.BlockSpec((None, tm, LANES),
                            lambda i: (i // n_prompt_tiles, i % tiles_per_seg, 0))
    row = lambda w: pl.BlockSpec((tm, w), lambda i: (i, 0))
    x_specs = _split_specs(tm, D, n_prompt_tiles) if split_x else [row(D)]
    qw = B_HEADS * HEAD_DIM
    return pl.pallas_call(
        functools.partial(_even_in_kernel, n_prompt_tiles=n_prompt_tiles, split_x=split_x),
        grid=(T // tm,),
        in_specs=x_specs + [
            _mod_spec(layer, 0, tm), _mod_spec(layer, 1, tm), _const_spec((1, D), layer),
            _weight_spec((D, n_in), e), _const_spec((A_GROUPS, CHUNK, CHUNK), e),
            _const_spec((A_GROUPS, CHUNK, 1), e), _const_spec((1, A_WIDTH), e),
            _const_spec((1, qw), e), _const_spec((1, LANES), e),
            _const_spec((qw, qw)), _const_spec((LANES, LANES)), tab_spec, tab_spec],
        out_specs=[row(A_WIDTH), row(qw), row(2 * LANES), row(2 * LANES), row(LANES), row(LANES)],
        out_shape=[jax.ShapeDtypeStruct((T, A_WIDTH), BF16),
                   jax.ShapeDtypeStruct((T, qw), BF16),
                   jax.ShapeDtypeStruct((T, 2 * LANES), BF16),
                   jax.ShapeDtypeStruct((T, 2 * LANES), BF16),
                   jax.ShapeDtypeStruct((T, LANES), F32),
                   jax.ShapeDtypeStruct((T, LANES), F32)],
        scratch_shapes=[pltpu.VMEM((D, n_in), BF16)],
        compiler_params=_cparams(1),
        name="even_in_proj",
    )(*xs, mods, mods, g_pre, w_in, sgu_w, sgu_b, sgu_norm, q_norm, k_norm, bdq, bdk, cos, sin)


def _odd_in_kernel(x_ref, sh_ref, sc_ref, g_ref, w_ref, q_ref, k_ref, v_ref, kf_ref, vf_ref,
                   wbf_ref, *, n_prompt_tiles):
    i = pl.program_id(0)
    hw = C_HEADS * HEAD_DIM

    @pl.when(i == 0)
    def _():
        for c in range(3):
            wbf_ref[:, c * hw:(c + 1) * hw] = w_ref[:, c * hw:(c + 1) * hw].astype(BF16)

    h = _mod_norm(x_ref[...], g_ref[...], sh_ref[...], sc_ref[...]).astype(BF16)
    q = jnp.dot(h, wbf_ref[:, 0:hw], preferred_element_type=F32)
    q_ref[...] = (q * QSCALE).astype(BF16)
    k = jnp.dot(h, wbf_ref[:, hw:2 * hw], preferred_element_type=F32)
    k_ref[...] = k.astype(BF16)
    v = jnp.dot(h, wbf_ref[:, 2 * hw:3 * hw], preferred_element_type=F32)
    v_ref[...] = v.astype(BF16)

    @pl.when(i < n_prompt_tiles)
    def _():
        kf_ref[...] = k
        vf_ref[...] = v


def _odd_in(x, mods, layer, o, g_pre, w_in):
    T = x.shape[0]
    tm = TM_PROJ
    hw = C_HEADS * HEAD_DIM
    n_prompt_tiles = T // 2 // tm
    row = pl.BlockSpec((tm, hw), lambda i: (i, 0))
    prow = pl.BlockSpec((tm, hw), lambda i: (jnp.minimum(i, n_prompt_tiles - 1), 0))
    return pl.pallas_call(
        functools.partial(_odd_in_kernel, n_prompt_tiles=n_prompt_tiles),
        grid=(T // tm,),
        in_specs=[pl.BlockSpec((tm, D), lambda i: (i, 0)), _mod_spec(layer, 0, tm),
                  _mod_spec(layer, 1, tm), _const_spec((1, D), layer),
                  _weight_spec((D, 3 * hw), o)],
        out_specs=[row, row, row, prow, prow],
        out_shape=[jax.ShapeDtypeStruct((T, hw), BF16)] * 3
                  + [jax.ShapeDtypeStruct((T // 2, hw), F32)] * 2,
        scratch_shapes=[pltpu.VMEM((D, 3 * hw), BF16)],
        compiler_params=_cparams(1),
        name="odd_in_proj",
    )(x, mods, mods, g_pre, w_in)


def _softmax_pv(pieces, mh):
    m = None
    for s, _ in pieces:
        ms = jnp.max(s, axis=-1, keepdims=True)
        m = ms if m is None else jnp.maximum(m, ms)
    den = None
    acc = None
    for s, val in pieces:
        p = jnp.exp(s - m)
        ls = jnp.sum(p, axis=-1, keepdims=True)
        den = ls if den is None else den + ls
        o = jnp.dot(p.astype(BF16), jnp.where(mh, val, jnp.zeros_like(val)),
                    preferred_element_type=F32)
        acc = o if acc is None else acc + o
    return acc / den


def _attn_kernel(*refs, has_ctx, tq, pairs_per_kv):
    if has_ctx:
        q_ref, k_ref, v_ref, kc_ref, vc_ref, o_ref = refs
    else:
        q_ref, k_ref, v_ref, o_ref = refs
    lane = lax.broadcasted_iota(jnp.int32, (1, LANES), 1)
    masks = [lane < HEAD_DIM, lane >= HEAD_DIM]
    if has_ctx:
        kv_head = pl.program_id(1)
        sel = jnp.where(masks[0], 0, 1) == kv_head
        kc = jnp.where(sel, kc_ref[...], 0.0)
        kc = (kc + pltpu.roll(kc, HEAD_DIM, 1)).astype(BF16)
        vc = jnp.where(sel, vc_ref[...], 0.0)
        vc = (vc + pltpu.roll(vc, HEAD_DIM, 1)).astype(BF16)
    nt = (((1,), (1,)), ((), ()))
    for j in range(q_ref.shape[1] // LANES):
        cols = slice(j * LANES, (j + 1) * LANES)
        kcols = slice((j // pairs_per_kv) * LANES, (j // pairs_per_kv + 1) * LANES)
        k = k_ref[:, kcols]
        v = v_ref[:, kcols]
        for t in range(q_ref.shape[0] // tq):
            q = q_ref[t * tq:(t + 1) * tq, cols]
            out = None
            for mh in masks:
                qm = jnp.where(mh, q, jnp.zeros_like(q))
                pieces = [(lax.dot_general(qm, k, nt, preferred_element_type=F32), v)]
                if has_ctx:
                    pieces.append((lax.dot_general(qm, kc, nt, preferred_element_type=F32), vc))
                o = _softmax_pv(pieces, mh)
                out = o if out is None else out + o
            o_ref[t * tq:(t + 1) * tq, cols] = out.astype(BF16)


def _attention(q, k, v, *, n_batch, seq, row0, pairs_per_step, pairs_per_kv, ctx=None):
    blk0 = row0 // seq
    n_groups = q.shape[1] // (pairs_per_step * LANES)
    qw = pairs_per_step * LANES
    kw = qw // pairs_per_kv
    in_specs = [pl.BlockSpec((seq, qw), lambda b, g: (blk0 + b, g)),
                pl.BlockSpec((seq, kw), lambda b, g: (blk0 + b, g)),
                pl.BlockSpec((seq, kw), lambda b, g: (blk0 + b, g))]
    args = [q, k, v]
    if ctx is not None:
        assert kw == LANES
        kc, vc, e = ctx
        sc = kc.shape[2]
        cspec = pl.BlockSpec((None, None, sc, LANES), lambda b, g: (b, e, 0, 0))
        in_specs += [cspec, cspec]
        args += [kc, vc]
    return pl.pallas_call(
        functools.partial(_attn_kernel, has_ctx=ctx is not None, tq=min(TQ_ATTN, seq),
                          pairs_per_kv=pairs_per_kv),
        grid=(n_batch, n_groups),
        in_specs=in_specs,
        out_specs=pl.BlockSpec((seq, qw), lambda b, g: (b, g)),
        out_shape=jax.ShapeDtypeStruct((n_batch * seq, q.shape[1]), BF16),
        compiler_params=_cparams(2),
        name="attention",
    )(*args)


def _na_row_start(r):
    rows = SEG // GRID_W
    return min(max(r - NA_ROWS // 2, 0), rows - NA_ROWS)


def _na_windows():
    rows = SEG // GRID_W
    starts = []
    for qb in range(rows // NA_QROWS):
        lo = _na_row_start(qb * NA_QROWS)
        hi = _na_row_start(qb * NA_QROWS + NA_QROWS - 1) + NA_ROWS
        ws = min(lo, rows - NA_KROWS)
        assert ws <= lo and hi <= ws + NA_KROWS
        starts.append(ws)
    return starts


def _na_kernel(q_ref, k_ref, v_ref, kc_ref, vc_ref, r_ref, o_ref, bias_ref, tab_ref, *, starts):
    lane = lax.broadcasted_iota(jnp.int32, (1, LANES), 1)
    low = lane < HEAD_DIM

    @pl.when(pl.program_id(1) == 0)
    def _():
        qcol = lax.broadcasted_iota(jnp.int32, (GRID_W, LANES), 0)
        kcol = lax.broadcasted_iota(jnp.int32, (GRID_W, LANES), 1) & (GRID_W - 1)
        start = jnp.clip(qcol - NA_COLS // 2, 0, GRID_W - NA_COLS)
        inside = (kcol >= start) & (kcol < start + NA_COLS)
        for hh in range(LANES // HEAD_DIM):
            for d in range(2 * NA_ROWS - 1):
                base = jnp.broadcast_to(r_ref[hh, d:d + 1, :], (GRID_W, LANES))
                lo_t = pltpu.roll(base, LANES - (NA_COLS - 1), 1, stride=1, stride_axis=0)
                hi_t = pltpu.roll(base, GRID_W - (NA_COLS - 1), 1, stride=1, stride_axis=0)
                tab_ref[hh, d] = jnp.where(inside, jnp.where(low, lo_t, hi_t), NEG)
        neg = jnp.full((GRID_W, LANES), NEG, F32)
        for hh in range(LANES // HEAD_DIM):
            for qb, ws in enumerate(starts):
                for i in range(NA_QROWS):
                    r = qb * NA_QROWS + i
                    rs = _na_row_start(r)
                    for jp in range(NA_KROWS // 2):
                        kr = ws + 2 * jp
                        ok = [rs <= kr + d < rs + NA_ROWS for d in (0, 1)]
                        if not any(ok):
                            blk = neg
                        else:
                            t0 = tab_ref[hh, kr - r + NA_ROWS - 1] if ok[0] else neg
                            t1 = tab_ref[hh, kr + 1 - r + NA_ROWS - 1] if ok[1] else neg
                            blk = jnp.where(low, t0, t1)
                        bias_ref[hh, qb, i * GRID_W:(i + 1) * GRID_W,
                                 jp * LANES:(jp + 1) * LANES] = blk

    kc = kc_ref[...].astype(BF16)
    vc = vc_ref[...].astype(BF16)
    nt = (((1,), (1,)), ((), ()))
    nq = NA_QROWS * GRID_W
    nk = NA_KROWS * GRID_W
    for qb, ws in enumerate(starts):
        q = q_ref[qb * nq:(qb + 1) * nq, :]
        kw = k_ref[ws * GRID_W:ws * GRID_W + nk, :]
        vw = v_ref[ws * GRID_W:ws * GRID_W + nk, :]
        out = None
        for hh in range(LANES // HEAD_DIM):
            mh = low if hh == 0 else jnp.logical_not(low)
            qm = jnp.where(mh, q, jnp.zeros_like(q))
            s_win = lax.dot_general(qm, kw, nt, preferred_element_type=F32) + bias_ref[hh, qb]
            s_ctx = lax.dot_general(qm, kc, nt, preferred_element_type=F32)
            o = _softmax_pv([(s_win, vw), (s_ctx, vc)], mh)
            out = o if out is None else out + o
        o_ref[qb * nq:(qb + 1) * nq, :] = out.astype(BF16)


def _na_attention(q, k, v, kc, vc, o_idx, rpb, *, n_batch, row0):
    blk0 = row0 // SEG
    n_pairs = q.shape[1] // LANES
    sc = kc.shape[2]
    starts = _na_windows()
    heads = LANES // HEAD_DIM
    nr = 2 * NA_ROWS - 1
    qkv = pl.BlockSpec((SEG, LANES), lambda p, b: (blk0 + b, p))
    cspec = pl.BlockSpec((None, None, sc, LANES), lambda p, b: (b, o_idx, 0, p))
    rspec = pl.BlockSpec((None, heads, nr, LANES), lambda p, b: (o_idx, p, 0, 0))
    return pl.pallas_call(
        functools.partial(_na_kernel, starts=starts),
        grid=(n_pairs, n_batch),
        in_specs=[qkv, qkv, qkv, cspec, cspec, rspec],
        out_specs=pl.BlockSpec((SEG, LANES), lambda p, b: (b, p)),
        out_shape=jax.ShapeDtypeStruct((n_batch * SEG, n_pairs * LANES), BF16),
        scratch_shapes=[pltpu.VMEM((heads, len(starts), NA_QROWS * GRID_W, NA_KROWS * GRID_W), F32),
                        pltpu.VMEM((heads, nr, GRID_W, LANES), F32)],
        compiler_params=_cparams(2),
        name="na_attention",
    )(q, k, v, kc, vc, rpb)


def _out_kernel(*refs, n_prompt_tiles, split_x, has_a):
    refs = list(refs)
    x = _pick(n_prompt_tiles, refs.pop(0), refs.pop(0)) if split_x else refs.pop(0)[...]
    gt_ref, g_ref, fsh_ref, fsc_ref, gf_ref, w_ref = refs[:6]
    a_ref = refs[6] if has_a else None
    bp_ref, bs_ref, o_ref, h_ref, wbf_ref = refs[6 + has_a:]

    @pl.when(pl.program_id(0) == 0)
    def _():
        wbf_ref[...] = w_ref[...].astype(BF16)

    b = _pick(n_prompt_tiles, bp_ref, bs_ref)
    if has_a:
        half = a_ref.shape[1]
        y = (jnp.dot(a_ref[...], wbf_ref[0:half, :], preferred_element_type=F32)
             + jnp.dot(b, wbf_ref[half:2 * half, :], preferred_element_type=F32))
    else:
        y = jnp.dot(b, wbf_ref[...], preferred_element_type=F32)
    x1 = _gated_residual(x, y, g_ref[...], gt_ref[...])
    o_ref[...] = x1
    h_ref[...] = _mod_norm(x1, gf_ref[...], fsh_ref[...], fsc_ref[...]).astype(BF16)


def _out_proj(xs, a, bp, bs, mods, layer, li, T, g_post, g_ffn_pre, w_out):
    tm = TM_PROJ
    n_prompt_tiles = T // 2 // tm
    split_x = len(xs) == 2
    has_a = a is not None
    row = lambda w: pl.BlockSpec((tm, w), lambda i: (i, 0))
    x_specs = _split_specs(tm, D, n_prompt_tiles) if split_x else [row(D)]
    a_specs = [row(a.shape[1])] if has_a else []
    return pl.pallas_call(
        functools.partial(_out_kernel, n_prompt_tiles=n_prompt_tiles, split_x=split_x, has_a=has_a),
        grid=(T // tm,),
        in_specs=x_specs + [_mod_spec(layer, 2, tm), _const_spec((1, D), layer),
                            _mod_spec(layer, 3, tm), _mod_spec(layer, 4, tm),
                            _const_spec((1, D), layer), _weight_spec((D, D), li)] + a_specs
                 + _split_specs(tm, bp.shape[1], n_prompt_tiles),
        out_specs=[row(D), row(D)],
        out_shape=[jax.ShapeDtypeStruct((T, D), F32), jax.ShapeDtypeStruct((T, D), BF16)],
        scratch_shapes=[pltpu.VMEM((D, D), BF16)],
        compiler_params=_cparams(1),
        name="out_proj",
    )(*xs, mods, g_post, mods, mods, g_ffn_pre, w_out, *([a] if has_a else []), bp, bs)


def _ffn_kernel(*refs, n_prompt_tiles, split_out):
    x_ref, h_ref, gt_ref, gpost_ref, wg_ref, wu_ref, wo_ref = refs[:7]
    out_refs = refs[7:-1]
    acc_ref = refs[-1]
    i = pl.program_id(0)
    j = pl.program_id(1)

    @pl.when(j == 0)
    def _():
        acc_ref[...] = jnp.zeros_like(acc_ref)

    h = h_ref[...]
    g = jnp.dot(h, wg_ref[...].astype(BF16), preferred_element_type=F32)
    u = jnp.dot(h, wu_ref[...].astype(BF16), preferred_element_type=F32)
    act = (g * jax.nn.sigmoid(g) * u).astype(BF16)
    acc_ref[...] += jnp.dot(act, wo_ref[...].astype(BF16), preferred_element_type=F32)

    last = j == pl.num_programs(1) - 1

    def result():
        return _gated_residual(x_ref[...], acc_ref[...], gpost_ref[...], gt_ref[...])

    if split_out:
        @pl.when(last & (i < n_prompt_tiles))
        def _():
            out_refs[0][...] = result()

        @pl.when(last & (i >= n_prompt_tiles))
        def _():
            out_refs[1][...] = result()
    else:
        @pl.when(last)
        def _():
            out_refs[0][...] = result()


def _ffn(x, h, mods, layer, g_post, w_in, w_out, split_out):
    T = x.shape[0]
    tm, tf = TM_FFN, TF_FFN
    nf = D_FF // tf
    n_prompt_tiles = T // 2 // tm
    xrow = pl.BlockSpec((tm, D), lambda i, f: (i, 0))
    if split_out:
        out_specs = _split_specs(tm, D, n_prompt_tiles)
        out_shape = [jax.ShapeDtypeStruct((T // 2, D), F32)] * 2
    else:
        out_specs = [xrow]
        out_shape = [jax.ShapeDtypeStruct((T, D), F32)]
    return pl.pallas_call(
        functools.partial(_ffn_kernel, n_prompt_tiles=n_prompt_tiles, split_out=split_out),
        grid=(T // tm, nf),
        in_specs=[xrow, xrow, _mod_spec(layer, 5, tm), _const_spec((1, D), layer),
                  pl.BlockSpec((None, D, tf), lambda i, f: (layer, 0, f)),
                  pl.BlockSpec((None, D, tf), lambda i, f: (layer, 0, nf + f)),
                  pl.BlockSpec((None, tf, D), lambda i, f: (layer, f, 0))],
        out_specs=out_specs,
        out_shape=out_shape,
        scratch_shapes=[pltpu.VMEM((tm, D), F32)],
        compiler_params=_cparams(2),
        name="ffn",
    )(x, h, mods, g_post, w_in, w_in, w_out)


def _rope_tables():
    t = jnp.arange(SEG)
    nf = HEAD_DIM // 4
    freqs = ROPE_BASE ** (-jnp.arange(nf, dtype=F32) / nf)

    def cs(pos):
        ang = pos.astype(F32)[:, None] * freqs[None, :]
        return jnp.cos(ang), jnp.sin(ang)

    cr, sr = cs(t // GRID_W)
    cc, sn = cs(t % GRID_W)
    cos = jnp.concatenate([cr, cr, cc, cc], axis=1)
    sin = jnp.concatenate([-sr, sr, -sn, sn], axis=1)
    reps = LANES // HEAD_DIM
    cos = jnp.tile(cos, (1, reps))
    sin = jnp.tile(sin, (1, reps))
    return (jnp.stack([jnp.ones_like(cos), cos]), jnp.stack([jnp.zeros_like(sin), sin]))


def _block_diag_ones(n):
    i = jnp.arange(n) // HEAD_DIM
    return (i[:, None] == i[None, :]).astype(BF16)


def kernel(x_prompt, x_sample, cache_attn_k, cache_attn_v, cache_na_k, cache_na_v, c, c_ctx,
           mod_w, mod_b, norm_mix_pre, norm_mix_post, norm_ffn_pre, norm_ffn_post,
           even_w_in, even_w_out, sgu_w, sgu_b, sgu_norm, q_norm, k_norm,
           odd_w_in, odd_w_out, na_rpb, ffn_w_in, ffn_w_out):
    nb_p, seq_p, _ = x_prompt.shape
    nb_s, seq_s, _ = x_sample.shape
    tp = nb_p * seq_p
    T = 2 * tp
    assert seq_s == SEG and tp == nb_s * seq_s and tp % SEG == 0

    cond = jnp.concatenate([jnp.broadcast_to(c_ctx[None, :], (tp // SEG, D)), c], axis=0)
    mods = _modulation(cond, mod_w, mod_b)

    cos, sin = _rope_tables()
    qw = B_HEADS * HEAD_DIM
    bdq = _block_diag_ones(qw)
    bdk = _block_diag_ones(LANES)
    past = cache_attn_k.shape[2]
    ctx_ak = cache_attn_k.reshape(nb_s, -1, past, B_KV_HEADS * HEAD_DIM)
    ctx_av = cache_attn_v.reshape(nb_s, -1, past, B_KV_HEADS * HEAD_DIM)
    ctx_nk = cache_na_k.reshape(nb_s, -1, past, C_HEADS * HEAD_DIM)
    ctx_nv = cache_na_v.reshape(nb_s, -1, past, C_HEADS * HEAD_DIM)
    g_mix_pre, g_mix_post, g_ffn_pre, g_ffn_post = (
        a.reshape(DEPTH, 1, D) for a in (norm_mix_pre, norm_mix_post, norm_ffn_pre, norm_ffn_post))
    n_even = even_w_in.shape[0]
    sgu_b3 = sgu_b.reshape(n_even, A_GROUPS, CHUNK, 1)
    sgu_n3 = sgu_norm.reshape(n_even, 1, A_WIDTH)
    qn3 = jnp.tile(q_norm, (1, B_HEADS)).reshape(n_even, 1, qw)
    kn3 = jnp.tile(k_norm, (1, B_KV_HEADS)).reshape(n_even, 1, LANES)
    rpb = jnp.pad(na_rpb, ((0, 0), (0, 0), (0, 0), (0, LANES - na_rpb.shape[3])))

    xs = (x_prompt.reshape(tp, D), x_sample.reshape(nb_s * seq_s, D))
    attn_k, attn_v, na_k, na_v = [], [], [], []
    for l in range(DEPTH):
        if l % 2 == 0:
            e = l // 2
            a_out, q, kd, vd, kf, vf = _even_in(xs, mods, l, e, T, g_mix_pre, even_w_in, sgu_w,
                                                sgu_b3, sgu_n3, qn3, kn3, bdq, bdk, cos, sin)
            n_pairs = qw // LANES
            pairs_per_kv = n_pairs // B_KV_HEADS
            mix_p = _attention(q, kd, vd, n_batch=nb_p, seq=seq_p, row0=0,
                               pairs_per_step=n_pairs, pairs_per_kv=pairs_per_kv)
            mix_s = _attention(q, kd, vd, n_batch=nb_s, seq=seq_s, row0=tp,
                               pairs_per_step=pairs_per_kv, pairs_per_kv=pairs_per_kv,
                               ctx=(ctx_ak, ctx_av, e))
            attn_k.append(kf[:tp].reshape(nb_p, seq_p, B_KV_HEADS, HEAD_DIM))
            attn_v.append(vf[:tp].reshape(nb_p, seq_p, B_KV_HEADS, HEAD_DIM))
            w_out, li = even_w_out, e
        else:
            o = l // 2
            a_out = None
            q, k, v, kf, vf = _odd_in(xs[0], mods, l, o, g_mix_pre, odd_w_in)
            mix_p = _attention(q, k, v, n_batch=nb_p, seq=seq_p, row0=0,
                               pairs_per_step=PAIRS_PER_STEP, pairs_per_kv=1)
            mix_s = _na_attention(q, k, v, ctx_nk, ctx_nv, o, rpb, n_batch=nb_s, row0=tp)
            na_k.append(kf.reshape(nb_p, seq_p, C_HEADS, HEAD_DIM))
            na_v.append(vf.reshape(nb_p, seq_p, C_HEADS, HEAD_DIM))
            w_out, li = odd_w_out, o
        x, h = _out_proj(xs, a_out, mix_p, mix_s, mods, l, li, T, g_mix_post, g_ffn_pre, w_out)
        xs = tuple(_ffn(x, h, mods, l, g_ffn_post, ffn_w_in, ffn_w_out,
                        split_out=(l == DEPTH - 1)))

    y_prompt = xs[0].reshape(nb_p, seq_p, D)
    y_sample = xs[1].reshape(nb_s, seq_s, D)
    return (y_prompt, y_sample, jnp.stack(attn_k, axis=1), jnp.stack(attn_v, axis=1),
            jnp.stack(na_k, axis=1), jnp.stack(na_v, axis=1))
```

```python
import functools

import jax
import jax.numpy as jnp
from jax import lax
from jax.experimental import pallas as pl
from jax.experimental.pallas import tpu as pltpu

F32 = jnp.float32
BF16 = jnp.bfloat16

D = 1024
DEPTH = 4
HEAD_DIM = 64
GRID_W = 64
CHUNK = 128
A_WIDTH = D // 2
A_GROUPS = 4
B_HEADS = 8
B_KV_HEADS = 2
C_HEADS = 16
NA_ROWS = 8
NA_COLS = 16
D_FF = 2816
ROPE_BASE = 10000.0
EPS = 1e-6
NEG = -1e30
SEG = 1024
N_SEG = 8
LANES = 128
QSCALE = HEAD_DIM ** -0.5

TM_PROJ = 512
TM_FFN = 1024
TF_FFN = 256
TQ_ATTN = 256
PAIRS_PER_STEP = 4
NA_QROWS = 4
NA_KROWS = 12
VMEM_LIMIT = 56 * 1024 * 1024


def _cparams(n_axes):
    return pltpu.CompilerParams(dimension_semantics=("arbitrary",) * n_axes,
                                vmem_limit_bytes=VMEM_LIMIT)


def _rms(x):
    return x * lax.rsqrt(jnp.mean(x * x, axis=-1, keepdims=True) + EPS)


def _mod_norm(x, g, shift, scale):
    return _rms(x) * (g * (1.0 + scale)) + shift


def _gated_residual(x, y, g, gate):
    return x + _rms(y) * (gate * g)


def _const_spec(shape, *lead):
    block = (None,) * len(lead) + tuple(shape)
    return pl.BlockSpec(block, lambda *_: tuple(lead) + (0,) * len(shape))


def _weight_spec(shape, *lead):
    block = (None,) * len(lead) + tuple(shape)
    return pl.BlockSpec(block, lambda *_: tuple(lead) + (0,) * len(shape),
                        pipeline_mode=pl.Buffered(1))


def _mod_spec(layer, j, tm):
    return pl.BlockSpec((None, None, None, 1, D),
                        lambda i, *_: (layer, j, (i * tm) // SEG, 0, 0))


def _split_specs(tm, width, n_prompt_tiles, col=0):
    return [pl.BlockSpec((tm, width), lambda i, *_: (jnp.minimum(i, n_prompt_tiles - 1), col)),
            pl.BlockSpec((tm, width), lambda i, *_: (jnp.maximum(i - n_prompt_tiles, 0), col))]


def _pick(n_prompt_tiles, p_ref, s_ref):
    return jnp.where(pl.program_id(0) < n_prompt_tiles, p_ref[...], s_ref[...])


def _mod_kernel(cond_ref, w_ref, b_ref, o_ref):
    s = cond_ref[...]
    s = s * jax.nn.sigmoid(s)
    o_ref[...] = jnp.dot(s.astype(BF16), w_ref[...].astype(BF16),
                         preferred_element_type=F32) + b_ref[...]


def _modulation(cond, mod_w, mod_b):
    b = mod_b.reshape(DEPTH, 6, 1, D)
    out = pl.pallas_call(
        _mod_kernel,
        grid=(DEPTH, 6),
        in_specs=[pl.BlockSpec((N_SEG, D), lambda l, j: (0, 0)),
                  pl.BlockSpec((None, D, D), lambda l, j: (l, 0, j)),
                  pl.BlockSpec((None, None, 1, D), lambda l, j: (l, j, 0, 0))],
        out_specs=pl.BlockSpec((None, None, N_SEG, D), lambda l, j: (l, j, 0, 0)),
        out_shape=jax.ShapeDtypeStruct((DEPTH, 6, N_SEG, D), F32),
        compiler_params=_cparams(2),
        name="modulation",
    )(cond, mod_w, b)
    return out.reshape(DEPTH, 6, N_SEG, 1, D)


def _head_sumsq(y, bd_ref):
    sq = y * y
    hi = sq.astype(BF16)
    lo = (sq - hi.astype(F32)).astype(BF16)
    bd = bd_ref[...]
    return (jnp.dot(hi, bd, preferred_element_type=F32)
            + jnp.dot(lo, bd, preferred_element_type=F32))


def _rope(y, cos, sin):
    lane = lax.broadcasted_iota(jnp.int32, (1, LANES), 1)
    first = (lane & 16) == 0
    partner = jnp.where(first, pltpu.roll(y, LANES - 16, 1), pltpu.roll(y, 16, 1))
    return y * cos + partner * sin


def _even_in_kernel(*refs, n_prompt_tiles, split_x):
    refs = list(refs)
    x = _pick(n_prompt_tiles, refs.pop(0), refs.pop(0)) if split_x else refs.pop(0)[...]
    (sh_ref, sc_ref, g_ref, w_ref, sguw_ref, sgub_ref, sgun_ref, qn_ref, kn_ref, bdq_ref, bdk_ref,
     cos_ref, sin_ref, a_ref, q_ref, kd_ref, vd_ref, kf_ref, vf_ref, wbf_ref) = refs
    tm = a_ref.shape[0]

    @pl.when(pl.program_id(0) == 0)
    def _():
        wbf_ref[...] = w_ref[...].astype(BF16)

    h = _mod_norm(x, g_ref[...], sh_ref[...], sc_ref[...]).astype(BF16)

    u = jax.nn.gelu(jnp.dot(h, wbf_ref[:, 0:A_WIDTH], preferred_element_type=F32), approximate=True)
    v = jax.nn.gelu(jnp.dot(h, wbf_ref[:, A_WIDTH:2 * A_WIDTH], preferred_element_type=F32),
                    approximate=True)
    mu = jnp.mean(v, axis=-1, keepdims=True)
    vc = v - mu
    var = jnp.mean(vc * vc, axis=-1, keepdims=True)
    vn = (vc * lax.rsqrt(var + EPS) * sgun_ref[...]).astype(BF16)
    n_chunks = tm // CHUNK
    gch = A_WIDTH // A_GROUPS
    for g in range(A_GROUPS):
        rhs = jnp.concatenate([vn[n * CHUNK:(n + 1) * CHUNK, g * gch:(g + 1) * gch]
                               for n in range(n_chunks)], axis=1)
        mixed = jnp.dot(sguw_ref[g].astype(BF16), rhs, preferred_element_type=F32)
        bias = sgub_ref[g]
        for n in range(n_chunks):
            blk = (mixed[:, n * gch:(n + 1) * gch] + bias) * u[n * CHUNK:(n + 1) * CHUNK,
                                                              g * gch:(g + 1) * gch]
            a_ref[n * CHUNK:(n + 1) * CHUNK, g * gch:(g + 1) * gch] = blk.astype(BF16)

    c0 = 2 * A_WIDTH
    qw = B_HEADS * HEAD_DIM
    q = jnp.dot(h, wbf_ref[:, c0:c0 + qw], preferred_element_type=F32)
    q = q * lax.rsqrt(_head_sumsq(q, bdq_ref) * (1.0 / HEAD_DIM) + EPS) * qn_ref[...]
    cos = cos_ref[...]
    sin = sin_ref[...]
    for j in range(qw // LANES):
        qj = _rope(q[:, j * LANES:(j + 1) * LANES], cos, sin) * QSCALE
        q_ref[:, j * LANES:(j + 1) * LANES] = qj.astype(BF16)

    kw = B_KV_HEADS * HEAD_DIM
    k = jnp.dot(h, wbf_ref[:, c0 + qw:c0 + qw + kw], preferred_element_type=F32)
    k = k * lax.rsqrt(_head_sumsq(k, bdk_ref) * (1.0 / HEAD_DIM) + EPS) * kn_ref[...]
    kf_ref[...] = k
    vals = jnp.dot(h, wbf_ref[:, c0 + qw + kw:c0 + qw + 2 * kw], preferred_element_type=F32)
    vf_ref[...] = vals
    kr = _rope(k, cos, sin)
    lane = lax.broadcasted_iota(jnp.int32, (1, LANES), 1)
    low = lane < HEAD_DIM
    for src, dst in ((kr, kd_ref), (vals, vd_ref)):
        sw = pltpu.roll(src, HEAD_DIM, 1)
        dst[:, 0:LANES] = jnp.where(low, src, sw).astype(BF16)
        dst[:, LANES:2 * LANES] = jnp.where(low, sw, src).astype(BF16)


def _even_in(xs, mods, layer, e, T, g_pre, w_in, sgu_w, sgu_b, sgu_norm, q_norm, k_norm,
             bdq, bdk, cos, sin):
    tm = TM_PROJ
    n_in = w_in.shape[2]
    tiles_per_seg = SEG // tm
    n_prompt_tiles = T // 2 // tm
    split_x = len(xs) == 2
    tab_spec = pl.BlockSpec((None, tm, LANES),
                            lambda i: (i // n_prompt_tiles, i % tiles_per_seg, 0))
    row = lambda w: pl.BlockSpec((tm, w), lambda i: (i, 0))
    x_specs = _split_specs(tm, D, n_prompt_tiles) if split_x else [row(D)]
    qw = B_HEADS * HEAD_DIM
    return pl.pallas_call(
        functools.partial(_even_in_kernel, n_prompt_tiles=n_prompt_tiles, split_x=split_x),
        grid=(T // tm,),
        in_specs=x_specs + [
            _mod_spec(layer, 0, tm), _mod_spec(layer, 1, tm), _const_spec((1, D), layer),
            _weight_spec((D, n_in), e), _const_spec((A_GROUPS, CHUNK, CHUNK), e),
            _const_spec((A_GROUPS, CHUNK, 1), e), _const_spec((1, A_WIDTH), e),
            _const_spec((1, qw), e), _const_spec((1, LANES), e),
            _const_spec((qw, qw)), _const_spec((LANES, LANES)), tab_spec, tab_spec],
        out_specs=[row(A_WIDTH), row(qw), row(2 * LANES), row(2 * LANES), row(LANES), row(LANES)],
        out_shape=[jax.ShapeDtypeStruct((T, A_WIDTH), BF16),
                   jax.ShapeDtypeStruct((T, qw), BF16),
                   jax.ShapeDtypeStruct((T, 2 * LANES), BF16),
                   jax.ShapeDtypeStruct((T, 2 * LANES), BF16),
                   jax.ShapeDtypeStruct((T, LANES), F32),
                   jax.ShapeDtypeStruct((T, LANES), F32)],
        scratch_shapes=[pltpu.VMEM((D, n_in), BF16)],
        compiler_params=_cparams(1),
        name="even_in_proj",
    )(*xs, mods, mods, g_pre, w_in, sgu_w, sgu_b, sgu_norm, q_norm, k_norm, bdq, bdk, cos, sin)


def _odd_in_kernel(x_ref, sh_ref, sc_ref, g_ref, w_ref, q_ref, k_ref, v_ref, kf_ref, vf_ref,
                   wbf_ref, *, n_prompt_tiles):
    i = pl.program_id(0)
    hw = C_HEADS * HEAD_DIM

    @pl.when(i == 0)
    def _():
        for c in range(3):
            wbf_ref[:, c * hw:(c + 1) * hw] = w_ref[:, c * hw:(c + 1) * hw].astype(BF16)

    h = _mod_norm(x_ref[...], g_ref[...], sh_ref[...], sc_ref[...]).astype(BF16)
    q = jnp.dot(h, wbf_ref[:, 0:hw], preferred_element_type=F32)
    q_ref[...] = (q * QSCALE).astype(BF16)
    k = jnp.dot(h, wbf_ref[:, hw:2 * hw], preferred_element_type=F32)
    k_ref[...] = k.astype(BF16)
    v = jnp.dot(h, wbf_ref[:, 2 * hw:3 * hw], preferred_element_type=F32)
    v_ref[...] = v.astype(BF16)

    @pl.when(i < n_prompt_tiles)
    def _():
        kf_ref[...] = k
        vf_ref[...] = v


def _odd_in(x, mods, layer, o, g_pre, w_in):
    T = x.shape[0]
    tm = TM_PROJ
    hw = C_HEADS * HEAD_DIM
    n_prompt_tiles = T // 2 // tm
    row = pl.BlockSpec((tm, hw), lambda i: (i, 0))
    prow = pl.BlockSpec((tm, hw), lambda i: (jnp.minimum(i, n_prompt_tiles - 1), 0))
    return pl.pallas_call(
        functools.partial(_odd_in_kernel, n_prompt_tiles=n_prompt_tiles),
        grid=(T // tm,),
        in_specs=[pl.BlockSpec((tm, D), lambda i: (i, 0)), _mod_spec(layer, 0, tm),
                  _mod_spec(layer, 1, tm), _const_spec((1, D), layer),
                  _weight_spec((D, 3 * hw), o)],
        out_specs=[row, row, row, prow, prow],
        out_shape=[jax.ShapeDtypeStruct((T, hw), BF16)] * 3
                  + [jax.ShapeDtypeStruct((T // 2, hw), F32)] * 2,
        scratch_shapes=[pltpu.VMEM((D, 3 * hw), BF16)],
        compiler_params=_cparams(1),
        name="odd_in_proj",
    )(x, mods, mods, g_pre, w_in)


def _softmax_pv(pieces, mh):
    m = None
    for s, _ in pieces:
        ms = jnp.max(s, axis=-1, keepdims=True)
        m = ms if m is None else jnp.maximum(m, ms)
    den = None
    acc = None
    for s, val in pieces:
        p = jnp.exp(s - m)
        ls = jnp.sum(p, axis=-1, keepdims=True)
        den = ls if den is None else den + ls
        o = jnp.dot(p.astype(BF16), jnp.where(mh, val, jnp.zeros_like(val)),
                    preferred_element_type=F32)
        acc = o if acc is None else acc + o
    return acc / den


def _attn_kernel(*refs, has_ctx, tq, pairs_per_kv):
    if has_ctx:
        q_ref, k_ref, v_ref, kc_ref, vc_ref, o_ref = refs
    else:
        q_ref, k_ref, v_ref, o_ref = refs
    lane = lax.broadcasted_iota(jnp.int32, (1, LANES), 1)
    masks = [lane < HEAD_DIM, lane >= HEAD_DIM]
    if has_ctx:
        kv_head = pl.program_id(1)
        sel = jnp.where(masks[0], 0, 1) == kv_head
        kc = jnp.where(sel, kc_ref[...], 0.0)
        kc = (kc + pltpu.roll(kc, HEAD_DIM, 1)).astype(BF16)
        vc = jnp.where(sel, vc_ref[...], 0.0)
        vc = (vc + pltpu.roll(vc, HEAD_DIM, 1)).astype(BF16)
    nt = (((1,), (1,)), ((), ()))
    for j in range(q_ref.shape[1] // LANES):
        cols = slice(j * LANES, (j + 1) * LANES)
        kcols = slice((j // pairs_per_kv) * LANES, (j // pairs_per_kv + 1) * LANES)
        k = k_ref[:, kcols]
        v = v_ref[:, kcols]
        for t in range(q_ref.shape[0] // tq):
            q = q_ref[t * tq:(t + 1) * tq, cols]
            out = None
            for mh in masks:
                qm = jnp.where(mh, q, jnp.zeros_like(q))
                pieces = [(lax.dot_general(qm, k, nt, preferred_element_type=F32), v)]
                if has_ctx:
                    pieces.append((lax.dot_general(qm, kc, nt, preferred_element_type=F32), vc))
                o = _softmax_pv(pieces, mh)
                out = o if out is None else out + o
            o_ref[t * tq:(t + 1) * tq, cols] = out.astype(BF16)


def _attention(q, k, v, *, n_batch, seq, row0, pairs_per_step, pairs_per_kv, ctx=None):
    blk0 = row0 // seq
    n_groups = q.shape[1] // (pairs_per_step * LANES)
    qw = pairs_per_step * LANES
    kw = qw // pairs_per_kv
    in_specs = [pl.BlockSpec((seq, qw), lambda b, g: (blk0 + b, g)),
                pl.BlockSpec((seq, kw), lambda b, g: (blk0 + b, g)),
                pl.BlockSpec((seq, kw), lambda b, g: (blk0 + b, g))]
    args = [q, k, v]
    if ctx is not None:
        assert kw == LANES
        kc, vc, e = ctx
        sc = kc.shape[2]
        cspec = pl.BlockSpec((None, None, sc, LANES), lambda b, g: (b, e, 0, 0))
        in_specs += [cspec, cspec]
        args += [kc, vc]
    return pl.pallas_call(
        functools.partial(_attn_kernel, has_ctx=ctx is not None, tq=min(TQ_ATTN, seq),
                          pairs_per_kv=pairs_per_kv),
        grid=(n_batch, n_groups),
        in_specs=in_specs,
        out_specs=pl.BlockSpec((seq, qw), lambda b, g: (b, g)),
        out_shape=jax.ShapeDtypeStruct((n_batch * seq, q.shape[1]), BF16),
        compiler_params=_cparams(2),
        name="attention",
    )(*args)


def _na_row_start(r):
    rows = SEG // GRID_W
    return min(max(r - NA_ROWS // 2, 0), rows - NA_ROWS)


def _na_windows():
    rows = SEG // GRID_W
    starts = []
    for qb in range(rows // NA_QROWS):
        lo = _na_row_start(qb * NA_QROWS)
        hi = _na_row_start(qb * NA_QROWS + NA_QROWS - 1) + NA_ROWS
        ws = min(lo, rows - NA_KROWS)
        assert ws <= lo and hi <= ws + NA_KROWS
        starts.append(ws)
    return starts


def _na_kernel(q_ref, k_ref, v_ref, kc_ref, vc_ref, r_ref, o_ref, bias_ref, tab_ref, *, starts):
    lane = lax.broadcasted_iota(jnp.int32, (1, LANES), 1)
    low = lane < HEAD_DIM

    @pl.when(pl.program_id(1) == 0)
    def _():
        qcol = lax.broadcasted_iota(jnp.int32, (GRID_W, LANES), 0)
        kcol = lax.broadcasted_iota(jnp.int32, (GRID_W, LANES), 1) & (GRID_W - 1)
        start = jnp.clip(qcol - NA_COLS // 2, 0, GRID_W - NA_COLS)
        inside = (kcol >= start) & (kcol < start + NA_COLS)
        for hh in range(LANES // HEAD_DIM):
            for d in range(2 * NA_ROWS - 1):
                base = jnp.broadcast_to(r_ref[hh, d:d + 1, :], (GRID_W, LANES))
                lo_t = pltpu.roll(base, LANES - (NA_COLS - 1), 1, stride=1, stride_axis=0)
                hi_t = pltpu.roll(base, GRID_W - (NA_COLS - 1), 1, stride=1, stride_axis=0)
                tab_ref[hh, d] = jnp.where(inside, jnp.where(low, lo_t, hi_t), NEG)
        neg = jnp.full((GRID_W, LANES), NEG, F32)
        for hh in range(LANES // HEAD_DIM):
            for qb, ws in enumerate(starts):
                for i in range(NA_QROWS):
                    r = qb * NA_QROWS + i
                    rs = _na_row_start(r)
                    for jp in range(NA_KROWS // 2):
                        kr = ws + 2 * jp
                        ok = [rs <= kr + d < rs + NA_ROWS for d in (0, 1)]
                        if not any(ok):
                            blk = neg
                        else:
                            t0 = tab_ref[hh, kr - r + NA_ROWS - 1] if ok[0] else neg
                            t1 = tab_ref[hh, kr + 1 - r + NA_ROWS - 1] if ok[1] else neg
                            blk = jnp.where(low, t0, t1)
                        bias_ref[hh, qb, i * GRID_W:(i + 1) * GRID_W,
                                 jp * LANES:(jp + 1) * LANES] = blk

    kc = kc_ref[...].astype(BF16)
    vc = vc_ref[...].astype(BF16)
    nt = (((1,), (1,)), ((), ()))
    nq = NA_QROWS * GRID_W
    nk = NA_KROWS * GRID_W
    for qb, ws in enumerate(starts):
        q = q_ref[qb * nq:(qb + 1) * nq, :]
        kw = k_ref[ws * GRID_W:ws * GRID_W + nk, :]
        vw = v_ref[ws * GRID_W:ws * GRID_W + nk, :]
        out = None
        for hh in range(LANES // HEAD_DIM):
            mh = low if hh == 0 else jnp.logical_not(low)
            qm = jnp.where(mh, q, jnp.zeros_like(q))
            s_win = lax.dot_general(qm, kw, nt, preferred_element_type=F32) + bias_ref[hh, qb]
            s_ctx = lax.dot_general(qm, kc, nt, preferred_element_type=F32)
            o = _softmax_pv([(s_win, vw), (s_ctx, vc)], mh)
            out = o if out is None else out + o
        o_ref[qb * nq:(qb + 1) * nq, :] = out.astype(BF16)


def _na_attention(q, k, v, kc, vc, o_idx, rpb, *, n_batch, row0):
    blk0 = row0 // SEG
    n_pairs = q.shape[1] // LANES
    sc = kc.shape[2]
    starts = _na_windows()
    heads = LANES // HEAD_DIM
    nr = 2 * NA_ROWS - 1
    qkv = pl.BlockSpec((SEG, LANES), lambda p, b: (blk0 + b, p))
    cspec = pl.BlockSpec((None, None, sc, LANES), lambda p, b: (b, o_idx, 0, p))
    rspec = pl.BlockSpec((None, heads, nr, LANES), lambda p, b: (o_idx, p, 0, 0))
    return pl.pallas_call(
        functools.partial(_na_kernel, starts=starts),
        grid=(n_pairs, n_batch),
        in_specs=[qkv, qkv, qkv, cspec, cspec, rspec],
        out_specs=pl.BlockSpec((SEG, LANES), lambda p, b: (b, p)),
        out_shape=jax.ShapeDtypeStruct((n_batch * SEG, n_pairs * LANES), BF16),
        scratch_shapes=[pltpu.VMEM((heads, len(starts), NA_QROWS * GRID_W, NA_KROWS * GRID_W), F32),
                        pltpu.VMEM((heads, nr, GRID_W, LANES), F32)],
        compiler_params=_cparams(2),
        name="na_attention",
    )(q, k, v, kc, vc, rpb)


def _out_kernel(*refs, n_prompt_tiles, split_x, has_a):
    refs = list(refs)
    x = _pick(n_prompt_tiles, refs.pop(0), refs.pop(0)) if split_x else refs.pop(0)[...]
    gt_ref, g_ref, fsh_ref, fsc_ref, gf_ref, w_ref = refs[:6]
    a_ref = refs[6] if has_a else None
    bp_ref, bs_ref, o_ref, h_ref, wbf_ref = refs[6 + has_a:]

    @pl.when(pl.program_id(0) == 0)
    def _():
        wbf_ref[...] = w_ref[...].astype(BF16)

    b = _pick(n_prompt_tiles, bp_ref, bs_ref)
    if has_a:
        half = a_ref.shape[1]
        y = (jnp.dot(a_ref[...], wbf_ref[0:half, :], preferred_element_type=F32)
             + jnp.dot(b, wbf_ref[half:2 * half, :], preferred_element_type=F32))
    else:
        y = jnp.dot(b, wbf_ref[...], preferred_element_type=F32)
    x1 = _gated_residual(x, y, g_ref[...], gt_ref[...])
    o_ref[...] = x1
    h_ref[...] = _mod_norm(x1, gf_ref[...], fsh_ref[...], fsc_ref[...]).astype(BF16)


def _out_proj(xs, a, bp, bs, mods, layer, li, T, g_post, g_ffn_pre, w_out):
    tm = TM_PROJ
    n_prompt_tiles = T // 2 // tm
    split_x = len(xs) == 2
    has_a = a is not None
    row = lambda w: pl.BlockSpec((tm, w), lambda i: (i, 0))
    x_specs = _split_specs(tm, D, n_prompt_tiles) if split_x else [row(D)]
    a_specs = [row(a.shape[1])] if has_a else []
    return pl.pallas_call(
        functools.partial(_out_kernel, n_prompt_tiles=n_prompt_tiles, split_x=split_x, has_a=has_a),
        grid=(T // tm,),
        in_specs=x_specs + [_mod_spec(layer, 2, tm), _const_spec((1, D), layer),
                            _mod_spec(layer, 3, tm), _mod_spec(layer, 4, tm),
                            _const_spec((1, D), layer), _weight_spec((D, D), li)] + a_specs
                 + _split_specs(tm, bp.shape[1], n_prompt_tiles),
        out_specs=[row(D), row(D)],
        out_shape=[jax.ShapeDtypeStruct((T, D), F32), jax.ShapeDtypeStruct((T, D), BF16)],
        scratch_shapes=[pltpu.VMEM((D, D), BF16)],
        compiler_params=_cparams(1),
        name="out_proj",
    )(*xs, mods, g_post, mods, mods, g_ffn_pre, w_out, *([a] if has_a else []), bp, bs)


def _ffn_kernel(x_ref, h_ref, gt_ref, gpost_ref, win_ref, wout_ref, *rest,
                layer, n_prompt_tiles, split_out):
    out_refs = rest[:-6]
    wg_buf, wu_buf, wo_buf, act_buf, acc_ref, sem = rest[-6:]
    tf = wg_buf.shape[2]
    nf = D_FF // tf

    def aligned(f):
        return f * tf if isinstance(f, int) else pl.multiple_of(f * tf, tf)

    def in_copies(f, slot):
        col = aligned(f)
        return (pltpu.make_async_copy(win_ref.at[layer, :, pl.ds(col, tf)], wg_buf.at[slot],
                                      sem.at[0, slot]),
                pltpu.make_async_copy(win_ref.at[layer, :, pl.ds(D_FF + col, tf)], wu_buf.at[slot],
                                      sem.at[1, slot]))

    def out_copy(f, slot):
        row = aligned(f)
        return pltpu.make_async_copy(wout_ref.at[layer, pl.ds(row, tf), :], wo_buf.at[slot],
                                     sem.at[2, slot])

    def hidden(slot):
        h = h_ref[...]
        g = jnp.dot(h, wg_buf[slot].astype(BF16), preferred_element_type=F32)
        u = jnp.dot(h, wu_buf[slot].astype(BF16), preferred_element_type=F32)
        act_buf[slot] = (g * jax.nn.sigmoid(g) * u).astype(BF16)

    def project(slot):
        acc_ref[...] += jnp.dot(act_buf[slot], wo_buf[slot].astype(BF16),
                                preferred_element_type=F32)

    def step(f, slot):
        for c in in_copies(f, slot):
            c.wait()
        out_copy(f - 1, 1 - slot).wait()

        @pl.when(f + 1 < nf)
        def _():
            for c in in_copies(f + 1, 1 - slot):
                c.start()

        out_copy(f, slot).start()
        hidden(slot)
        project(1 - slot)

    for c in in_copies(0, 0):
        c.start()
    out_copy(0, 0).start()
    for c in in_copies(1, 1):
        c.start()
    acc_ref[...] = jnp.zeros_like(acc_ref)
    for c in in_copies(0, 0):
        c.wait()
    hidden(0)

    assert nf % 2 == 1

    def pair(k, carry):
        step(2 * k + 1, 1)
        step(2 * k + 2, 0)
        return carry

    lax.fori_loop(0, (nf - 1) // 2, pair, 0)
    out_copy(nf - 1, 0).wait()
    project(0)

    def result():
        return _gated_residual(x_ref[...], acc_ref[...], gpost_ref[...], gt_ref[...])

    if split_out:
        i = pl.program_id(0)

        @pl.when(i < n_prompt_tiles)
        def _():
            out_refs[0][...] = result()

        @pl.when(i >= n_prompt_tiles)
        def _():
            out_refs[1][...] = result()
    else:
        out_refs[0][...] = result()


def _ffn(x, h, mods, layer, g_post, w_in, w_out, split_out):
    T = x.shape[0]
    tm, tf = TM_FFN, TF_FFN
    n_prompt_tiles = T // 2 // tm
    xrow = pl.BlockSpec((tm, D), lambda i: (i, 0))
    if split_out:
        out_specs = _split_specs(tm, D, n_prompt_tiles)
        out_shape = [jax.ShapeDtypeStruct((T // 2, D), F32)] * 2
    else:
        out_specs = [xrow]
        out_shape = [jax.ShapeDtypeStruct((T, D), F32)]
    hbm = pl.BlockSpec(memory_space=pl.ANY)
    return pl.pallas_call(
        functools.partial(_ffn_kernel, layer=layer, n_prompt_tiles=n_prompt_tiles,
                          split_out=split_out),
        grid=(T // tm,),
        in_specs=[xrow, xrow, _mod_spec(layer, 5, tm), _const_spec((1, D), layer), hbm, hbm],
        out_specs=out_specs,
        out_shape=out_shape,
        scratch_shapes=[pltpu.VMEM((2, D, tf), F32), pltpu.VMEM((2, D, tf), F32),
                        pltpu.VMEM((2, tf, D), F32), pltpu.VMEM((2, tm, tf), BF16),
                        pltpu.VMEM((tm, D), F32), pltpu.SemaphoreType.DMA((3, 2))],
        compiler_params=_cparams(1),
        name="ffn",
    )(x, h, mods, g_post, w_in, w_out)


def _rope_tables():
    t = jnp.arange(SEG)
    nf = HEAD_DIM // 4
    freqs = ROPE_BASE ** (-jnp.arange(nf, dtype=F32) / nf)

    def cs(pos):
        ang = pos.astype(F32)[:, None] * freqs[None, :]
        return jnp.cos(ang), jnp.sin(ang)

    cr, sr = cs(t // GRID_W)
    cc, sn = cs(t % GRID_W)
    cos = jnp.concatenate([cr, cr, cc, cc], axis=1)
    sin = jnp.concatenate([-sr, sr, -sn, sn], axis=1)
    reps = LANES // HEAD_DIM
    cos = jnp.tile(cos, (1, reps))
    sin = jnp.tile(sin, (1, reps))
    return (jnp.stack([jnp.ones_like(cos), cos]), jnp.stack([jnp.zeros_like(sin), sin]))


def _block_diag_ones(n):
    i = jnp.arange(n) // HEAD_DIM
    return (i[:, None] == i[None, :]).astype(BF16)


def kernel(x_prompt, x_sample, cache_attn_k, cache_attn_v, cache_na_k, cache_na_v, c, c_ctx,
           mod_w, mod_b, norm_mix_pre, norm_mix_post, norm_ffn_pre, norm_ffn_post,
           even_w_in, even_w_out, sgu_w, sgu_b, sgu_norm, q_norm, k_norm,
           odd_w_in, odd_w_out, na_rpb, ffn_w_in, ffn_w_out):
    nb_p, seq_p, _ = x_prompt.shape
    nb_s, seq_s, _ = x_sample.shape
    tp = nb_p * seq_p
    T = 2 * tp
    assert seq_s == SEG and tp == nb_s * seq_s and tp % SEG == 0

    cond = jnp.concatenate([jnp.broadcast_to(c_ctx[None, :], (tp // SEG, D)), c], axis=0)
    mods = _modulation(cond, mod_w, mod_b)

    cos, sin = _rope_tables()
    qw = B_HEADS * HEAD_DIM
    bdq = _block_diag_ones(qw)
    bdk = _block_diag_ones(LANES)
    past = cache_attn_k.shape[2]
    ctx_ak = cache_attn_k.reshape(nb_s, -1, past, B_KV_HEADS * HEAD_DIM)
    ctx_av = cache_attn_v.reshape(nb_s, -1, past, B_KV_HEADS * HEAD_DIM)
    ctx_nk = cache_na_k.reshape(nb_s, -1, past, C_HEADS * HEAD_DIM)
    ctx_nv = cache_na_v.reshape(nb_s, -1, past, C_HEADS * HEAD_DIM)
    g_mix_pre, g_mix_post, g_ffn_pre, g_ffn_post = (
        a.reshape(DEPTH, 1, D) for a in (norm_mix_pre, norm_mix_post, norm_ffn_pre, norm_ffn_post))
    n_even = even_w_in.shape[0]
    sgu_b3 = sgu_b.reshape(n_even, A_GROUPS, CHUNK, 1)
    sgu_n3 = sgu_norm.reshape(n_even, 1, A_WIDTH)
    qn3 = jnp.tile(q_norm, (1, B_HEADS)).reshape(n_even, 1, qw)
    kn3 = jnp.tile(k_norm, (1, B_KV_HEADS)).reshape(n_even, 1, LANES)
    rpb = jnp.pad(na_rpb, ((0, 0), (0, 0), (0, 0), (0, LANES - na_rpb.shape[3])))

    xs = (x_prompt.reshape(tp, D), x_sample.reshape(nb_s * seq_s, D))
    attn_k, attn_v, na_k, na_v = [], [], [], []
    for l in range(DEPTH):
        if l % 2 == 0:
            e = l // 2
            a_out, q, kd, vd, kf, vf = _even_in(xs, mods, l, e, T, g_mix_pre, even_w_in, sgu_w,
                                                sgu_b3, sgu_n3, qn3, kn3, bdq, bdk, cos, sin)
            n_pairs = qw // LANES
            pairs_per_kv = n_pairs // B_KV_HEADS
            mix_p = _attention(q, kd, vd, n_batch=nb_p, seq=seq_p, row0=0,
                               pairs_per_step=n_pairs, pairs_per_kv=pairs_per_kv)
            mix_s = _attention(q, kd, vd, n_batch=nb_s, seq=seq_s, row0=tp,
                               pairs_per_step=pairs_per_kv, pairs_per_kv=pairs_per_kv,
                               ctx=(ctx_ak, ctx_av, e))
            attn_k.append(kf[:tp].reshape(nb_p, seq_p, B_KV_HEADS, HEAD_DIM))
            attn_v.append(vf[:tp].reshape(nb_p, seq_p, B_KV_HEADS, HEAD_DIM))
            w_out, li = even_w_out, e
        else:
            o = l // 2
            a_out = None
            q, k, v, kf, vf = _odd_in(xs[0], mods, l, o, g_mix_pre, odd_w_in)
            mix_p = _attention(q, k, v, n_batch=nb_p, seq=seq_p, row0=0,
                               pairs_per_step=PAIRS_PER_STEP, pairs_per_kv=1)
            mix_s = _na_attention(q, k, v, ctx_nk, ctx_nv, o, rpb, n_batch=nb_s, row0=tp)
            na_k.append(kf.reshape(nb_p, seq_p, C_HEADS, HEAD_DIM))
            na_v.append(vf.reshape(nb_p, seq_p, C_HEADS, HEAD_DIM))
            w_out, li = odd_w_out, o
        x, h = _out_proj(xs, a_out, mix_p, mix_s, mods, l, li, T, g_mix_post, g_ffn_pre, w_out)
        xs = tuple(_ffn(x, h, mods, l, g_ffn_post, ffn_w_in, ffn_w_out,
                        split_out=(l == DEPTH - 1)))

    y_prompt = xs[0].reshape(nb_p, seq_p, D)
    y_sample = xs[1].reshape(nb_s, seq_s, D)
    return (y_prompt, y_sample, jnp.stack(attn_k, axis=1), jnp.stack(attn_v, axis=1),
            jnp.stack(na_k, axis=1), jnp.stack(na_v, axis=1))
```

```python
import functools

import jax
import jax.numpy as jnp
from jax import lax
from jax.experimental import pallas as pl
from jax.experimental.pallas import tpu as pltpu

F32 = jnp.float32
BF16 = jnp.bfloat16

D = 1024
DEPTH = 4
HEAD_DIM = 64
GRID_W = 64
CHUNK = 128
A_WIDTH = D // 2
A_GROUPS = 4
B_HEADS = 8
B_KV_HEADS = 2
C_HEADS = 16
NA_ROWS = 8
NA_COLS = 16
D_FF = 2816
ROPE_BASE = 10000.0
EPS = 1e-6
NEG = -1e30
SEG = 1024
N_SEG = 8
LANES = 128
QSCALE = HEAD_DIM ** -0.5

TM_PROJ = 512
TM_FFN = 1024
TF_FFN = 256
TQ_ATTN = 256
PAIRS_PER_STEP = 4
NA_QROWS = 4
NA_KROWS = 12
VMEM_LIMIT = 56 * 1024 * 1024


def _cparams(n_axes):
    return pltpu.CompilerParams(dimension_semantics=("arbitrary",) * n_axes,
                                vmem_limit_bytes=VMEM_LIMIT)


def _rms(x):
    return x * lax.rsqrt(jnp.mean(x * x, axis=-1, keepdims=True) + EPS)


def _mod_norm(x, g, shift, scale):
    return _rms(x) * (g * (1.0 + scale)) + shift


def _gated_residual(x, y, g, gate):
    return x + _rms(y) * (gate * g)


def _const_spec(shape, *lead):
    block = (None,) * len(lead) + tuple(shape)
    return pl.BlockSpec(block, lambda *_: tuple(lead) + (0,) * len(shape))


def _weight_spec(shape, *lead):
    block = (None,) * len(lead) + tuple(shape)
    return pl.BlockSpec(block, lambda *_: tuple(lead) + (0,) * len(shape),
                        pipeline_mode=pl.Buffered(1))


def _mod_spec(layer, j, tm):
    return pl.BlockSpec((None, None, None, 1, D),
                        lambda i, *_: (layer, j, (i * tm) // SEG, 0, 0))


def _split_specs(tm, width, n_prompt_tiles, col=0):
    return [pl.BlockSpec((tm, width), lambda i, *_: (jnp.minimum(i, n_prompt_tiles - 1), col)),
            pl.BlockSpec((tm, width), lambda i, *_: (jnp.maximum(i - n_prompt_tiles, 0), col))]


def _pick(n_prompt_tiles, p_ref, s_ref):
    return jnp.where(pl.program_id(0) < n_prompt_tiles, p_ref[...], s_ref[...])


def _mod_kernel(cond_ref, w_ref, b_ref, o_ref):
    s = cond_ref[...]
    s = s * jax.nn.sigmoid(s)
    o_ref[...] = jnp.dot(s.astype(BF16), w_ref[...].astype(BF16),
                         preferred_element_type=F32) + b_ref[...]


def _modulation(cond, mod_w, mod_b):
    b = mod_b.reshape(DEPTH, 6, 1, D)
    out = pl.pallas_call(
        _mod_kernel,
        grid=(DEPTH, 6),
        in_specs=[pl.BlockSpec((N_SEG, D), lambda l, j: (0, 0)),
                  pl.BlockSpec((None, D, D), lambda l, j: (l, 0, j)),
                  pl.BlockSpec((None, None, 1, D), lambda l, j: (l, j, 0, 0))],
        out_specs=pl.BlockSpec((None, None, N_SEG, D), lambda l, j: (l, j, 0, 0)),
        out_shape=jax.ShapeDtypeStruct((DEPTH, 6, N_SEG, D), F32),
        compiler_params=_cparams(2),
        name="modulation",
    )(cond, mod_w, b)
    return out.reshape(DEPTH, 6, N_SEG, 1, D)


def _head_sumsq(y, bd_ref):
    sq = y * y
    hi = sq.astype(BF16)
    lo = (sq - hi.astype(F32)).astype(BF16)
    bd = bd_ref[...]
    return (jnp.dot(hi, bd, preferred_element_type=F32)
            + jnp.dot(lo, bd, preferred_element_type=F32))


def _rope(y, cos, sin):
    lane = lax.broadcasted_iota(jnp.int32, (1, LANES), 1)
    first = (lane & 16) == 0
    partner = jnp.where(first, pltpu.roll(y, LANES - 16, 1), pltpu.roll(y, 16, 1))
    return y * cos + partner * sin


def _even_in_kernel(*refs, n_prompt_tiles, split_x):
    refs = list(refs)
    x = _pick(n_prompt_tiles, refs.pop(0), refs.pop(0)) if split_x else refs.pop(0)[...]
    (sh_ref, sc_ref, g_ref, w_ref, sguw_ref, sgub_ref, sgun_ref, qn_ref, kn_ref, bdq_ref, bdk_ref,
     cos_ref, sin_ref, a_ref, q_ref, kd_ref, vd_ref, kf_ref, vf_ref, wbf_ref) = refs
    tm = a_ref.shape[0]

    @pl.when(pl.program_id(0) == 0)
    def _():
        wbf_ref[...] = w_ref[...].astype(BF16)

    h = _mod_norm(x, g_ref[...], sh_ref[...], sc_ref[...]).astype(BF16)

    u = jax.nn.gelu(jnp.dot(h, wbf_ref[:, 0:A_WIDTH], preferred_element_type=F32), approximate=True)
    v = jax.nn.gelu(jnp.dot(h, wbf_ref[:, A_WIDTH:2 * A_WIDTH], preferred_element_type=F32),
                    approximate=True)
    mu = jnp.mean(v, axis=-1, keepdims=True)
    vc = v - mu
    var = jnp.mean(vc * vc, axis=-1, keepdims=True)
    vn = (vc * lax.rsqrt(var + EPS) * sgun_ref[...]).astype(BF16)
    n_chunks = tm // CHUNK
    gch = A_WIDTH // A_GROUPS
    for g in range(A_GROUPS):
        rhs = jnp.concatenate([vn[n * CHUNK:(n + 1) * CHUNK, g * gch:(g + 1) * gch]
                               for n in range(n_chunks)], axis=1)
        mixed = jnp.dot(sguw_ref[g].astype(BF16), rhs, preferred_element_type=F32)
        bias = sgub_ref[g]
        for n in range(n_chunks):
            blk = (mixed[:, n * gch:(n + 1) * gch] + bias) * u[n * CHUNK:(n + 1) * CHUNK,
                                                              g * gch:(g + 1) * gch]
            a_ref[n * CHUNK:(n + 1) * CHUNK, g * gch:(g + 1) * gch] = blk.astype(BF16)

    c0 = 2 * A_WIDTH
    qw = B_HEADS * HEAD_DIM
    q = jnp.dot(h, wbf_ref[:, c0:c0 + qw], preferred_element_type=F32)
    q = q * lax.rsqrt(_head_sumsq(q, bdq_ref) * (1.0 / HEAD_DIM) + EPS) * qn_ref[...]
    cos = cos_ref[...]
    sin = sin_ref[...]
    for j in range(qw // LANES):
        qj = _rope(q[:, j * LANES:(j + 1) * LANES], cos, sin) * QSCALE
        q_ref[:, j * LANES:(j + 1) * LANES] = qj.astype(BF16)

    kw = B_KV_HEADS * HEAD_DIM
    k = jnp.dot(h, wbf_ref[:, c0 + qw:c0 + qw + kw], preferred_element_type=F32)
    k = k * lax.rsqrt(_head_sumsq(k, bdk_ref) * (1.0 / HEAD_DIM) + EPS) * kn_ref[...]
    kf_ref[...] = k
    vals = jnp.dot(h, wbf_ref[:, c0 + qw + kw:c0 + qw + 2 * kw], preferred_element_type=F32)
    vf_ref[...] = vals
    kr = _rope(k, cos, sin)
    lane = lax.broadcasted_iota(jnp.int32, (1, LANES), 1)
    low = lane < HEAD_DIM
    for src, dst in ((kr, kd_ref), (vals, vd_ref)):
        sw = pltpu.roll(src, HEAD_DIM, 1)
        dst[:, 0:LANES] = jnp.where(low, src, sw).astype(BF16)
        dst[:, LANES:2 * LANES] = jnp.where(low, sw, src).astype(BF16)


def _even_in(xs, mods, layer, e, T, g_pre, w_in, sgu_w, sgu_b, sgu_norm, q_norm, k_norm,
             bdq, bdk, cos, sin):
    tm = TM_PROJ
    n_in = w_in.shape[2]
    tiles_per_seg = SEG // tm
    n_prompt_tiles = T // 2 // tm
    split_x = len(xs) == 2
    tab_spec = pl.BlockSpec((None, tm, LANES),
                            lambda i: (i // n_prompt_tiles, i % tiles_per_seg, 0))
    row = lambda w: pl.BlockSpec((tm, w), lambda i: (i, 0))
    x_specs = _split_specs(tm, D, n_prompt_tiles) if split_x else [row(D)]
    qw = B_HEADS * HEAD_DIM
    return pl.pallas_call(
        functools.partial(_even_in_kernel, n_prompt_tiles=n_prompt_tiles, split_x=split_x),
        grid=(T // tm,),
        in_specs=x_specs + [
            _mod_spec(layer, 0, tm), _mod_spec(layer, 1, tm), _const_spec((1, D), layer),
            _weight_spec((D, n_in), e), _const_spec((A_GROUPS, CHUNK, CHUNK), e),
            _const_spec((A_GROUPS, CHUNK, 1), e), _const_spec((1, A_WIDTH), e),
            _const_spec((1, qw), e), _const_spec((1, LANES), e),
            _const_spec((qw, qw)), _const_spec((LANES, LANES)), tab_spec, tab_spec],
        out_specs=[row(A_WIDTH), row(qw), row(2 * LANES), row(2 * LANES), row(LANES), row(LANES)],
        out_shape=[jax.ShapeDtypeStruct((T, A_WIDTH), BF16),
                   jax.ShapeDtypeStruct((T, qw), BF16),
                   jax.ShapeDtypeStruct((T, 2 * LANES), BF16),
                   jax.ShapeDtypeStruct((T, 2 * LANES), BF16),
                   jax.ShapeDtypeStruct((T, LANES), F32),
                   jax.ShapeDtypeStruct((T, LANES), F32)],
        scratch_shapes=[pltpu.VMEM((D, n_in), BF16)],
        compiler_params=_cparams(1),
        name="even_in_proj",
    )(*xs, mods, mods, g_pre, w_in, sgu_w, sgu_b, sgu_norm, q_norm, k_norm, bdq, bdk, cos, sin)


def _odd_in_kernel(x_ref, sh_ref, sc_ref, g_ref, w_ref, q_ref, k_ref, v_ref, kf_ref, vf_ref,
                   wbf_ref, *, n_prompt_tiles):
    i = pl.program_id(0)
    hw = C_HEADS * HEAD_DIM

    @pl.when(i == 0)
    def _():
        for c in range(3):
            wbf_ref[:, c * hw:(c + 1) * hw] = w_ref[:, c * hw:(c + 1) * hw].astype(BF16)

    h = _mod_norm(x_ref[...], g_ref[...], sh_ref[...], sc_ref[...]).astype(BF16)
    q = jnp.dot(h, wbf_ref[:, 0:hw], preferred_element_type=F32)
    q_ref[...] = (q * QSCALE).astype(BF16)
    k = jnp.dot(h, wbf_ref[:, hw:2 * hw], preferred_element_type=F32)
    k_ref[...] = k.astype(BF16)
    v = jnp.dot(h, wbf_ref[:, 2 * hw:3 * hw], preferred_element_type=F32)
    v_ref[...] = v.astype(BF16)

    @pl.when(i < n_prompt_tiles)
    def _():
        kf_ref[...] = k
        vf_ref[...] = v


def _odd_in(x, mods, layer, o, g_pre, w_in):
    T = x.shape[0]
    tm = TM_PROJ
    hw = C_HEADS * HEAD_DIM
    n_prompt_tiles = T // 2 // tm
    row = pl.BlockSpec((tm, hw), lambda i: (i, 0))
    prow = pl.BlockSpec((tm, hw), lambda i: (jnp.minimum(i, n_prompt_tiles - 1), 0))
    return pl.pallas_call(
        functools.partial(_odd_in_kernel, n_prompt_tiles=n_prompt_tiles),
        grid=(T // tm,),
        in_specs=[pl.BlockSpec((tm, D), lambda i: (i, 0)), _mod_spec(layer, 0, tm),
                  _mod_spec(layer, 1, tm), _const_spec((1, D), layer),
                  _weight_spec((D, 3 * hw), o)],
        out_specs=[row, row, row, prow, prow],
        out_shape=[jax.ShapeDtypeStruct((T, hw), BF16)] * 3
                  + [jax.ShapeDtypeStruct((T // 2, hw), F32)] * 2,
        scratch_shapes=[pltpu.VMEM((D, 3 * hw), BF16)],
        compiler_params=_cparams(1),
        name="odd_in_proj",
    )(x, mods, mods, g_pre, w_in)


def _softmax_pv(pieces, mh):
    m = None
    for s, _ in pieces:
        ms = jnp.max(s, axis=-1, keepdims=True)
        m = ms if m is None else jnp.maximum(m, ms)
    den = None
    acc = None
    for s, val in pieces:
        p = jnp.exp(s - m)
        ls = jnp.sum(p, axis=-1, keepdims=True)
        den = ls if den is None else den + ls
        o = jnp.dot(p.astype(BF16), jnp.where(mh, val, jnp.zeros_like(val)),
                    preferred_element_type=F32)
        acc = o if acc is None else acc + o
    return acc / den


def _attn_kernel(*refs, has_ctx, tq, pairs_per_kv):
    if has_ctx:
        q_ref, k_ref, v_ref, kc_ref, vc_ref, o_ref = refs
    else:
        q_ref, k_ref, v_ref, o_ref = refs
    lane = lax.broadcasted_iota(jnp.int32, (1, LANES), 1)
    masks = [lane < HEAD_DIM, lane >= HEAD_DIM]
    if has_ctx:
        kv_head = pl.program_id(1)
        sel = jnp.where(masks[0], 0, 1) == kv_head
        kc = jnp.where(sel, kc_ref[...], 0.0)
        kc = (kc + pltpu.roll(kc, HEAD_DIM, 1)).astype(BF16)
        vc = jnp.where(sel, vc_ref[...], 0.0)
        vc = (vc + pltpu.roll(vc, HEAD_DIM, 1)).astype(BF16)
    nt = (((1,), (1,)), ((), ()))
    for j in range(q_ref.shape[1] // LANES):
        cols = slice(j * LANES, (j + 1) * LANES)
        kcols = slice((j // pairs_per_kv) * LANES, (j // pairs_per_kv + 1) * LANES)
        k = k_ref[:, kcols]
        v = v_ref[:, kcols]
        for t in range(q_ref.shape[0] // tq):
            q = q_ref[t * tq:(t + 1) * tq, cols]
            out = None
            for mh in masks:
                qm = jnp.where(mh, q, jnp.zeros_like(q))
                pieces = [(lax.dot_general(qm, k, nt, preferred_element_type=F32), v)]
                if has_ctx:
                    pieces.append((lax.dot_general(qm, kc, nt, preferred_element_type=F32), vc))
                o = _softmax_pv(pieces, mh)
                out = o if out is None else out + o
            o_ref[t * tq:(t + 1) * tq, cols] = out.astype(BF16)


def _attention(q, k, v, *, n_batch, seq, row0, pairs_per_step, pairs_per_kv, ctx=None):
    blk0 = row0 // seq
    n_groups = q.shape[1] // (pairs_per_step * LANES)
    qw = pairs_per_step * LANES
    kw = qw // pairs_per_kv
    in_specs = [pl.BlockSpec((seq, qw), lambda b, g: (blk0 + b, g)),
                pl.BlockSpec((seq, kw), lambda b, g: (blk0 + b, g)),
                pl.BlockSpec((seq, kw), lambda b, g: (blk0 + b, g))]
    args = [q, k, v]
    if ctx is not None:
        assert kw == LANES
        kc, vc, e = ctx
        sc = kc.shape[2]
        cspec = pl.BlockSpec((None, None, sc, LANES), lambda b, g: (b, e, 0, 0))
        in_specs += [cspec, cspec]
        args += [kc, vc]
    return pl.pallas_call(
        functools.partial(_attn_kernel, has_ctx=ctx is not None, tq=min(TQ_ATTN, seq),
                          pairs_per_kv=pairs_per_kv),
        grid=(n_batch, n_groups),
        in_specs=in_specs,
        out_specs=pl.BlockSpec((seq, qw), lambda b, g: (b, g)),
        out_shape=jax.ShapeDtypeStruct((n_batch * seq, q.shape[1]), BF16),
        compiler_params=_cparams(2),
        name="attention",
    )(*args)


def _na_row_start(r):
    rows = SEG // GRID_W
    return min(max(r - NA_ROWS // 2, 0), rows - NA_ROWS)


def _na_windows():
    rows = SEG // GRID_W
    starts = []
    for qb in range(rows // NA_QROWS):
        lo = _na_row_start(qb * NA_QROWS)
        hi = _na_row_start(qb * NA_QROWS + NA_QROWS - 1) + NA_ROWS
        ws = min(lo, rows - NA_KROWS)
        assert ws <= lo and hi <= ws + NA_KROWS
        starts.append(ws)
    return starts


def _na_kernel(q_ref, k_ref, v_ref, kc_ref, vc_ref, r_ref, o_ref, bias_ref, tab_ref, *, starts):
    lane = lax.broadcasted_iota(jnp.int32, (1, LANES), 1)
    low = lane < HEAD_DIM

    @pl.when(pl.program_id(1) == 0)
    def _():
        qcol = lax.broadcasted_iota(jnp.int32, (GRID_W, LANES), 0)
        kcol = lax.broadcasted_iota(jnp.int32, (GRID_W, LANES), 1) & (GRID_W - 1)
        start = jnp.clip(qcol - NA_COLS // 2, 0, GRID_W - NA_COLS)
        inside = (kcol >= start) & (kcol < start + NA_COLS)
        for hh in range(LANES // HEAD_DIM):
            for d in range(2 * NA_ROWS - 1):
                base = jnp.broadcast_to(r_ref[hh, d:d + 1, :], (GRID_W, LANES))
                lo_t = pltpu.roll(base, LANES - (NA_COLS - 1), 1, stride=1, stride_axis=0)
                hi_t = pltpu.roll(base, GRID_W - (NA_COLS - 1), 1, stride=1, stride_axis=0)
                tab_ref[hh, d] = jnp.where(inside, jnp.where(low, lo_t, hi_t), NEG)
        neg = jnp.full((GRID_W, LANES), NEG, F32)
        for hh in range(LANES // HEAD_DIM):
            for qb, ws in enumerate(starts):
                for i in range(NA_QROWS):
                    r = qb * NA_QROWS + i
                    rs = _na_row_start(r)
                    for jp in range(NA_KROWS // 2):
                        kr = ws + 2 * jp
                        ok = [rs <= kr + d < rs + NA_ROWS for d in (0, 1)]
                        if not any(ok):
                            blk = neg
                        else:
                            t0 = tab_ref[hh, kr - r + NA_ROWS - 1] if ok[0] else neg
                            t1 = tab_ref[hh, kr + 1 - r + NA_ROWS - 1] if ok[1] else neg
                            blk = jnp.where(low, t0, t1)
                        bias_ref[hh, qb, i * GRID_W:(i + 1) * GRID_W,
                                 jp * LANES:(jp + 1) * LANES] = blk

    kc = kc_ref[...].astype(BF16)
    vc = vc_ref[...].astype(BF16)
    nt = (((1,), (1,)), ((), ()))
    nq = NA_QROWS * GRID_W
    nk = NA_KROWS * GRID_W
    for qb, ws in enumerate(starts):
        q = q_ref[qb * nq:(qb + 1) * nq, :]
        kw = k_ref[ws * GRID_W:ws * GRID_W + nk, :]
        vw = v_ref[ws * GRID_W:ws * GRID_W + nk, :]
        out = None
        for hh in range(LANES // HEAD_DIM):
            mh = low if hh == 0 else jnp.logical_not(low)
            qm = jnp.where(mh, q, jnp.zeros_like(q))
            s_win = lax.dot_general(qm, kw, nt, preferred_element_type=F32) + bias_ref[hh, qb]
            s_ctx = lax.dot_general(qm, kc, nt, preferred_element_type=F32)
            o = _softmax_pv([(s_win, vw), (s_ctx, vc)], mh)
            out = o if out is None else out + o
        o_ref[qb * nq:(qb + 1) * nq, :] = out.astype(BF16)


def _na_attention(q, k, v, kc, vc, o_idx, rpb, *, n_batch, row0):
    blk0 = row0 // SEG
    n_pairs = q.shape[1] // LANES
    sc = kc.shape[2]
    starts = _na_windows()
    heads = LANES // HEAD_DIM
    nr = 2 * NA_ROWS - 1
    qkv = pl.BlockSpec((SEG, LANES), lambda p, b: (blk0 + b, p))
    cspec = pl.BlockSpec((None, None, sc, LANES), lambda p, b: (b, o_idx, 0, p))
    rspec = pl.BlockSpec((None, heads, nr, LANES), lambda p, b: (o_idx, p, 0, 0))
    return pl.pallas_call(
        functools.partial(_na_kernel, starts=starts),
        grid=(n_pairs, n_batch),
        in_specs=[qkv, qkv, qkv, cspec, cspec, rspec],
        out_specs=pl.BlockSpec((SEG, LANES), lambda p, b: (b, p)),
        out_shape=jax.ShapeDtypeStruct((n_batch * SEG, n_pairs * LANES), BF16),
        scratch_shapes=[pltpu.VMEM((heads, len(starts), NA_QROWS * GRID_W, NA_KROWS * GRID_W), F32),
                        pltpu.VMEM((heads, nr, GRID_W, LANES), F32)],
        compiler_params=_cparams(2),
        name="na_attention",
    )(q, k, v, kc, vc, rpb)


def _out_kernel(*refs, n_prompt_tiles, split_x, has_a):
    refs = list(refs)
    x = _pick(n_prompt_tiles, refs.pop(0), refs.pop(0)) if split_x else refs.pop(0)[...]
    gt_ref, g_ref, fsh_ref, fsc_ref, gf_ref, w_ref = refs[:6]
    a_ref = refs[6] if has_a else None
    bp_ref, bs_ref, o_ref, h_ref, wbf_ref = refs[6 + has_a:]

    @pl.when(pl.program_id(0) == 0)
    def _():
        wbf_ref[...] = w_ref[...].astype(BF16)

    b = _pick(n_prompt_tiles, bp_ref, bs_ref)
    if has_a:
        half = a_ref.shape[1]
        y = (jnp.dot(a_ref[...], wbf_ref[0:half, :], preferred_element_type=F32)
             + jnp.dot(b, wbf_ref[half:2 * half, :], preferred_element_type=F32))
    else:
        y = jnp.dot(b, wbf_ref[...], preferred_element_type=F32)
    x1 = _gated_residual(x, y, g_ref[...], gt_ref[...])
    o_ref[...] = x1
    h_ref[...] = _mod_norm(x1, gf_ref[...], fsh_ref[...], fsc_ref[...]).astype(BF16)


def _out_proj(xs, a, bp, bs, mods, layer, li, T, g_post, g_ffn_pre, w_out):
    tm = TM_PROJ
    n_prompt_tiles = T // 2 // tm
    split_x = len(xs) == 2
    has_a = a is not None
    row = lambda w: pl.BlockSpec((tm, w), lambda i: (i, 0))
    x_specs = _split_specs(tm, D, n_prompt_tiles) if split_x else [row(D)]
    a_specs = [row(a.shape[1])] if has_a else []
    return pl.pallas_call(
        functools.partial(_out_kernel, n_prompt_tiles=n_prompt_tiles, split_x=split_x, has_a=has_a),
        grid=(T // tm,),
        in_specs=x_specs + [_mod_spec(layer, 2, tm), _const_spec((1, D), layer),
                            _mod_spec(layer, 3, tm), _mod_spec(layer, 4, tm),
                            _const_spec((1, D), layer), _weight_spec((D, D), li)] + a_specs
                 + _split_specs(tm, bp.shape[1], n_prompt_tiles),
        out_specs=[row(D), row(D)],
        out_shape=[jax.ShapeDtypeStruct((T, D), F32), jax.ShapeDtypeStruct((T, D), BF16)],
        scratch_shapes=[pltpu.VMEM((D, D), BF16)],
        compiler_params=_cparams(1),
        name="out_proj",
    )(*xs, mods, g_post, mods, mods, g_ffn_pre, w_out, *([a] if has_a else []), bp, bs)


def _ffn_kernel(x_ref, h_ref, gt_ref, gpost_ref, win_ref, wout_ref, *rest,
                layer, n_prompt_tiles, split_out):
    out_refs = rest[:-9]
    wg_buf, wu_buf, wo_buf, wg_res, wu_res, wo_res, act_buf, acc_ref, sem = rest[-9:]
    nf, _, tf = wg_res.shape
    first_tile = pl.program_id(0) == 0

    def aligned(f):
        return f * tf if isinstance(f, int) else pl.multiple_of(f * tf, tf)

    def in_copies(f, slot):
        col = aligned(f)
        return (pltpu.make_async_copy(win_ref.at[layer, :, pl.ds(col, tf)], wg_buf.at[slot],
                                      sem.at[0, slot]),
                pltpu.make_async_copy(win_ref.at[layer, :, pl.ds(D_FF + col, tf)], wu_buf.at[slot],
                                      sem.at[1, slot]))

    def out_copy(f, slot):
        row = aligned(f)
        return pltpu.make_async_copy(wout_ref.at[layer, pl.ds(row, tf), :], wo_buf.at[slot],
                                     sem.at[2, slot])

    def fetch(f, slot):
        @pl.when(first_tile)
        def _():
            for c in in_copies(f, slot):
                c.wait()
            out_copy(f, slot).wait()

            nxt = min(f + 1, nf - 1) if isinstance(f, int) else jnp.minimum(f + 1, nf - 1)

            @pl.when(jnp.asarray(f + 1 < nf))
            def _():
                for c in in_copies(nxt, 1 - slot):
                    c.start()
                out_copy(nxt, 1 - slot).start()

            wg_res[f] = wg_buf[slot].astype(BF16)
            wu_res[f] = wu_buf[slot].astype(BF16)
            wo_res[f] = wo_buf[slot].astype(BF16)

    def hidden(f, slot):
        h = h_ref[...]
        g = jnp.dot(h, wg_res[f], preferred_element_type=F32)
        u = jnp.dot(h, wu_res[f], preferred_element_type=F32)
        act_buf[slot] = (g * jax.nn.sigmoid(g) * u).astype(BF16)

    def project(f, slot):
        acc_ref[...] += jnp.dot(act_buf[slot], wo_res[f], preferred_element_type=F32)

    def step(f, slot):
        fetch(f, slot)
        hidden(f, slot)
        project(f - 1, 1 - slot)

    @pl.when(first_tile)
    def _():
        for c in in_copies(0, 0):
            c.start()
        out_copy(0, 0).start()

    acc_ref[...] = jnp.zeros_like(acc_ref)
    fetch(0, 0)
    hidden(0, 0)

    assert nf % 2 == 1

    def pair(k, carry):
        step(2 * k + 1, 1)
        step(2 * k + 2, 0)
        return carry

    lax.fori_loop(0, (nf - 1) // 2, pair, 0)
    project(nf - 1, 0)

    def result():
        return _gated_residual(x_ref[...], acc_ref[...], gpost_ref[...], gt_ref[...])

    if split_out:
        i = pl.program_id(0)

        @pl.when(i < n_prompt_tiles)
        def _():
            out_refs[0][...] = result()

        @pl.when(i >= n_prompt_tiles)
        def _():
            out_refs[1][...] = result()
    else:
        out_refs[0][...] = result()


def _ffn(x, h, mods, layer, g_post, w_in, w_out, split_out):
    T = x.shape[0]
    tm, tf = TM_FFN, TF_FFN
    nf = D_FF // tf
    n_prompt_tiles = T // 2 // tm
    xrow = pl.BlockSpec((tm, D), lambda i: (i, 0))
    if split_out:
        out_specs = _split_specs(tm, D, n_prompt_tiles)
        out_shape = [jax.ShapeDtypeStruct((T // 2, D), F32)] * 2
    else:
        out_specs = [xrow]
        out_shape = [jax.ShapeDtypeStruct((T, D), F32)]
    hbm = pl.BlockSpec(memory_space=pl.ANY)
    return pl.pallas_call(
        functools.partial(_ffn_kernel, layer=layer, n_prompt_tiles=n_prompt_tiles,
                          split_out=split_out),
        grid=(T // tm,),
        in_specs=[xrow, xrow, _mod_spec(layer, 5, tm), _const_spec((1, D), layer), hbm, hbm],
        out_specs=out_specs,
        out_shape=out_shape,
        scratch_shapes=[pltpu.VMEM((2, D, tf), F32), pltpu.VMEM((2, D, tf), F32),
                        pltpu.VMEM((2, tf, D), F32), pltpu.VMEM((nf, D, tf), BF16),
                        pltpu.VMEM((nf, D, tf), BF16), pltpu.VMEM((nf, tf, D), BF16),
                        pltpu.VMEM((2, tm, tf), BF16), pltpu.VMEM((tm, D), F32),
                        pltpu.SemaphoreType.DMA((3, 2))],
        compiler_params=_cparams(1),
        name="ffn",
    )(x, h, mods, g_post, w_in, w_out)


def _rope_tables():
    t = jnp.arange(SEG)
    nf = HEAD_DIM // 4
    freqs = ROPE_BASE ** (-jnp.arange(nf, dtype=F32) / nf)

    def cs(pos):
        ang = pos.astype(F32)[:, None] * freqs[None, :]
        return jnp.cos(ang), jnp.sin(ang)

    cr, sr = cs(t // GRID_W)
    cc, sn = cs(t % GRID_W)
    cos = jnp.concatenate([cr, cr, cc, cc], axis=1)
    sin = jnp.concatenate([-sr, sr, -sn, sn], axis=1)
    reps = LANES // HEAD_DIM
    cos = jnp.tile(cos, (1, reps))
    sin = jnp.tile(sin, (1, reps))
    return (jnp.stack([jnp.ones_like(cos), cos]), jnp.stack([jnp.zeros_like(sin), sin]))


def _block_diag_ones(n):
    i = jnp.arange(n) // HEAD_DIM
    return (i[:, None] == i[None, :]).astype(BF16)


def kernel(x_prompt, x_sample, cache_attn_k, cache_attn_v, cache_na_k, cache_na_v, c, c_ctx,
           mod_w, mod_b, norm_mix_pre, norm_mix_post, norm_ffn_pre, norm_ffn_post,
           even_w_in, even_w_out, sgu_w, sgu_b, sgu_norm, q_norm, k_norm,
           odd_w_in, odd_w_out, na_rpb, ffn_w_in, ffn_w_out):
    nb_p, seq_p, _ = x_prompt.shape
    nb_s, seq_s, _ = x_sample.shape
    tp = nb_p * seq_p
    T = 2 * tp
    assert seq_s == SEG and tp == nb_s * seq_s and tp % SEG == 0

    cond = jnp.concatenate([jnp.broadcast_to(c_ctx[None, :], (tp // SEG, D)), c], axis=0)
    mods = _modulation(cond, mod_w, mod_b)

    cos, sin = _rope_tables()
    qw = B_HEADS * HEAD_DIM
    bdq = _block_diag_ones(qw)
    bdk = _block_diag_ones(LANES)
    past = cache_attn_k.shape[2]
    ctx_ak = cache_attn_k.reshape(nb_s, -1, past, B_KV_HEADS * HEAD_DIM)
    ctx_av = cache_attn_v.reshape(nb_s, -1, past, B_KV_HEADS * HEAD_DIM)
    ctx_nk = cache_na_k.reshape(nb_s, -1, past, C_HEADS * HEAD_DIM)
    ctx_nv = cache_na_v.reshape(nb_s, -1, past, C_HEADS * HEAD_DIM)
    g_mix_pre, g_mix_post, g_ffn_pre, g_ffn_post = (
        a.reshape(DEPTH, 1, D) for a in (norm_mix_pre, norm_mix_post, norm_ffn_pre, norm_ffn_post))
    n_even = even_w_in.shape[0]
    sgu_b3 = sgu_b.reshape(n_even, A_GROUPS, CHUNK, 1)
    sgu_n3 = sgu_norm.reshape(n_even, 1, A_WIDTH)
    qn3 = jnp.tile(q_norm, (1, B_HEADS)).reshape(n_even, 1, qw)
    kn3 = jnp.tile(k_norm, (1, B_KV_HEADS)).reshape(n_even, 1, LANES)
    rpb = jnp.pad(na_rpb, ((0, 0), (0, 0), (0, 0), (0, LANES - na_rpb.shape[3])))

    xs = (x_prompt.reshape(tp, D), x_sample.reshape(nb_s * seq_s, D))
    attn_k, attn_v, na_k, na_v = [], [], [], []
    for l in range(DEPTH):
        if l % 2 == 0:
            e = l // 2
            a_out, q, kd, vd, kf, vf = _even_in(xs, mods, l, e, T, g_mix_pre, even_w_in, sgu_w,
                                                sgu_b3, sgu_n3, qn3, kn3, bdq, bdk, cos, sin)
            n_pairs = qw // LANES
            pairs_per_kv = n_pairs // B_KV_HEADS
            mix_p = _attention(q, kd, vd, n_batch=nb_p, seq=seq_p, row0=0,
                               pairs_per_step=n_pairs, pairs_per_kv=pairs_per_kv)
            mix_s = _attention(q, kd, vd, n_batch=nb_s, seq=seq_s, row0=tp,
                               pairs_per_step=pairs_per_kv, pairs_per_kv=pairs_per_kv,
                               ctx=(ctx_ak, ctx_av, e))
            attn_k.append(kf[:tp].reshape(nb_p, seq_p, B_KV_HEADS, HEAD_DIM))
            attn_v.append(vf[:tp].reshape(nb_p, seq_p, B_KV_HEADS, HEAD_DIM))
            w_out, li = even_w_out, e
        else:
            o = l // 2
            a_out = None
            q, k, v, kf, vf = _odd_in(xs[0], mods, l, o, g_mix_pre, odd_w_in)
            mix_p = _attention(q, k, v, n_batch=nb_p, seq=seq_p, row0=0,
                               pairs_per_step=PAIRS_PER_STEP, pairs_per_kv=1)
            mix_s = _na_attention(q, k, v, ctx_nk, ctx_nv, o, rpb, n_batch=nb_s, row0=tp)
            na_k.append(kf.reshape(nb_p, seq_p, C_HEADS, HEAD_DIM))
            na_v.append(vf.reshape(nb_p, seq_p, C_HEADS, HEAD_DIM))
            w_out, li = odd_w_out, o
        x, h = _out_proj(xs, a_out, mix_p, mix_s, mods, l, li, T, g_mix_post, g_ffn_pre, w_out)
        xs = tuple(_ffn(x, h, mods, l, g_ffn_post, ffn_w_in, ffn_w_out,
                        split_out=(l == DEPTH - 1)))

    y_prompt = xs[0].reshape(nb_p, seq_p, D)
    y_sample = xs[1].reshape(nb_s, seq_s, D)
    return (y_prompt, y_sample, jnp.stack(attn_k, axis=1), jnp.stack(attn_v, axis=1),
            jnp.stack(na_k, axis=1), jnp.stack(na_v, axis=1))
```

```python
import functools

import jax
import jax.numpy as jnp
from jax import lax
from jax.experimental import pallas as pl
from jax.experimental.pallas import tpu as pltpu

F32 = jnp.float32
BF16 = jnp.bfloat16

D = 1024
DEPTH = 4
HEAD_DIM = 64
GRID_W = 64
CHUNK = 128
A_WIDTH = D // 2
A_GROUPS = 4
B_HEADS = 8
B_KV_HEADS = 2
C_HEADS = 16
NA_ROWS = 8
NA_COLS = 16
D_FF = 2816
ROPE_BASE = 10000.0
EPS = 1e-6
NEG = -1e30
SEG = 1024
N_SEG = 8
LANES = 128
LOG2E = 1.4426950408889634
QSCALE = HEAD_DIM ** -0.5 * LOG2E

TM_PROJ = 512
TM_FFN = 1024
TF_FFN = 256
TQ_ATTN = 256
PAIRS_PER_STEP = 4
NA_QROWS = 4
NA_KROWS = 12
VMEM_LIMIT = 56 * 1024 * 1024


def _cparams(n_axes):
    return pltpu.CompilerParams(dimension_semantics=("arbitrary",) * n_axes,
                                vmem_limit_bytes=VMEM_LIMIT)


def _rms(x):
    return x * lax.rsqrt(jnp.mean(x * x, axis=-1, keepdims=True) + EPS)


def _mod_norm(x, g, shift, scale):
    return _rms(x) * (g * (1.0 + scale)) + shift


def _gated_residual(x, y, g, gate):
    return x + _rms(y) * (gate * g)


def _const_spec(shape, *lead):
    block = (None,) * len(lead) + tuple(shape)
    return pl.BlockSpec(block, lambda *_: tuple(lead) + (0,) * len(shape))


def _weight_spec(shape, *lead):
    block = (None,) * len(lead) + tuple(shape)
    return pl.BlockSpec(block, lambda *_: tuple(lead) + (0,) * len(shape),
                        pipeline_mode=pl.Buffered(1))


def _mod_spec(layer, j, tm):
    return pl.BlockSpec((None, None, None, 1, D),
                        lambda i, *_: (layer, j, (i * tm) // SEG, 0, 0))


def _split_specs(tm, width, n_prompt_tiles, col=0):
    return [pl.BlockSpec((tm, width), lambda i, *_: (jnp.minimum(i, n_prompt_tiles - 1), col)),
            pl.BlockSpec((tm, width), lambda i, *_: (jnp.maximum(i - n_prompt_tiles, 0), col))]


def _pick(n_prompt_tiles, p_ref, s_ref):
    return jnp.where(pl.program_id(0) < n_prompt_tiles, p_ref[...], s_ref[...])


def _mod_kernel(cond_ref, w_ref, b_ref, o_ref):
    s = cond_ref[...]
    s = s * jax.nn.sigmoid(s)
    o_ref[...] = jnp.dot(s.astype(BF16), w_ref[...].astype(BF16),
                         preferred_element_type=F32) + b_ref[...]


def _modulation(cond, mod_w, mod_b):
    b = mod_b.reshape(DEPTH, 6, 1, D)
    out = pl.pallas_call(
        _mod_kernel,
        grid=(DEPTH, 6),
        in_specs=[pl.BlockSpec((N_SEG, D), lambda l, j: (0, 0)),
                  pl.BlockSpec((None, D, D), lambda l, j: (l, 0, j)),
                  pl.BlockSpec((None, None, 1, D), lambda l, j: (l, j, 0, 0))],
        out_specs=pl.BlockSpec((None, None, N_SEG, D), lambda l, j: (l, j, 0, 0)),
        out_shape=jax.ShapeDtypeStruct((DEPTH, 6, N_SEG, D), F32),
        compiler_params=_cparams(2),
        name="modulation",
    )(cond, mod_w, b)
    return out.reshape(DEPTH, 6, N_SEG, 1, D)


def _head_sumsq(y, bd_ref):
    sq = y * y
    hi = sq.astype(BF16)
    lo = (sq - hi.astype(F32)).astype(BF16)
    bd = bd_ref[...]
    return (jnp.dot(hi, bd, preferred_element_type=F32)
            + jnp.dot(lo, bd, preferred_element_type=F32))


def _rope(y, cos, sin):
    lane = lax.broadcasted_iota(jnp.int32, (1, LANES), 1)
    first = (lane & 16) == 0
    partner = jnp.where(first, pltpu.roll(y, LANES - 16, 1), pltpu.roll(y, 16, 1))
    return y * cos + partner * sin


def _even_in_kernel(*refs, n_prompt_tiles, split_x):
    refs = list(refs)
    x = _pick(n_prompt_tiles, refs.pop(0), refs.pop(0)) if split_x else refs.pop(0)[...]
    (sh_ref, sc_ref, g_ref, w_ref, sguw_ref, sgub_ref, sgun_ref, qn_ref, kn_ref, bdq_ref, bdk_ref,
     cos_ref, sin_ref, a_ref, q_ref, kd_ref, vd_ref, kf_ref, vf_ref, wbf_ref) = refs
    tm = a_ref.shape[0]

    @pl.when(pl.program_id(0) == 0)
    def _():
        wbf_ref[...] = w_ref[...].astype(BF16)

    h = _mod_norm(x, g_ref[...], sh_ref[...], sc_ref[...]).astype(BF16)

    u = jax.nn.gelu(jnp.dot(h, wbf_ref[:, 0:A_WIDTH], preferred_element_type=F32), approximate=True)
    v = jax.nn.gelu(jnp.dot(h, wbf_ref[:, A_WIDTH:2 * A_WIDTH], preferred_element_type=F32),
                    approximate=True)
    mu = jnp.mean(v, axis=-1, keepdims=True)
    vc = v - mu
    var = jnp.mean(vc * vc, axis=-1, keepdims=True)
    vn = (vc * lax.rsqrt(var + EPS) * sgun_ref[...]).astype(BF16)
    n_chunks = tm // CHUNK
    gch = A_WIDTH // A_GROUPS
    for g in range(A_GROUPS):
        rhs = jnp.concatenate([vn[n * CHUNK:(n + 1) * CHUNK, g * gch:(g + 1) * gch]
                               for n in range(n_chunks)], axis=1)
        mixed = jnp.dot(sguw_ref[g].astype(BF16), rhs, preferred_element_type=F32)
        bias = sgub_ref[g]
        for n in range(n_chunks):
            blk = (mixed[:, n * gch:(n + 1) * gch] + bias) * u[n * CHUNK:(n + 1) * CHUNK,
                                                              g * gch:(g + 1) * gch]
            a_ref[n * CHUNK:(n + 1) * CHUNK, g * gch:(g + 1) * gch] = blk.astype(BF16)

    c0 = 2 * A_WIDTH
    qw = B_HEADS * HEAD_DIM
    q = jnp.dot(h, wbf_ref[:, c0:c0 + qw], preferred_element_type=F32)
    q = q * lax.rsqrt(_head_sumsq(q, bdq_ref) * (1.0 / HEAD_DIM) + EPS) * qn_ref[...]
    cos = cos_ref[...]
    sin = sin_ref[...]
    for j in range(qw // LANES):
        qj = _rope(q[:, j * LANES:(j + 1) * LANES], cos, sin) * QSCALE
        q_ref[:, j * LANES:(j + 1) * LANES] = qj.astype(BF16)

    kw = B_KV_HEADS * HEAD_DIM
    k = jnp.dot(h, wbf_ref[:, c0 + qw:c0 + qw + kw], preferred_element_type=F32)
    k = k * lax.rsqrt(_head_sumsq(k, bdk_ref) * (1.0 / HEAD_DIM) + EPS) * kn_ref[...]
    kf_ref[...] = k
    vals = jnp.dot(h, wbf_ref[:, c0 + qw + kw:c0 + qw + 2 * kw], preferred_element_type=F32)
    vf_ref[...] = vals
    kr = _rope(k, cos, sin)
    lane = lax.broadcasted_iota(jnp.int32, (1, LANES), 1)
    low = lane < HEAD_DIM
    for src, dst in ((kr, kd_ref), (vals, vd_ref)):
        sw = pltpu.roll(src, HEAD_DIM, 1)
        dst[:, 0:LANES] = jnp.where(low, src, sw).astype(BF16)
        dst[:, LANES:2 * LANES] = jnp.where(low, sw, src).astype(BF16)


def _even_in(xs, mods, layer, e, T, g_pre, w_in, sgu_w, sgu_b, sgu_norm, q_norm, k_norm,
             bdq, bdk, cos, sin):
    tm = TM_PROJ
    n_in = w_in.shape[2]
    tiles_per_seg = SEG // tm
    n_prompt_tiles = T // 2 // tm
    split_x = len(xs) == 2
    tab_spec = pl.BlockSpec((None, tm, LANES),
                            lambda i: (i // n_prompt_tiles, i % tiles_per_seg, 0))
    row = lambda w: pl.BlockSpec((tm, w), lambda i: (i, 0))
    x_specs = _split_specs(tm, D, n_prompt_tiles) if split_x else [row(D)]
    qw = B_HEADS * HEAD_DIM
    return pl.pallas_call(
        functools.partial(_even_in_kernel, n_prompt_tiles=n_prompt_tiles, split_x=split_x),
        grid=(T // tm,),
        in_specs=x_specs + [
            _mod_spec(layer, 0, tm), _mod_spec(layer, 1, tm), _const_spec((1, D), layer),
            _weight_spec((D, n_in), e), _const_spec((A_GROUPS, CHUNK, CHUNK), e),
            _const_spec((A_GROUPS, CHUNK, 1), e), _const_spec((1, A_WIDTH), e),
            _const_spec((1, qw), e), _const_spec((1, LANES), e),
            _const_spec((qw, qw)), _const_spec((LANES, LANES)), tab_spec, tab_spec],
        out_specs=[row(A_WIDTH), row(qw), row(2 * LANES), row(2 * LANES), row(LANES), row(LANES)],
        out_shape=[jax.ShapeDtypeStruct((T, A_WIDTH), BF16),
                   jax.ShapeDtypeStruct((T, qw), BF16),
                   jax.ShapeDtypeStruct((T, 2 * LANES), BF16),
                   jax.ShapeDtypeStruct((T, 2 * LANES), BF16),
                   jax.ShapeDtypeStruct((T, LANES), F32),
                   jax.ShapeDtypeStruct((T, LANES), F32)],
        scratch_shapes=[pltpu.VMEM((D, n_in), BF16)],
        compiler_params=_cparams(1),
        name="even_in_proj",
    )(*xs, mods, mods, g_pre, w_in, sgu_w, sgu_b, sgu_norm, q_norm, k_norm, bdq, bdk, cos, sin)


def _odd_in_kernel(x_ref, sh_ref, sc_ref, g_ref, w_ref, q_ref, k_ref, v_ref, kf_ref, vf_ref,
                   wbf_ref, *, n_prompt_tiles):
    i = pl.program_id(0)
    hw = C_HEADS * HEAD_DIM

    @pl.when(i == 0)
    def _():
        for c in range(3):
            wbf_ref[:, c * hw:(c + 1) * hw] = w_ref[:, c * hw:(c + 1) * hw].astype(BF16)

    h = _mod_norm(x_ref[...], g_ref[...], sh_ref[...], sc_ref[...]).astype(BF16)
    q = jnp.dot(h, wbf_ref[:, 0:hw], preferred_element_type=F32)
    q_ref[...] = (q * QSCALE).astype(BF16)
    k = jnp.dot(h, wbf_ref[:, hw:2 * hw], preferred_element_type=F32)
    k_ref[...] = k.astype(BF16)
    v = jnp.dot(h, wbf_ref[:, 2 * hw:3 * hw], preferred_element_type=F32)
    v_ref[...] = v.astype(BF16)

    @pl.when(i < n_prompt_tiles)
    def _():
        kf_ref[...] = k
        vf_ref[...] = v


def _odd_in(x, mods, layer, o, g_pre, w_in):
    T = x.shape[0]
    tm = TM_PROJ
    hw = C_HEADS * HEAD_DIM
    n_prompt_tiles = T // 2 // tm
    row = pl.BlockSpec((tm, hw), lambda i: (i, 0))
    prow = pl.BlockSpec((tm, hw), lambda i: (jnp.minimum(i, n_prompt_tiles - 1), 0))
    return pl.pallas_call(
        functools.partial(_odd_in_kernel, n_prompt_tiles=n_prompt_tiles),
        grid=(T // tm,),
        in_specs=[pl.BlockSpec((tm, D), lambda i: (i, 0)), _mod_spec(layer, 0, tm),
                  _mod_spec(layer, 1, tm), _const_spec((1, D), layer),
                  _weight_spec((D, 3 * hw), o)],
        out_specs=[row, row, row, prow, prow],
        out_shape=[jax.ShapeDtypeStruct((T, hw), BF16)] * 3
                  + [jax.ShapeDtypeStruct((T // 2, hw), F32)] * 2,
        scratch_shapes=[pltpu.VMEM((D, 3 * hw), BF16)],
        compiler_params=_cparams(1),
        name="odd_in_proj",
    )(x, mods, mods, g_pre, w_in)


def _softmax_pv(pieces):
    m = None
    for s, _ in pieces:
        ms = jnp.max(s, axis=-1, keepdims=True)
        m = ms if m is None else jnp.maximum(m, ms)
    den = None
    acc = None
    for s, val in pieces:
        p = jnp.exp2(s - m)
        ls = jnp.sum(p, axis=-1, keepdims=True)
        den = ls if den is None else den + ls
        o = jnp.dot(p.astype(BF16), val, preferred_element_type=F32)
        acc = o if acc is None else acc + o
    return acc / den


def _attn_kernel(*refs, has_ctx, tq, pairs_per_kv):
    if has_ctx:
        q_ref, k_ref, v_ref, kc_ref, vc_ref, o_ref = refs
    else:
        q_ref, k_ref, v_ref, o_ref = refs
    lane = lax.broadcasted_iota(jnp.int32, (1, LANES), 1)
    masks = [lane < HEAD_DIM, lane >= HEAD_DIM]
    if has_ctx:
        kv_head = pl.program_id(1)
        sel = jnp.where(masks[0], 0, 1) == kv_head
        kc = jnp.where(sel, kc_ref[...], 0.0)
        kc = (kc + pltpu.roll(kc, HEAD_DIM, 1)).astype(BF16)
        vc = jnp.where(sel, vc_ref[...], 0.0)
        vc = (vc + pltpu.roll(vc, HEAD_DIM, 1)).astype(BF16)
        vch = [jnp.where(mh, vc, jnp.zeros_like(vc)) for mh in masks]
    nt = (((1,), (1,)), ((), ()))
    for j in range(q_ref.shape[1] // LANES):
        cols = slice(j * LANES, (j + 1) * LANES)
        kcols = slice((j // pairs_per_kv) * LANES, (j // pairs_per_kv + 1) * LANES)
        k = k_ref[:, kcols]
        v = v_ref[:, kcols]
        vh = [jnp.where(mh, v, jnp.zeros_like(v)) for mh in masks]
        for t in range(q_ref.shape[0] // tq):
            q = q_ref[t * tq:(t + 1) * tq, cols]
            out = None
            for hh, mh in enumerate(masks):
                qm = jnp.where(mh, q, jnp.zeros_like(q))
                pieces = [(lax.dot_general(qm, k, nt, preferred_element_type=F32), vh[hh])]
                if has_ctx:
                    pieces.append((lax.dot_general(qm, kc, nt, preferred_element_type=F32),
                                   vch[hh]))
                o = _softmax_pv(pieces)
                out = o if out is None else out + o
            o_ref[t * tq:(t + 1) * tq, cols] = out.astype(BF16)


def _attention(q, k, v, *, n_batch, seq, row0, pairs_per_step, pairs_per_kv, ctx=None):
    blk0 = row0 // seq
    n_groups = q.shape[1] // (pairs_per_step * LANES)
    qw = pairs_per_step * LANES
    kw = qw // pairs_per_kv
    in_specs = [pl.BlockSpec((seq, qw), lambda b, g: (blk0 + b, g)),
                pl.BlockSpec((seq, kw), lambda b, g: (blk0 + b, g)),
                pl.BlockSpec((seq, kw), lambda b, g: (blk0 + b, g))]
    args = [q, k, v]
    if ctx is not None:
        assert kw == LANES
        kc, vc, e = ctx
        sc = kc.shape[2]
        cspec = pl.BlockSpec((None, None, sc, LANES), lambda b, g: (b, e, 0, 0))
        in_specs += [cspec, cspec]
        args += [kc, vc]
    return pl.pallas_call(
        functools.partial(_attn_kernel, has_ctx=ctx is not None, tq=min(TQ_ATTN, seq),
                          pairs_per_kv=pairs_per_kv),
        grid=(n_batch, n_groups),
        in_specs=in_specs,
        out_specs=pl.BlockSpec((seq, qw), lambda b, g: (b, g)),
        out_shape=jax.ShapeDtypeStruct((n_batch * seq, q.shape[1]), BF16),
        compiler_params=_cparams(2),
        name="attention",
    )(*args)


def _na_row_start(r):
    rows = SEG // GRID_W
    return min(max(r - NA_ROWS // 2, 0), rows - NA_ROWS)


def _na_windows():
    rows = SEG // GRID_W
    windows = []
    for qb in range(rows // NA_QROWS):
        lo = _na_row_start(qb * NA_QROWS)
        hi = _na_row_start(qb * NA_QROWS + NA_QROWS - 1) + NA_ROWS
        n = hi - lo + (hi - lo) % 2
        ws = min(lo, rows - n)
        assert ws <= lo and hi <= ws + n <= rows and n <= NA_KROWS
        windows.append((ws, n))
    return windows


def _na_kernel(q_ref, k_ref, v_ref, kc_ref, vc_ref, r_ref, o_ref, bias_ref, tab_ref, *, windows):
    lane = lax.broadcasted_iota(jnp.int32, (1, LANES), 1)
    low = lane < HEAD_DIM

    @pl.when(pl.program_id(1) == 0)
    def _():
        qcol = lax.broadcasted_iota(jnp.int32, (GRID_W, LANES), 0)
        kcol = lax.broadcasted_iota(jnp.int32, (GRID_W, LANES), 1) & (GRID_W - 1)
        start = jnp.clip(qcol - NA_COLS // 2, 0, GRID_W - NA_COLS)
        inside = (kcol >= start) & (kcol < start + NA_COLS)
        for hh in range(LANES // HEAD_DIM):
            for d in range(2 * NA_ROWS - 1):
                base = jnp.broadcast_to(r_ref[hh, d:d + 1, :], (GRID_W, LANES))
                lo_t = pltpu.roll(base, LANES - (NA_COLS - 1), 1, stride=1, stride_axis=0)
                hi_t = pltpu.roll(base, GRID_W - (NA_COLS - 1), 1, stride=1, stride_axis=0)
                tab_ref[hh, d] = jnp.where(inside, jnp.where(low, lo_t, hi_t) * LOG2E, NEG)
        neg = jnp.full((GRID_W, LANES), NEG, F32)
        for hh in range(LANES // HEAD_DIM):
            for qb, (ws, nrows) in enumerate(windows):
                for i in range(NA_QROWS):
                    r = qb * NA_QROWS + i
                    rs = _na_row_start(r)
                    for jp in range(nrows // 2):
                        kr = ws + 2 * jp
                        ok = [rs <= kr + d < rs + NA_ROWS for d in (0, 1)]
                        if not any(ok):
                            blk = neg
                        else:
                            t0 = tab_ref[hh, kr - r + NA_ROWS - 1] if ok[0] else neg
                            t1 = tab_ref[hh, kr + 1 - r + NA_ROWS - 1] if ok[1] else neg
                            blk = jnp.where(low, t0, t1)
                        bias_ref[hh, qb, i * GRID_W:(i + 1) * GRID_W,
                                 jp * LANES:(jp + 1) * LANES] = blk

    masks = [low, jnp.logical_not(low)]
    kc = kc_ref[...].astype(BF16)
    vc = vc_ref[...].astype(BF16)
    vch = [jnp.where(mh, vc, jnp.zeros_like(vc)) for mh in masks]
    nt = (((1,), (1,)), ((), ()))
    nq = NA_QROWS * GRID_W
    for qb, (ws, nrows) in enumerate(windows):
        nk = nrows * GRID_W
        q = q_ref[qb * nq:(qb + 1) * nq, :]
        kw = k_ref[ws * GRID_W:ws * GRID_W + nk, :]
        vw = v_ref[ws * GRID_W:ws * GRID_W + nk, :]
        out = None
        for hh, mh in enumerate(masks):
            qm = jnp.where(mh, q, jnp.zeros_like(q))
            s_win = (lax.dot_general(qm, kw, nt, preferred_element_type=F32)
                     + bias_ref[hh, qb, :, 0:nk])
            s_ctx = lax.dot_general(qm, kc, nt, preferred_element_type=F32)
            o = _softmax_pv([(s_win, jnp.where(mh, vw, jnp.zeros_like(vw))), (s_ctx, vch[hh])])
            out = o if out is None else out + o
        o_ref[qb * nq:(qb + 1) * nq, :] = out.astype(BF16)


def _na_attention(q, k, v, kc, vc, o_idx, rpb, *, n_batch, row0):
    blk0 = row0 // SEG
    n_pairs = q.shape[1] // LANES
    sc = kc.shape[2]
    windows = _na_windows()
    heads = LANES // HEAD_DIM
    nr = 2 * NA_ROWS - 1
    qkv = pl.BlockSpec((SEG, LANES), lambda p, b: (blk0 + b, p))
    cspec = pl.BlockSpec((None, None, sc, LANES), lambda p, b: (b, o_idx, 0, p))
    rspec = pl.BlockSpec((None, heads, nr, LANES), lambda p, b: (o_idx, p, 0, 0))
    return pl.pallas_call(
        functools.partial(_na_kernel, windows=windows),
        grid=(n_pairs, n_batch),
        in_specs=[qkv, qkv, qkv, cspec, cspec, rspec],
        out_specs=pl.BlockSpec((SEG, LANES), lambda p, b: (b, p)),
        out_shape=jax.ShapeDtypeStruct((n_batch * SEG, n_pairs * LANES), BF16),
        scratch_shapes=[pltpu.VMEM((heads, len(windows), NA_QROWS * GRID_W, NA_KROWS * GRID_W), F32),
                        pltpu.VMEM((heads, nr, GRID_W, LANES), F32)],
        compiler_params=_cparams(2),
        name="na_attention",
    )(q, k, v, kc, vc, rpb)


def _out_kernel(*refs, n_prompt_tiles, split_x, has_a):
    refs = list(refs)
    x = _pick(n_prompt_tiles, refs.pop(0), refs.pop(0)) if split_x else refs.pop(0)[...]
    gt_ref, g_ref, fsh_ref, fsc_ref, gf_ref, w_ref = refs[:6]
    a_ref = refs[6] if has_a else None
    bp_ref, bs_ref, o_ref, h_ref, wbf_ref = refs[6 + has_a:]

    @pl.when(pl.program_id(0) == 0)
    def _():
        wbf_ref[...] = w_ref[...].astype(BF16)

    b = _pick(n_prompt_tiles, bp_ref, bs_ref)
    if has_a:
        half = a_ref.shape[1]
        y = (jnp.dot(a_ref[...], wbf_ref[0:half, :], preferred_element_type=F32)
             + jnp.dot(b, wbf_ref[half:2 * half, :], preferred_element_type=F32))
    else:
        y = jnp.dot(b, wbf_ref[...], preferred_element_type=F32)
    x1 = _gated_residual(x, y, g_ref[...], gt_ref[...])
    o_ref[...] = x1
    h_ref[...] = _mod_norm(x1, gf_ref[...], fsh_ref[...], fsc_ref[...]).astype(BF16)


def _out_proj(xs, a, bp, bs, mods, layer, li, T, g_post, g_ffn_pre, w_out):
    tm = TM_PROJ
    n_prompt_tiles = T // 2 // tm
    split_x = len(xs) == 2
    has_a = a is not None
    row = lambda w: pl.BlockSpec((tm, w), lambda i: (i, 0))
    x_specs = _split_specs(tm, D, n_prompt_tiles) if split_x else [row(D)]
    a_specs = [row(a.shape[1])] if has_a else []
    return pl.pallas_call(
        functools.partial(_out_kernel, n_prompt_tiles=n_prompt_tiles, split_x=split_x, has_a=has_a),
        grid=(T // tm,),
        in_specs=x_specs + [_mod_spec(layer, 2, tm), _const_spec((1, D), layer),
                            _mod_spec(layer, 3, tm), _mod_spec(layer, 4, tm),
                            _const_spec((1, D), layer), _weight_spec((D, D), li)] + a_specs
                 + _split_specs(tm, bp.shape[1], n_prompt_tiles),
        out_specs=[row(D), row(D)],
        out_shape=[jax.ShapeDtypeStruct((T, D), F32), jax.ShapeDtypeStruct((T, D), BF16)],
        scratch_shapes=[pltpu.VMEM((D, D), BF16)],
        compiler_params=_cparams(1),
        name="out_proj",
    )(*xs, mods, g_post, mods, mods, g_ffn_pre, w_out, *([a] if has_a else []), bp, bs)


def _ffn_kernel(x_ref, h_ref, gt_ref, gpost_ref, win_ref, wout_ref, *rest,
                layer, n_prompt_tiles, split_out):
    out_refs = rest[:-9]
    wg_buf, wu_buf, wo_buf, wg_res, wu_res, wo_res, act_buf, acc_ref, sem = rest[-9:]
    nf, _, tf = wg_res.shape
    first_tile = pl.program_id(0) == 0

    def aligned(f):
        return f * tf if isinstance(f, int) else pl.multiple_of(f * tf, tf)

    def in_copies(f, slot):
        col = aligned(f)
        return (pltpu.make_async_copy(win_ref.at[layer, :, pl.ds(col, tf)], wg_buf.at[slot],
                                      sem.at[0, slot]),
                pltpu.make_async_copy(win_ref.at[layer, :, pl.ds(D_FF + col, tf)], wu_buf.at[slot],
                                      sem.at[1, slot]))

    def out_copy(f, slot):
        row = aligned(f)
        return pltpu.make_async_copy(wout_ref.at[layer, pl.ds(row, tf), :], wo_buf.at[slot],
                                     sem.at[2, slot])

    def fetch(f, slot):
        @pl.when(first_tile)
        def _():
            for c in in_copies(f, slot):
                c.wait()
            out_copy(f, slot).wait()

            nxt = min(f + 1, nf - 1) if isinstance(f, int) else jnp.minimum(f + 1, nf - 1)

            @pl.when(jnp.asarray(f + 1 < nf))
            def _():
                for c in in_copies(nxt, 1 - slot):
                    c.start()
                out_copy(nxt, 1 - slot).start()

            wg_res[f] = wg_buf[slot].astype(BF16)
            wu_res[f] = wu_buf[slot].astype(BF16)
            wo_res[f] = wo_buf[slot].astype(BF16)

    def hidden(f, slot):
        h = h_ref[...]
        g = jnp.dot(h, wg_res[f], preferred_element_type=F32)
        u = jnp.dot(h, wu_res[f], preferred_element_type=F32)
        act_buf[slot] = (g * jax.nn.sigmoid(g) * u).astype(BF16)

    def project(f, slot):
        acc_ref[...] += jnp.dot(act_buf[slot], wo_res[f], preferred_element_type=F32)

    def step(f, slot):
        fetch(f, slot)
        hidden(f, slot)
        project(f - 1, 1 - slot)

    @pl.when(first_tile)
    def _():
        for c in in_copies(0, 0):
            c.start()
        out_copy(0, 0).start()

    acc_ref[...] = jnp.zeros_like(acc_ref)
    fetch(0, 0)
    hidden(0, 0)

    assert nf % 2 == 1

    def pair(k, carry):
        step(2 * k + 1, 1)
        step(2 * k + 2, 0)
        return carry

    lax.fori_loop(0, (nf - 1) // 2, pair, 0)
    project(nf - 1, 0)

    def result():
        return _gated_residual(x_ref[...], acc_ref[...], gpost_ref[...], gt_ref[...])

    if split_out:
        i = pl.program_id(0)

        @pl.when(i < n_prompt_tiles)
        def _():
            out_refs[0][...] = result()

        @pl.when(i >= n_prompt_tiles)
        def _():
            out_refs[1][...] = result()
    else:
        out_refs[0][...] = result()


def _ffn(x, h, mods, layer, g_post, w_in, w_out, split_out):
    T = x.shape[0]
    tm, tf = TM_FFN, TF_FFN
    nf = D_FF // tf
    n_prompt_tiles = T // 2 // tm
    xrow = pl.BlockSpec((tm, D), lambda i: (i, 0))
    if split_out:
        out_specs = _split_specs(tm, D, n_prompt_tiles)
        out_shape = [jax.ShapeDtypeStruct((T // 2, D), F32)] * 2
    else:
        out_specs = [xrow]
        out_shape = [jax.ShapeDtypeStruct((T, D), F32)]
    hbm = pl.BlockSpec(memory_space=pl.ANY)
    return pl.pallas_call(
        functools.partial(_ffn_kernel, layer=layer, n_prompt_tiles=n_prompt_tiles,
                          split_out=split_out),
        grid=(T // tm,),
        in_specs=[xrow, xrow, _mod_spec(layer, 5, tm), _const_spec((1, D), layer), hbm, hbm],
        out_specs=out_specs,
        out_shape=out_shape,
        scratch_shapes=[pltpu.VMEM((2, D, tf), F32), pltpu.VMEM((2, D, tf), F32),
                        pltpu.VMEM((2, tf, D), F32), pltpu.VMEM((nf, D, tf), BF16),
                        pltpu.VMEM((nf, D, tf), BF16), pltpu.VMEM((nf, tf, D), BF16),
                        pltpu.VMEM((2, tm, tf), BF16), pltpu.VMEM((tm, D), F32),
                        pltpu.SemaphoreType.DMA((3, 2))],
        compiler_params=_cparams(1),
        name="ffn",
    )(x, h, mods, g_post, w_in, w_out)


def _rope_tables():
    t = jnp.arange(SEG)
    nf = HEAD_DIM // 4
    freqs = ROPE_BASE ** (-jnp.arange(nf, dtype=F32) / nf)

    def cs(pos):
        ang = pos.astype(F32)[:, None] * freqs[None, :]
        return jnp.cos(ang), jnp.sin(ang)

    cr, sr = cs(t // GRID_W)
    cc, sn = cs(t % GRID_W)
    cos = jnp.concatenate([cr, cr, cc, cc], axis=1)
    sin = jnp.concatenate([-sr, sr, -sn, sn], axis=1)
    reps = LANES // HEAD_DIM
    cos = jnp.tile(cos, (1, reps))
    sin = jnp.tile(sin, (1, reps))
    return (jnp.stack([jnp.ones_like(cos), cos]), jnp.stack([jnp.zeros_like(sin), sin]))


def _block_diag_ones(n):
    i = jnp.arange(n) // HEAD_DIM
    return (i[:, None] == i[None, :]).astype(BF16)


def kernel(x_prompt, x_sample, cache_attn_k, cache_attn_v, cache_na_k, cache_na_v, c, c_ctx,
           mod_w, mod_b, norm_mix_pre, norm_mix_post, norm_ffn_pre, norm_ffn_post,
           even_w_in, even_w_out, sgu_w, sgu_b, sgu_norm, q_norm, k_norm,
           odd_w_in, odd_w_out, na_rpb, ffn_w_in, ffn_w_out):
    nb_p, seq_p, _ = x_prompt.shape
    nb_s, seq_s, _ = x_sample.shape
    tp = nb_p * seq_p
    T = 2 * tp
    assert seq_s == SEG and tp == nb_s * seq_s and tp % SEG == 0

    cond = jnp.concatenate([jnp.broadcast_to(c_ctx[None, :], (tp // SEG, D)), c], axis=0)
    mods = _modulation(cond, mod_w, mod_b)

    cos, sin = _rope_tables()
    qw = B_HEADS * HEAD_DIM
    bdq = _block_diag_ones(qw)
    bdk = _block_diag_ones(LANES)
    past = cache_attn_k.shape[2]
    ctx_ak = cache_attn_k.reshape(nb_s, -1, past, B_KV_HEADS * HEAD_DIM)
    ctx_av = cache_attn_v.reshape(nb_s, -1, past, B_KV_HEADS * HEAD_DIM)
    ctx_nk = cache_na_k.reshape(nb_s, -1, past, C_HEADS * HEAD_DIM)
    ctx_nv = cache_na_v.reshape(nb_s, -1, past, C_HEADS * HEAD_DIM)
    g_mix_pre, g_mix_post, g_ffn_pre, g_ffn_post = (
        a.reshape(DEPTH, 1, D) for a in (norm_mix_pre, norm_mix_post, norm_ffn_pre, norm_ffn_post))
    n_even = even_w_in.shape[0]
    sgu_b3 = sgu_b.reshape(n_even, A_GROUPS, CHUNK, 1)
    sgu_n3 = sgu_norm.reshape(n_even, 1, A_WIDTH)
    qn3 = jnp.tile(q_norm, (1, B_HEADS)).reshape(n_even, 1, qw)
    kn3 = jnp.tile(k_norm, (1, B_KV_HEADS)).reshape(n_even, 1, LANES)
    rpb = jnp.pad(na_rpb, ((0, 0), (0, 0), (0, 0), (0, LANES - na_rpb.shape[3])))

    xs = (x_prompt.reshape(tp, D), x_sample.reshape(nb_s * seq_s, D))
    attn_k, attn_v, na_k, na_v = [], [], [], []
    for l in range(DEPTH):
        if l % 2 == 0:
            e = l // 2
            a_out, q, kd, vd, kf, vf = _even_in(xs, mods, l, e, T, g_mix_pre, even_w_in, sgu_w,
                                                sgu_b3, sgu_n3, qn3, kn3, bdq, bdk, cos, sin)
            n_pairs = qw // LANES
            pairs_per_kv = n_pairs // B_KV_HEADS
            mix_p = _attention(q, kd, vd, n_batch=nb_p, seq=seq_p, row0=0,
                               pairs_per_step=n_pairs, pairs_per_kv=pairs_per_kv)
            mix_s = _attention(q, kd, vd, n_batch=nb_s, seq=seq_s, row0=tp,
                               pairs_per_step=pairs_per_kv, pairs_per_kv=pairs_per_kv,
                               ctx=(ctx_ak, ctx_av, e))
            attn_k.append(kf[:tp].reshape(nb_p, seq_p, B_KV_HEADS, HEAD_DIM))
            attn_v.append(vf[:tp].reshape(nb_p, seq_p, B_KV_HEADS, HEAD_DIM))
            w_out, li = even_w_out, e
        else:
            o = l // 2
            a_out = None
            q, k, v, kf, vf = _odd_in(xs[0], mods, l, o, g_mix_pre, odd_w_in)
            mix_p = _attention(q, k, v, n_batch=nb_p, seq=seq_p, row0=0,
                               pairs_per_step=PAIRS_PER_STEP, pairs_per_kv=1)
            mix_s = _na_attention(q, k, v, ctx_nk, ctx_nv, o, rpb, n_batch=nb_s, row0=tp)
            na_k.append(kf.reshape(nb_p, seq_p, C_HEADS * HEAD_DIM))
            na_v.append(vf.reshape(nb_p, seq_p, C_HEADS * HEAD_DIM))
            w_out, li = odd_w_out, o
        x, h = _out_proj(xs, a_out, mix_p, mix_s, mods, l, li, T, g_mix_post, g_ffn_pre, w_out)
        xs = tuple(_ffn(x, h, mods, l, g_ffn_post, ffn_w_in, ffn_w_out,
                        split_out=(l == DEPTH - 1)))

    y_prompt = xs[0].reshape(nb_p, seq_p, D)
    y_sample = xs[1].reshape(nb_s, seq_s, D)
    n_odd = len(na_k)
    new_na_k = jnp.stack(na_k, axis=1).reshape(nb_p, n_odd, seq_p, C_HEADS, HEAD_DIM)
    new_na_v = jnp.stack(na_v, axis=1).reshape(nb_p, n_odd, seq_p, C_HEADS, HEAD_DIM)
    return (y_prompt, y_sample, jnp.stack(attn_k, axis=1), jnp.stack(attn_v, axis=1),
            new_na_k, new_na_v)
```

```python
import functools

import jax
import jax.numpy as jnp
from jax import lax
from jax.experimental import pallas as pl
from jax.experimental.pallas import tpu as pltpu

F32 = jnp.float32
BF16 = jnp.bfloat16

D = 1024
DEPTH = 4
HEAD_DIM = 64
GRID_W = 64
CHUNK = 128
A_WIDTH = D // 2
A_GROUPS = 4
B_HEADS = 8
B_KV_HEADS = 2
C_HEADS = 16
NA_ROWS = 8
NA_COLS = 16
D_FF = 2816
ROPE_BASE = 10000.0
EPS = 1e-6
NEG = -1e30
SEG = 1024
N_SEG = 8
LANES = 128
LOG2E = 1.4426950408889634
QSCALE = HEAD_DIM ** -0.5 * LOG2E

TM_PROJ = 512
TM_FFN = 1024
TF_FFN = 256
TQ_ATTN = 256
PAIRS_PER_STEP = 4
NA_QROWS = 4
NA_KROWS = 12
VMEM_LIMIT = 56 * 1024 * 1024


def _cparams(n_axes):
    return pltpu.CompilerParams(dimension_semantics=("arbitrary",) * n_axes,
                                vmem_limit_bytes=VMEM_LIMIT)


def _rms(x):
    return x * lax.rsqrt(jnp.mean(x * x, axis=-1, keepdims=True) + EPS)


def _mod_norm(x, g, shift, scale):
    return _rms(x) * (g * (1.0 + scale)) + shift


def _gated_residual(x, y, g, gate):
    return x + _rms(y) * (gate * g)


def _const_spec(shape, *lead):
    block = (None,) * len(lead) + tuple(shape)
    return pl.BlockSpec(block, lambda *_: tuple(lead) + (0,) * len(shape))


def _weight_spec(shape, *lead):
    block = (None,) * len(lead) + tuple(shape)
    return pl.BlockSpec(block, lambda *_: tuple(lead) + (0,) * len(shape),
                        pipeline_mode=pl.Buffered(1))


def _mod_spec(layer, j, tm):
    return pl.BlockSpec((None, None, None, 1, D),
                        lambda i, *_: (layer, j, (i * tm) // SEG, 0, 0))


def _split_specs(tm, width, n_prompt_tiles, col=0):
    return [pl.BlockSpec((tm, width), lambda i, *_: (jnp.minimum(i, n_prompt_tiles - 1), col)),
            pl.BlockSpec((tm, width), lambda i, *_: (jnp.maximum(i - n_prompt_tiles, 0), col))]


def _pick(n_prompt_tiles, p_ref, s_ref):
    return jnp.where(pl.program_id(0) < n_prompt_tiles, p_ref[...], s_ref[...])


def _mod_kernel(cond_ref, w_ref, b_ref, o_ref):
    s = cond_ref[...]
    s = s * jax.nn.sigmoid(s)
    o_ref[...] = jnp.dot(s.astype(BF16), w_ref[...].astype(BF16),
                         preferred_element_type=F32) + b_ref[...]


def _modulation(cond, mod_w, mod_b):
    b = mod_b.reshape(DEPTH, 6, 1, D)
    out = pl.pallas_call(
        _mod_kernel,
        grid=(DEPTH, 6),
        in_specs=[pl.BlockSpec((N_SEG, D), lambda l, j: (0, 0)),
                  pl.BlockSpec((None, D, D), lambda l, j: (l, 0, j)),
                  pl.BlockSpec((None, None, 1, D), lambda l, j: (l, j, 0, 0))],
        out_specs=pl.BlockSpec((None, None, N_SEG, D), lambda l, j: (l, j, 0, 0)),
        out_shape=jax.ShapeDtypeStruct((DEPTH, 6, N_SEG, D), F32),
        compiler_params=_cparams(2),
        name="modulation",
    )(cond, mod_w, b)
    return out.reshape(DEPTH, 6, N_SEG, 1, D)


def _head_sumsq(y, bd_ref):
    sq = y * y
    hi = sq.astype(BF16)
    lo = (sq - hi.astype(F32)).astype(BF16)
    bd = bd_ref[...]
    return (jnp.dot(hi, bd, preferred_element_type=F32)
            + jnp.dot(lo, bd, preferred_element_type=F32))


def _rope(y, cos, sin):
    lane = lax.broadcasted_iota(jnp.int32, (1, LANES), 1)
    first = (lane & 16) == 0
    partner = jnp.where(first, pltpu.roll(y, LANES - 16, 1), pltpu.roll(y, 16, 1))
    return y * cos + partner * sin


def _even_in_kernel(*refs, n_prompt_tiles, split_x):
    refs = list(refs)
    x = _pick(n_prompt_tiles, refs.pop(0), refs.pop(0)) if split_x else refs.pop(0)[...]
    (sh_ref, sc_ref, g_ref, w_ref, sguw_ref, sgub_ref, sgun_ref, qn_ref, kn_ref, bdq_ref, bdk_ref,
     cos_ref, sin_ref, a_ref, q_ref, kd_ref, vd_ref, kf_ref, vf_ref, wbf_ref) = refs
    tm = a_ref.shape[0]

    @pl.when(pl.program_id(0) == 0)
    def _():
        wbf_ref[...] = w_ref[...].astype(BF16)

    h = _mod_norm(x, g_ref[...], sh_ref[...], sc_ref[...]).astype(BF16)

    u = jax.nn.gelu(jnp.dot(h, wbf_ref[:, 0:A_WIDTH], preferred_element_type=F32), approximate=True)
    v = jax.nn.gelu(jnp.dot(h, wbf_ref[:, A_WIDTH:2 * A_WIDTH], preferred_element_type=F32),
                    approximate=True)
    mu = jnp.mean(v, axis=-1, keepdims=True)
    vc = v - mu
    var = jnp.mean(vc * vc, axis=-1, keepdims=True)
    vn = (vc * lax.rsqrt(var + EPS) * sgun_ref[...]).astype(BF16)
    n_chunks = tm // CHUNK
    gch = A_WIDTH // A_GROUPS
    for g in range(A_GROUPS):
        rhs = jnp.concatenate([vn[n * CHUNK:(n + 1) * CHUNK, g * gch:(g + 1) * gch]
                               for n in range(n_chunks)], axis=1)
        mixed = jnp.dot(sguw_ref[g].astype(BF16), rhs, preferred_element_type=F32)
        bias = sgub_ref[g]
        for n in range(n_chunks):
            blk = (mixed[:, n * gch:(n + 1) * gch] + bias) * u[n * CHUNK:(n + 1) * CHUNK,
                                                              g * gch:(g + 1) * gch]
            a_ref[n * CHUNK:(n + 1) * CHUNK, g * gch:(g + 1) * gch] = blk.astype(BF16)

    c0 = 2 * A_WIDTH
    qw = B_HEADS * HEAD_DIM
    q = jnp.dot(h, wbf_ref[:, c0:c0 + qw], preferred_element_type=F32)
    q = q * lax.rsqrt(_head_sumsq(q, bdq_ref) * (1.0 / HEAD_DIM) + EPS) * qn_ref[...]
    cos = cos_ref[...]
    sin = sin_ref[...]
    for j in range(qw // LANES):
        qj = _rope(q[:, j * LANES:(j + 1) * LANES], cos, sin) * QSCALE
        q_ref[:, j * LANES:(j + 1) * LANES] = qj.astype(BF16)

    kw = B_KV_HEADS * HEAD_DIM
    k = jnp.dot(h, wbf_ref[:, c0 + qw:c0 + qw + kw], preferred_element_type=F32)
    k = k * lax.rsqrt(_head_sumsq(k, bdk_ref) * (1.0 / HEAD_DIM) + EPS) * kn_ref[...]
    kf_ref[...] = k
    vals = jnp.dot(h, wbf_ref[:, c0 + qw + kw:c0 + qw + 2 * kw], preferred_element_type=F32)
    vf_ref[...] = vals
    kr = _rope(k, cos, sin)
    lane = lax.broadcasted_iota(jnp.int32, (1, LANES), 1)
    low = lane < HEAD_DIM
    for src, dst in ((kr, kd_ref), (vals, vd_ref)):
        sw = pltpu.roll(src, HEAD_DIM, 1)
        dst[:, 0:LANES] = jnp.where(low, src, sw).astype(BF16)
        dst[:, LANES:2 * LANES] = jnp.where(low, sw, src).astype(BF16)


def _even_in(xs, mods, layer, e, T, g_pre, w_in, sgu_w, sgu_b, sgu_norm, q_norm, k_norm,
             bdq, bdk, cos, sin):
    tm = TM_PROJ
    n_in = w_in.shape[2]
    tiles_per_seg = SEG // tm
    n_prompt_tiles = T // 2 // tm
    split_x = len(xs) == 2
    tab_spec = pl.BlockSpec((None, tm, LANES),
                            lambda i: (i // n_prompt_tiles, i % tiles_per_seg, 0))
    row = lambda w: pl.BlockSpec((tm, w), lambda i: (i, 0))
    x_specs = _split_specs(tm, D, n_prompt_tiles) if split_x else [row(D)]
    qw = B_HEADS * HEAD_DIM
    return pl.pallas_call(
        functools.partial(_even_in_kernel, n_prompt_tiles=n_prompt_tiles, split_x=split_x),
        grid=(T // tm,),
        in_specs=x_specs + [
            _mod_spec(layer, 0, tm), _mod_spec(layer, 1, tm), _const_spec((1, D), layer),
            _weight_spec((D, n_in), e), _const_spec((A_GROUPS, CHUNK, CHUNK), e),
            _const_spec((A_GROUPS, CHUNK, 1), e), _const_spec((1, A_WIDTH), e),
            _const_spec((1, qw), e), _const_spec((1, LANES), e),
            _const_spec((qw, qw)), _const_spec((LANES, LANES)), tab_spec, tab_spec],
        out_specs=[row(A_WIDTH), row(qw), row(2 * LANES), row(2 * LANES), row(LANES), row(LANES)],
        out_shape=[jax.ShapeDtypeStruct((T, A_WIDTH), BF16),
                   jax.ShapeDtypeStruct((T, qw), BF16),
                   jax.ShapeDtypeStruct((T, 2 * LANES), BF16),
                   jax.ShapeDtypeStruct((T, 2 * LANES), BF16),
                   jax.ShapeDtypeStruct((T, LANES), F32),
                   jax.ShapeDtypeStruct((T, LANES), F32)],
        scratch_shapes=[pltpu.VMEM((D, n_in), BF16)],
        compiler_params=_cparams(1),
        name="even_in_proj",
    )(*xs, mods, mods, g_pre, w_in, sgu_w, sgu_b, sgu_norm, q_norm, k_norm, bdq, bdk, cos, sin)


def _odd_in_kernel(x_ref, sh_ref, sc_ref, g_ref, w_ref, *rest, n_prompt_tiles):
    q_ref, k_ref, v_ref, kf_ref, vf_ref, wbf_ref = rest[-6:]
    i = pl.program_id(0)
    hw = C_HEADS * HEAD_DIM

    @pl.when(i == 0)
    def _():
        for c in range(3):
            wbf_ref[:, c * hw:(c + 1) * hw] = w_ref[:, c * hw:(c + 1) * hw].astype(BF16)

    h = _mod_norm(x_ref[...], g_ref[...], sh_ref[...], sc_ref[...]).astype(BF16)
    q = jnp.dot(h, wbf_ref[:, 0:hw], preferred_element_type=F32)
    q_ref[...] = (q * QSCALE).astype(BF16)
    k = jnp.dot(h, wbf_ref[:, hw:2 * hw], preferred_element_type=F32)
    k_ref[...] = k.astype(BF16)
    v = jnp.dot(h, wbf_ref[:, 2 * hw:3 * hw], preferred_element_type=F32)
    v_ref[...] = v.astype(BF16)

    @pl.when(i < n_prompt_tiles)
    def _():
        kf_ref[...] = k.reshape(kf_ref.shape)
        vf_ref[...] = v.reshape(vf_ref.shape)


def _odd_in(x, mods, layer, o, n_odd, seq_p, g_pre, w_in, caches):
    T = x.shape[0]
    tm = TM_PROJ
    hw = C_HEADS * HEAD_DIM
    n_prompt_tiles = T // 2 // tm
    bt = tm // seq_p
    row = pl.BlockSpec((tm, hw), lambda i: (i, 0))
    crow = pl.BlockSpec((bt, None, seq_p, hw),
                        lambda i: (jnp.minimum(i, n_prompt_tiles - 1), o, 0, 0))
    cshape = jax.ShapeDtypeStruct((T // 2 // seq_p, n_odd, seq_p, hw), F32)
    n_in = 5
    return pl.pallas_call(
        functools.partial(_odd_in_kernel, n_prompt_tiles=n_prompt_tiles),
        grid=(T // tm,),
        in_specs=[pl.BlockSpec((tm, D), lambda i: (i, 0)), _mod_spec(layer, 0, tm),
                  _mod_spec(layer, 1, tm), _const_spec((1, D), layer),
                  _weight_spec((D, 3 * hw), o)]
                 + [pl.BlockSpec(memory_space=pl.ANY)] * len(caches),
        out_specs=[row, row, row, crow, crow],
        out_shape=[jax.ShapeDtypeStruct((T, hw), BF16)] * 3 + [cshape, cshape],
        input_output_aliases={n_in + j: 3 + j for j in range(len(caches))},
        scratch_shapes=[pltpu.VMEM((D, 3 * hw), BF16)],
        compiler_params=_cparams(1),
        name="odd_in_proj",
    )(x, mods, mods, g_pre, w_in, *caches)


def _softmax_pv(pieces):
    m = None
    for s, _ in pieces:
        ms = jnp.max(s, axis=-1, keepdims=True)
        m = ms if m is None else jnp.maximum(m, ms)
    den = None
    acc = None
    for s, val in pieces:
        p = jnp.exp2(s - m)
        ls = jnp.sum(p, axis=-1, keepdims=True)
        den = ls if den is None else den + ls
        o = jnp.dot(p.astype(BF16), val, preferred_element_type=F32)
        acc = o if acc is None else acc + o
    return acc / den


def _attn_kernel(*refs, has_ctx, tq, pairs_per_kv):
    if has_ctx:
        q_ref, k_ref, v_ref, kc_ref, vc_ref, o_ref = refs
    else:
        q_ref, k_ref, v_ref, o_ref = refs
    lane = lax.broadcasted_iota(jnp.int32, (1, LANES), 1)
    masks = [lane < HEAD_DIM, lane >= HEAD_DIM]
    if has_ctx:
        kv_head = pl.program_id(1)
        sel = jnp.where(masks[0], 0, 1) == kv_head
        kc = jnp.where(sel, kc_ref[...], 0.0)
        kc = (kc + pltpu.roll(kc, HEAD_DIM, 1)).astype(BF16)
        vc = jnp.where(sel, vc_ref[...], 0.0)
        vc = (vc + pltpu.roll(vc, HEAD_DIM, 1)).astype(BF16)
        vch = [jnp.where(mh, vc, jnp.zeros_like(vc)) for mh in masks]
    nt = (((1,), (1,)), ((), ()))
    for j in range(q_ref.shape[1] // LANES):
        cols = slice(j * LANES, (j + 1) * LANES)
        kcols = slice((j // pairs_per_kv) * LANES, (j // pairs_per_kv + 1) * LANES)
        k = k_ref[:, kcols]
        v = v_ref[:, kcols]
        vh = [jnp.where(mh, v, jnp.zeros_like(v)) for mh in masks]
        for t in range(q_ref.shape[0] // tq):
            q = q_ref[t * tq:(t + 1) * tq, cols]
            out = None
            for hh, mh in enumerate(masks):
                qm = jnp.where(mh, q, jnp.zeros_like(q))
                pieces = [(lax.dot_general(qm, k, nt, preferred_element_type=F32), vh[hh])]
                if has_ctx:
                    pieces.append((lax.dot_general(qm, kc, nt, preferred_element_type=F32),
                                   vch[hh]))
                o = _softmax_pv(pieces)
                out = o if out is None else out + o
            o_ref[t * tq:(t + 1) * tq, cols] = out.astype(BF16)


def _attention(q, k, v, *, n_batch, seq, row0, pairs_per_step, pairs_per_kv, ctx=None):
    blk0 = row0 // seq
    n_groups = q.shape[1] // (pairs_per_step * LANES)
    qw = pairs_per_step * LANES
    kw = qw // pairs_per_kv
    in_specs = [pl.BlockSpec((seq, qw), lambda b, g: (blk0 + b, g)),
                pl.BlockSpec((seq, kw), lambda b, g: (blk0 + b, g)),
                pl.BlockSpec((seq, kw), lambda b, g: (blk0 + b, g))]
    args = [q, k, v]
    if ctx is not None:
        assert kw == LANES
        kc, vc, e = ctx
        sc = kc.shape[2]
        cspec = pl.BlockSpec((None, None, sc, LANES), lambda b, g: (b, e, 0, 0))
        in_specs += [cspec, cspec]
        args += [kc, vc]
    return pl.pallas_call(
        functools.partial(_attn_kernel, has_ctx=ctx is not None, tq=min(TQ_ATTN, seq),
                          pairs_per_kv=pairs_per_kv),
        grid=(n_batch, n_groups),
        in_specs=in_specs,
        out_specs=pl.BlockSpec((seq, qw), lambda b, g: (b, g)),
        out_shape=jax.ShapeDtypeStruct((n_batch * seq, q.shape[1]), BF16),
        compiler_params=_cparams(2),
        name="attention",
    )(*args)


def _na_row_start(r):
    rows = SEG // GRID_W
    return min(max(r - NA_ROWS // 2, 0), rows - NA_ROWS)


def _na_windows():
    rows = SEG // GRID_W
    windows = []
    for qb in range(rows // NA_QROWS):
        lo = _na_row_start(qb * NA_QROWS)
        hi = _na_row_start(qb * NA_QROWS + NA_QROWS - 1) + NA_ROWS
        n = hi - lo + (hi - lo) % 2
        ws = min(lo, rows - n)
        assert ws <= lo and hi <= ws + n <= rows and n <= NA_KROWS
        windows.append((ws, n))
    return windows


def _na_kernel(q_ref, k_ref, v_ref, kc_ref, vc_ref, r_ref, o_ref, bias_ref, tab_ref, *, windows):
    lane = lax.broadcasted_iota(jnp.int32, (1, LANES), 1)
    low = lane < HEAD_DIM

    @pl.when(pl.program_id(1) == 0)
    def _():
        qcol = lax.broadcasted_iota(jnp.int32, (GRID_W, LANES), 0)
        kcol = lax.broadcasted_iota(jnp.int32, (GRID_W, LANES), 1) & (GRID_W - 1)
        start = jnp.clip(qcol - NA_COLS // 2, 0, GRID_W - NA_COLS)
        inside = (kcol >= start) & (kcol < start + NA_COLS)
        for hh in range(LANES // HEAD_DIM):
            for d in range(2 * NA_ROWS - 1):
                base = jnp.broadcast_to(r_ref[hh, d:d + 1, :], (GRID_W, LANES))
                lo_t = pltpu.roll(base, LANES - (NA_COLS - 1), 1, stride=1, stride_axis=0)
                hi_t = pltpu.roll(base, GRID_W - (NA_COLS - 1), 1, stride=1, stride_axis=0)
                tab_ref[hh, d] = jnp.where(inside, jnp.where(low, lo_t, hi_t) * LOG2E, NEG)
        neg = jnp.full((GRID_W, LANES), NEG, F32)
        for hh in range(LANES // HEAD_DIM):
            for qb, (ws, nrows) in enumerate(windows):
                for i in range(NA_QROWS):
                    r = qb * NA_QROWS + i
                    rs = _na_row_start(r)
                    for jp in range(nrows // 2):
                        kr = ws + 2 * jp
                        ok = [rs <= kr + d < rs + NA_ROWS for d in (0, 1)]
                        if not any(ok):
                            blk = neg
                        else:
                            t0 = tab_ref[hh, kr - r + NA_ROWS - 1] if ok[0] else neg
                            t1 = tab_ref[hh, kr + 1 - r + NA_ROWS - 1] if ok[1] else neg
                            blk = jnp.where(low, t0, t1)
                        bias_ref[hh, qb, i * GRID_W:(i + 1) * GRID_W,
                                 jp * LANES:(jp + 1) * LANES] = blk

    masks = [low, jnp.logical_not(low)]
    kc = kc_ref[...].astype(BF16)
    vc = vc_ref[...].astype(BF16)
    vch = [jnp.where(mh, vc, jnp.zeros_like(vc)) for mh in masks]
    nt = (((1,), (1,)), ((), ()))
    nq = NA_QROWS * GRID_W
    for qb, (ws, nrows) in enumerate(windows):
        nk = nrows * GRID_W
        q = q_ref[qb * nq:(qb + 1) * nq, :]
        kw = k_ref[ws * GRID_W:ws * GRID_W + nk, :]
        vw = v_ref[ws * GRID_W:ws * GRID_W + nk, :]
        out = None
        for hh, mh in enumerate(masks):
            qm = jnp.where(mh, q, jnp.zeros_like(q))
            s_win = (lax.dot_general(qm, kw, nt, preferred_element_type=F32)
                     + bias_ref[hh, qb, :, 0:nk])
            s_ctx = lax.dot_general(qm, kc, nt, preferred_element_type=F32)
            o = _softmax_pv([(s_win, jnp.where(mh, vw, jnp.zeros_like(vw))), (s_ctx, vch[hh])])
            out = o if out is None else out + o
        o_ref[qb * nq:(qb + 1) * nq, :] = out.astype(BF16)


def _na_attention(q, k, v, kc, vc, o_idx, rpb, *, n_batch, row0):
    blk0 = row0 // SEG
    n_pairs = q.shape[1] // LANES
    sc = kc.shape[2]
    windows = _na_windows()
    heads = LANES // HEAD_DIM
    nr = 2 * NA_ROWS - 1
    qkv = pl.BlockSpec((SEG, LANES), lambda p, b: (blk0 + b, p))
    cspec = pl.BlockSpec((None, None, sc, LANES), lambda p, b: (b, o_idx, 0, p))
    rspec = pl.BlockSpec((None, heads, nr, LANES), lambda p, b: (o_idx, p, 0, 0))
    return pl.pallas_call(
        functools.partial(_na_kernel, windows=windows),
        grid=(n_pairs, n_batch),
        in_specs=[qkv, qkv, qkv, cspec, cspec, rspec],
        out_specs=pl.BlockSpec((SEG, LANES), lambda p, b: (b, p)),
        out_shape=jax.ShapeDtypeStruct((n_batch * SEG, n_pairs * LANES), BF16),
        scratch_shapes=[pltpu.VMEM((heads, len(windows), NA_QROWS * GRID_W, NA_KROWS * GRID_W), F32),
                        pltpu.VMEM((heads, nr, GRID_W, LANES), F32)],
        compiler_params=_cparams(2),
        name="na_attention",
    )(q, k, v, kc, vc, rpb)


def _out_kernel(*refs, n_prompt_tiles, split_x, has_a):
    refs = list(refs)
    x = _pick(n_prompt_tiles, refs.pop(0), refs.pop(0)) if split_x else refs.pop(0)[...]
    gt_ref, g_ref, fsh_ref, fsc_ref, gf_ref, w_ref = refs[:6]
    a_ref = refs[6] if has_a else None
    bp_ref, bs_ref, o_ref, h_ref, wbf_ref = refs[6 + has_a:]

    @pl.when(pl.program_id(0) == 0)
    def _():
        wbf_ref[...] = w_ref[...].astype(BF16)

    b = _pick(n_prompt_tiles, bp_ref, bs_ref)
    if has_a:
        half = a_ref.shape[1]
        y = (jnp.dot(a_ref[...], wbf_ref[0:half, :], preferred_element_type=F32)
             + jnp.dot(b, wbf_ref[half:2 * half, :], preferred_element_type=F32))
    else:
        y = jnp.dot(b, wbf_ref[...], preferred_element_type=F32)
    x1 = _gated_residual(x, y, g_ref[...], gt_ref[...])
    o_ref[...] = x1
    h_ref[...] = _mod_norm(x1, gf_ref[...], fsh_ref[...], fsc_ref[...]).astype(BF16)


def _out_proj(xs, a, bp, bs, mods, layer, li, T, g_post, g_ffn_pre, w_out):
    tm = TM_PROJ
    n_prompt_tiles = T // 2 // tm
    split_x = len(xs) == 2
    has_a = a is not None
    row = lambda w: pl.BlockSpec((tm, w), lambda i: (i, 0))
    x_specs = _split_specs(tm, D, n_prompt_tiles) if split_x else [row(D)]
    a_specs = [row(a.shape[1])] if has_a else []
    return pl.pallas_call(
        functools.partial(_out_kernel, n_prompt_tiles=n_prompt_tiles, split_x=split_x, has_a=has_a),
        grid=(T // tm,),
        in_specs=x_specs + [_mod_spec(layer, 2, tm), _const_spec((1, D), layer),
                            _mod_spec(layer, 3, tm), _mod_spec(layer, 4, tm),
                            _const_spec((1, D), layer), _weight_spec((D, D), li)] + a_specs
                 + _split_specs(tm, bp.shape[1], n_prompt_tiles),
        out_specs=[row(D), row(D)],
        out_shape=[jax.ShapeDtypeStruct((T, D), F32), jax.ShapeDtypeStruct((T, D), BF16)],
        scratch_shapes=[pltpu.VMEM((D, D), BF16)],
        compiler_params=_cparams(1),
        name="out_proj",
    )(*xs, mods, g_post, mods, mods, g_ffn_pre, w_out, *([a] if has_a else []), bp, bs)


def _ffn_kernel(x_ref, h_ref, gt_ref, gpost_ref, win_ref, wout_ref, *rest,
                layer, n_prompt_tiles, split_out):
    out_refs = rest[:-9]
    wg_buf, wu_buf, wo_buf, wg_res, wu_res, wo_res, act_buf, acc_ref, sem = rest[-9:]
    nf, _, tf = wg_res.shape
    first_tile = pl.program_id(0) == 0

    def aligned(f):
        return f * tf if isinstance(f, int) else pl.multiple_of(f * tf, tf)

    def in_copies(f, slot):
        col = aligned(f)
        return (pltpu.make_async_copy(win_ref.at[layer, :, pl.ds(col, tf)], wg_buf.at[slot],
                                      sem.at[0, slot]),
                pltpu.make_async_copy(win_ref.at[layer, :, pl.ds(D_FF + col, tf)], wu_buf.at[slot],
                                      sem.at[1, slot]))

    def out_copy(f, slot):
        row = aligned(f)
        return pltpu.make_async_copy(wout_ref.at[layer, pl.ds(row, tf), :], wo_buf.at[slot],
                                     sem.at[2, slot])

    def fetch(f, slot):
        @pl.when(first_tile)
        def _():
            for c in in_copies(f, slot):
                c.wait()
            out_copy(f, slot).wait()

            nxt = min(f + 1, nf - 1) if isinstance(f, int) else jnp.minimum(f + 1, nf - 1)

            @pl.when(jnp.asarray(f + 1 < nf))
            def _():
                for c in in_copies(nxt, 1 - slot):
                    c.start()
                out_copy(nxt, 1 - slot).start()

            wg_res[f] = wg_buf[slot].astype(BF16)
            wu_res[f] = wu_buf[slot].astype(BF16)
            wo_res[f] = wo_buf[slot].astype(BF16)

    def hidden(f, slot):
        h = h_ref[...]
        g = jnp.dot(h, wg_res[f], preferred_element_type=F32)
        u = jnp.dot(h, wu_res[f], preferred_element_type=F32)
        act_buf[slot] = (g * jax.nn.sigmoid(g) * u).astype(BF16)

    def project(f, slot):
        acc_ref[...] += jnp.dot(act_buf[slot], wo_res[f], preferred_element_type=F32)

    def step(f, slot):
        fetch(f, slot)
        hidden(f, slot)
        project(f - 1, 1 - slot)

    @pl.when(first_tile)
    def _():
        for c in in_copies(0, 0):
            c.start()
        out_copy(0, 0).start()

    acc_ref[...] = jnp.zeros_like(acc_ref)
    fetch(0, 0)
    hidden(0, 0)

    assert nf % 2 == 1

    def pair(k, carry):
        step(2 * k + 1, 1)
        step(2 * k + 2, 0)
        return carry

    lax.fori_loop(0, (nf - 1) // 2, pair, 0)
    project(nf - 1, 0)

    def result():
        return _gated_residual(x_ref[...], acc_ref[...], gpost_ref[...], gt_ref[...])

    if split_out:
        i = pl.program_id(0)

        @pl.when(i < n_prompt_tiles)
        def _():
            out_refs[0][...] = result()

        @pl.when(i >= n_prompt_tiles)
        def _():
            out_refs[1][...] = result()
    else:
        out_refs[0][...] = result()


def _ffn(x, h, mods, layer, g_post, w_in, w_out, split_out):
    T = x.shape[0]
    tm, tf = TM_FFN, TF_FFN
    nf = D_FF // tf
    n_prompt_tiles = T // 2 // tm
    xrow = pl.BlockSpec((tm, D), lambda i: (i, 0))
    if split_out:
        out_specs = _split_specs(tm, D, n_prompt_tiles)
        out_shape = [jax.ShapeDtypeStruct((T // 2, D), F32)] * 2
    else:
        out_specs = [xrow]
        out_shape = [jax.ShapeDtypeStruct((T, D), F32)]
    hbm = pl.BlockSpec(memory_space=pl.ANY)
    return pl.pallas_call(
        functools.partial(_ffn_kernel, layer=layer, n_prompt_tiles=n_prompt_tiles,
                          split_out=split_out),
        grid=(T // tm,),
        in_specs=[xrow, xrow, _mod_spec(layer, 5, tm), _const_spec((1, D), layer), hbm, hbm],
        out_specs=out_specs,
        out_shape=out_shape,
        scratch_shapes=[pltpu.VMEM((2, D, tf), F32), pltpu.VMEM((2, D, tf), F32),
                        pltpu.VMEM((2, tf, D), F32), pltpu.VMEM((nf, D, tf), BF16),
                        pltpu.VMEM((nf, D, tf), BF16), pltpu.VMEM((nf, tf, D), BF16),
                        pltpu.VMEM((2, tm, tf), BF16), pltpu.VMEM((tm, D), F32),
                        pltpu.SemaphoreType.DMA((3, 2))],
        compiler_params=_cparams(1),
        name="ffn",
    )(x, h, mods, g_post, w_in, w_out)


def _rope_tables():
    t = jnp.arange(SEG)
    nf = HEAD_DIM // 4
    freqs = ROPE_BASE ** (-jnp.arange(nf, dtype=F32) / nf)

    def cs(pos):
        ang = pos.astype(F32)[:, None] * freqs[None, :]
        return jnp.cos(ang), jnp.sin(ang)

    cr, sr = cs(t // GRID_W)
    cc, sn = cs(t % GRID_W)
    cos = jnp.concatenate([cr, cr, cc, cc], axis=1)
    sin = jnp.concatenate([-sr, sr, -sn, sn], axis=1)
    reps = LANES // HEAD_DIM
    cos = jnp.tile(cos, (1, reps))
    sin = jnp.tile(sin, (1, reps))
    return (jnp.stack([jnp.ones_like(cos), cos]), jnp.stack([jnp.zeros_like(sin), sin]))


def _block_diag_ones(n):
    i = jnp.arange(n) // HEAD_DIM
    return (i[:, None] == i[None, :]).astype(BF16)


def kernel(x_prompt, x_sample, cache_attn_k, cache_attn_v, cache_na_k, cache_na_v, c, c_ctx,
           mod_w, mod_b, norm_mix_pre, norm_mix_post, norm_ffn_pre, norm_ffn_post,
           even_w_in, even_w_out, sgu_w, sgu_b, sgu_norm, q_norm, k_norm,
           odd_w_in, odd_w_out, na_rpb, ffn_w_in, ffn_w_out):
    nb_p, seq_p, _ = x_prompt.shape
    nb_s, seq_s, _ = x_sample.shape
    tp = nb_p * seq_p
    T = 2 * tp
    assert seq_s == SEG and tp == nb_s * seq_s and tp % SEG == 0

    cond = jnp.concatenate([jnp.broadcast_to(c_ctx[None, :], (tp // SEG, D)), c], axis=0)
    mods = _modulation(cond, mod_w, mod_b)

    cos, sin = _rope_tables()
    qw = B_HEADS * HEAD_DIM
    bdq = _block_diag_ones(qw)
    bdk = _block_diag_ones(LANES)
    past = cache_attn_k.shape[2]
    ctx_ak = cache_attn_k.reshape(nb_s, -1, past, B_KV_HEADS * HEAD_DIM)
    ctx_av = cache_attn_v.reshape(nb_s, -1, past, B_KV_HEADS * HEAD_DIM)
    ctx_nk = cache_na_k.reshape(nb_s, -1, past, C_HEADS * HEAD_DIM)
    ctx_nv = cache_na_v.reshape(nb_s, -1, past, C_HEADS * HEAD_DIM)
    g_mix_pre, g_mix_post, g_ffn_pre, g_ffn_post = (
        a.reshape(DEPTH, 1, D) for a in (norm_mix_pre, norm_mix_post, norm_ffn_pre, norm_ffn_post))
    n_even = even_w_in.shape[0]
    sgu_b3 = sgu_b.reshape(n_even, A_GROUPS, CHUNK, 1)
    sgu_n3 = sgu_norm.reshape(n_even, 1, A_WIDTH)
    qn3 = jnp.tile(q_norm, (1, B_HEADS)).reshape(n_even, 1, qw)
    kn3 = jnp.tile(k_norm, (1, B_KV_HEADS)).reshape(n_even, 1, LANES)
    rpb = jnp.pad(na_rpb, ((0, 0), (0, 0), (0, 0), (0, LANES - na_rpb.shape[3])))

    xs = (x_prompt.reshape(tp, D), x_sample.reshape(nb_s * seq_s, D))
    n_odd = odd_w_in.shape[0]
    attn_k, attn_v = [], []
    na_cache = tuple(jnp.zeros((nb_p, n_odd, seq_p, C_HEADS * HEAD_DIM), F32) for _ in range(2))
    for l in range(DEPTH):
        if l % 2 == 0:
            e = l // 2
            a_out, q, kd, vd, kf, vf = _even_in(xs, mods, l, e, T, g_mix_pre, even_w_in, sgu_w,
                                                sgu_b3, sgu_n3, qn3, kn3, bdq, bdk, cos, sin)
            n_pairs = qw // LANES
            pairs_per_kv = n_pairs // B_KV_HEADS
            mix_p = _attention(q, kd, vd, n_batch=nb_p, seq=seq_p, row0=0,
                               pairs_per_step=n_pairs, pairs_per_kv=pairs_per_kv)
            mix_s = _attention(q, kd, vd, n_batch=nb_s, seq=seq_s, row0=tp,
                               pairs_per_step=pairs_per_kv, pairs_per_kv=pairs_per_kv,
                               ctx=(ctx_ak, ctx_av, e))
            attn_k.append(kf[:tp].reshape(nb_p, seq_p, B_KV_HEADS, HEAD_DIM))
            attn_v.append(vf[:tp].reshape(nb_p, seq_p, B_KV_HEADS, HEAD_DIM))
            w_out, li = even_w_out, e
        else:
            o = l // 2
            a_out = None
            q, k, v, *na_cache = _odd_in(xs[0], mods, l, o, n_odd, seq_p, g_mix_pre, odd_w_in,
                                         tuple(na_cache))
            mix_p = _attention(q, k, v, n_batch=nb_p, seq=seq_p, row0=0,
                               pairs_per_step=PAIRS_PER_STEP, pairs_per_kv=1)
            mix_s = _na_attention(q, k, v, ctx_nk, ctx_nv, o, rpb, n_batch=nb_s, row0=tp)
            w_out, li = odd_w_out, o
        x, h = _out_proj(xs, a_out, mix_p, mix_s, mods, l, li, T, g_mix_post, g_ffn_pre, w_out)
        xs = tuple(_ffn(x, h, mods, l, g_ffn_post, ffn_w_in, ffn_w_out,
                        split_out=(l == DEPTH - 1)))

    y_prompt = xs[0].reshape(nb_p, seq_p, D)
    y_sample = xs[1].reshape(nb_s, seq_s, D)
    new_na_k, new_na_v = (a.reshape(nb_p, n_odd, seq_p, C_HEADS, HEAD_DIM) for a in na_cache)
    return (y_prompt, y_sample, jnp.stack(attn_k, axis=1), jnp.stack(attn_v, axis=1),
            new_na_k, new_na_v)
```

```python
import functools

import jax
import jax.numpy as jnp
from jax import lax
from jax.experimental import pallas as pl
from jax.experimental.pallas import tpu as pltpu

F32 = jnp.float32
BF16 = jnp.bfloat16

D = 1024
DEPTH = 4
HEAD_DIM = 64
GRID_W = 64
CHUNK = 128
A_WIDTH = D // 2
A_GROUPS = 4
B_HEADS = 8
B_KV_HEADS = 2
C_HEADS = 16
NA_ROWS = 8
NA_COLS = 16
D_FF = 2816
ROPE_BASE = 10000.0
EPS = 1e-6
NEG = -1e30
SEG = 1024
N_SEG = 8
LANES = 128
LOG2E = 1.4426950408889634
QSCALE = HEAD_DIM ** -0.5 * LOG2E

TM_PROJ = 512
TM_FFN = 1024
TF_FFN = 256
TQ_ATTN = 256
PAIRS_PER_STEP = 4
OUT_BLOCKS = 4
NA_QROWS = 4
NA_KROWS = 12
VMEM_LIMIT = 56 * 1024 * 1024


def _cparams(n_axes):
    return pltpu.CompilerParams(dimension_semantics=("arbitrary",) * n_axes,
                                vmem_limit_bytes=VMEM_LIMIT)


def _rms(x):
    return x * lax.rsqrt(jnp.mean(x * x, axis=-1, keepdims=True) + EPS)


def _mod_norm(x, g, shift, scale):
    return _rms(x) * (g * (1.0 + scale)) + shift


def _gated_residual(x, y, g, gate):
    return x + _rms(y) * (gate * g)


def _const_spec(shape, *lead):
    block = (None,) * len(lead) + tuple(shape)
    return pl.BlockSpec(block, lambda *_: tuple(lead) + (0,) * len(shape))


def _weight_spec(shape, *lead):
    block = (None,) * len(lead) + tuple(shape)
    return pl.BlockSpec(block, lambda *_: tuple(lead) + (0,) * len(shape),
                        pipeline_mode=pl.Buffered(1))


def _same_tile(i):
    return i


def _mod_spec(layer, j, tm, tile=_same_tile):
    return pl.BlockSpec((None, None, None, 1, D),
                        lambda i, *_: (layer, j, (tile(i) * tm) // SEG, 0, 0))


def _split_specs(tm, width, n_prompt_tiles, col=0, tile=_same_tile):
    return [pl.BlockSpec((tm, width),
                         lambda i, *_: (jnp.minimum(tile(i), n_prompt_tiles - 1), col)),
            pl.BlockSpec((tm, width),
                         lambda i, *_: (jnp.maximum(tile(i) - n_prompt_tiles, 0), col))]


def _pick(n_prompt_tiles, p_ref, s_ref, tile=None):
    tile = pl.program_id(0) if tile is None else tile
    return jnp.where(tile < n_prompt_tiles, p_ref[...], s_ref[...])


def _mod_kernel(cond_ref, w_ref, b_ref, o_ref):
    s = cond_ref[...]
    s = s * jax.nn.sigmoid(s)
    o_ref[...] = jnp.dot(s.astype(BF16), w_ref[...].astype(BF16),
                         preferred_element_type=F32) + b_ref[...]


def _modulation(cond, mod_w, mod_b):
    b = mod_b.reshape(DEPTH, 6, 1, D)
    out = pl.pallas_call(
        _mod_kernel,
        grid=(DEPTH, 6),
        in_specs=[pl.BlockSpec((N_SEG, D), lambda l, j: (0, 0)),
                  pl.BlockSpec((None, D, D), lambda l, j: (l, 0, j)),
                  pl.BlockSpec((None, None, 1, D), lambda l, j: (l, j, 0, 0))],
        out_specs=pl.BlockSpec((None, None, N_SEG, D), lambda l, j: (l, j, 0, 0)),
        out_shape=jax.ShapeDtypeStruct((DEPTH, 6, N_SEG, D), F32),
        compiler_params=_cparams(2),
        name="modulation",
    )(cond, mod_w, b)
    return out.reshape(DEPTH, 6, N_SEG, 1, D)


def _head_sumsq(y, bd_ref):
    sq = y * y
    hi = sq.astype(BF16)
    lo = (sq - hi.astype(F32)).astype(BF16)
    bd = bd_ref[...]
    return (jnp.dot(hi, bd, preferred_element_type=F32)
            + jnp.dot(lo, bd, preferred_element_type=F32))


def _rope(y, cos, sin):
    lane = lax.broadcasted_iota(jnp.int32, (1, LANES), 1)
    first = (lane & 16) == 0
    partner = jnp.where(first, pltpu.roll(y, LANES - 16, 1), pltpu.roll(y, 16, 1))
    return y * cos + partner * sin


def _even_in_kernel(*refs, n_prompt_tiles, split_x):
    refs = list(refs)
    x = _pick(n_prompt_tiles, refs.pop(0), refs.pop(0)) if split_x else refs.pop(0)[...]
    (sh_ref, sc_ref, g_ref, w_ref, sguw_ref, sgub_ref, sgun_ref, qn_ref, kn_ref, bdq_ref, bdk_ref,
     cos_ref, sin_ref, a_ref, q_ref, kd_ref, vd_ref, kf_ref, vf_ref, wbf_ref) = refs
    tm = a_ref.shape[0]

    @pl.when(pl.program_id(0) == 0)
    def _():
        wbf_ref[...] = w_ref[...].astype(BF16)

    h = _mod_norm(x, g_ref[...], sh_ref[...], sc_ref[...]).astype(BF16)

    u = jax.nn.gelu(jnp.dot(h, wbf_ref[:, 0:A_WIDTH], preferred_element_type=F32), approximate=True)
    v = jax.nn.gelu(jnp.dot(h, wbf_ref[:, A_WIDTH:2 * A_WIDTH], preferred_element_type=F32),
                    approximate=True)
    mu = jnp.mean(v, axis=-1, keepdims=True)
    vc = v - mu
    var = jnp.mean(vc * vc, axis=-1, keepdims=True)
    vn = (vc * lax.rsqrt(var + EPS) * sgun_ref[...]).astype(BF16)
    n_chunks = tm // CHUNK
    gch = A_WIDTH // A_GROUPS
    for g in range(A_GROUPS):
        rhs = jnp.concatenate([vn[n * CHUNK:(n + 1) * CHUNK, g * gch:(g + 1) * gch]
                               for n in range(n_chunks)], axis=1)
        mixed = jnp.dot(sguw_ref[g].astype(BF16), rhs, preferred_element_type=F32)
        bias = sgub_ref[g]
        for n in range(n_chunks):
            blk = (mixed[:, n * gch:(n + 1) * gch] + bias) * u[n * CHUNK:(n + 1) * CHUNK,
                                                              g * gch:(g + 1) * gch]
            a_ref[n * CHUNK:(n + 1) * CHUNK, g * gch:(g + 1) * gch] = blk.astype(BF16)

    c0 = 2 * A_WIDTH
    qw = B_HEADS * HEAD_DIM
    q = jnp.dot(h, wbf_ref[:, c0:c0 + qw], preferred_element_type=F32)
    q = q * lax.rsqrt(_head_sumsq(q, bdq_ref) * (1.0 / HEAD_DIM) + EPS) * qn_ref[...]
    cos = cos_ref[...]
    sin = sin_ref[...]
    for j in range(qw // LANES):
        qj = _rope(q[:, j * LANES:(j + 1) * LANES], cos, sin) * QSCALE
        q_ref[:, j * LANES:(j + 1) * LANES] = qj.astype(BF16)

    kw = B_KV_HEADS * HEAD_DIM
    k = jnp.dot(h, wbf_ref[:, c0 + qw:c0 + qw + kw], preferred_element_type=F32)
    k = k * lax.rsqrt(_head_sumsq(k, bdk_ref) * (1.0 / HEAD_DIM) + EPS) * kn_ref[...]
    kf_ref[...] = k
    vals = jnp.dot(h, wbf_ref[:, c0 + qw + kw:c0 + qw + 2 * kw], preferred_element_type=F32)
    vf_ref[...] = vals
    kr = _rope(k, cos, sin)
    lane = lax.broadcasted_iota(jnp.int32, (1, LANES), 1)
    low = lane < HEAD_DIM
    for src, dst in ((kr, kd_ref), (vals, vd_ref)):
        sw = pltpu.roll(src, HEAD_DIM, 1)
        dst[:, 0:LANES] = jnp.where(low, src, sw).astype(BF16)
        dst[:, LANES:2 * LANES] = jnp.where(low, sw, src).astype(BF16)


def _even_in(xs, mods, layer, e, T, g_pre, w_in, sgu_w, sgu_b, sgu_norm, q_norm, k_norm,
             bdq, bdk, cos, sin):
    tm = TM_PROJ
    n_in = w_in.shape[2]
    tiles_per_seg = SEG // tm
    n_prompt_tiles = T // 2 // tm
    split_x = len(xs) == 2
    tab_spec = pl.BlockSpec((None, tm, LANES),
                            lambda i: (i // n_prompt_tiles, i % tiles_per_seg, 0))
    row = lambda w: pl.BlockSpec((tm, w), lambda i: (i, 0))
    x_specs = _split_specs(tm, D, n_prompt_tiles) if split_x else [row(D)]
    qw = B_HEADS * HEAD_DIM
    return pl.pallas_call(
        functools.partial(_even_in_kernel, n_prompt_tiles=n_prompt_tiles, split_x=split_x),
        grid=(T // tm,),
        in_specs=x_specs + [
            _mod_spec(layer, 0, tm), _mod_spec(layer, 1, tm), _const_spec((1, D), layer),
            _weight_spec((D, n_in), e), _const_spec((A_GROUPS, CHUNK, CHUNK), e),
            _const_spec((A_GROUPS, CHUNK, 1), e), _const_spec((1, A_WIDTH), e),
            _const_spec((1, qw), e), _const_spec((1, LANES), e),
            _const_spec((qw, qw)), _const_spec((LANES, LANES)), tab_spec, tab_spec],
        out_specs=[row(A_WIDTH), row(qw), row(2 * LANES), row(2 * LANES), row(LANES), row(LANES)],
        out_shape=[jax.ShapeDtypeStruct((T, A_WIDTH), BF16),
                   jax.ShapeDtypeStruct((T, qw), BF16),
                   jax.ShapeDtypeStruct((T, 2 * LANES), BF16),
                   jax.ShapeDtypeStruct((T, 2 * LANES), BF16),
                   jax.ShapeDtypeStruct((T, LANES), F32),
                   jax.ShapeDtypeStruct((T, LANES), F32)],
        scratch_shapes=[pltpu.VMEM((D, n_in), BF16)],
        compiler_params=_cparams(1),
        name="even_in_proj",
    )(*xs, mods, mods, g_pre, w_in, sgu_w, sgu_b, sgu_norm, q_norm, k_norm, bdq, bdk, cos, sin)


def _odd_in_kernel(x_ref, sh_ref, sc_ref, g_ref, w_ref, *rest, n_prompt_tiles):
    q_ref, k_ref, v_ref, kf_ref, vf_ref, wbf_ref = rest[-6:]
    i = pl.program_id(0)
    hw = C_HEADS * HEAD_DIM

    @pl.when(i == 0)
    def _():
        for c in range(3):
            wbf_ref[:, c * hw:(c + 1) * hw] = w_ref[:, c * hw:(c + 1) * hw].astype(BF16)

    h = _mod_norm(x_ref[...], g_ref[...], sh_ref[...], sc_ref[...]).astype(BF16)
    q = jnp.dot(h, wbf_ref[:, 0:hw], preferred_element_type=F32)
    q_ref[...] = (q * QSCALE).astype(BF16)
    k = jnp.dot(h, wbf_ref[:, hw:2 * hw], preferred_element_type=F32)
    k_ref[...] = k.astype(BF16)
    v = jnp.dot(h, wbf_ref[:, 2 * hw:3 * hw], preferred_element_type=F32)
    v_ref[...] = v.astype(BF16)

    @pl.when(i < n_prompt_tiles)
    def _():
        kf_ref[...] = k.reshape(kf_ref.shape)
        vf_ref[...] = v.reshape(vf_ref.shape)


def _odd_in(x, mods, layer, o, n_odd, seq_p, g_pre, w_in, caches):
    T = x.shape[0]
    tm = TM_PROJ
    hw = C_HEADS * HEAD_DIM
    n_prompt_tiles = T // 2 // tm
    bt = tm // seq_p
    row = pl.BlockSpec((tm, hw), lambda i: (i, 0))
    crow = pl.BlockSpec((bt, None, seq_p, hw),
                        lambda i: (jnp.minimum(i, n_prompt_tiles - 1), o, 0, 0))
    cshape = jax.ShapeDtypeStruct((T // 2 // seq_p, n_odd, seq_p, hw), F32)
    n_in = 5
    return pl.pallas_call(
        functools.partial(_odd_in_kernel, n_prompt_tiles=n_prompt_tiles),
        grid=(T // tm,),
        in_specs=[pl.BlockSpec((tm, D), lambda i: (i, 0)), _mod_spec(layer, 0, tm),
                  _mod_spec(layer, 1, tm), _const_spec((1, D), layer),
                  _weight_spec((D, 3 * hw), o)]
                 + [pl.BlockSpec(memory_space=pl.ANY)] * len(caches),
        out_specs=[row, row, row, crow, crow],
        out_shape=[jax.ShapeDtypeStruct((T, hw), BF16)] * 3 + [cshape, cshape],
        input_output_aliases={n_in + j: 3 + j for j in range(len(caches))},
        scratch_shapes=[pltpu.VMEM((D, 3 * hw), BF16)],
        compiler_params=_cparams(1),
        name="odd_in_proj",
    )(x, mods, mods, g_pre, w_in, *caches)


def _softmax_pv(pieces):
    m = None
    for s, _ in pieces:
        ms = jnp.max(s, axis=-1, keepdims=True)
        m = ms if m is None else jnp.maximum(m, ms)
    den = None
    acc = None
    for s, val in pieces:
        p = jnp.exp2(s - m)
        ls = jnp.sum(p, axis=-1, keepdims=True)
        den = ls if den is None else den + ls
        o = jnp.dot(p.astype(BF16), val, preferred_element_type=F32)
        acc = o if acc is None else acc + o
    return acc / den


def _attn_kernel(*refs, has_ctx, tq, pairs_per_kv):
    if has_ctx:
        q_ref, k_ref, v_ref, kc_ref, vc_ref, o_ref = refs
    else:
        q_ref, k_ref, v_ref, o_ref = refs
    lane = lax.broadcasted_iota(jnp.int32, (1, LANES), 1)
    masks = [lane < HEAD_DIM, lane >= HEAD_DIM]
    if has_ctx:
        kv_head = pl.program_id(1)
        sel = jnp.where(masks[0], 0, 1) == kv_head
        kc = jnp.where(sel, kc_ref[...], 0.0)
        kc = (kc + pltpu.roll(kc, HEAD_DIM, 1)).astype(BF16)
        vc = jnp.where(sel, vc_ref[...], 0.0)
        vc = (vc + pltpu.roll(vc, HEAD_DIM, 1)).astype(BF16)
        vch = [jnp.where(mh, vc, jnp.zeros_like(vc)) for mh in masks]
    nt = (((1,), (1,)), ((), ()))
    for j in range(q_ref.shape[1] // LANES):
        cols = slice(j * LANES, (j + 1) * LANES)
        kcols = slice((j // pairs_per_kv) * LANES, (j // pairs_per_kv + 1) * LANES)
        k = k_ref[:, kcols]
        v = v_ref[:, kcols]
        vh = [jnp.where(mh, v, jnp.zeros_like(v)) for mh in masks]
        for t in range(q_ref.shape[0] // tq):
            q = q_ref[t * tq:(t + 1) * tq, cols]
            out = None
            for hh, mh in enumerate(masks):
                qm = jnp.where(mh, q, jnp.zeros_like(q))
                pieces = [(lax.dot_general(qm, k, nt, preferred_element_type=F32), vh[hh])]
                if has_ctx:
                    pieces.append((lax.dot_general(qm, kc, nt, preferred_element_type=F32),
                                   vch[hh]))
                o = _softmax_pv(pieces)
                out = o if out is None else out + o
            o_ref[t * tq:(t + 1) * tq, cols] = out.astype(BF16)


def _attention(q, k, v, *, n_batch, seq, row0, pairs_per_step, pairs_per_kv, ctx=None):
    blk0 = row0 // seq
    n_groups = q.shape[1] // (pairs_per_step * LANES)
    qw = pairs_per_step * LANES
    kw = qw // pairs_per_kv
    in_specs = [pl.BlockSpec((seq, qw), lambda b, g: (blk0 + b, g)),
                pl.BlockSpec((seq, kw), lambda b, g: (blk0 + b, g)),
                pl.BlockSpec((seq, kw), lambda b, g: (blk0 + b, g))]
    args = [q, k, v]
    if ctx is not None:
        assert kw == LANES
        kc, vc, e = ctx
        sc = kc.shape[2]
        cspec = pl.BlockSpec((None, None, sc, LANES), lambda b, g: (b, e, 0, 0))
        in_specs += [cspec, cspec]
        args += [kc, vc]
    return pl.pallas_call(
        functools.partial(_attn_kernel, has_ctx=ctx is not None, tq=min(TQ_ATTN, seq),
                          pairs_per_kv=pairs_per_kv),
        grid=(n_batch, n_groups),
        in_specs=in_specs,
        out_specs=pl.BlockSpec((seq, qw), lambda b, g: (b, g)),
        out_shape=jax.ShapeDtypeStruct((n_batch * seq, q.shape[1]), BF16),
        compiler_params=_cparams(2),
        name="attention",
    )(*args)


def _na_row_start(r):
    rows = SEG // GRID_W
    return min(max(r - NA_ROWS // 2, 0), rows - NA_ROWS)


def _na_windows():
    rows = SEG // GRID_W
    windows = []
    for qb in range(rows // NA_QROWS):
        lo = _na_row_start(qb * NA_QROWS)
        hi = _na_row_start(qb * NA_QROWS + NA_QROWS - 1) + NA_ROWS
        n = hi - lo + (hi - lo) % 2
        ws = min(lo, rows - n)
        assert ws <= lo and hi <= ws + n <= rows and n <= NA_KROWS
        windows.append((ws, n))
    return windows


def _na_kernel(q_ref, k_ref, v_ref, kc_ref, vc_ref, r_ref, o_ref, bias_ref, tab_ref, *, windows):
    lane = lax.broadcasted_iota(jnp.int32, (1, LANES), 1)
    low = lane < HEAD_DIM

    @pl.when(pl.program_id(1) == 0)
    def _():
        qcol = lax.broadcasted_iota(jnp.int32, (GRID_W, LANES), 0)
        kcol = lax.broadcasted_iota(jnp.int32, (GRID_W, LANES), 1) & (GRID_W - 1)
        start = jnp.clip(qcol - NA_COLS // 2, 0, GRID_W - NA_COLS)
        inside = (kcol >= start) & (kcol < start + NA_COLS)
        for hh in range(LANES // HEAD_DIM):
            for d in range(2 * NA_ROWS - 1):
                base = jnp.broadcast_to(r_ref[hh, d:d + 1, :], (GRID_W, LANES))
                lo_t = pltpu.roll(base, LANES - (NA_COLS - 1), 1, stride=1, stride_axis=0)
                hi_t = pltpu.roll(base, GRID_W - (NA_COLS - 1), 1, stride=1, stride_axis=0)
                tab_ref[hh, d] = jnp.where(inside, jnp.where(low, lo_t, hi_t) * LOG2E, NEG)
        neg = jnp.full((GRID_W, LANES), NEG, F32)
        for hh in range(LANES // HEAD_DIM):
            for qb, (ws, nrows) in enumerate(windows):
                for i in range(NA_QROWS):
                    r = qb * NA_QROWS + i
                    rs = _na_row_start(r)
                    for jp in range(nrows // 2):
                        kr = ws + 2 * jp
                        ok = [rs <= kr + d < rs + NA_ROWS for d in (0, 1)]
                        if not any(ok):
                            blk = neg
                        else:
                            t0 = tab_ref[hh, kr - r + NA_ROWS - 1] if ok[0] else neg
                            t1 = tab_ref[hh, kr + 1 - r + NA_ROWS - 1] if ok[1] else neg
                            blk = jnp.where(low, t0, t1)
                        bias_ref[hh, qb, i * GRID_W:(i + 1) * GRID_W,
                                 jp * LANES:(jp + 1) * LANES] = blk

    masks = [low, jnp.logical_not(low)]
    kc = kc_ref[...].astype(BF16)
    vc = vc_ref[...].astype(BF16)
    vch = [jnp.where(mh, vc, jnp.zeros_like(vc)) for mh in masks]
    nt = (((1,), (1,)), ((), ()))
    nq = NA_QROWS * GRID_W
    for qb, (ws, nrows) in enumerate(windows):
        nk = nrows * GRID_W
        q = q_ref[qb * nq:(qb + 1) * nq, :]
        kw = k_ref[ws * GRID_W:ws * GRID_W + nk, :]
        vw = v_ref[ws * GRID_W:ws * GRID_W + nk, :]
        out = None
        for hh, mh in enumerate(masks):
            qm = jnp.where(mh, q, jnp.zeros_like(q))
            s_win = (lax.dot_general(qm, kw, nt, preferred_element_type=F32)
                     + bias_ref[hh, qb, :, 0:nk])
            s_ctx = lax.dot_general(qm, kc, nt, preferred_element_type=F32)
            o = _softmax_pv([(s_win, jnp.where(mh, vw, jnp.zeros_like(vw))), (s_ctx, vch[hh])])
            out = o if out is None else out + o
        o_ref[qb * nq:(qb + 1) * nq, :] = out.astype(BF16)


def _na_attention(q, k, v, kc, vc, o_idx, rpb, *, n_batch, row0):
    blk0 = row0 // SEG
    n_pairs = q.shape[1] // LANES
    sc = kc.shape[2]
    windows = _na_windows()
    heads = LANES // HEAD_DIM
    nr = 2 * NA_ROWS - 1
    qkv = pl.BlockSpec((SEG, LANES), lambda p, b: (blk0 + b, p))
    cspec = pl.BlockSpec((None, None, sc, LANES), lambda p, b: (b, o_idx, 0, p))
    rspec = pl.BlockSpec((None, heads, nr, LANES), lambda p, b: (o_idx, p, 0, 0))
    return pl.pallas_call(
        functools.partial(_na_kernel, windows=windows),
        grid=(n_pairs, n_batch),
        in_specs=[qkv, qkv, qkv, cspec, cspec, rspec],
        out_specs=pl.BlockSpec((SEG, LANES), lambda p, b: (b, p)),
        out_shape=jax.ShapeDtypeStruct((n_batch * SEG, n_pairs * LANES), BF16),
        scratch_shapes=[pltpu.VMEM((heads, len(windows), NA_QROWS * GRID_W, NA_KROWS * GRID_W), F32),
                        pltpu.VMEM((heads, nr, GRID_W, LANES), F32)],
        compiler_params=_cparams(2),
        name="na_attention",
    )(q, k, v, kc, vc, rpb)


def _out_kernel(*refs, n_tiles, n_prompt_tiles, split_x, has_a):
    refs = list(refs)
    i = pl.program_id(0)
    t_proj = jnp.minimum(i, n_tiles - 1)
    t_fin = jnp.maximum(i - 1, 0)
    x_refs = [refs.pop(0), refs.pop(0)] if split_x else [refs.pop(0)]
    gt_ref, g_ref, fsh_ref, fsc_ref, gf_ref, w_ref = refs[:6]
    a_ref = refs[6] if has_a else None
    bp_ref, bs_ref, o_ref, h_ref, wbf_ref, b_ref, y0_ref, y1_ref = refs[6 + has_a:]
    nb, _, cw = wbf_ref.shape
    rows_per = o_ref.shape[0] // nb

    @pl.when(i == 0)
    def _():
        for c in range(nb):
            wbf_ref[c] = w_ref[:, c * cw:(c + 1) * cw].astype(BF16)
        y1_ref[...] = jnp.zeros_like(y1_ref)

    b_ref[...] = _pick(n_prompt_tiles, bp_ref, bs_ref, t_proj)

    def step(y_new_ref, y_prev_ref):
        def body(c, carry):
            if has_a:
                half = a_ref.shape[1]
                y = (jnp.dot(a_ref[...], wbf_ref[c, 0:half, :], preferred_element_type=F32)
                     + jnp.dot(b_ref[...], wbf_ref[c, half:2 * half, :],
                               preferred_element_type=F32))
            else:
                y = jnp.dot(b_ref[...], wbf_ref[c], preferred_element_type=F32)
            y_new_ref[c] = y

            rows = pl.ds(pl.multiple_of(c * rows_per, rows_per), rows_per)
            y_prev = jnp.concatenate([y_prev_ref[cb, rows, :] for cb in range(nb)], axis=1)
            if split_x:
                x = jnp.where(t_fin < n_prompt_tiles, x_refs[0][rows, :], x_refs[1][rows, :])
            else:
                x = x_refs[0][rows, :]
            x1 = _gated_residual(x, y_prev, g_ref[...], gt_ref[...])
            o_ref[rows, :] = x1
            h_ref[rows, :] = _mod_norm(x1, gf_ref[...], fsh_ref[...], fsc_ref[...]).astype(BF16)
            return carry

        lax.fori_loop(0, nb, body, 0)

    @pl.when(i % 2 == 0)
    def _():
        step(y0_ref, y1_ref)

    @pl.when(i % 2 == 1)
    def _():
        step(y1_ref, y0_ref)


def _out_proj(xs, a, bp, bs, mods, layer, li, T, g_post, g_ffn_pre, w_out):
    tm = TM_PROJ
    nb = OUT_BLOCKS
    n_tiles = T // tm
    n_prompt_tiles = n_tiles // 2
    split_x = len(xs) == 2
    has_a = a is not None
    proj = lambda i: jnp.minimum(i, n_tiles - 1)
    fin = lambda i: jnp.maximum(i - 1, 0)
    row = lambda w, tile: pl.BlockSpec((tm, w), lambda i: (tile(i), 0))
    x_specs = _split_specs(tm, D, n_prompt_tiles, tile=fin) if split_x else [row(D, fin)]
    a_specs = [row(a.shape[1], proj)] if has_a else []
    return pl.pallas_call(
        functools.partial(_out_kernel, n_tiles=n_tiles, n_prompt_tiles=n_prompt_tiles,
                          split_x=split_x, has_a=has_a),
        grid=(n_tiles + 1,),
        in_specs=x_specs + [_mod_spec(layer, 2, tm, fin), _const_spec((1, D), layer),
                            _mod_spec(layer, 3, tm, fin), _mod_spec(layer, 4, tm, fin),
                            _const_spec((1, D), layer), _weight_spec((D, D), li)] + a_specs
                 + _split_specs(tm, bp.shape[1], n_prompt_tiles, tile=proj),
        out_specs=[row(D, fin), row(D, fin)],
        out_shape=[jax.ShapeDtypeStruct((T, D), F32), jax.ShapeDtypeStruct((T, D), BF16)],
        scratch_shapes=[pltpu.VMEM((nb, D, D // nb), BF16), pltpu.VMEM((tm, bp.shape[1]), BF16),
                        pltpu.VMEM((nb, tm, D // nb), F32), pltpu.VMEM((nb, tm, D // nb), F32)],
        compiler_params=_cparams(1),
        name="out_proj",
    )(*xs, mods, g_post, mods, mods, g_ffn_pre, w_out, *([a] if has_a else []), bp, bs)


def _ffn_kernel(x_ref, h_ref, gt_ref, gpost_ref, win_ref, wout_ref, *rest,
                layer, n_prompt_tiles, split_out):
    out_refs = rest[:-9]
    wg_buf, wu_buf, wo_buf, wg_res, wu_res, wo_res, act_buf, acc_ref, sem = rest[-9:]
    nf, _, tf = wg_res.shape
    first_tile = pl.program_id(0) == 0

    def aligned(f):
        return f * tf if isinstance(f, int) else pl.multiple_of(f * tf, tf)

    def in_copies(f, slot):
        col = aligned(f)
        return (pltpu.make_async_copy(win_ref.at[layer, :, pl.ds(col, tf)], wg_buf.at[slot],
                                      sem.at[0, slot]),
                pltpu.make_async_copy(win_ref.at[layer, :, pl.ds(D_FF + col, tf)], wu_buf.at[slot],
                                      sem.at[1, slot]))

    def out_copy(f, slot):
        row = aligned(f)
        return pltpu.make_async_copy(wout_ref.at[layer, pl.ds(row, tf), :], wo_buf.at[slot],
                                     sem.at[2, slot])

    def fetch(f, slot):
        @pl.when(first_tile)
        def _():
            for c in in_copies(f, slot):
                c.wait()
            out_copy(f, slot).wait()

            nxt = min(f + 1, nf - 1) if isinstance(f, int) else jnp.minimum(f + 1, nf - 1)

            @pl.when(jnp.asarray(f + 1 < nf))
            def _():
                for c in in_copies(nxt, 1 - slot):
                    c.start()
                out_copy(nxt, 1 - slot).start()

            wg_res[f] = wg_buf[slot].astype(BF16)
            wu_res[f] = wu_buf[slot].astype(BF16)
            wo_res[f] = wo_buf[slot].astype(BF16)

    def hidden(f, slot):
        h = h_ref[...]
        g = jnp.dot(h, wg_res[f], preferred_element_type=F32)
        u = jnp.dot(h, wu_res[f], preferred_element_type=F32)
        act_buf[slot] = (g * jax.nn.sigmoid(g) * u).astype(BF16)

    def project(f, slot):
        acc_ref[...] += jnp.dot(act_buf[slot], wo_res[f], preferred_element_type=F32)

    def step(f, slot):
        fetch(f, slot)
        hidden(f, slot)
        project(f - 1, 1 - slot)

    @pl.when(first_tile)
    def _():
        for c in in_copies(0, 0):
            c.start()
        out_copy(0, 0).start()

    acc_ref[...] = jnp.zeros_like(acc_ref)
    fetch(0, 0)
    hidden(0, 0)

    assert nf % 2 == 1

    def pair(k, carry):
        step(2 * k + 1, 1)
        step(2 * k + 2, 0)
        return carry

    lax.fori_loop(0, (nf - 1) // 2, pair, 0)
    project(nf - 1, 0)

    def result():
        return _gated_residual(x_ref[...], acc_ref[...], gpost_ref[...], gt_ref[...])

    if split_out:
        i = pl.program_id(0)

        @pl.when(i < n_prompt_tiles)
        def _():
            out_refs[0][...] = result()

        @pl.when(i >= n_prompt_tiles)
        def _():
            out_refs[1][...] = result()
    else:
        out_refs[0][...] = result()


def _ffn(x, h, mods, layer, g_post, w_in, w_out, split_out):
    T = x.shape[0]
    tm, tf = TM_FFN, TF_FFN
    nf = D_FF // tf
    n_prompt_tiles = T // 2 // tm
    xrow = pl.BlockSpec((tm, D), lambda i: (i, 0))
    if split_out:
        out_specs = _split_specs(tm, D, n_prompt_tiles)
        out_shape = [jax.ShapeDtypeStruct((T // 2, D), F32)] * 2
    else:
        out_specs = [xrow]
        out_shape = [jax.ShapeDtypeStruct((T, D), F32)]
    hbm = pl.BlockSpec(memory_space=pl.ANY)
    return pl.pallas_call(
        functools.partial(_ffn_kernel, layer=layer, n_prompt_tiles=n_prompt_tiles,
                          split_out=split_out),
        grid=(T // tm,),
        in_specs=[xrow, xrow, _mod_spec(layer, 5, tm), _const_spec((1, D), layer), hbm, hbm],
        out_specs=out_specs,
        out_shape=out_shape,
        scratch_shapes=[pltpu.VMEM((2, D, tf), F32), pltpu.VMEM((2, D, tf), F32),
                        pltpu.VMEM((2, tf, D), F32), pltpu.VMEM((nf, D, tf), BF16),
                        pltpu.VMEM((nf, D, tf), BF16), pltpu.VMEM((nf, tf, D), BF16),
                        pltpu.VMEM((2, tm, tf), BF16), pltpu.VMEM((tm, D), F32),
                        pltpu.SemaphoreType.DMA((3, 2))],
        compiler_params=_cparams(1),
        name="ffn",
    )(x, h, mods, g_post, w_in, w_out)


def _rope_tables():
    t = jnp.arange(SEG)
    nf = HEAD_DIM // 4
    freqs = ROPE_BASE ** (-jnp.arange(nf, dtype=F32) / nf)

    def cs(pos):
        ang = pos.astype(F32)[:, None] * freqs[None, :]
        return jnp.cos(ang), jnp.sin(ang)

    cr, sr = cs(t // GRID_W)
    cc, sn = cs(t % GRID_W)
    cos = jnp.concatenate([cr, cr, cc, cc], axis=1)
    sin = jnp.concatenate([-sr, sr, -sn, sn], axis=1)
    reps = LANES // HEAD_DIM
    cos = jnp.tile(cos, (1, reps))
    sin = jnp.tile(sin, (1, reps))
    return (jnp.stack([jnp.ones_like(cos), cos]), jnp.stack([jnp.zeros_like(sin), sin]))


def _block_diag_ones(n):
    i = jnp.arange(n) // HEAD_DIM
    return (i[:, None] == i[None, :]).astype(BF16)


def kernel(x_prompt, x_sample, cache_attn_k, cache_attn_v, cache_na_k, cache_na_v, c, c_ctx,
           mod_w, mod_b, norm_mix_pre, norm_mix_post, norm_ffn_pre, norm_ffn_post,
           even_w_in, even_w_out, sgu_w, sgu_b, sgu_norm, q_norm, k_norm,
           odd_w_in, odd_w_out, na_rpb, ffn_w_in, ffn_w_out):
    nb_p, seq_p, _ = x_prompt.shape
    nb_s, seq_s, _ = x_sample.shape
    tp = nb_p * seq_p
    T = 2 * tp
    assert seq_s == SEG and tp == nb_s * seq_s and tp % SEG == 0

    cond = jnp.concatenate([jnp.broadcast_to(c_ctx[None, :], (tp // SEG, D)), c], axis=0)
    mods = _modulation(cond, mod_w, mod_b)

    cos, sin = _rope_tables()
    qw = B_HEADS * HEAD_DIM
    bdq = _block_diag_ones(qw)
    bdk = _block_diag_ones(LANES)
    past = cache_attn_k.shape[2]
    ctx_ak = cache_attn_k.reshape(nb_s, -1, past, B_KV_HEADS * HEAD_DIM)
    ctx_av = cache_attn_v.reshape(nb_s, -1, past, B_KV_HEADS * HEAD_DIM)
    ctx_nk = cache_na_k.reshape(nb_s, -1, past, C_HEADS * HEAD_DIM)
    ctx_nv = cache_na_v.reshape(nb_s, -1, past, C_HEADS * HEAD_DIM)
    g_mix_pre, g_mix_post, g_ffn_pre, g_ffn_post = (
        a.reshape(DEPTH, 1, D) for a in (norm_mix_pre, norm_mix_post, norm_ffn_pre, norm_ffn_post))
    n_even = even_w_in.shape[0]
    sgu_b3 = sgu_b.reshape(n_even, A_GROUPS, CHUNK, 1)
    sgu_n3 = sgu_norm.reshape(n_even, 1, A_WIDTH)
    qn3 = jnp.tile(q_norm, (1, B_HEADS)).reshape(n_even, 1, qw)
    kn3 = jnp.tile(k_norm, (1, B_KV_HEADS)).reshape(n_even, 1, LANES)
    rpb = jnp.pad(na_rpb, ((0, 0), (0, 0), (0, 0), (0, LANES - na_rpb.shape[3])))

    xs = (x_prompt.reshape(tp, D), x_sample.reshape(nb_s * seq_s, D))
    n_odd = odd_w_in.shape[0]
    attn_k, attn_v = [], []
    na_cache = tuple(jnp.zeros((nb_p, n_odd, seq_p, C_HEADS * HEAD_DIM), F32) for _ in range(2))
    for l in range(DEPTH):
        if l % 2 == 0:
            e = l // 2
            a_out, q, kd, vd, kf, vf = _even_in(xs, mods, l, e, T, g_mix_pre, even_w_in, sgu_w,
                                                sgu_b3, sgu_n3, qn3, kn3, bdq, bdk, cos, sin)
            n_pairs = qw // LANES
            pairs_per_kv = n_pairs // B_KV_HEADS
            mix_p = _attention(q, kd, vd, n_batch=nb_p, seq=seq_p, row0=0,
                               pairs_per_step=n_pairs, pairs_per_kv=pairs_per_kv)
            mix_s = _attention(q, kd, vd, n_batch=nb_s, seq=seq_s, row0=tp,
                               pairs_per_step=pairs_per_kv, pairs_per_kv=pairs_per_kv,
                               ctx=(ctx_ak, ctx_av, e))
            attn_k.append(kf[:tp].reshape(nb_p, seq_p, B_KV_HEADS, HEAD_DIM))
            attn_v.append(vf[:tp].reshape(nb_p, seq_p, B_KV_HEADS, HEAD_DIM))
            w_out, li = even_w_out, e
        else:
            o = l // 2
            a_out = None
            q, k, v, *na_cache = _odd_in(xs[0], mods, l, o, n_odd, seq_p, g_mix_pre, odd_w_in,
                                         tuple(na_cache))
            mix_p = _attention(q, k, v, n_batch=nb_p, seq=seq_p, row0=0,
                               pairs_per_step=PAIRS_PER_STEP, pairs_per_kv=1)
            mix_s = _na_attention(q, k, v, ctx_nk, ctx_nv, o, rpb, n_batch=nb_s, row0=tp)
            w_out, li = odd_w_out, o
        x, h = _out_proj(xs, a_out, mix_p, mix_s, mods, l, li, T, g_mix_post, g_ffn_pre, w_out)
        xs = tuple(_ffn(x, h, mods, l, g_ffn_post, ffn_w_in, ffn_w_out,
                        split_out=(l == DEPTH - 1)))

    y_prompt = xs[0].reshape(nb_p, seq_p, D)
    y_sample = xs[1].reshape(nb_s, seq_s, D)
    new_na_k, new_na_v = (a.reshape(nb_p, n_odd, seq_p, C_HEADS, HEAD_DIM) for a in na_cache)
    return (y_prompt, y_sample, jnp.stack(attn_k, axis=1), jnp.stack(attn_v, axis=1),
            new_na_k, new_na_v)
```

```python
import functools

import jax
import jax.numpy as jnp
from jax import lax
from jax.experimental import pallas as pl
from jax.experimental.pallas import tpu as pltpu

F32 = jnp.float32
BF16 = jnp.bfloat16

D = 1024
DEPTH = 4
HEAD_DIM = 64
GRID_W = 64
CHUNK = 128
A_WIDTH = D // 2
A_GROUPS = 4
B_HEADS = 8
B_KV_HEADS = 2
C_HEADS = 16
NA_ROWS = 8
NA_COLS = 16
D_FF = 2816
ROPE_BASE = 10000.0
EPS = 1e-6
NEG = -1e30
SEG = 1024
N_SEG = 8
LANES = 128
LOG2E = 1.4426950408889634
QSCALE = HEAD_DIM ** -0.5 * LOG2E

TM_PROJ = 512
TM_FFN = 512
TF_FFN = 256
TQ_ATTN = 256
PAIRS_PER_STEP = 4
NA_QROWS = 4
NA_KROWS = 12
VMEM_LIMIT = 56 * 1024 * 1024


def _cparams(n_axes):
    return pltpu.CompilerParams(dimension_semantics=("arbitrary",) * n_axes,
                                vmem_limit_bytes=VMEM_LIMIT)


def _rms(x):
    return x * lax.rsqrt(jnp.mean(x * x, axis=-1, keepdims=True) + EPS)


def _mod_norm(x, g, shift, scale):
    return _rms(x) * (g * (1.0 + scale)) + shift


def _gated_residual(x, y, g, gate):
    return x + _rms(y) * (gate * g)


def _const_spec(shape, *lead):
    block = (None,) * len(lead) + tuple(shape)
    return pl.BlockSpec(block, lambda *_: tuple(lead) + (0,) * len(shape))


def _weight_spec(shape, *lead):
    block = (None,) * len(lead) + tuple(shape)
    return pl.BlockSpec(block, lambda *_: tuple(lead) + (0,) * len(shape),
                        pipeline_mode=pl.Buffered(1))


def _mod_spec(layer, j, tm):
    return pl.BlockSpec((None, None, None, 1, D),
                        lambda i, *_: (layer, j, (i * tm) // SEG, 0, 0))


def _split_specs(tm, width, n_prompt_tiles, col=0):
    return [pl.BlockSpec((tm, width), lambda i, *_: (jnp.minimum(i, n_prompt_tiles - 1), col)),
            pl.BlockSpec((tm, width), lambda i, *_: (jnp.maximum(i - n_prompt_tiles, 0), col))]


def _pick(n_prompt_tiles, p_ref, s_ref):
    return jnp.where(pl.program_id(0) < n_prompt_tiles, p_ref[...], s_ref[...])


def _mod_kernel(cond_ref, w_ref, b_ref, o_ref):
    s = cond_ref[...]
    s = s * jax.nn.sigmoid(s)
    o_ref[...] = jnp.dot(s.astype(BF16), w_ref[...].astype(BF16),
                         preferred_element_type=F32) + b_ref[...]


def _modulation(cond, mod_w, mod_b):
    b = mod_b.reshape(DEPTH, 6, 1, D)
    out = pl.pallas_call(
        _mod_kernel,
        grid=(DEPTH, 6),
        in_specs=[pl.BlockSpec((N_SEG, D), lambda l, j: (0, 0)),
                  pl.BlockSpec((None, D, D), lambda l, j: (l, 0, j)),
                  pl.BlockSpec((None, None, 1, D), lambda l, j: (l, j, 0, 0))],
        out_specs=pl.BlockSpec((None, None, N_SEG, D), lambda l, j: (l, j, 0, 0)),
        out_shape=jax.ShapeDtypeStruct((DEPTH, 6, N_SEG, D), F32),
        compiler_params=_cparams(2),
        name="modulation",
    )(cond, mod_w, b)
    return out.reshape(DEPTH, 6, N_SEG, 1, D)


def _head_sumsq(y, bd_ref):
    sq = y * y
    hi = sq.astype(BF16)
    lo = (sq - hi.astype(F32)).astype(BF16)
    bd = bd_ref[...]
    return (jnp.dot(hi, bd, preferred_element_type=F32)
            + jnp.dot(lo, bd, preferred_element_type=F32))


def _rope(y, cos, sin):
    lane = lax.broadcasted_iota(jnp.int32, (1, LANES), 1)
    first = (lane & 16) == 0
    partner = jnp.where(first, pltpu.roll(y, LANES - 16, 1), pltpu.roll(y, 16, 1))
    return y * cos + partner * sin


def _even_in_kernel(*refs, n_prompt_tiles, split_x):
    refs = list(refs)
    x = _pick(n_prompt_tiles, refs.pop(0), refs.pop(0)) if split_x else refs.pop(0)[...]
    (sh_ref, sc_ref, g_ref, w_ref, sguw_ref, sgub_ref, sgun_ref, qn_ref, kn_ref, bdq_ref, bdk_ref,
     cos_ref, sin_ref, a_ref, q_ref, kd_ref, vd_ref, kf_ref, vf_ref, wbf_ref) = refs
    tm = a_ref.shape[0]

    @pl.when(pl.program_id(0) == 0)
    def _():
        wbf_ref[...] = w_ref[...].astype(BF16)

    h = _mod_norm(x, g_ref[...], sh_ref[...], sc_ref[...]).astype(BF16)

    u = jax.nn.gelu(jnp.dot(h, wbf_ref[:, 0:A_WIDTH], preferred_element_type=F32), approximate=True)
    v = jax.nn.gelu(jnp.dot(h, wbf_ref[:, A_WIDTH:2 * A_WIDTH], preferred_element_type=F32),
                    approximate=True)
    mu = jnp.mean(v, axis=-1, keepdims=True)
    vc = v - mu
    var = jnp.mean(vc * vc, axis=-1, keepdims=True)
    vn = (vc * lax.rsqrt(var + EPS) * sgun_ref[...]).astype(BF16)
    n_chunks = tm // CHUNK
    gch = A_WIDTH // A_GROUPS
    for g in range(A_GROUPS):
        rhs = jnp.concatenate([vn[n * CHUNK:(n + 1) * CHUNK, g * gch:(g + 1) * gch]
                               for n in range(n_chunks)], axis=1)
        mixed = jnp.dot(sguw_ref[g].astype(BF16), rhs, preferred_element_type=F32)
        bias = sgub_ref[g]
        for n in range(n_chunks):
            blk = (mixed[:, n * gch:(n + 1) * gch] + bias) * u[n * CHUNK:(n + 1) * CHUNK,
                                                              g * gch:(g + 1) * gch]
            a_ref[n * CHUNK:(n + 1) * CHUNK, g * gch:(g + 1) * gch] = blk.astype(BF16)

    c0 = 2 * A_WIDTH
    qw = B_HEADS * HEAD_DIM
    q = jnp.dot(h, wbf_ref[:, c0:c0 + qw], preferred_element_type=F32)
    q = q * lax.rsqrt(_head_sumsq(q, bdq_ref) * (1.0 / HEAD_DIM) + EPS) * qn_ref[...]
    cos = cos_ref[...]
    sin = sin_ref[...]
    for j in range(qw // LANES):
        qj = _rope(q[:, j * LANES:(j + 1) * LANES], cos, sin) * QSCALE
        q_ref[:, j * LANES:(j + 1) * LANES] = qj.astype(BF16)

    kw = B_KV_HEADS * HEAD_DIM
    k = jnp.dot(h, wbf_ref[:, c0 + qw:c0 + qw + kw], preferred_element_type=F32)
    k = k * lax.rsqrt(_head_sumsq(k, bdk_ref) * (1.0 / HEAD_DIM) + EPS) * kn_ref[...]
    kf_ref[...] = k
    vals = jnp.dot(h, wbf_ref[:, c0 + qw + kw:c0 + qw + 2 * kw], preferred_element_type=F32)
    vf_ref[...] = vals
    kr = _rope(k, cos, sin)
    lane = lax.broadcasted_iota(jnp.int32, (1, LANES), 1)
    low = lane < HEAD_DIM
    for src, dst in ((kr, kd_ref), (vals, vd_ref)):
        sw = pltpu.roll(src, HEAD_DIM, 1)
        dst[:, 0:LANES] = jnp.where(low, src, sw).astype(BF16)
        dst[:, LANES:2 * LANES] = jnp.where(low, sw, src).astype(BF16)


def _even_in(xs, mods, layer, e, T, g_pre, w_in, sgu_w, sgu_b, sgu_norm, q_norm, k_norm,
             bdq, bdk, cos, sin):
    tm = TM_PROJ
    n_in = w_in.shape[2]
    tiles_per_seg = SEG // tm
    n_prompt_tiles = T // 2 // tm
    split_x = len(xs) == 2
    tab_spec = pl.BlockSpec((None, tm, LANES),
                            lambda i: (i // n_prompt_tiles, i % tiles_per_seg, 0))
    row = lambda w: pl.BlockSpec((tm, w), lambda i: (i, 0))
    x_specs = _split_specs(tm, D, n_prompt_tiles) if split_x else [row(D)]
    qw = B_HEADS * HEAD_DIM
    return pl.pallas_call(
        functools.partial(_even_in_kernel, n_prompt_tiles=n_prompt_tiles, split_x=split_x),
        grid=(T // tm,),
        in_specs=x_specs + [
            _mod_spec(layer, 0, tm), _mod_spec(layer, 1, tm), _const_spec((1, D), layer),
            _weight_spec((D, n_in), e), _const_spec((A_GROUPS, CHUNK, CHUNK), e),
            _const_spec((A_GROUPS, CHUNK, 1), e), _const_spec((1, A_WIDTH), e),
            _const_spec((1, qw), e), _const_spec((1, LANES), e),
            _const_spec((qw, qw)), _const_spec((LANES, LANES)), tab_spec, tab_spec],
        out_specs=[row(A_WIDTH), row(qw), row(2 * LANES), row(2 * LANES), row(LANES), row(LANES)],
        out_shape=[jax.ShapeDtypeStruct((T, A_WIDTH), BF16),
                   jax.ShapeDtypeStruct((T, qw), BF16),
                   jax.ShapeDtypeStruct((T, 2 * LANES), BF16),
                   jax.ShapeDtypeStruct((T, 2 * LANES), BF16),
                   jax.ShapeDtypeStruct((T, LANES), F32),
                   jax.ShapeDtypeStruct((T, LANES), F32)],
        scratch_shapes=[pltpu.VMEM((D, n_in), BF16)],
        compiler_params=_cparams(1),
        name="even_in_proj",
    )(*xs, mods, mods, g_pre, w_in, sgu_w, sgu_b, sgu_norm, q_norm, k_norm, bdq, bdk, cos, sin)


def _odd_in_kernel(x_ref, sh_ref, sc_ref, g_ref, w_ref, *rest, n_prompt_tiles):
    q_ref, k_ref, v_ref, kf_ref, vf_ref, wbf_ref = rest[-6:]
    i = pl.program_id(0)
    hw = C_HEADS * HEAD_DIM

    @pl.when(i == 0)
    def _():
        for c in range(3):
            wbf_ref[:, c * hw:(c + 1) * hw] = w_ref[:, c * hw:(c + 1) * hw].astype(BF16)

    h = _mod_norm(x_ref[...], g_ref[...], sh_ref[...], sc_ref[...]).astype(BF16)
    q = jnp.dot(h, wbf_ref[:, 0:hw], preferred_element_type=F32)
    q_ref[...] = (q * QSCALE).astype(BF16)
    k = jnp.dot(h, wbf_ref[:, hw:2 * hw], preferred_element_type=F32)
    k_ref[...] = k.astype(BF16)
    v = jnp.dot(h, wbf_ref[:, 2 * hw:3 * hw], preferred_element_type=F32)
    v_ref[...] = v.astype(BF16)

    @pl.when(i < n_prompt_tiles)
    def _():
        kf_ref[...] = k.reshape(kf_ref.shape)
        vf_ref[...] = v.reshape(vf_ref.shape)


def _odd_in(x, mods, layer, o, n_odd, seq_p, g_pre, w_in, caches):
    T = x.shape[0]
    tm = TM_PROJ
    hw = C_HEADS * HEAD_DIM
    n_prompt_tiles = T // 2 // tm
    bt = tm // seq_p
    row = pl.BlockSpec((tm, hw), lambda i: (i, 0))
    crow = pl.BlockSpec((bt, None, seq_p, hw),
                        lambda i: (jnp.minimum(i, n_prompt_tiles - 1), o, 0, 0))
    cshape = jax.ShapeDtypeStruct((T // 2 // seq_p, n_odd, seq_p, hw), F32)
    n_in = 5
    return pl.pallas_call(
        functools.partial(_odd_in_kernel, n_prompt_tiles=n_prompt_tiles),
        grid=(T // tm,),
        in_specs=[pl.BlockSpec((tm, D), lambda i: (i, 0)), _mod_spec(layer, 0, tm),
                  _mod_spec(layer, 1, tm), _const_spec((1, D), layer),
                  _weight_spec((D, 3 * hw), o)]
                 + [pl.BlockSpec(memory_space=pl.ANY)] * len(caches),
        out_specs=[row, row, row, crow, crow],
        out_shape=[jax.ShapeDtypeStruct((T, hw), BF16)] * 3 + [cshape, cshape],
        input_output_aliases={n_in + j: 3 + j for j in range(len(caches))},
        scratch_shapes=[pltpu.VMEM((D, 3 * hw), BF16)],
        compiler_params=_cparams(1),
        name="odd_in_proj",
    )(x, mods, mods, g_pre, w_in, *caches)


def _softmax_pv(pieces):
    m = None
    for s, _ in pieces:
        ms = jnp.max(s, axis=-1, keepdims=True)
        m = ms if m is None else jnp.maximum(m, ms)
    den = None
    acc = None
    for s, val in pieces:
        p = jnp.exp2(s - m)
        ls = jnp.sum(p, axis=-1, keepdims=True)
        den = ls if den is None else den + ls
        o = jnp.dot(p.astype(BF16), val, preferred_element_type=F32)
        acc = o if acc is None else acc + o
    return acc / den


def _attn_kernel(*refs, has_ctx, tq, pairs_per_kv):
    if has_ctx:
        q_ref, k_ref, v_ref, kc_ref, vc_ref, o_ref = refs
    else:
        q_ref, k_ref, v_ref, o_ref = refs
    lane = lax.broadcasted_iota(jnp.int32, (1, LANES), 1)
    masks = [lane < HEAD_DIM, lane >= HEAD_DIM]
    if has_ctx:
        kv_head = pl.program_id(1)
        sel = jnp.where(masks[0], 0, 1) == kv_head
        kc = jnp.where(sel, kc_ref[...], 0.0)
        kc = (kc + pltpu.roll(kc, HEAD_DIM, 1)).astype(BF16)
        vc = jnp.where(sel, vc_ref[...], 0.0)
        vc = (vc + pltpu.roll(vc, HEAD_DIM, 1)).astype(BF16)
        vch = [jnp.where(mh, vc, jnp.zeros_like(vc)) for mh in masks]
    nt = (((1,), (1,)), ((), ()))
    for j in range(q_ref.shape[1] // LANES):
        cols = slice(j * LANES, (j + 1) * LANES)
        kcols = slice((j // pairs_per_kv) * LANES, (j // pairs_per_kv + 1) * LANES)
        k = k_ref[:, kcols]
        v = v_ref[:, kcols]
        vh = [jnp.where(mh, v, jnp.zeros_like(v)) for mh in masks]
        for t in range(q_ref.shape[0] // tq):
            q = q_ref[t * tq:(t + 1) * tq, cols]
            out = None
            for hh, mh in enumerate(masks):
                qm = jnp.where(mh, q, jnp.zeros_like(q))
                pieces = [(lax.dot_general(qm, k, nt, preferred_element_type=F32), vh[hh])]
                if has_ctx:
                    pieces.append((lax.dot_general(qm, kc, nt, preferred_element_type=F32),
                                   vch[hh]))
                o = _softmax_pv(pieces)
                out = o if out is None else out + o
            o_ref[t * tq:(t + 1) * tq, cols] = out.astype(BF16)


def _attention(q, k, v, *, n_batch, seq, row0, pairs_per_step, pairs_per_kv, ctx=None):
    blk0 = row0 // seq
    n_groups = q.shape[1] // (pairs_per_step * LANES)
    qw = pairs_per_step * LANES
    kw = qw // pairs_per_kv
    in_specs = [pl.BlockSpec((seq, qw), lambda b, g: (blk0 + b, g)),
                pl.BlockSpec((seq, kw), lambda b, g: (blk0 + b, g)),
                pl.BlockSpec((seq, kw), lambda b, g: (blk0 + b, g))]
    args = [q, k, v]
    if ctx is not None:
        assert kw == LANES
        kc, vc, e = ctx
        sc = kc.shape[2]
        cspec = pl.BlockSpec((None, None, sc, LANES), lambda b, g: (b, e, 0, 0))
        in_specs += [cspec, cspec]
        args += [kc, vc]
    return pl.pallas_call(
        functools.partial(_attn_kernel, has_ctx=ctx is not None, tq=min(TQ_ATTN, seq),
                          pairs_per_kv=pairs_per_kv),
        grid=(n_batch, n_groups),
        in_specs=in_specs,
        out_specs=pl.BlockSpec((seq, qw), lambda b, g: (b, g)),
        out_shape=jax.ShapeDtypeStruct((n_batch * seq, q.shape[1]), BF16),
        compiler_params=_cparams(2),
        name="attention",
    )(*args)


def _na_row_start(r):
    rows = SEG // GRID_W
    return min(max(r - NA_ROWS // 2, 0), rows - NA_ROWS)


def _na_windows():
    rows = SEG // GRID_W
    windows = []
    for qb in range(rows // NA_QROWS):
        lo = _na_row_start(qb * NA_QROWS)
        hi = _na_row_start(qb * NA_QROWS + NA_QROWS - 1) + NA_ROWS
        n = hi - lo + (hi - lo) % 2
        ws = min(lo, rows - n)
        assert ws <= lo and hi <= ws + n <= rows and n <= NA_KROWS
        windows.append((ws, n))
    return windows


def _na_kernel(q_ref, k_ref, v_ref, kc_ref, vc_ref, r_ref, o_ref, bias_ref, tab_ref, *, windows):
    lane = lax.broadcasted_iota(jnp.int32, (1, LANES), 1)
    low = lane < HEAD_DIM

    @pl.when(pl.program_id(1) == 0)
    def _():
        qcol = lax.broadcasted_iota(jnp.int32, (GRID_W, LANES), 0)
        kcol = lax.broadcasted_iota(jnp.int32, (GRID_W, LANES), 1) & (GRID_W - 1)
        start = jnp.clip(qcol - NA_COLS // 2, 0, GRID_W - NA_COLS)
        inside = (kcol >= start) & (kcol < start + NA_COLS)
        for hh in range(LANES // HEAD_DIM):
            for d in range(2 * NA_ROWS - 1):
                base = jnp.broadcast_to(r_ref[hh, d:d + 1, :], (GRID_W, LANES))
                lo_t = pltpu.roll(base, LANES - (NA_COLS - 1), 1, stride=1, stride_axis=0)
                hi_t = pltpu.roll(base, GRID_W - (NA_COLS - 1), 1, stride=1, stride_axis=0)
                tab_ref[hh, d] = jnp.where(inside, jnp.where(low, lo_t, hi_t) * LOG2E, NEG)
        neg = jnp.full((GRID_W, LANES), NEG, F32)
        for hh in range(LANES // HEAD_DIM):
            for qb, (ws, nrows) in enumerate(windows):
                for i in range(NA_QROWS):
                    r = qb * NA_QROWS + i
                    rs = _na_row_start(r)
                    for jp in range(nrows // 2):
                        kr = ws + 2 * jp
                        ok = [rs <= kr + d < rs + NA_ROWS for d in (0, 1)]
                        if not any(ok):
                            blk = neg
                        else:
                            t0 = tab_ref[hh, kr - r + NA_ROWS - 1] if ok[0] else neg
                            t1 = tab_ref[hh, kr + 1 - r + NA_ROWS - 1] if ok[1] else neg
                            blk = jnp.where(low, t0, t1)
                        bias_ref[hh, qb, i * GRID_W:(i + 1) * GRID_W,
                                 jp * LANES:(jp + 1) * LANES] = blk

    masks = [low, jnp.logical_not(low)]
    kc = kc_ref[...].astype(BF16)
    vc = vc_ref[...].astype(BF16)
    vch = [jnp.where(mh, vc, jnp.zeros_like(vc)) for mh in masks]
    nt = (((1,), (1,)), ((), ()))
    nq = NA_QROWS * GRID_W
    for qb, (ws, nrows) in enumerate(windows):
        nk = nrows * GRID_W
        q = q_ref[qb * nq:(qb + 1) * nq, :]
        kw = k_ref[ws * GRID_W:ws * GRID_W + nk, :]
        vw = v_ref[ws * GRID_W:ws * GRID_W + nk, :]
        out = None
        for hh, mh in enumerate(masks):
            qm = jnp.where(mh, q, jnp.zeros_like(q))
            s_win = (lax.dot_general(qm, kw, nt, preferred_element_type=F32)
                     + bias_ref[hh, qb, :, 0:nk])
            s_ctx = lax.dot_general(qm, kc, nt, preferred_element_type=F32)
            o = _softmax_pv([(s_win, jnp.where(mh, vw, jnp.zeros_like(vw))), (s_ctx, vch[hh])])
            out = o if out is None else out + o
        o_ref[qb * nq:(qb + 1) * nq, :] = out.astype(BF16)


def _na_attention(q, k, v, kc, vc, o_idx, rpb, *, n_batch, row0):
    blk0 = row0 // SEG
    n_pairs = q.shape[1] // LANES
    sc = kc.shape[2]
    windows = _na_windows()
    heads = LANES // HEAD_DIM
    nr = 2 * NA_ROWS - 1
    qkv = pl.BlockSpec((SEG, LANES), lambda p, b: (blk0 + b, p))
    cspec = pl.BlockSpec((None, None, sc, LANES), lambda p, b: (b, o_idx, 0, p))
    rspec = pl.BlockSpec((None, heads, nr, LANES), lambda p, b: (o_idx, p, 0, 0))
    return pl.pallas_call(
        functools.partial(_na_kernel, windows=windows),
        grid=(n_pairs, n_batch),
        in_specs=[qkv, qkv, qkv, cspec, cspec, rspec],
        out_specs=pl.BlockSpec((SEG, LANES), lambda p, b: (b, p)),
        out_shape=jax.ShapeDtypeStruct((n_batch * SEG, n_pairs * LANES), BF16),
        scratch_shapes=[pltpu.VMEM((heads, len(windows), NA_QROWS * GRID_W, NA_KROWS * GRID_W), F32),
                        pltpu.VMEM((heads, nr, GRID_W, LANES), F32)],
        compiler_params=_cparams(2),
        name="na_attention",
    )(q, k, v, kc, vc, rpb)


def _mix_ffn_kernel(*refs, layer, li, n_prompt_tiles, split_x, has_a, split_out):
    refs = list(refs)
    i = pl.program_id(0)
    x_refs = [refs.pop(0), refs.pop(0)] if split_x else [refs.pop(0)]
    gt_ref, gmix_ref, fsh_ref, fsc_ref, gpre_ref, fgt_ref, gpost_ref = refs[:7]
    refs = refs[7:]
    a_ref = refs.pop(0) if has_a else None
    bp_ref, bs_ref, wp_ref, win_ref, wout_ref = refs[:5]
    out_refs = refs[5:-12]
    (wg_buf, wu_buf, wo_buf, wg_res, wu_res, wo_res, wp_res, x1_ref, h_ref, act_buf, acc_ref,
     sem) = refs[-12:]
    nf, _, tf = wg_res.shape
    first_tile = i == 0

    def aligned(f):
        return f * tf if isinstance(f, int) else pl.multiple_of(f * tf, tf)

    def in_copies(f, slot):
        col = aligned(f)
        return (pltpu.make_async_copy(win_ref.at[layer, :, pl.ds(col, tf)], wg_buf.at[slot],
                                      sem.at[0, slot]),
                pltpu.make_async_copy(win_ref.at[layer, :, pl.ds(D_FF + col, tf)], wu_buf.at[slot],
                                      sem.at[1, slot]))

    def out_copy(f, slot):
        row = aligned(f)
        return pltpu.make_async_copy(wout_ref.at[layer, pl.ds(row, tf), :], wo_buf.at[slot],
                                     sem.at[2, slot])

    def proj_copy(c, slot):
        return pltpu.make_async_copy(wp_ref.at[li, pl.ds(c * tf, tf), :], wo_buf.at[slot],
                                     sem.at[2, slot])

    @pl.when(first_tile)
    def _():
        n_chunks = D // tf
        proj_copy(0, 0).start()
        for c in range(n_chunks):
            if c + 1 < n_chunks:
                proj_copy(c + 1, (c + 1) % 2).start()
            proj_copy(c, c % 2).wait()
            wp_res[c * tf:(c + 1) * tf, :] = wo_buf[c % 2].astype(BF16)
        for c in in_copies(0, 0):
            c.start()
        out_copy(0, 0).start()

    def fetch(f, slot):
        @pl.when(first_tile)
        def _():
            for c in in_copies(f, slot):
                c.wait()
            out_copy(f, slot).wait()

            nxt = min(f + 1, nf - 1) if isinstance(f, int) else jnp.minimum(f + 1, nf - 1)

            @pl.when(jnp.asarray(f + 1 < nf))
            def _():
                for c in in_copies(nxt, 1 - slot):
                    c.start()
                out_copy(nxt, 1 - slot).start()

            wg_res[f] = wg_buf[slot].astype(BF16)
            wu_res[f] = wu_buf[slot].astype(BF16)
            wo_res[f] = wo_buf[slot].astype(BF16)

    def hidden(f, slot):
        h = h_ref[...]
        g = jnp.dot(h, wg_res[f], preferred_element_type=F32)
        u = jnp.dot(h, wu_res[f], preferred_element_type=F32)
        act_buf[slot] = (g * jax.nn.sigmoid(g) * u).astype(BF16)

    def project(f, slot):
        acc_ref[...] += jnp.dot(act_buf[slot], wo_res[f], preferred_element_type=F32)

    def step(f, slot):
        fetch(f, slot)
        hidden(f, slot)
        project(f - 1, 1 - slot)

    b = _pick(n_prompt_tiles, bp_ref, bs_ref)
    if has_a:
        half = a_ref.shape[1]
        y = (jnp.dot(a_ref[...], wp_res[0:half, :], preferred_element_type=F32)
             + jnp.dot(b, wp_res[half:2 * half, :], preferred_element_type=F32))
    else:
        y = jnp.dot(b, wp_res[...], preferred_element_type=F32)
    x = _pick(n_prompt_tiles, *x_refs) if split_x else x_refs[0][...]
    x1 = _gated_residual(x, y, gmix_ref[...], gt_ref[...])
    x1_ref[...] = x1
    h_ref[...] = _mod_norm(x1, gpre_ref[...], fsh_ref[...], fsc_ref[...]).astype(BF16)

    acc_ref[...] = jnp.zeros_like(acc_ref)
    fetch(0, 0)
    hidden(0, 0)

    assert nf % 2 == 1

    def pair(k, carry):
        step(2 * k + 1, 1)
        step(2 * k + 2, 0)
        return carry

    lax.fori_loop(0, (nf - 1) // 2, pair, 0)
    project(nf - 1, 0)

    def result():
        return _gated_residual(x1_ref[...], acc_ref[...], gpost_ref[...], fgt_ref[...])

    if split_out:
        @pl.when(i < n_prompt_tiles)
        def _():
            out_refs[0][...] = result()

        @pl.when(i >= n_prompt_tiles)
        def _():
            out_refs[1][...] = result()
    else:
        out_refs[0][...] = result()


def _mix_ffn(xs, a, bp, bs, mods, layer, li, T, g_mix_post, g_ffn_pre, g_ffn_post, w_proj,
             w_in, w_out, split_out):
    tm, tf = TM_FFN, TF_FFN
    nf = D_FF // tf
    n_prompt_tiles = T // 2 // tm
    split_x = len(xs) == 2
    has_a = a is not None
    row = lambda w: pl.BlockSpec((tm, w), lambda i: (i, 0))
    x_specs = _split_specs(tm, D, n_prompt_tiles) if split_x else [row(D)]
    a_specs = [row(a.shape[1])] if has_a else []
    if split_out:
        out_specs = _split_specs(tm, D, n_prompt_tiles)
        out_shape = [jax.ShapeDtypeStruct((T // 2, D), F32)] * 2
    else:
        out_specs = [row(D)]
        out_shape = [jax.ShapeDtypeStruct((T, D), F32)]
    hbm = pl.BlockSpec(memory_space=pl.ANY)
    gain = _const_spec((1, D), layer)
    return pl.pallas_call(
        functools.partial(_mix_ffn_kernel, layer=layer, li=li, n_prompt_tiles=n_prompt_tiles,
                          split_x=split_x, has_a=has_a, split_out=split_out),
        grid=(T // tm,),
        in_specs=x_specs + [_mod_spec(layer, 2, tm), gain, _mod_spec(layer, 3, tm),
                            _mod_spec(layer, 4, tm), gain, _mod_spec(layer, 5, tm), gain]
                 + a_specs + _split_specs(tm, bp.shape[1], n_prompt_tiles) + [hbm, hbm, hbm],
        out_specs=out_specs,
        out_shape=out_shape,
        scratch_shapes=[pltpu.VMEM((2, D, tf), F32), pltpu.VMEM((2, D, tf), F32),
                        pltpu.VMEM((2, tf, D), F32), pltpu.VMEM((nf, D, tf), BF16),
                        pltpu.VMEM((nf, D, tf), BF16), pltpu.VMEM((nf, tf, D), BF16),
                        pltpu.VMEM((D, D), BF16), pltpu.VMEM((tm, D), F32),
                        pltpu.VMEM((tm, D), BF16), pltpu.VMEM((2, tm, tf), BF16),
                        pltpu.VMEM((tm, D), F32), pltpu.SemaphoreType.DMA((3, 2))],
        compiler_params=_cparams(1),
        name="mix_ffn",
    )(*xs, mods, g_mix_post, mods, mods, g_ffn_pre, mods, g_ffn_post,
      *([a] if has_a else []), bp, bs, w_proj, w_in, w_out)


def _rope_tables():
    t = jnp.arange(SEG)
    nf = HEAD_DIM // 4
    freqs = ROPE_BASE ** (-jnp.arange(nf, dtype=F32) / nf)

    def cs(pos):
        ang = pos.astype(F32)[:, None] * freqs[None, :]
        return jnp.cos(ang), jnp.sin(ang)

    cr, sr = cs(t // GRID_W)
    cc, sn = cs(t % GRID_W)
    cos = jnp.concatenate([cr, cr, cc, cc], axis=1)
    sin = jnp.concatenate([-sr, sr, -sn, sn], axis=1)
    reps = LANES // HEAD_DIM
    cos = jnp.tile(cos, (1, reps))
    sin = jnp.tile(sin, (1, reps))
    return (jnp.stack([jnp.ones_like(cos), cos]), jnp.stack([jnp.zeros_like(sin), sin]))


def _block_diag_ones(n):
    i = jnp.arange(n) // HEAD_DIM
    return (i[:, None] == i[None, :]).astype(BF16)


def kernel(x_prompt, x_sample, cache_attn_k, cache_attn_v, cache_na_k, cache_na_v, c, c_ctx,
           mod_w, mod_b, norm_mix_pre, norm_mix_post, norm_ffn_pre, norm_ffn_post,
           even_w_in, even_w_out, sgu_w, sgu_b, sgu_norm, q_norm, k_norm,
           odd_w_in, odd_w_out, na_rpb, ffn_w_in, ffn_w_out):
    nb_p, seq_p, _ = x_prompt.shape
    nb_s, seq_s, _ = x_sample.shape
    tp = nb_p * seq_p
    T = 2 * tp
    assert seq_s == SEG and tp == nb_s * seq_s and tp % SEG == 0

    cond = jnp.concatenate([jnp.broadcast_to(c_ctx[None, :], (tp // SEG, D)), c], axis=0)
    mods = _modulation(cond, mod_w, mod_b)

    cos, sin = _rope_tables()
    qw = B_HEADS * HEAD_DIM
    bdq = _block_diag_ones(qw)
    bdk = _block_diag_ones(LANES)
    past = cache_attn_k.shape[2]
    ctx_ak = cache_attn_k.reshape(nb_s, -1, past, B_KV_HEADS * HEAD_DIM)
    ctx_av = cache_attn_v.reshape(nb_s, -1, past, B_KV_HEADS * HEAD_DIM)
    ctx_nk = cache_na_k.reshape(nb_s, -1, past, C_HEADS * HEAD_DIM)
    ctx_nv = cache_na_v.reshape(nb_s, -1, past, C_HEADS * HEAD_DIM)
    g_mix_pre, g_mix_post, g_ffn_pre, g_ffn_post = (
        a.reshape(DEPTH, 1, D) for a in (norm_mix_pre, norm_mix_post, norm_ffn_pre, norm_ffn_post))
    n_even = even_w_in.shape[0]
    sgu_b3 = sgu_b.reshape(n_even, A_GROUPS, CHUNK, 1)
    sgu_n3 = sgu_norm.reshape(n_even, 1, A_WIDTH)
    qn3 = jnp.tile(q_norm, (1, B_HEADS)).reshape(n_even, 1, qw)
    kn3 = jnp.tile(k_norm, (1, B_KV_HEADS)).reshape(n_even, 1, LANES)
    rpb = jnp.pad(na_rpb, ((0, 0), (0, 0), (0, 0), (0, LANES - na_rpb.shape[3])))

    xs = (x_prompt.reshape(tp, D), x_sample.reshape(nb_s * seq_s, D))
    n_odd = odd_w_in.shape[0]
    attn_k, attn_v = [], []
    na_cache = tuple(jnp.zeros((nb_p, n_odd, seq_p, C_HEADS * HEAD_DIM), F32) for _ in range(2))
    for l in range(DEPTH):
        if l % 2 == 0:
            e = l // 2
            a_out, q, kd, vd, kf, vf = _even_in(xs, mods, l, e, T, g_mix_pre, even_w_in, sgu_w,
                                                sgu_b3, sgu_n3, qn3, kn3, bdq, bdk, cos, sin)
            n_pairs = qw // LANES
            pairs_per_kv = n_pairs // B_KV_HEADS
            mix_p = _attention(q, kd, vd, n_batch=nb_p, seq=seq_p, row0=0,
                               pairs_per_step=n_pairs, pairs_per_kv=pairs_per_kv)
            mix_s = _attention(q, kd, vd, n_batch=nb_s, seq=seq_s, row0=tp,
                               pairs_per_step=pairs_per_kv, pairs_per_kv=pairs_per_kv,
                               ctx=(ctx_ak, ctx_av, e))
            attn_k.append(kf[:tp].reshape(nb_p, seq_p, B_KV_HEADS, HEAD_DIM))
            attn_v.append(vf[:tp].reshape(nb_p, seq_p, B_KV_HEADS, HEAD_DIM))
            w_out, li = even_w_out, e
        else:
            o = l // 2
            a_out = None
            q, k, v, *na_cache = _odd_in(xs[0], mods, l, o, n_odd, seq_p, g_mix_pre, odd_w_in,
                                         tuple(na_cache))
            mix_p = _attention(q, k, v, n_batch=nb_p, seq=seq_p, row0=0,
                               pairs_per_step=PAIRS_PER_STEP, pairs_per_kv=1)
            mix_s = _na_attention(q, k, v, ctx_nk, ctx_nv, o, rpb, n_batch=nb_s, row0=tp)
            w_out, li = odd_w_out, o
        xs = tuple(_mix_ffn(xs, a_out, mix_p, mix_s, mods, l, li, T, g_mix_post, g_ffn_pre,
                            g_ffn_post, w_out, ffn_w_in, ffn_w_out, split_out=(l == DEPTH - 1)))

    y_prompt = xs[0].reshape(nb_p, seq_p, D)
    y_sample = xs[1].reshape(nb_s, seq_s, D)
    new_na_k, new_na_v = (a.reshape(nb_p, n_odd, seq_p, C_HEADS, HEAD_DIM) for a in na_cache)
    return (y_prompt, y_sample, jnp.stack(attn_k, axis=1), jnp.stack(attn_v, axis=1),
            new_na_k, new_na_v)
```

```python
import functools

import jax
import jax.numpy as jnp
from jax import lax
from jax.experimental import pallas as pl
from jax.experimental.pallas import tpu as pltpu

F32 = jnp.float32
BF16 = jnp.bfloat16

D = 1024
DEPTH = 4
HEAD_DIM = 64
GRID_W = 64
CHUNK = 128
A_WIDTH = D // 2
A_GROUPS = 4
B_HEADS = 8
B_KV_HEADS = 2
C_HEADS = 16
NA_ROWS = 8
NA_COLS = 16
D_FF = 2816
ROPE_BASE = 10000.0
EPS = 1e-6
NEG = -1e30
SEG = 1024
N_SEG = 8
LANES = 128
LOG2E = 1.4426950408889634
QSCALE = HEAD_DIM ** -0.5 * LOG2E

TM_PROJ = 512
TM_FFN = 1024
TF_FFN = 256
TQ_ATTN = 256
PAIRS_PER_STEP = 4
PROMPT_BATCHES_PER_STEP = 2
NA_QROWS = 4
NA_KROWS = 12
VMEM_LIMIT = 56 * 1024 * 1024


def _cparams(n_axes):
    return pltpu.CompilerParams(dimension_semantics=("arbitrary",) * n_axes,
                                vmem_limit_bytes=VMEM_LIMIT)


def _rms(x):
    return x * lax.rsqrt(jnp.mean(x * x, axis=-1, keepdims=True) + EPS)


def _mod_norm(x, g, shift, scale):
    return _rms(x) * (g * (1.0 + scale)) + shift


def _gated_residual(x, y, g, gate):
    return x + _rms(y) * (gate * g)


def _const_spec(shape, *lead):
    block = (None,) * len(lead) + tuple(shape)
    return pl.BlockSpec(block, lambda *_: tuple(lead) + (0,) * len(shape))


def _weight_spec(shape, *lead):
    block = (None,) * len(lead) + tuple(shape)
    return pl.BlockSpec(block, lambda *_: tuple(lead) + (0,) * len(shape),
                        pipeline_mode=pl.Buffered(1))


def _mod_spec(layer, j, tm):
    return pl.BlockSpec((None, None, None, 1, D),
                        lambda i, *_: (layer, j, (i * tm) // SEG, 0, 0))


def _split_specs(tm, width, n_prompt_tiles, col=0):
    return [pl.BlockSpec((tm, width), lambda i, *_: (jnp.minimum(i, n_prompt_tiles - 1), col)),
            pl.BlockSpec((tm, width), lambda i, *_: (jnp.maximum(i - n_prompt_tiles, 0), col))]


def _pick(n_prompt_tiles, p_ref, s_ref):
    return jnp.where(pl.program_id(0) < n_prompt_tiles, p_ref[...], s_ref[...])


def _mod_kernel(cond_ref, w_ref, b_ref, o_ref):
    s = cond_ref[...]
    s = s * jax.nn.sigmoid(s)
    o_ref[...] = jnp.dot(s.astype(BF16), w_ref[...].astype(BF16),
                         preferred_element_type=F32) + b_ref[...]


def _modulation(cond, mod_w, mod_b):
    b = mod_b.reshape(DEPTH, 6, 1, D)
    out = pl.pallas_call(
        _mod_kernel,
        grid=(DEPTH, 6),
        in_specs=[pl.BlockSpec((N_SEG, D), lambda l, j: (0, 0)),
                  pl.BlockSpec((None, D, D), lambda l, j: (l, 0, j)),
                  pl.BlockSpec((None, None, 1, D), lambda l, j: (l, j, 0, 0))],
        out_specs=pl.BlockSpec((None, None, N_SEG, D), lambda l, j: (l, j, 0, 0)),
        out_shape=jax.ShapeDtypeStruct((DEPTH, 6, N_SEG, D), F32),
        compiler_params=_cparams(2),
        name="modulation",
    )(cond, mod_w, b)
    return out.reshape(DEPTH, 6, N_SEG, 1, D)


def _head_sumsq(y, bd_ref):
    sq = y * y
    hi = sq.astype(BF16)
    lo = (sq - hi.astype(F32)).astype(BF16)
    bd = bd_ref[...]
    return (jnp.dot(hi, bd, preferred_element_type=F32)
            + jnp.dot(lo, bd, preferred_element_type=F32))


def _rope(y, cos, sin):
    lane = lax.broadcasted_iota(jnp.int32, (1, LANES), 1)
    first = (lane & 16) == 0
    partner = jnp.where(first, pltpu.roll(y, LANES - 16, 1), pltpu.roll(y, 16, 1))
    return y * cos + partner * sin


def _even_in_kernel(*refs, n_prompt_tiles, split_x):
    refs = list(refs)
    x = _pick(n_prompt_tiles, refs.pop(0), refs.pop(0)) if split_x else refs.pop(0)[...]
    (sh_ref, sc_ref, g_ref, w_ref, sguw_ref, sgub_ref, sgun_ref, qn_ref, kn_ref, bdq_ref, bdk_ref,
     cos_ref, sin_ref, a_ref, q_ref, kd_ref, vd_ref, kf_ref, vf_ref, wbf_ref) = refs
    tm = a_ref.shape[0]

    @pl.when(pl.program_id(0) == 0)
    def _():
        wbf_ref[...] = w_ref[...].astype(BF16)

    h = _mod_norm(x, g_ref[...], sh_ref[...], sc_ref[...]).astype(BF16)

    u = jax.nn.gelu(jnp.dot(h, wbf_ref[:, 0:A_WIDTH], preferred_element_type=F32), approximate=True)
    v = jax.nn.gelu(jnp.dot(h, wbf_ref[:, A_WIDTH:2 * A_WIDTH], preferred_element_type=F32),
                    approximate=True)
    mu = jnp.mean(v, axis=-1, keepdims=True)
    vc = v - mu
    var = jnp.mean(vc * vc, axis=-1, keepdims=True)
    vn = (vc * lax.rsqrt(var + EPS) * sgun_ref[...]).astype(BF16)
    n_chunks = tm // CHUNK
    gch = A_WIDTH // A_GROUPS
    for g in range(A_GROUPS):
        rhs = jnp.concatenate([vn[n * CHUNK:(n + 1) * CHUNK, g * gch:(g + 1) * gch]
                               for n in range(n_chunks)], axis=1)
        mixed = jnp.dot(sguw_ref[g].astype(BF16), rhs, preferred_element_type=F32)
        bias = sgub_ref[g]
        for n in range(n_chunks):
            blk = (mixed[:, n * gch:(n + 1) * gch] + bias) * u[n * CHUNK:(n + 1) * CHUNK,
                                                              g * gch:(g + 1) * gch]
            a_ref[n * CHUNK:(n + 1) * CHUNK, g * gch:(g + 1) * gch] = blk.astype(BF16)

    c0 = 2 * A_WIDTH
    qw = B_HEADS * HEAD_DIM
    q = jnp.dot(h, wbf_ref[:, c0:c0 + qw], preferred_element_type=F32)
    q = q * lax.rsqrt(_head_sumsq(q, bdq_ref) * (1.0 / HEAD_DIM) + EPS) * qn_ref[...]
    cos = cos_ref[...]
    sin = sin_ref[...]
    for j in range(qw // LANES):
        qj = _rope(q[:, j * LANES:(j + 1) * LANES], cos, sin) * QSCALE
        q_ref[:, j * LANES:(j + 1) * LANES] = qj.astype(BF16)

    kw = B_KV_HEADS * HEAD_DIM
    k = jnp.dot(h, wbf_ref[:, c0 + qw:c0 + qw + kw], preferred_element_type=F32)
    k = k * lax.rsqrt(_head_sumsq(k, bdk_ref) * (1.0 / HEAD_DIM) + EPS) * kn_ref[...]
    kf_ref[...] = k
    vals = jnp.dot(h, wbf_ref[:, c0 + qw + kw:c0 + qw + 2 * kw], preferred_element_type=F32)
    vf_ref[...] = vals
    kr = _rope(k, cos, sin)
    lane = lax.broadcasted_iota(jnp.int32, (1, LANES), 1)
    low = lane < HEAD_DIM
    for src, dst in ((kr, kd_ref), (vals, vd_ref)):
        sw = pltpu.roll(src, HEAD_DIM, 1)
        dst[:, 0:LANES] = jnp.where(low, src, sw).astype(BF16)
        dst[:, LANES:2 * LANES] = jnp.where(low, sw, src).astype(BF16)


def _even_in(xs, mods, layer, e, T, g_pre, w_in, sgu_w, sgu_b, sgu_norm, q_norm, k_norm,
             bdq, bdk, cos, sin):
    tm = TM_PROJ
    n_in = w_in.shape[2]
    tiles_per_seg = SEG // tm
    n_prompt_tiles = T // 2 // tm
    split_x = len(xs) == 2
    tab_spec = pl.BlockSpec((None, tm, LANES),
                            lambda i: (i // n_prompt_tiles, i % tiles_per_seg, 0))
    row = lambda w: pl.BlockSpec((tm, w), lambda i: (i, 0))
    x_specs = _split_specs(tm, D, n_prompt_tiles) if split_x else [row(D)]
    qw = B_HEADS * HEAD_DIM
    return pl.pallas_call(
        functools.partial(_even_in_kernel, n_prompt_tiles=n_prompt_tiles, split_x=split_x),
        grid=(T // tm,),
        in_specs=x_specs + [
            _mod_spec(layer, 0, tm), _mod_spec(layer, 1, tm), _const_spec((1, D), layer),
            _weight_spec((D, n_in), e), _const_spec((A_GROUPS, CHUNK, CHUNK), e),
            _const_spec((A_GROUPS, CHUNK, 1), e), _const_spec((1, A_WIDTH), e),
            _const_spec((1, qw), e), _const_spec((1, LANES), e),
            _const_spec((qw, qw)), _const_spec((LANES, LANES)), tab_spec, tab_spec],
        out_specs=[row(A_WIDTH), row(qw), row(2 * LANES), row(2 * LANES), row(LANES), row(LANES)],
        out_shape=[jax.ShapeDtypeStruct((T, A_WIDTH), BF16),
                   jax.ShapeDtypeStruct((T, qw), BF16),
                   jax.ShapeDtypeStruct((T, 2 * LANES), BF16),
                   jax.ShapeDtypeStruct((T, 2 * LANES), BF16),
                   jax.ShapeDtypeStruct((T, LANES), F32),
                   jax.ShapeDtypeStruct((T, LANES), F32)],
        scratch_shapes=[pltpu.VMEM((D, n_in), BF16)],
        compiler_params=_cparams(1),
        name="even_in_proj",
    )(*xs, mods, mods, g_pre, w_in, sgu_w, sgu_b, sgu_norm, q_norm, k_norm, bdq, bdk, cos, sin)


def _odd_in_kernel(x_ref, sh_ref, sc_ref, g_ref, w_ref, *rest, n_prompt_tiles, slot, all_slots):
    q_ref, k_ref, v_ref, kf_ref, vf_ref, wbf_ref = rest[-6:]
    i = pl.program_id(0)
    hw = C_HEADS * HEAD_DIM

    @pl.when(i == 0)
    def _():
        for c in range(3):
            wbf_ref[:, c * hw:(c + 1) * hw] = w_ref[:, c * hw:(c + 1) * hw].astype(BF16)

    h = _mod_norm(x_ref[...], g_ref[...], sh_ref[...], sc_ref[...]).astype(BF16)
    q = jnp.dot(h, wbf_ref[:, 0:hw], preferred_element_type=F32)
    q_ref[...] = (q * QSCALE).astype(BF16)
    k = jnp.dot(h, wbf_ref[:, hw:2 * hw], preferred_element_type=F32)
    k_ref[...] = k.astype(BF16)
    v = jnp.dot(h, wbf_ref[:, 2 * hw:3 * hw], preferred_element_type=F32)
    v_ref[...] = v.astype(BF16)

    @pl.when(i < n_prompt_tiles)
    def _():
        for src, dst in ((k, kf_ref), (v, vf_ref)):
            if all_slots:
                val = src.reshape((dst.shape[0],) + dst.shape[2:])
                for s in range(dst.shape[1]):
                    dst[:, s] = val if s == slot else jnp.zeros_like(val)
            else:
                dst[...] = src.reshape(dst.shape)


def _odd_in(x, mods, layer, o, n_odd, seq_p, g_pre, w_in, caches):
    T = x.shape[0]
    tm = TM_PROJ
    hw = C_HEADS * HEAD_DIM
    n_prompt_tiles = T // 2 // tm
    bt = tm // seq_p
    row = pl.BlockSpec((tm, hw), lambda i: (i, 0))
    all_slots = not caches
    if all_slots:
        crow = pl.BlockSpec((bt, n_odd, seq_p, hw),
                            lambda i: (jnp.minimum(i, n_prompt_tiles - 1), 0, 0, 0))
    else:
        crow = pl.BlockSpec((bt, None, seq_p, hw),
                            lambda i: (jnp.minimum(i, n_prompt_tiles - 1), o, 0, 0))
    cshape = jax.ShapeDtypeStruct((T // 2 // seq_p, n_odd, seq_p, hw), F32)
    n_in = 5
    return pl.pallas_call(
        functools.partial(_odd_in_kernel, n_prompt_tiles=n_prompt_tiles, slot=o,
                          all_slots=all_slots),
        grid=(T // tm,),
        in_specs=[pl.BlockSpec((tm, D), lambda i: (i, 0)), _mod_spec(layer, 0, tm),
                  _mod_spec(layer, 1, tm), _const_spec((1, D), layer),
                  _weight_spec((D, 3 * hw), o)]
                 + [pl.BlockSpec(memory_space=pl.ANY)] * len(caches),
        out_specs=[row, row, row, crow, crow],
        out_shape=[jax.ShapeDtypeStruct((T, hw), BF16)] * 3 + [cshape, cshape],
        input_output_aliases={n_in + j: 3 + j for j in range(len(caches))},
        scratch_shapes=[pltpu.VMEM((D, 3 * hw), BF16)],
        compiler_params=_cparams(1),
        name="odd_in_proj",
    )(x, mods, mods, g_pre, w_in, *caches)


def _softmax_pv(pieces):
    m = None
    for s, _ in pieces:
        ms = jnp.max(s, axis=-1, keepdims=True)
        m = ms if m is None else jnp.maximum(m, ms)
    den = None
    acc = None
    for s, val in pieces:
        p = jnp.exp2(s - m)
        ls = jnp.sum(p, axis=-1, keepdims=True)
        den = ls if den is None else den + ls
        o = jnp.dot(p.astype(BF16), val, preferred_element_type=F32)
        acc = o if acc is None else acc + o
    return acc / den


def _attn_kernel(*refs, has_ctx, seq, tq, pairs_per_kv):
    if has_ctx:
        q_ref, k_ref, v_ref, kc_ref, vc_ref, o_ref = refs
    else:
        q_ref, k_ref, v_ref, o_ref = refs
    lane = lax.broadcasted_iota(jnp.int32, (1, LANES), 1)
    masks = [lane < HEAD_DIM, lane >= HEAD_DIM]
    if has_ctx:
        kv_head = pl.program_id(1)
        sel = jnp.where(masks[0], 0, 1) == kv_head
        kc = jnp.where(sel, kc_ref[...], 0.0)
        kc = (kc + pltpu.roll(kc, HEAD_DIM, 1)).astype(BF16)
        vc = jnp.where(sel, vc_ref[...], 0.0)
        vc = (vc + pltpu.roll(vc, HEAD_DIM, 1)).astype(BF16)
        vch = [jnp.where(mh, vc, jnp.zeros_like(vc)) for mh in masks]
    nt = (((1,), (1,)), ((), ()))
    units = [(r0, j) for r0 in range(0, q_ref.shape[0], seq) for j in range(q_ref.shape[1] // LANES)]
    for r0, j in units:
        cols = slice(j * LANES, (j + 1) * LANES)
        kcols = slice((j // pairs_per_kv) * LANES, (j // pairs_per_kv + 1) * LANES)
        k = k_ref[r0:r0 + seq, kcols]
        v = v_ref[r0:r0 + seq, kcols]
        vh = [jnp.where(mh, v, jnp.zeros_like(v)) for mh in masks]
        for t in range(r0 // tq, (r0 + seq) // tq):
            q = q_ref[t * tq:(t + 1) * tq, cols]
            out = None
            for hh, mh in enumerate(masks):
                qm = jnp.where(mh, q, jnp.zeros_like(q))
                pieces = [(lax.dot_general(qm, k, nt, preferred_element_type=F32), vh[hh])]
                if has_ctx:
                    pieces.append((lax.dot_general(qm, kc, nt, preferred_element_type=F32),
                                   vch[hh]))
                o = _softmax_pv(pieces)
                out = o if out is None else out + o
            o_ref[t * tq:(t + 1) * tq, cols] = out.astype(BF16)


def _attention(q, k, v, *, n_batch, seq, row0, pairs_per_step, pairs_per_kv, batches_per_step=1,
               ctx=None):
    rows = batches_per_step * seq
    assert row0 % rows == 0 and n_batch % batches_per_step == 0
    blk0 = row0 // rows
    n_groups = q.shape[1] // (pairs_per_step * LANES)
    qw = pairs_per_step * LANES
    kw = qw // pairs_per_kv
    in_specs = [pl.BlockSpec((rows, qw), lambda b, g: (blk0 + b, g)),
                pl.BlockSpec((rows, kw), lambda b, g: (blk0 + b, g)),
                pl.BlockSpec((rows, kw), lambda b, g: (blk0 + b, g))]
    args = [q, k, v]
    if ctx is not None:
        assert kw == LANES and batches_per_step == 1
        kc, vc, e = ctx
        sc = kc.shape[2]
        cspec = pl.BlockSpec((None, None, sc, LANES), lambda b, g: (b, e, 0, 0))
        in_specs += [cspec, cspec]
        args += [kc, vc]
    return pl.pallas_call(
        functools.partial(_attn_kernel, has_ctx=ctx is not None, seq=seq, tq=min(TQ_ATTN, seq),
                          pairs_per_kv=pairs_per_kv),
        grid=(n_batch // batches_per_step, n_groups),
        in_specs=in_specs,
        out_specs=pl.BlockSpec((rows, qw), lambda b, g: (b, g)),
        out_shape=jax.ShapeDtypeStruct((n_batch * seq, q.shape[1]), BF16),
        compiler_params=_cparams(2),
        name="attention",
    )(*args)


def _na_row_start(r):
    rows = SEG // GRID_W
    return min(max(r - NA_ROWS // 2, 0), rows - NA_ROWS)


def _na_windows():
    rows = SEG // GRID_W
    windows = []
    for qb in range(rows // NA_QROWS):
        lo = _na_row_start(qb * NA_QROWS)
        hi = _na_row_start(qb * NA_QROWS + NA_QROWS - 1) + NA_ROWS
        n = hi - lo + (hi - lo) % 2
        ws = min(lo, rows - n)
        assert ws <= lo and hi <= ws + n <= rows and n <= NA_KROWS
        windows.append((ws, n))
    return windows


def _na_kernel(q_ref, k_ref, v_ref, kc_ref, vc_ref, r_ref, o_ref, bias_ref, tab_ref, *, windows):
    lane = lax.broadcasted_iota(jnp.int32, (1, LANES), 1)
    low = lane < HEAD_DIM

    @pl.when(pl.program_id(1) == 0)
    def _():
        qcol = lax.broadcasted_iota(jnp.int32, (GRID_W, LANES), 0)
        kcol = lax.broadcasted_iota(jnp.int32, (GRID_W, LANES), 1) & (GRID_W - 1)
        start = jnp.clip(qcol - NA_COLS // 2, 0, GRID_W - NA_COLS)
        inside = (kcol >= start) & (kcol < start + NA_COLS)
        for hh in range(LANES // HEAD_DIM):
            for d in range(2 * NA_ROWS - 1):
                base = jnp.broadcast_to(r_ref[hh, d:d + 1, :], (GRID_W, LANES))
                lo_t = pltpu.roll(base, LANES - (NA_COLS - 1), 1, stride=1, stride_axis=0)
                hi_t = pltpu.roll(base, GRID_W - (NA_COLS - 1), 1, stride=1, stride_axis=0)
                tab_ref[hh, d] = jnp.where(inside, jnp.where(low, lo_t, hi_t) * LOG2E, NEG)
        neg = jnp.full((GRID_W, LANES), NEG, F32)
        for hh in range(LANES // HEAD_DIM):
            for qb, (ws, nrows) in enumerate(windows):
                for i in range(NA_QROWS):
                    r = qb * NA_QROWS + i
                    rs = _na_row_start(r)
                    for jp in range(nrows // 2):
                        kr = ws + 2 * jp
                        ok = [rs <= kr + d < rs + NA_ROWS for d in (0, 1)]
                        if not any(ok):
                            blk = neg
                        else:
                            t0 = tab_ref[hh, kr - r + NA_ROWS - 1] if ok[0] else neg
                            t1 = tab_ref[hh, kr + 1 - r + NA_ROWS - 1] if ok[1] else neg
                            blk = jnp.where(low, t0, t1)
                        bias_ref[hh, qb, i * GRID_W:(i + 1) * GRID_W,
                                 jp * LANES:(jp + 1) * LANES] = blk

    masks = [low, jnp.logical_not(low)]
    kc = kc_ref[...].astype(BF16)
    vc = vc_ref[...].astype(BF16)
    vch = [jnp.where(mh, vc, jnp.zeros_like(vc)) for mh in masks]
    nt = (((1,), (1,)), ((), ()))
    nq = NA_QROWS * GRID_W
    for qb, (ws, nrows) in enumerate(windows):
        nk = nrows * GRID_W
        q = q_ref[qb * nq:(qb + 1) * nq, :]
        kw = k_ref[ws * GRID_W:ws * GRID_W + nk, :]
        vw = v_ref[ws * GRID_W:ws * GRID_W + nk, :]
        out = None
        for hh, mh in enumerate(masks):
            qm = jnp.where(mh, q, jnp.zeros_like(q))
            s_win = (lax.dot_general(qm, kw, nt, preferred_element_type=F32)
                     + bias_ref[hh, qb, :, 0:nk])
            s_ctx = lax.dot_general(qm, kc, nt, preferred_element_type=F32)
            o = _softmax_pv([(s_win, jnp.where(mh, vw, jnp.zeros_like(vw))), (s_ctx, vch[hh])])
            out = o if out is None else out + o
        o_ref[qb * nq:(qb + 1) * nq, :] = out.astype(BF16)


def _na_attention(q, k, v, kc, vc, o_idx, rpb, *, n_batch, row0):
    blk0 = row0 // SEG
    n_pairs = q.shape[1] // LANES
    sc = kc.shape[2]
    windows = _na_windows()
    heads = LANES // HEAD_DIM
    nr = 2 * NA_ROWS - 1
    qkv = pl.BlockSpec((SEG, LANES), lambda p, b: (blk0 + b, p))
    cspec = pl.BlockSpec((None, None, sc, LANES), lambda p, b: (b, o_idx, 0, p))
    rspec = pl.BlockSpec((None, heads, nr, LANES), lambda p, b: (o_idx, p, 0, 0))
    return pl.pallas_call(
        functools.partial(_na_kernel, windows=windows),
        grid=(n_pairs, n_batch),
        in_specs=[qkv, qkv, qkv, cspec, cspec, rspec],
        out_specs=pl.BlockSpec((SEG, LANES), lambda p, b: (b, p)),
        out_shape=jax.ShapeDtypeStruct((n_batch * SEG, n_pairs * LANES), BF16),
        scratch_shapes=[pltpu.VMEM((heads, len(windows), NA_QROWS * GRID_W, NA_KROWS * GRID_W), F32),
                        pltpu.VMEM((heads, nr, GRID_W, LANES), F32)],
        compiler_params=_cparams(2),
        name="na_attention",
    )(q, k, v, kc, vc, rpb)


def _out_kernel(*refs, n_prompt_tiles, split_x, has_a):
    refs = list(refs)
    x = _pick(n_prompt_tiles, refs.pop(0), refs.pop(0)) if split_x else refs.pop(0)[...]
    gt_ref, g_ref, fsh_ref, fsc_ref, gf_ref, w_ref = refs[:6]
    a_ref = refs[6] if has_a else None
    bp_ref, bs_ref, o_ref, h_ref, wbf_ref = refs[6 + has_a:]

    @pl.when(pl.program_id(0) == 0)
    def _():
        wbf_ref[...] = w_ref[...].astype(BF16)

    b = _pick(n_prompt_tiles, bp_ref, bs_ref)
    if has_a:
        half = a_ref.shape[1]
        y = (jnp.dot(a_ref[...], wbf_ref[0:half, :], preferred_element_type=F32)
             + jnp.dot(b, wbf_ref[half:2 * half, :], preferred_element_type=F32))
    else:
        y = jnp.dot(b, wbf_ref[...], preferred_element_type=F32)
    x1 = _gated_residual(x, y, g_ref[...], gt_ref[...])
    o_ref[...] = x1
    h_ref[...] = _mod_norm(x1, gf_ref[...], fsh_ref[...], fsc_ref[...]).astype(BF16)


def _out_proj(xs, a, bp, bs, mods, layer, li, T, g_post, g_ffn_pre, w_out):
    tm = TM_PROJ
    n_prompt_tiles = T // 2 // tm
    split_x = len(xs) == 2
    has_a = a is not None
    row = lambda w: pl.BlockSpec((tm, w), lambda i: (i, 0))
    x_specs = _split_specs(tm, D, n_prompt_tiles) if split_x else [row(D)]
    a_specs = [row(a.shape[1])] if has_a else []
    return pl.pallas_call(
        functools.partial(_out_kernel, n_prompt_tiles=n_prompt_tiles, split_x=split_x, has_a=has_a),
        grid=(T // tm,),
        in_specs=x_specs + [_mod_spec(layer, 2, tm), _const_spec((1, D), layer),
                            _mod_spec(layer, 3, tm), _mod_spec(layer, 4, tm),
                            _const_spec((1, D), layer), _weight_spec((D, D), li)] + a_specs
                 + _split_specs(tm, bp.shape[1], n_prompt_tiles),
        out_specs=[row(D), row(D)],
        out_shape=[jax.ShapeDtypeStruct((T, D), F32), jax.ShapeDtypeStruct((T, D), BF16)],
        scratch_shapes=[pltpu.VMEM((D, D), BF16)],
        compiler_params=_cparams(1),
        name="out_proj",
    )(*xs, mods, g_post, mods, mods, g_ffn_pre, w_out, *([a] if has_a else []), bp, bs)


def _ffn_kernel(x_ref, h_ref, gt_ref, gpost_ref, win_ref, wout_ref, *rest,
                layer, n_prompt_tiles, split_out):
    out_refs = rest[:-9]
    wg_buf, wu_buf, wo_buf, wg_res, wu_res, wo_res, act_buf, acc_ref, sem = rest[-9:]
    nf, _, tf = wg_res.shape
    first_tile = pl.program_id(0) == 0

    def aligned(f):
        return f * tf if isinstance(f, int) else pl.multiple_of(f * tf, tf)

    def in_copies(f, slot):
        col = aligned(f)
        return (pltpu.make_async_copy(win_ref.at[layer, :, pl.ds(col, tf)], wg_buf.at[slot],
                                      sem.at[0, slot]),
                pltpu.make_async_copy(win_ref.at[layer, :, pl.ds(D_FF + col, tf)], wu_buf.at[slot],
                                      sem.at[1, slot]))

    def out_copy(f, slot):
        row = aligned(f)
        return pltpu.make_async_copy(wout_ref.at[layer, pl.ds(row, tf), :], wo_buf.at[slot],
                                     sem.at[2, slot])

    def fetch(f, slot):
        @pl.when(first_tile)
        def _():
            for c in in_copies(f, slot):
                c.wait()
            out_copy(f, slot).wait()

            nxt = min(f + 1, nf - 1) if isinstance(f, int) else jnp.minimum(f + 1, nf - 1)

            @pl.when(jnp.asarray(f + 1 < nf))
            def _():
                for c in in_copies(nxt, 1 - slot):
                    c.start()
                out_copy(nxt, 1 - slot).start()

            wg_res[f] = wg_buf[slot].astype(BF16)
            wu_res[f] = wu_buf[slot].astype(BF16)
            wo_res[f] = wo_buf[slot].astype(BF16)

    def hidden(f, slot):
        h = h_ref[...]
        g = jnp.dot(h, wg_res[f], preferred_element_type=F32)
        u = jnp.dot(h, wu_res[f], preferred_element_type=F32)
        act_buf[slot] = (g * jax.nn.sigmoid(g) * u).astype(BF16)

    def project(f, slot):
        acc_ref[...] += jnp.dot(act_buf[slot], wo_res[f], preferred_element_type=F32)

    def step(f, slot):
        fetch(f, slot)
        hidden(f, slot)
        project(f - 1, 1 - slot)

    @pl.when(first_tile)
    def _():
        for c in in_copies(0, 0):
            c.start()
        out_copy(0, 0).start()

    acc_ref[...] = jnp.zeros_like(acc_ref)
    fetch(0, 0)
    hidden(0, 0)

    assert nf % 2 == 1

    def pair(k, carry):
        step(2 * k + 1, 1)
        step(2 * k + 2, 0)
        return carry

    lax.fori_loop(0, (nf - 1) // 2, pair, 0)
    project(nf - 1, 0)

    def result():
        return _gated_residual(x_ref[...], acc_ref[...], gpost_ref[...], gt_ref[...])

    if split_out:
        i = pl.program_id(0)

        @pl.when(i < n_prompt_tiles)
        def _():
            out_refs[0][...] = result()

        @pl.when(i >= n_prompt_tiles)
        def _():
            out_refs[1][...] = result()
    else:
        out_refs[0][...] = result()


def _ffn(x, h, mods, layer, g_post, w_in, w_out, split_out):
    T = x.shape[0]
    tm, tf = TM_FFN, TF_FFN
    nf = D_FF // tf
    n_prompt_tiles = T // 2 // tm
    xrow = pl.BlockSpec((tm, D), lambda i: (i, 0))
    if split_out:
        out_specs = _split_specs(tm, D, n_prompt_tiles)
        out_shape = [jax.ShapeDtypeStruct((T // 2, D), F32)] * 2
    else:
        out_specs = [xrow]
        out_shape = [jax.ShapeDtypeStruct((T, D), F32)]
    hbm = pl.BlockSpec(memory_space=pl.ANY)
    return pl.pallas_call(
        functools.partial(_ffn_kernel, layer=layer, n_prompt_tiles=n_prompt_tiles,
                          split_out=split_out),
        grid=(T // tm,),
        in_specs=[xrow, xrow, _mod_spec(layer, 5, tm), _const_spec((1, D), layer), hbm, hbm],
        out_specs=out_specs,
        out_shape=out_shape,
        scratch_shapes=[pltpu.VMEM((2, D, tf), F32), pltpu.VMEM((2, D, tf), F32),
                        pltpu.VMEM((2, tf, D), F32), pltpu.VMEM((nf, D, tf), BF16),
                        pltpu.VMEM((nf, D, tf), BF16), pltpu.VMEM((nf, tf, D), BF16),
                        pltpu.VMEM((2, tm, tf), BF16), pltpu.VMEM((tm, D), F32),
                        pltpu.SemaphoreType.DMA((3, 2))],
        compiler_params=_cparams(1),
        name="ffn",
    )(x, h, mods, g_post, w_in, w_out)


def _rope_tables():
    t = jnp.arange(SEG)
    nf = HEAD_DIM // 4
    freqs = ROPE_BASE ** (-jnp.arange(nf, dtype=F32) / nf)

    def cs(pos):
        ang = pos.astype(F32)[:, None] * freqs[None, :]
        return jnp.cos(ang), jnp.sin(ang)

    cr, sr = cs(t // GRID_W)
    cc, sn = cs(t % GRID_W)
    cos = jnp.concatenate([cr, cr, cc, cc], axis=1)
    sin = jnp.concatenate([-sr, sr, -sn, sn], axis=1)
    reps = LANES // HEAD_DIM
    cos = jnp.tile(cos, (1, reps))
    sin = jnp.tile(sin, (1, reps))
    return (jnp.stack([jnp.ones_like(cos), cos]), jnp.stack([jnp.zeros_like(sin), sin]))


def _block_diag_ones(n):
    i = jnp.arange(n) // HEAD_DIM
    return (i[:, None] == i[None, :]).astype(BF16)


def kernel(x_prompt, x_sample, cache_attn_k, cache_attn_v, cache_na_k, cache_na_v, c, c_ctx,
           mod_w, mod_b, norm_mix_pre, norm_mix_post, norm_ffn_pre, norm_ffn_post,
           even_w_in, even_w_out, sgu_w, sgu_b, sgu_norm, q_norm, k_norm,
           odd_w_in, odd_w_out, na_rpb, ffn_w_in, ffn_w_out):
    nb_p, seq_p, _ = x_prompt.shape
    nb_s, seq_s, _ = x_sample.shape
    tp = nb_p * seq_p
    T = 2 * tp
    assert seq_s == SEG and tp == nb_s * seq_s and tp % SEG == 0

    cond = jnp.concatenate([jnp.broadcast_to(c_ctx[None, :], (tp // SEG, D)), c], axis=0)
    mods = _modulation(cond, mod_w, mod_b)

    cos, sin = _rope_tables()
    qw = B_HEADS * HEAD_DIM
    bdq = _block_diag_ones(qw)
    bdk = _block_diag_ones(LANES)
    past = cache_attn_k.shape[2]
    ctx_ak = cache_attn_k.reshape(nb_s, -1, past, B_KV_HEADS * HEAD_DIM)
    ctx_av = cache_attn_v.reshape(nb_s, -1, past, B_KV_HEADS * HEAD_DIM)
    ctx_nk = cache_na_k.reshape(nb_s, -1, past, C_HEADS * HEAD_DIM)
    ctx_nv = cache_na_v.reshape(nb_s, -1, past, C_HEADS * HEAD_DIM)
    g_mix_pre, g_mix_post, g_ffn_pre, g_ffn_post = (
        a.reshape(DEPTH, 1, D) for a in (norm_mix_pre, norm_mix_post, norm_ffn_pre, norm_ffn_post))
    n_even = even_w_in.shape[0]
    sgu_b3 = sgu_b.reshape(n_even, A_GROUPS, CHUNK, 1)
    sgu_n3 = sgu_norm.reshape(n_even, 1, A_WIDTH)
    qn3 = jnp.tile(q_norm, (1, B_HEADS)).reshape(n_even, 1, qw)
    kn3 = jnp.tile(k_norm, (1, B_KV_HEADS)).reshape(n_even, 1, LANES)
    rpb = jnp.pad(na_rpb, ((0, 0), (0, 0), (0, 0), (0, LANES - na_rpb.shape[3])))

    xs = (x_prompt.reshape(tp, D), x_sample.reshape(nb_s * seq_s, D))
    n_odd = odd_w_in.shape[0]
    attn_k, attn_v, na_cache = [], [], ()
    for l in range(DEPTH):
        if l % 2 == 0:
            e = l // 2
            a_out, q, kd, vd, kf, vf = _even_in(xs, mods, l, e, T, g_mix_pre, even_w_in, sgu_w,
                                                sgu_b3, sgu_n3, qn3, kn3, bdq, bdk, cos, sin)
            n_pairs = qw // LANES
            pairs_per_kv = n_pairs // B_KV_HEADS
            mix_p = _attention(q, kd, vd, n_batch=nb_p, seq=seq_p, row0=0, pairs_per_step=n_pairs,
                               pairs_per_kv=pairs_per_kv, batches_per_step=PROMPT_BATCHES_PER_STEP)
            mix_s = _attention(q, kd, vd, n_batch=nb_s, seq=seq_s, row0=tp,
                               pairs_per_step=pairs_per_kv, pairs_per_kv=pairs_per_kv,
                               ctx=(ctx_ak, ctx_av, e))
            attn_k.append(kf[:tp].reshape(nb_p, seq_p, B_KV_HEADS, HEAD_DIM))
            attn_v.append(vf[:tp].reshape(nb_p, seq_p, B_KV_HEADS, HEAD_DIM))
            w_out, li = even_w_out, e
        else:
            o = l // 2
            a_out = None
            q, k, v, *na_cache = _odd_in(xs[0], mods, l, o, n_odd, seq_p, g_mix_pre, odd_w_in,
                                         tuple(na_cache))
            mix_p = _attention(q, k, v, n_batch=nb_p, seq=seq_p, row0=0,
                               pairs_per_step=PAIRS_PER_STEP, pairs_per_kv=1,
                               batches_per_step=PROMPT_BATCHES_PER_STEP)
            mix_s = _na_attention(q, k, v, ctx_nk, ctx_nv, o, rpb, n_batch=nb_s, row0=tp)
            w_out, li = odd_w_out, o
        x, h = _out_proj(xs, a_out, mix_p, mix_s, mods, l, li, T, g_mix_post, g_ffn_pre, w_out)
        xs = tuple(_ffn(x, h, mods, l, g_ffn_post, ffn_w_in, ffn_w_out,
                        split_out=(l == DEPTH - 1)))

    y_prompt = xs[0].reshape(nb_p, seq_p, D)
    y_sample = xs[1].reshape(nb_s, seq_s, D)
    new_na_k, new_na_v = (a.reshape(nb_p, n_odd, seq_p, C_HEADS, HEAD_DIM) for a in na_cache)
    return (y_prompt, y_sample, jnp.stack(attn_k, axis=1), jnp.stack(attn_v, axis=1),
            new_na_k, new_na_v)
```

```python
import functools

import jax
import jax.numpy as jnp
from jax import lax
from jax.experimental import pallas as pl
from jax.experimental.pallas import tpu as pltpu

F32 = jnp.float32
BF16 = jnp.bfloat16

D = 1024
DEPTH = 4
HEAD_DIM = 64
GRID_W = 64
CHUNK = 128
A_WIDTH = D // 2
A_GROUPS = 4
B_HEADS = 8
B_KV_HEADS = 2
C_HEADS = 16
NA_ROWS = 8
NA_COLS = 16
D_FF = 2816
ROPE_BASE = 10000.0
EPS = 1e-6
NEG = -1e30
SEG = 1024
N_SEG = 8
LANES = 128
LOG2E = 1.4426950408889634
QSCALE = HEAD_DIM ** -0.5 * LOG2E

TM_PROJ = 512
TM_FFN = 1024
TF_FFN = 256
TQ_ATTN = 256
PAIRS_PER_STEP = 4
PROMPT_BATCHES_PER_STEP = 4
NA_QROWS = 4
NA_KROWS = 12
VMEM_LIMIT = 56 * 1024 * 1024


def _cparams(n_axes):
    return pltpu.CompilerParams(dimension_semantics=("arbitrary",) * n_axes,
                                vmem_limit_bytes=VMEM_LIMIT)


def _rms(x):
    return x * lax.rsqrt(jnp.mean(x * x, axis=-1, keepdims=True) + EPS)


def _mod_norm(x, g, shift, scale):
    return _rms(x) * (g * (1.0 + scale)) + shift


def _gated_residual(x, y, g, gate):
    return x + _rms(y) * (gate * g)


def _const_spec(shape, *lead):
    block = (None,) * len(lead) + tuple(shape)
    return pl.BlockSpec(block, lambda *_: tuple(lead) + (0,) * len(shape))


def _weight_spec(shape, *lead):
    block = (None,) * len(lead) + tuple(shape)
    return pl.BlockSpec(block, lambda *_: tuple(lead) + (0,) * len(shape),
                        pipeline_mode=pl.Buffered(1))


def _same_tile(i):
    return i


def _mod_spec(layer, j, tm, tile=_same_tile):
    return pl.BlockSpec((None, None, None, 1, D),
                        lambda i, *_: (layer, j, (tile(i) * tm) // SEG, 0, 0))


def _split_specs(tm, width, n_prompt_tiles, col=0, tile=_same_tile):
    return [pl.BlockSpec((tm, width),
                         lambda i, *_: (jnp.minimum(tile(i), n_prompt_tiles - 1), col)),
            pl.BlockSpec((tm, width),
                         lambda i, *_: (jnp.maximum(tile(i) - n_prompt_tiles, 0), col))]


def _pick(n_prompt_tiles, p_ref, s_ref, tile=None):
    tile = pl.program_id(0) if tile is None else tile
    return jnp.where(tile < n_prompt_tiles, p_ref[...], s_ref[...])


def _mod_kernel(cond_ref, w_ref, b_ref, o_ref):
    s = cond_ref[...]
    s = s * jax.nn.sigmoid(s)
    o_ref[...] = jnp.dot(s.astype(BF16), w_ref[...].astype(BF16),
                         preferred_element_type=F32) + b_ref[...]


def _modulation(cond, mod_w, mod_b):
    b = mod_b.reshape(DEPTH, 6, 1, D)
    out = pl.pallas_call(
        _mod_kernel,
        grid=(DEPTH, 6),
        in_specs=[pl.BlockSpec((N_SEG, D), lambda l, j: (0, 0)),
                  pl.BlockSpec((None, D, D), lambda l, j: (l, 0, j)),
                  pl.BlockSpec((None, None, 1, D), lambda l, j: (l, j, 0, 0))],
        out_specs=pl.BlockSpec((None, None, N_SEG, D), lambda l, j: (l, j, 0, 0)),
        out_shape=jax.ShapeDtypeStruct((DEPTH, 6, N_SEG, D), F32),
        compiler_params=_cparams(2),
        name="modulation",
    )(cond, mod_w, b)
    return out.reshape(DEPTH, 6, N_SEG, 1, D)


def _head_sumsq(y, bd_ref):
    sq = y * y
    hi = sq.astype(BF16)
    lo = (sq - hi.astype(F32)).astype(BF16)
    bd = bd_ref[...]
    return (jnp.dot(hi, bd, preferred_element_type=F32)
            + jnp.dot(lo, bd, preferred_element_type=F32))


def _rope(y, cos, sin):
    lane = lax.broadcasted_iota(jnp.int32, (1, LANES), 1)
    first = (lane & 16) == 0
    partner = jnp.where(first, pltpu.roll(y, LANES - 16, 1), pltpu.roll(y, 16, 1))
    return y * cos + partner * sin


def _even_in_kernel(*refs, n_tiles, n_prompt_tiles, split_x):
    refs = list(refs)
    i = pl.program_id(0)
    t_proj = jnp.minimum(i, n_tiles - 1)
    x = _pick(n_prompt_tiles, refs.pop(0), refs.pop(0), t_proj) if split_x else refs.pop(0)[...]
    (sh_ref, sc_ref, g_ref, w_ref, sguw_ref, sgub_ref, sgun_ref, qn_ref, kn_ref, bdq_ref, bdk_ref,
     cos_ref, sin_ref, a_ref, q_ref, kd_ref, vd_ref, kf_ref, vf_ref, wbf_ref, h_ref, z0_ref,
     z1_ref) = refs
    tm = a_ref.shape[0]
    qw = B_HEADS * HEAD_DIM
    kw = B_KV_HEADS * HEAD_DIM
    c_q = 2 * A_WIDTH
    c_k = c_q + qw
    n_chunks = tm // CHUNK
    gch = A_WIDTH // A_GROUPS
    lane = lax.broadcasted_iota(jnp.int32, (1, LANES), 1)
    low = lane < HEAD_DIM

    @pl.when(i == 0)
    def _():
        wbf_ref[...] = w_ref[...].astype(BF16)
        z1_ref[...] = jnp.zeros_like(z1_ref)

    h_ref[...] = _mod_norm(x, g_ref[...], sh_ref[...], sc_ref[...]).astype(BF16)

    def step(z_new_ref, z_ref):
        def project(c0, c1):
            z_new_ref[:, c0:c1] = jnp.dot(h_ref[...], wbf_ref[:, c0:c1], preferred_element_type=F32)

        project(0, A_WIDTH)
        u = jax.nn.gelu(z_ref[:, 0:A_WIDTH], approximate=True)
        project(A_WIDTH, c_q)
        v = jax.nn.gelu(z_ref[:, A_WIDTH:c_q], approximate=True)
        mu = jnp.mean(v, axis=-1, keepdims=True)
        vc = v - mu
        var = jnp.mean(vc * vc, axis=-1, keepdims=True)
        vn = (vc * lax.rsqrt(var + EPS) * sgun_ref[...]).astype(BF16)
        for g in range(A_GROUPS):
            rhs = jnp.concatenate([vn[n * CHUNK:(n + 1) * CHUNK, g * gch:(g + 1) * gch]
                                   for n in range(n_chunks)], axis=1)
            mixed = jnp.dot(sguw_ref[g].astype(BF16), rhs, preferred_element_type=F32)
            bias = sgub_ref[g]
            for n in range(n_chunks):
                blk = (mixed[:, n * gch:(n + 1) * gch] + bias) * u[n * CHUNK:(n + 1) * CHUNK,
                                                                  g * gch:(g + 1) * gch]
                a_ref[n * CHUNK:(n + 1) * CHUNK, g * gch:(g + 1) * gch] = blk.astype(BF16)

        project(c_q, c_k)
        q = z_ref[:, c_q:c_k]
        q = q * lax.rsqrt(_head_sumsq(q, bdq_ref) * (1.0 / HEAD_DIM) + EPS) * qn_ref[...]
        cos = cos_ref[...]
        sin = sin_ref[...]
        for j in range(qw // LANES):
            qj = _rope(q[:, j * LANES:(j + 1) * LANES], cos, sin) * QSCALE
            q_ref[:, j * LANES:(j + 1) * LANES] = qj.astype(BF16)

        project(c_k, c_k + 2 * kw)
        k = z_ref[:, c_k:c_k + kw]
        k = k * lax.rsqrt(_head_sumsq(k, bdk_ref) * (1.0 / HEAD_DIM) + EPS) * kn_ref[...]
        kf_ref[...] = k
        vals = z_ref[:, c_k + kw:c_k + 2 * kw]
        vf_ref[...] = vals
        for src, dst in ((_rope(k, cos, sin), kd_ref), (vals, vd_ref)):
            sw = pltpu.roll(src, HEAD_DIM, 1)
            dst[:, 0:LANES] = jnp.where(low, src, sw).astype(BF16)
            dst[:, LANES:2 * LANES] = jnp.where(low, sw, src).astype(BF16)

    @pl.when(i % 2 == 0)
    def _():
        step(z0_ref, z1_ref)

    @pl.when(i % 2 == 1)
    def _():
        step(z1_ref, z0_ref)


def _even_in(xs, mods, layer, e, T, g_pre, w_in, sgu_w, sgu_b, sgu_norm, q_norm, k_norm,
             bdq, bdk, cos, sin):
    tm = TM_PROJ
    n_in = w_in.shape[2]
    n_tiles = T // tm
    tiles_per_seg = SEG // tm
    n_prompt_tiles = n_tiles // 2
    split_x = len(xs) == 2
    proj = lambda i: jnp.minimum(i, n_tiles - 1)
    post = lambda i: jnp.maximum(i - 1, 0)
    tab_spec = pl.BlockSpec((None, tm, LANES),
                            lambda i: (post(i) // n_prompt_tiles, post(i) % tiles_per_seg, 0))
    row = lambda w, tile: pl.BlockSpec((tm, w), lambda i: (tile(i), 0))
    x_specs = (_split_specs(tm, D, n_prompt_tiles, tile=proj) if split_x else [row(D, proj)])
    qw = B_HEADS * HEAD_DIM
    return pl.pallas_call(
        functools.partial(_even_in_kernel, n_tiles=n_tiles, n_prompt_tiles=n_prompt_tiles,
                          split_x=split_x),
        grid=(n_tiles + 1,),
        in_specs=x_specs + [
            _mod_spec(layer, 0, tm, proj), _mod_spec(layer, 1, tm, proj),
            _const_spec((1, D), layer),
            _weight_spec((D, n_in), e), _const_spec((A_GROUPS, CHUNK, CHUNK), e),
            _const_spec((A_GROUPS, CHUNK, 1), e), _const_spec((1, A_WIDTH), e),
            _const_spec((1, qw), e), _const_spec((1, LANES), e),
            _const_spec((qw, qw)), _const_spec((LANES, LANES)), tab_spec, tab_spec],
        out_specs=[row(A_WIDTH, post), row(qw, post), row(2 * LANES, post), row(2 * LANES, post),
                   row(LANES, post), row(LANES, post)],
        out_shape=[jax.ShapeDtypeStruct((T, A_WIDTH), BF16),
                   jax.ShapeDtypeStruct((T, qw), BF16),
                   jax.ShapeDtypeStruct((T, 2 * LANES), BF16),
                   jax.ShapeDtypeStruct((T, 2 * LANES), BF16),
                   jax.ShapeDtypeStruct((T, LANES), F32),
                   jax.ShapeDtypeStruct((T, LANES), F32)],
        scratch_shapes=[pltpu.VMEM((D, n_in), BF16), pltpu.VMEM((tm, D), BF16),
                        pltpu.VMEM((tm, n_in), F32), pltpu.VMEM((tm, n_in), F32)],
        compiler_params=_cparams(1),
        name="even_in_proj",
    )(*xs, mods, mods, g_pre, w_in, sgu_w, sgu_b, sgu_norm, q_norm, k_norm, bdq, bdk, cos, sin)


def _odd_in_kernel(x_ref, sh_ref, sc_ref, g_ref, w_ref, *rest, n_prompt_tiles, slot, all_slots):
    q_ref, k_ref, v_ref, kf_ref, vf_ref, wbf_ref = rest[-6:]
    i = pl.program_id(0)
    hw = C_HEADS * HEAD_DIM

    @pl.when(i == 0)
    def _():
        for c in range(3):
            wbf_ref[:, c * hw:(c + 1) * hw] = w_ref[:, c * hw:(c + 1) * hw].astype(BF16)

    h = _mod_norm(x_ref[...], g_ref[...], sh_ref[...], sc_ref[...]).astype(BF16)
    q = jnp.dot(h, wbf_ref[:, 0:hw], preferred_element_type=F32)
    q_ref[...] = (q * QSCALE).astype(BF16)
    k = jnp.dot(h, wbf_ref[:, hw:2 * hw], preferred_element_type=F32)
    k_ref[...] = k.astype(BF16)
    v = jnp.dot(h, wbf_ref[:, 2 * hw:3 * hw], preferred_element_type=F32)
    v_ref[...] = v.astype(BF16)

    @pl.when(i < n_prompt_tiles)
    def _():
        for src, dst in ((k, kf_ref), (v, vf_ref)):
            if all_slots:
                val = src.reshape((dst.shape[0],) + dst.shape[2:])
                for s in range(dst.shape[1]):
                    dst[:, s] = val if s == slot else jnp.zeros_like(val)
            else:
                dst[...] = src.reshape(dst.shape)


def _odd_in(x, mods, layer, o, n_odd, seq_p, g_pre, w_in, caches):
    T = x.shape[0]
    tm = TM_PROJ
    hw = C_HEADS * HEAD_DIM
    n_prompt_tiles = T // 2 // tm
    bt = tm // seq_p
    row = pl.BlockSpec((tm, hw), lambda i: (i, 0))
    all_slots = not caches
    if all_slots:
        crow = pl.BlockSpec((bt, n_odd, seq_p, hw),
                            lambda i: (jnp.minimum(i, n_prompt_tiles - 1), 0, 0, 0))
    else:
        crow = pl.BlockSpec((bt, None, seq_p, hw),
                            lambda i: (jnp.minimum(i, n_prompt_tiles - 1), o, 0, 0))
    cshape = jax.ShapeDtypeStruct((T // 2 // seq_p, n_odd, seq_p, hw), F32)
    n_in = 5
    return pl.pallas_call(
        functools.partial(_odd_in_kernel, n_prompt_tiles=n_prompt_tiles, slot=o,
                          all_slots=all_slots),
        grid=(T // tm,),
        in_specs=[pl.BlockSpec((tm, D), lambda i: (i, 0)), _mod_spec(layer, 0, tm),
                  _mod_spec(layer, 1, tm), _const_spec((1, D), layer),
                  _weight_spec((D, 3 * hw), o)]
                 + [pl.BlockSpec(memory_space=pl.ANY)] * len(caches),
        out_specs=[row, row, row, crow, crow],
        out_shape=[jax.ShapeDtypeStruct((T, hw), BF16)] * 3 + [cshape, cshape],
        input_output_aliases={n_in + j: 3 + j for j in range(len(caches))},
        scratch_shapes=[pltpu.VMEM((D, 3 * hw), BF16)],
        compiler_params=_cparams(1),
        name="odd_in_proj",
    )(x, mods, mods, g_pre, w_in, *caches)


def _softmax_pv(pieces):
    m = None
    for s, _ in pieces:
        ms = jnp.max(s, axis=-1, keepdims=True)
        m = ms if m is None else jnp.maximum(m, ms)
    den = None
    acc = None
    for s, val in pieces:
        p = jnp.exp2(s - m)
        ls = jnp.sum(p, axis=-1, keepdims=True)
        den = ls if den is None else den + ls
        o = jnp.dot(p.astype(BF16), val, preferred_element_type=F32)
        acc = o if acc is None else acc + o
    return acc / den


def _attn_kernel(*refs, has_ctx, seq, tq, pairs_per_kv):
    if has_ctx:
        q_ref, k_ref, v_ref, kc_ref, vc_ref, o_ref = refs
    else:
        q_ref, k_ref, v_ref, o_ref = refs
    lane = lax.broadcasted_iota(jnp.int32, (1, LANES), 1)
    masks = [lane < HEAD_DIM, lane >= HEAD_DIM]
    if has_ctx:
        kv_head = pl.program_id(1)
        sel = jnp.where(masks[0], 0, 1) == kv_head
        kc = jnp.where(sel, kc_ref[...], 0.0)
        kc = (kc + pltpu.roll(kc, HEAD_DIM, 1)).astype(BF16)
        vc = jnp.where(sel, vc_ref[...], 0.0)
        vc = (vc + pltpu.roll(vc, HEAD_DIM, 1)).astype(BF16)
        vch = [jnp.where(mh, vc, jnp.zeros_like(vc)) for mh in masks]
    nt = (((1,), (1,)), ((), ()))
    units = [(r0, j) for r0 in range(0, q_ref.shape[0], seq) for j in range(q_ref.shape[1] // LANES)]
    for r0, j in units:
        cols = slice(j * LANES, (j + 1) * LANES)
        kcols = slice((j // pairs_per_kv) * LANES, (j // pairs_per_kv + 1) * LANES)
        k = k_ref[r0:r0 + seq, kcols]
        v = v_ref[r0:r0 + seq, kcols]
        vh = [jnp.where(mh, v, jnp.zeros_like(v)) for mh in masks]
        for t in range(r0 // tq, (r0 + seq) // tq):
            q = q_ref[t * tq:(t + 1) * tq, cols]
            out = None
            for hh, mh in enumerate(masks):
                qm = jnp.where(mh, q, jnp.zeros_like(q))
                pieces = [(lax.dot_general(qm, k, nt, preferred_element_type=F32), vh[hh])]
                if has_ctx:
                    pieces.append((lax.dot_general(qm, kc, nt, preferred_element_type=F32),
                                   vch[hh]))
                o = _softmax_pv(pieces)
                out = o if out is None else out + o
            o_ref[t * tq:(t + 1) * tq, cols] = out.astype(BF16)


def _attention(q, k, v, *, n_batch, seq, row0, pairs_per_step, pairs_per_kv, batches_per_step=1,
               ctx=None):
    rows = batches_per_step * seq
    assert row0 % rows == 0 and n_batch % batches_per_step == 0
    blk0 = row0 // rows
    n_groups = q.shape[1] // (pairs_per_step * LANES)
    qw = pairs_per_step * LANES
    kw = qw // pairs_per_kv
    in_specs = [pl.BlockSpec((rows, qw), lambda b, g: (blk0 + b, g)),
                pl.BlockSpec((rows, kw), lambda b, g: (blk0 + b, g)),
                pl.BlockSpec((rows, kw), lambda b, g: (blk0 + b, g))]
    args = [q, k, v]
    if ctx is not None:
        assert kw == LANES and batches_per_step == 1
        kc, vc, e = ctx
        sc = kc.shape[2]
        cspec = pl.BlockSpec((None, None, sc, LANES), lambda b, g: (b, e, 0, 0))
        in_specs += [cspec, cspec]
        args += [kc, vc]
    return pl.pallas_call(
        functools.partial(_attn_kernel, has_ctx=ctx is not None, seq=seq, tq=min(TQ_ATTN, seq),
                          pairs_per_kv=pairs_per_kv),
        grid=(n_batch // batches_per_step, n_groups),
        in_specs=in_specs,
        out_specs=pl.BlockSpec((rows, qw), lambda b, g: (b, g)),
        out_shape=jax.ShapeDtypeStruct((n_batch * seq, q.shape[1]), BF16),
        compiler_params=_cparams(2),
        name="attention",
    )(*args)


def _na_row_start(r):
    rows = SEG // GRID_W
    return min(max(r - NA_ROWS // 2, 0), rows - NA_ROWS)


def _na_windows():
    rows = SEG // GRID_W
    windows = []
    for qb in range(rows // NA_QROWS):
        lo = _na_row_start(qb * NA_QROWS)
        hi = _na_row_start(qb * NA_QROWS + NA_QROWS - 1) + NA_ROWS
        n = hi - lo + (hi - lo) % 2
        ws = min(lo, rows - n)
        assert ws <= lo and hi <= ws + n <= rows and n <= NA_KROWS
        windows.append((ws, n))
    return windows


def _na_kernel(q_ref, k_ref, v_ref, kc_ref, vc_ref, r_ref, o_ref, bias_ref, tab_ref, *, windows):
    lane = lax.broadcasted_iota(jnp.int32, (1, LANES), 1)
    low = lane < HEAD_DIM

    @pl.when(pl.program_id(1) == 0)
    def _():
        qcol = lax.broadcasted_iota(jnp.int32, (GRID_W, LANES), 0)
        kcol = lax.broadcasted_iota(jnp.int32, (GRID_W, LANES), 1) & (GRID_W - 1)
        start = jnp.clip(qcol - NA_COLS // 2, 0, GRID_W - NA_COLS)
        inside = (kcol >= start) & (kcol < start + NA_COLS)
        for hh in range(LANES // HEAD_DIM):
            for d in range(2 * NA_ROWS - 1):
                base = jnp.broadcast_to(r_ref[hh, d:d + 1, :], (GRID_W, LANES))
                lo_t = pltpu.roll(base, LANES - (NA_COLS - 1), 1, stride=1, stride_axis=0)
                hi_t = pltpu.roll(base, GRID_W - (NA_COLS - 1), 1, stride=1, stride_axis=0)
                tab_ref[hh, d] = jnp.where(inside, jnp.where(low, lo_t, hi_t) * LOG2E, NEG)
        neg = jnp.full((GRID_W, LANES), NEG, F32)
        for hh in range(LANES // HEAD_DIM):
            for qb, (ws, nrows) in enumerate(windows):
                for i in range(NA_QROWS):
                    r = qb * NA_QROWS + i
                    rs = _na_row_start(r)
                    for jp in range(nrows // 2):
                        kr = ws + 2 * jp
                        ok = [rs <= kr + d < rs + NA_ROWS for d in (0, 1)]
                        if not any(ok):
                            blk = neg
                        else:
                            t0 = tab_ref[hh, kr - r + NA_ROWS - 1] if ok[0] else neg
                            t1 = tab_ref[hh, kr + 1 - r + NA_ROWS - 1] if ok[1] else neg
                            blk = jnp.where(low, t0, t1)
                        bias_ref[hh, qb, i * GRID_W:(i + 1) * GRID_W,
                                 jp * LANES:(jp + 1) * LANES] = blk

    masks = [low, jnp.logical_not(low)]
    kc = kc_ref[...].astype(BF16)
    vc = vc_ref[...].astype(BF16)
    vch = [jnp.where(mh, vc, jnp.zeros_like(vc)) for mh in masks]
    nt = (((1,), (1,)), ((), ()))
    nq = NA_QROWS * GRID_W
    for qb, (ws, nrows) in enumerate(windows):
        nk = nrows * GRID_W
        q = q_ref[qb * nq:(qb + 1) * nq, :]
        kw = k_ref[ws * GRID_W:ws * GRID_W + nk, :]
        vw = v_ref[ws * GRID_W:ws * GRID_W + nk, :]
        out = None
        for hh, mh in enumerate(masks):
            qm = jnp.where(mh, q, jnp.zeros_like(q))
            s_win = (lax.dot_general(qm, kw, nt, preferred_element_type=F32)
                     + bias_ref[hh, qb, :, 0:nk])
            s_ctx = lax.dot_general(qm, kc, nt, preferred_element_type=F32)
            o = _softmax_pv([(s_win, jnp.where(mh, vw, jnp.zeros_like(vw))), (s_ctx, vch[hh])])
            out = o if out is None else out + o
        o_ref[qb * nq:(qb + 1) * nq, :] = out.astype(BF16)


def _na_attention(q, k, v, kc, vc, o_idx, rpb, *, n_batch, row0):
    blk0 = row0 // SEG
    n_pairs = q.shape[1] // LANES
    sc = kc.shape[2]
    windows = _na_windows()
    heads = LANES // HEAD_DIM
    nr = 2 * NA_ROWS - 1
    qkv = pl.BlockSpec((SEG, LANES), lambda p, b: (blk0 + b, p))
    cspec = pl.BlockSpec((None, None, sc, LANES), lambda p, b: (b, o_idx, 0, p))
    rspec = pl.BlockSpec((None, heads, nr, LANES), lambda p, b: (o_idx, p, 0, 0))
    return pl.pallas_call(
        functools.partial(_na_kernel, windows=windows),
        grid=(n_pairs, n_batch),
        in_specs=[qkv, qkv, qkv, cspec, cspec, rspec],
        out_specs=pl.BlockSpec((SEG, LANES), lambda p, b: (b, p)),
        out_shape=jax.ShapeDtypeStruct((n_batch * SEG, n_pairs * LANES), BF16),
        scratch_shapes=[pltpu.VMEM((heads, len(windows), NA_QROWS * GRID_W, NA_KROWS * GRID_W), F32),
                        pltpu.VMEM((heads, nr, GRID_W, LANES), F32)],
        compiler_params=_cparams(2),
        name="na_attention",
    )(q, k, v, kc, vc, rpb)


def _out_kernel(*refs, n_prompt_tiles, split_x, has_a):
    refs = list(refs)
    x = _pick(n_prompt_tiles, refs.pop(0), refs.pop(0)) if split_x else refs.pop(0)[...]
    gt_ref, g_ref, fsh_ref, fsc_ref, gf_ref, w_ref = refs[:6]
    a_ref = refs[6] if has_a else None
    bp_ref, bs_ref, o_ref, h_ref, wbf_ref = refs[6 + has_a:]

    @pl.when(pl.program_id(0) == 0)
    def _():
        wbf_ref[...] = w_ref[...].astype(BF16)

    b = _pick(n_prompt_tiles, bp_ref, bs_ref)
    if has_a:
        half = a_ref.shape[1]
        y = (jnp.dot(a_ref[...], wbf_ref[0:half, :], preferred_element_type=F32)
             + jnp.dot(b, wbf_ref[half:2 * half, :], preferred_element_type=F32))
    else:
        y = jnp.dot(b, wbf_ref[...], preferred_element_type=F32)
    x1 = _gated_residual(x, y, g_ref[...], gt_ref[...])
    o_ref[...] = x1
    h_ref[...] = _mod_norm(x1, gf_ref[...], fsh_ref[...], fsc_ref[...]).astype(BF16)


def _out_proj(xs, a, bp, bs, mods, layer, li, T, g_post, g_ffn_pre, w_out):
    tm = TM_PROJ
    n_prompt_tiles = T // 2 // tm
    split_x = len(xs) == 2
    has_a = a is not None
    row = lambda w: pl.BlockSpec((tm, w), lambda i: (i, 0))
    x_specs = _split_specs(tm, D, n_prompt_tiles) if split_x else [row(D)]
    a_specs = [row(a.shape[1])] if has_a else []
    return pl.pallas_call(
        functools.partial(_out_kernel, n_prompt_tiles=n_prompt_tiles, split_x=split_x, has_a=has_a),
        grid=(T // tm,),
        in_specs=x_specs + [_mod_spec(layer, 2, tm), _const_spec((1, D), layer),
                            _mod_spec(layer, 3, tm), _mod_spec(layer, 4, tm),
                            _const_spec((1, D), layer), _weight_spec((D, D), li)] + a_specs
                 + _split_specs(tm, bp.shape[1], n_prompt_tiles),
        out_specs=[row(D), row(D)],
        out_shape=[jax.ShapeDtypeStruct((T, D), F32), jax.ShapeDtypeStruct((T, D), BF16)],
        scratch_shapes=[pltpu.VMEM((D, D), BF16)],
        compiler_params=_cparams(1),
        name="out_proj",
    )(*xs, mods, g_post, mods, mods, g_ffn_pre, w_out, *([a] if has_a else []), bp, bs)


def _ffn_kernel(x_ref, h_ref, gt_ref, gpost_ref, win_ref, wout_ref, *rest,
                layer, n_prompt_tiles, split_out):
    out_refs = rest[:-9]
    wg_buf, wu_buf, wo_buf, wg_res, wu_res, wo_res, act_buf, acc_ref, sem = rest[-9:]
    nf, _, tf = wg_res.shape
    first_tile = pl.program_id(0) == 0

    def aligned(f):
        return f * tf if isinstance(f, int) else pl.multiple_of(f * tf, tf)

    def in_copies(f, slot):
        col = aligned(f)
        return (pltpu.make_async_copy(win_ref.at[layer, :, pl.ds(col, tf)], wg_buf.at[slot],
                                      sem.at[0, slot]),
                pltpu.make_async_copy(win_ref.at[layer, :, pl.ds(D_FF + col, tf)], wu_buf.at[slot],
                                      sem.at[1, slot]))

    def out_copy(f, slot):
        row = aligned(f)
        return pltpu.make_async_copy(wout_ref.at[layer, pl.ds(row, tf), :], wo_buf.at[slot],
                                     sem.at[2, slot])

    def fetch(f, slot):
        @pl.when(first_tile)
        def _():
            for c in in_copies(f, slot):
                c.wait()
            out_copy(f, slot).wait()

            nxt = min(f + 1, nf - 1) if isinstance(f, int) else jnp.minimum(f + 1, nf - 1)

            @pl.when(jnp.asarray(f + 1 < nf))
            def _():
                for c in in_copies(nxt, 1 - slot):
                    c.start()
                out_copy(nxt, 1 - slot).start()

            wg_res[f] = wg_buf[slot].astype(BF16)
            wu_res[f] = wu_buf[slot].astype(BF16)
            wo_res[f] = wo_buf[slot].astype(BF16)

    def hidden(f, slot):
        h = h_ref[...]
        g = jnp.dot(h, wg_res[f], preferred_element_type=F32)
        u = jnp.dot(h, wu_res[f], preferred_element_type=F32)
        act_buf[slot] = (g * jax.nn.sigmoid(g) * u).astype(BF16)

    def project(f, slot):
        acc_ref[...] += jnp.dot(act_buf[slot], wo_res[f], preferred_element_type=F32)

    def step(f, slot):
        fetch(f, slot)
        hidden(f, slot)
        project(f - 1, 1 - slot)

    @pl.when(first_tile)
    def _():
        for c in in_copies(0, 0):
            c.start()
        out_copy(0, 0).start()

    acc_ref[...] = jnp.zeros_like(acc_ref)
    fetch(0, 0)
    hidden(0, 0)

    assert nf % 2 == 1

    def pair(k, carry):
        step(2 * k + 1, 1)
        step(2 * k + 2, 0)
        return carry

    lax.fori_loop(0, (nf - 1) // 2, pair, 0)
    project(nf - 1, 0)

    def result():
        return _gated_residual(x_ref[...], acc_ref[...], gpost_ref[...], gt_ref[...])

    if split_out:
        i = pl.program_id(0)

        @pl.when(i < n_prompt_tiles)
        def _():
            out_refs[0][...] = result()

        @pl.when(i >= n_prompt_tiles)
        def _():
            out_refs[1][...] = result()
    else:
        out_refs[0][...] = result()


def _ffn(x, h, mods, layer, g_post, w_in, w_out, split_out):
    T = x.shape[0]
    tm, tf = TM_FFN, TF_FFN
    nf = D_FF // tf
    n_prompt_tiles = T // 2 // tm
    xrow = pl.BlockSpec((tm, D), lambda i: (i, 0))
    if split_out:
        out_specs = _split_specs(tm, D, n_prompt_tiles)
        out_shape = [jax.ShapeDtypeStruct((T // 2, D), F32)] * 2
    else:
        out_specs = [xrow]
        out_shape = [jax.ShapeDtypeStruct((T, D), F32)]
    hbm = pl.BlockSpec(memory_space=pl.ANY)
    return pl.pallas_call(
        functools.partial(_ffn_kernel, layer=layer, n_prompt_tiles=n_prompt_tiles,
                          split_out=split_out),
        grid=(T // tm,),
        in_specs=[xrow, xrow, _mod_spec(layer, 5, tm), _const_spec((1, D), layer), hbm, hbm],
        out_specs=out_specs,
        out_shape=out_shape,
        scratch_shapes=[pltpu.VMEM((2, D, tf), F32), pltpu.VMEM((2, D, tf), F32),
                        pltpu.VMEM((2, tf, D), F32), pltpu.VMEM((nf, D, tf), BF16),
                        pltpu.VMEM((nf, D, tf), BF16), pltpu.VMEM((nf, tf, D), BF16),
                        pltpu.VMEM((2, tm, tf), BF16), pltpu.VMEM((tm, D), F32),
                        pltpu.SemaphoreType.DMA((3, 2))],
        compiler_params=_cparams(1),
        name="ffn",
    )(x, h, mods, g_post, w_in, w_out)


def _rope_tables():
    t = jnp.arange(SEG)
    nf = HEAD_DIM // 4
    freqs = ROPE_BASE ** (-jnp.arange(nf, dtype=F32) / nf)

    def cs(pos):
        ang = pos.astype(F32)[:, None] * freqs[None, :]
        return jnp.cos(ang), jnp.sin(ang)

    cr, sr = cs(t // GRID_W)
    cc, sn = cs(t % GRID_W)
    cos = jnp.concatenate([cr, cr, cc, cc], axis=1)
    sin = jnp.concatenate([-sr, sr, -sn, sn], axis=1)
    reps = LANES // HEAD_DIM
    cos = jnp.tile(cos, (1, reps))
    sin = jnp.tile(sin, (1, reps))
    return (jnp.stack([jnp.ones_like(cos), cos]), jnp.stack([jnp.zeros_like(sin), sin]))


def _block_diag_ones(n):
    i = jnp.arange(n) // HEAD_DIM
    return (i[:, None] == i[None, :]).astype(BF16)


def kernel(x_prompt, x_sample, cache_attn_k, cache_attn_v, cache_na_k, cache_na_v, c, c_ctx,
           mod_w, mod_b, norm_mix_pre, norm_mix_post, norm_ffn_pre, norm_ffn_post,
           even_w_in, even_w_out, sgu_w, sgu_b, sgu_norm, q_norm, k_norm,
           odd_w_in, odd_w_out, na_rpb, ffn_w_in, ffn_w_out):
    nb_p, seq_p, _ = x_prompt.shape
    nb_s, seq_s, _ = x_sample.shape
    tp = nb_p * seq_p
    T = 2 * tp
    assert seq_s == SEG and tp == nb_s * seq_s and tp % SEG == 0

    cond = jnp.concatenate([jnp.broadcast_to(c_ctx[None, :], (tp // SEG, D)), c], axis=0)
    mods = _modulation(cond, mod_w, mod_b)

    cos, sin = _rope_tables()
    qw = B_HEADS * HEAD_DIM
    bdq = _block_diag_ones(qw)
    bdk = _block_diag_ones(LANES)
    past = cache_attn_k.shape[2]
    ctx_ak = cache_attn_k.reshape(nb_s, -1, past, B_KV_HEADS * HEAD_DIM)
    ctx_av = cache_attn_v.reshape(nb_s, -1, past, B_KV_HEADS * HEAD_DIM)
    ctx_nk = cache_na_k.reshape(nb_s, -1, past, C_HEADS * HEAD_DIM)
    ctx_nv = cache_na_v.reshape(nb_s, -1, past, C_HEADS * HEAD_DIM)
    g_mix_pre, g_mix_post, g_ffn_pre, g_ffn_post = (
        a.reshape(DEPTH, 1, D) for a in (norm_mix_pre, norm_mix_post, norm_ffn_pre, norm_ffn_post))
    n_even = even_w_in.shape[0]
    sgu_b3 = sgu_b.reshape(n_even, A_GROUPS, CHUNK, 1)
    sgu_n3 = sgu_norm.reshape(n_even, 1, A_WIDTH)
    qn3 = jnp.tile(q_norm, (1, B_HEADS)).reshape(n_even, 1, qw)
    kn3 = jnp.tile(k_norm, (1, B_KV_HEADS)).reshape(n_even, 1, LANES)
    rpb = jnp.pad(na_rpb, ((0, 0), (0, 0), (0, 0), (0, LANES - na_rpb.shape[3])))

    xs = (x_prompt.reshape(tp, D), x_sample.reshape(nb_s * seq_s, D))
    n_odd = odd_w_in.shape[0]
    attn_k, attn_v, na_cache = [], [], ()
    for l in range(DEPTH):
        if l % 2 == 0:
            e = l // 2
            a_out, q, kd, vd, kf, vf = _even_in(xs, mods, l, e, T, g_mix_pre, even_w_in, sgu_w,
                                                sgu_b3, sgu_n3, qn3, kn3, bdq, bdk, cos, sin)
            n_pairs = qw // LANES
            pairs_per_kv = n_pairs // B_KV_HEADS
            mix_p = _attention(q, kd, vd, n_batch=nb_p, seq=seq_p, row0=0, pairs_per_step=n_pairs,
                               pairs_per_kv=pairs_per_kv, batches_per_step=PROMPT_BATCHES_PER_STEP)
            mix_s = _attention(q, kd, vd, n_batch=nb_s, seq=seq_s, row0=tp,
                               pairs_per_step=pairs_per_kv, pairs_per_kv=pairs_per_kv,
                               ctx=(ctx_ak, ctx_av, e))
            attn_k.append(kf[:tp].reshape(nb_p, seq_p, B_KV_HEADS, HEAD_DIM))
            attn_v.append(vf[:tp].reshape(nb_p, seq_p, B_KV_HEADS, HEAD_DIM))
            w_out, li = even_w_out, e
        else:
            o = l // 2
            a_out = None
            q, k, v, *na_cache = _odd_in(xs[0], mods, l, o, n_odd, seq_p, g_mix_pre, odd_w_in,
                                         tuple(na_cache))
            mix_p = _attention(q, k, v, n_batch=nb_p, seq=seq_p, row0=0,
                               pairs_per_step=PAIRS_PER_STEP, pairs_per_kv=1,
                               batches_per_step=PROMPT_BATCHES_PER_STEP)
            mix_s = _na_attention(q, k, v, ctx_nk, ctx_nv, o, rpb, n_batch=nb_s, row0=tp)
            w_out, li = odd_w_out, o
        x, h = _out_proj(xs, a_out, mix_p, mix_s, mods, l, li, T, g_mix_post, g_ffn_pre, w_out)
        xs = tuple(_ffn(x, h, mods, l, g_ffn_post, ffn_w_in, ffn_w_out,
                        split_out=(l == DEPTH - 1)))

    y_prompt = xs[0].reshape(nb_p, seq_p, D)
    y_sample = xs[1].reshape(nb_s, seq_s, D)
    new_na_k, new_na_v = (a.reshape(nb_p, n_odd, seq_p, C_HEADS, HEAD_DIM) for a in na_cache)
    return (y_prompt, y_sample, jnp.stack(attn_k, axis=1), jnp.stack(attn_v, axis=1),
            new_na_k, new_na_v)
```

```python
import functools

import jax
import jax.numpy as jnp
from jax import lax
from jax.experimental import pallas as pl
from jax.experimental.pallas import tpu as pltpu

F32 = jnp.float32
BF16 = jnp.bfloat16

D = 1024
DEPTH = 4
HEAD_DIM = 64
GRID_W = 64
CHUNK = 128
A_WIDTH = D // 2
A_GROUPS = 4
B_HEADS = 8
B_KV_HEADS = 2
C_HEADS = 16
NA_ROWS = 8
NA_COLS = 16
D_FF = 2816
ROPE_BASE = 10000.0
EPS = 1e-6
NEG = -1e30
SEG = 1024
N_SEG = 8
LANES = 128
LOG2E = 1.4426950408889634
QSCALE = HEAD_DIM ** -0.5 * LOG2E

TM_PROJ = 512
TM_FFN = 1024
TF_FFN = 256
TQ_ATTN = 256
PAIRS_PER_STEP = 4
PROMPT_BATCHES_PER_STEP = 4
NA_QROWS = 4
NA_KROWS = 12
VMEM_LIMIT = 56 * 1024 * 1024


def _cparams(n_axes):
    return pltpu.CompilerParams(dimension_semantics=("arbitrary",) * n_axes,
                                vmem_limit_bytes=VMEM_LIMIT)


def _rms(x):
    return x * lax.rsqrt(jnp.mean(x * x, axis=-1, keepdims=True) + EPS)


def _mod_norm(x, g, shift, scale):
    return _rms(x) * (g * (1.0 + scale)) + shift


def _gated_residual(x, y, g, gate):
    return x + _rms(y) * (gate * g)


def _const_spec(shape, *lead):
    block = (None,) * len(lead) + tuple(shape)
    return pl.BlockSpec(block, lambda *_: tuple(lead) + (0,) * len(shape))


def _weight_spec(shape, *lead):
    block = (None,) * len(lead) + tuple(shape)
    return pl.BlockSpec(block, lambda *_: tuple(lead) + (0,) * len(shape),
                        pipeline_mode=pl.Buffered(1))


def _same_tile(i):
    return i


def _mod_spec(layer, j, tm, tile=_same_tile):
    return pl.BlockSpec((None, None, None, 1, D),
                        lambda i, *_: (layer, j, (tile(i) * tm) // SEG, 0, 0))


def _split_specs(tm, width, n_prompt_tiles, col=0, tile=_same_tile):
    return [pl.BlockSpec((tm, width),
                         lambda i, *_: (jnp.minimum(tile(i), n_prompt_tiles - 1), col)),
            pl.BlockSpec((tm, width),
                         lambda i, *_: (jnp.maximum(tile(i) - n_prompt_tiles, 0), col))]


def _pick(n_prompt_tiles, p_ref, s_ref, tile=None):
    tile = pl.program_id(0) if tile is None else tile
    return jnp.where(tile < n_prompt_tiles, p_ref[...], s_ref[...])


def _mod_kernel(cond_ref, w_ref, b_ref, o_ref):
    s = cond_ref[...]
    s = s * jax.nn.sigmoid(s)
    o_ref[...] = jnp.dot(s.astype(BF16), w_ref[...].astype(BF16),
                         preferred_element_type=F32) + b_ref[...]


def _modulation(cond, mod_w, mod_b):
    b = mod_b.reshape(DEPTH, 6, 1, D)
    out = pl.pallas_call(
        _mod_kernel,
        grid=(DEPTH, 6),
        in_specs=[pl.BlockSpec((N_SEG, D), lambda l, j: (0, 0)),
                  pl.BlockSpec((None, D, D), lambda l, j: (l, 0, j)),
                  pl.BlockSpec((None, None, 1, D), lambda l, j: (l, j, 0, 0))],
        out_specs=pl.BlockSpec((None, None, N_SEG, D), lambda l, j: (l, j, 0, 0)),
        out_shape=jax.ShapeDtypeStruct((DEPTH, 6, N_SEG, D), F32),
        compiler_params=_cparams(2),
        name="modulation",
    )(cond, mod_w, b)
    return out.reshape(DEPTH, 6, N_SEG, 1, D)


def _head_sumsq(y, bd_ref):
    sq = y * y
    hi = sq.astype(BF16)
    lo = (sq - hi.astype(F32)).astype(BF16)
    bd = bd_ref[...]
    return (jnp.dot(hi, bd, preferred_element_type=F32)
            + jnp.dot(lo, bd, preferred_element_type=F32))


def _rope(y, cos, sin):
    lane = lax.broadcasted_iota(jnp.int32, (1, LANES), 1)
    first = (lane & 16) == 0
    partner = jnp.where(first, pltpu.roll(y, LANES - 16, 1), pltpu.roll(y, 16, 1))
    return y * cos + partner * sin


def _even_in_kernel(*refs, n_tiles, n_prompt_tiles, split_x):
    refs = list(refs)
    i = pl.program_id(0)
    t_proj = jnp.minimum(i, n_tiles - 1)
    x = _pick(n_prompt_tiles, refs.pop(0), refs.pop(0), t_proj) if split_x else refs.pop(0)[...]
    (sh_ref, sc_ref, g_ref, w_ref, sguw_ref, sgub_ref, sgun_ref, qn_ref, kn_ref, bdq_ref, bdk_ref,
     cos_ref, sin_ref, a_ref, q_ref, kd_ref, vd_ref, kf_ref, vf_ref, wbf_ref, h_ref, z0_ref,
     z1_ref) = refs
    tm = a_ref.shape[0]
    qw = B_HEADS * HEAD_DIM
    kw = B_KV_HEADS * HEAD_DIM
    c_q = 2 * A_WIDTH
    c_k = c_q + qw
    n_chunks = tm // CHUNK
    gch = A_WIDTH // A_GROUPS
    lane = lax.broadcasted_iota(jnp.int32, (1, LANES), 1)
    low = lane < HEAD_DIM

    @pl.when(i == 0)
    def _():
        wbf_ref[...] = w_ref[...].astype(BF16)
        z1_ref[...] = jnp.zeros_like(z1_ref)

    h_ref[...] = _mod_norm(x, g_ref[...], sh_ref[...], sc_ref[...]).astype(BF16)

    def step(z_new_ref, z_ref):
        def project(c0, c1):
            z_new_ref[:, c0:c1] = jnp.dot(h_ref[...], wbf_ref[:, c0:c1], preferred_element_type=F32)

        project(0, A_WIDTH)
        u = jax.nn.gelu(z_ref[:, 0:A_WIDTH], approximate=True)
        project(A_WIDTH, c_q)
        v = jax.nn.gelu(z_ref[:, A_WIDTH:c_q], approximate=True)
        mu = jnp.mean(v, axis=-1, keepdims=True)
        vc = v - mu
        var = jnp.mean(vc * vc, axis=-1, keepdims=True)
        vn = (vc * lax.rsqrt(var + EPS) * sgun_ref[...]).astype(BF16)
        for g in range(A_GROUPS):
            rhs = jnp.concatenate([vn[n * CHUNK:(n + 1) * CHUNK, g * gch:(g + 1) * gch]
                                   for n in range(n_chunks)], axis=1)
            mixed = jnp.dot(sguw_ref[g].astype(BF16), rhs, preferred_element_type=F32)
            bias = sgub_ref[g]
            for n in range(n_chunks):
                blk = (mixed[:, n * gch:(n + 1) * gch] + bias) * u[n * CHUNK:(n + 1) * CHUNK,
                                                                  g * gch:(g + 1) * gch]
                a_ref[n * CHUNK:(n + 1) * CHUNK, g * gch:(g + 1) * gch] = blk.astype(BF16)

        project(c_q, c_k)
        q = z_ref[:, c_q:c_k]
        q = q * lax.rsqrt(_head_sumsq(q, bdq_ref) * (1.0 / HEAD_DIM) + EPS) * qn_ref[...]
        cos = cos_ref[...]
        sin = sin_ref[...]
        for j in range(qw // LANES):
            qj = _rope(q[:, j * LANES:(j + 1) * LANES], cos, sin) * QSCALE
            q_ref[:, j * LANES:(j + 1) * LANES] = qj.astype(BF16)

        project(c_k, c_k + 2 * kw)
        k = z_ref[:, c_k:c_k + kw]
        k = k * lax.rsqrt(_head_sumsq(k, bdk_ref) * (1.0 / HEAD_DIM) + EPS) * kn_ref[...]
        kf_ref[...] = k
        vals = z_ref[:, c_k + kw:c_k + 2 * kw]
        vf_ref[...] = vals
        for src, dst in ((_rope(k, cos, sin), kd_ref), (vals, vd_ref)):
            sw = pltpu.roll(src, HEAD_DIM, 1)
            dst[:, 0:LANES] = jnp.where(low, src, sw).astype(BF16)
            dst[:, LANES:2 * LANES] = jnp.where(low, sw, src).astype(BF16)

    @pl.when(i % 2 == 0)
    def _():
        step(z0_ref, z1_ref)

    @pl.when(i % 2 == 1)
    def _():
        step(z1_ref, z0_ref)


def _even_in(xs, mods, layer, e, T, g_pre, w_in, sgu_w, sgu_b, sgu_norm, q_norm, k_norm,
             bdq, bdk, cos, sin):
    tm = TM_PROJ
    n_in = w_in.shape[2]
    n_tiles = T // tm
    tiles_per_seg = SEG // tm
    n_prompt_tiles = n_tiles // 2
    split_x = len(xs) == 2
    proj = lambda i: jnp.minimum(i, n_tiles - 1)
    post = lambda i: jnp.maximum(i - 1, 0)
    tab_spec = pl.BlockSpec((None, tm, LANES),
                            lambda i: (post(i) // n_prompt_tiles, post(i) % tiles_per_seg, 0))
    row = lambda w, tile: pl.BlockSpec((tm, w), lambda i: (tile(i), 0))
    x_specs = (_split_specs(tm, D, n_prompt_tiles, tile=proj) if split_x else [row(D, proj)])
    qw = B_HEADS * HEAD_DIM
    return pl.pallas_call(
        functools.partial(_even_in_kernel, n_tiles=n_tiles, n_prompt_tiles=n_prompt_tiles,
                          split_x=split_x),
        grid=(n_tiles + 1,),
        in_specs=x_specs + [
            _mod_spec(layer, 0, tm, proj), _mod_spec(layer, 1, tm, proj),
            _const_spec((1, D), layer),
            _weight_spec((D, n_in), e), _const_spec((A_GROUPS, CHUNK, CHUNK), e),
            _const_spec((A_GROUPS, CHUNK, 1), e), _const_spec((1, A_WIDTH), e),
            _const_spec((1, qw), e), _const_spec((1, LANES), e),
            _const_spec((qw, qw)), _const_spec((LANES, LANES)), tab_spec, tab_spec],
        out_specs=[row(A_WIDTH, post), row(qw, post), row(2 * LANES, post), row(2 * LANES, post),
                   row(LANES, post), row(LANES, post)],
        out_shape=[jax.ShapeDtypeStruct((T, A_WIDTH), BF16),
                   jax.ShapeDtypeStruct((T, qw), BF16),
                   jax.ShapeDtypeStruct((T, 2 * LANES), BF16),
                   jax.ShapeDtypeStruct((T, 2 * LANES), BF16),
                   jax.ShapeDtypeStruct((T, LANES), F32),
                   jax.ShapeDtypeStruct((T, LANES), F32)],
        scratch_shapes=[pltpu.VMEM((D, n_in), BF16), pltpu.VMEM((tm, D), BF16),
                        pltpu.VMEM((tm, n_in), F32), pltpu.VMEM((tm, n_in), F32)],
        compiler_params=_cparams(1),
        name="even_in_proj",
    )(*xs, mods, mods, g_pre, w_in, sgu_w, sgu_b, sgu_norm, q_norm, k_norm, bdq, bdk, cos, sin)


def _odd_in_kernel(x_ref, sh_ref, sc_ref, g_ref, w_ref, *rest, n_prompt_tiles, slot, all_slots):
    q_ref, k_ref, v_ref, kf_ref, vf_ref, wbf_ref = rest[-6:]
    i = pl.program_id(0)
    hw = C_HEADS * HEAD_DIM

    @pl.when(i == 0)
    def _():
        for c in range(3):
            wbf_ref[:, c * hw:(c + 1) * hw] = w_ref[:, c * hw:(c + 1) * hw].astype(BF16)

    h = _mod_norm(x_ref[...], g_ref[...], sh_ref[...], sc_ref[...]).astype(BF16)
    q = jnp.dot(h, wbf_ref[:, 0:hw], preferred_element_type=F32)
    q_ref[...] = (q * QSCALE).astype(BF16)
    k = jnp.dot(h, wbf_ref[:, hw:2 * hw], preferred_element_type=F32)
    k_ref[...] = k.astype(BF16)
    v = jnp.dot(h, wbf_ref[:, 2 * hw:3 * hw], preferred_element_type=F32)
    v_ref[...] = v.astype(BF16)

    @pl.when(i < n_prompt_tiles)
    def _():
        for src, dst in ((k, kf_ref), (v, vf_ref)):
            if all_slots:
                val = src.reshape((dst.shape[0],) + dst.shape[2:])
                for s in range(dst.shape[1]):
                    dst[:, s] = val if s == slot else jnp.zeros_like(val)
            else:
                dst[...] = src.reshape(dst.shape)


def _odd_in(x, mods, layer, o, n_odd, seq_p, g_pre, w_in, caches):
    T = x.shape[0]
    tm = TM_PROJ
    hw = C_HEADS * HEAD_DIM
    n_prompt_tiles = T // 2 // tm
    bt = tm // seq_p
    row = pl.BlockSpec((tm, hw), lambda i: (i, 0))
    all_slots = not caches
    if all_slots:
        crow = pl.BlockSpec((bt, n_odd, seq_p, hw),
                            lambda i: (jnp.minimum(i, n_prompt_tiles - 1), 0, 0, 0))
    else:
        crow = pl.BlockSpec((bt, None, seq_p, hw),
                            lambda i: (jnp.minimum(i, n_prompt_tiles - 1), o, 0, 0))
    cshape = jax.ShapeDtypeStruct((T // 2 // seq_p, n_odd, seq_p, hw), F32)
    n_in = 5
    return pl.pallas_call(
        functools.partial(_odd_in_kernel, n_prompt_tiles=n_prompt_tiles, slot=o,
                          all_slots=all_slots),
        grid=(T // tm,),
        in_specs=[pl.BlockSpec((tm, D), lambda i: (i, 0)), _mod_spec(layer, 0, tm),
                  _mod_spec(layer, 1, tm), _const_spec((1, D), layer),
                  _weight_spec((D, 3 * hw), o)]
                 + [pl.BlockSpec(memory_space=pl.ANY)] * len(caches),
        out_specs=[row, row, row, crow, crow],
        out_shape=[jax.ShapeDtypeStruct((T, hw), BF16)] * 3 + [cshape, cshape],
        input_output_aliases={n_in + j: 3 + j for j in range(len(caches))},
        scratch_shapes=[pltpu.VMEM((D, 3 * hw), BF16)],
        compiler_params=_cparams(1),
        name="odd_in_proj",
    )(x, mods, mods, g_pre, w_in, *caches)


def _softmax(pieces):
    m = None
    for s, _ in pieces:
        ms = jnp.max(s, axis=-1, keepdims=True)
        m = ms if m is None else jnp.maximum(m, ms)
    den = None
    probs = []
    for s, _ in pieces:
        p = jnp.exp2(s - m)
        ls = jnp.sum(p, axis=-1, keepdims=True)
        den = ls if den is None else den + ls
        probs.append(p.astype(BF16))
    return probs, [val for _, val in pieces], den


def _weighted_values(probs, vals, den):
    acc = None
    for p, val in zip(probs, vals):
        o = jnp.dot(p, val, preferred_element_type=F32)
        acc = o if acc is None else acc + o
    return acc / den


def _pipelined_units(units, scores, store):
    n = len(units)
    pending_scores = scores(units[0])
    pending_probs = None
    for idx in range(n + 1):
        upcoming = scores(units[idx + 1]) if idx + 1 < n else None
        probs = _softmax(pending_scores) if idx < n else None
        if pending_probs is not None:
            store(units[idx - 1], _weighted_values(*pending_probs))
        pending_scores, pending_probs = upcoming, probs


def _attn_kernel(*refs, has_ctx, seq, tq, pairs_per_kv):
    if has_ctx:
        q_ref, k_ref, v_ref, kc_ref, vc_ref, o_ref = refs
    else:
        q_ref, k_ref, v_ref, o_ref = refs
    lane = lax.broadcasted_iota(jnp.int32, (1, LANES), 1)
    masks = [lane < HEAD_DIM, lane >= HEAD_DIM]
    if has_ctx:
        kv_head = pl.program_id(1)
        sel = jnp.where(masks[0], 0, 1) == kv_head
        kc = jnp.where(sel, kc_ref[...], 0.0)
        kc = (kc + pltpu.roll(kc, HEAD_DIM, 1)).astype(BF16)
        vc = jnp.where(sel, vc_ref[...], 0.0)
        vc = (vc + pltpu.roll(vc, HEAD_DIM, 1)).astype(BF16)
        vch = [jnp.where(mh, vc, jnp.zeros_like(vc)) for mh in masks]
    nt = (((1,), (1,)), ((), ()))
    groups = [(r0, j) for r0 in range(0, q_ref.shape[0], seq) for j in range(q_ref.shape[1] // LANES)]
    units = [(r0, j, t, hh) for r0, j in groups for t in range(r0 // tq, (r0 + seq) // tq)
             for hh in range(len(masks))]
    operands = {}

    def group_operands(r0, j):
        if (r0, j) not in operands:
            kcols = slice((j // pairs_per_kv) * LANES, (j // pairs_per_kv + 1) * LANES)
            v = v_ref[r0:r0 + seq, kcols]
            operands[r0, j] = (k_ref[r0:r0 + seq, kcols],
                               [jnp.where(mh, v, jnp.zeros_like(v)) for mh in masks])
        return operands[r0, j]

    def scores(unit):
        r0, j, t, hh = unit
        k, vh = group_operands(r0, j)
        q = q_ref[t * tq:(t + 1) * tq, j * LANES:(j + 1) * LANES]
        qm = jnp.where(masks[hh], q, jnp.zeros_like(q))
        pieces = [(lax.dot_general(qm, k, nt, preferred_element_type=F32), vh[hh])]
        if has_ctx:
            pieces.append((lax.dot_general(qm, kc, nt, preferred_element_type=F32), vch[hh]))
        return pieces

    partial = {}

    def store(unit, o):
        r0, j, t, hh = unit
        if hh == 0:
            partial[r0, j, t] = o
        else:
            out = partial.pop((r0, j, t)) + o
            o_ref[t * tq:(t + 1) * tq, j * LANES:(j + 1) * LANES] = out.astype(BF16)

    _pipelined_units(units, scores, store)


def _attention(q, k, v, *, n_batch, seq, row0, pairs_per_step, pairs_per_kv, batches_per_step=1,
               ctx=None):
    rows = batches_per_step * seq
    assert row0 % rows == 0 and n_batch % batches_per_step == 0
    blk0 = row0 // rows
    n_groups = q.shape[1] // (pairs_per_step * LANES)
    qw = pairs_per_step * LANES
    kw = qw // pairs_per_kv
    in_specs = [pl.BlockSpec((rows, qw), lambda b, g: (blk0 + b, g)),
                pl.BlockSpec((rows, kw), lambda b, g: (blk0 + b, g)),
                pl.BlockSpec((rows, kw), lambda b, g: (blk0 + b, g))]
    args = [q, k, v]
    if ctx is not None:
        assert kw == LANES and batches_per_step == 1
        kc, vc, e = ctx
        sc = kc.shape[2]
        cspec = pl.BlockSpec((None, None, sc, LANES), lambda b, g: (b, e, 0, 0))
        in_specs += [cspec, cspec]
        args += [kc, vc]
    return pl.pallas_call(
        functools.partial(_attn_kernel, has_ctx=ctx is not None, seq=seq, tq=min(TQ_ATTN, seq),
                          pairs_per_kv=pairs_per_kv),
        grid=(n_batch // batches_per_step, n_groups),
        in_specs=in_specs,
        out_specs=pl.BlockSpec((rows, qw), lambda b, g: (b, g)),
        out_shape=jax.ShapeDtypeStruct((n_batch * seq, q.shape[1]), BF16),
        compiler_params=_cparams(2),
        name="attention",
    )(*args)


def _na_row_start(r):
    rows = SEG // GRID_W
    return min(max(r - NA_ROWS // 2, 0), rows - NA_ROWS)


def _na_windows():
    rows = SEG // GRID_W
    windows = []
    for qb in range(rows // NA_QROWS):
        lo = _na_row_start(qb * NA_QROWS)
        hi = _na_row_start(qb * NA_QROWS + NA_QROWS - 1) + NA_ROWS
        n = hi - lo + (hi - lo) % 2
        ws = min(lo, rows - n)
        assert ws <= lo and hi <= ws + n <= rows and n <= NA_KROWS
        windows.append((ws, n))
    return windows


def _na_kernel(q_ref, k_ref, v_ref, kc_ref, vc_ref, r_ref, o_ref, bias_ref, tab_ref, *, windows):
    lane = lax.broadcasted_iota(jnp.int32, (1, LANES), 1)
    low = lane < HEAD_DIM

    @pl.when(pl.program_id(1) == 0)
    def _():
        qcol = lax.broadcasted_iota(jnp.int32, (GRID_W, LANES), 0)
        kcol = lax.broadcasted_iota(jnp.int32, (GRID_W, LANES), 1) & (GRID_W - 1)
        start = jnp.clip(qcol - NA_COLS // 2, 0, GRID_W - NA_COLS)
        inside = (kcol >= start) & (kcol < start + NA_COLS)
        for hh in range(LANES // HEAD_DIM):
            for d in range(2 * NA_ROWS - 1):
                base = jnp.broadcast_to(r_ref[hh, d:d + 1, :], (GRID_W, LANES))
                lo_t = pltpu.roll(base, LANES - (NA_COLS - 1), 1, stride=1, stride_axis=0)
                hi_t = pltpu.roll(base, GRID_W - (NA_COLS - 1), 1, stride=1, stride_axis=0)
                tab_ref[hh, d] = jnp.where(inside, jnp.where(low, lo_t, hi_t) * LOG2E, NEG)
        neg = jnp.full((GRID_W, LANES), NEG, F32)
        for hh in range(LANES // HEAD_DIM):
            for qb, (ws, nrows) in enumerate(windows):
                for i in range(NA_QROWS):
                    r = qb * NA_QROWS + i
                    rs = _na_row_start(r)
                    for jp in range(nrows // 2):
                        kr = ws + 2 * jp
                        ok = [rs <= kr + d < rs + NA_ROWS for d in (0, 1)]
                        if not any(ok):
                            blk = neg
                        else:
                            t0 = tab_ref[hh, kr - r + NA_ROWS - 1] if ok[0] else neg
                            t1 = tab_ref[hh, kr + 1 - r + NA_ROWS - 1] if ok[1] else neg
                            blk = jnp.where(low, t0, t1)
                        bias_ref[hh, qb, i * GRID_W:(i + 1) * GRID_W,
                                 jp * LANES:(jp + 1) * LANES] = blk

    masks = [low, jnp.logical_not(low)]
    kc = kc_ref[...].astype(BF16)
    vc = vc_ref[...].astype(BF16)
    vch = [jnp.where(mh, vc, jnp.zeros_like(vc)) for mh in masks]
    nt = (((1,), (1,)), ((), ()))
    nq = NA_QROWS * GRID_W
    units = [(qb, hh) for qb in range(len(windows)) for hh in range(len(masks))]

    def scores(unit):
        qb, hh = unit
        ws, nrows = windows[qb]
        nk = nrows * GRID_W
        q = q_ref[qb * nq:(qb + 1) * nq, :]
        kw = k_ref[ws * GRID_W:ws * GRID_W + nk, :]
        vw = v_ref[ws * GRID_W:ws * GRID_W + nk, :]
        qm = jnp.where(masks[hh], q, jnp.zeros_like(q))
        s_win = (lax.dot_general(qm, kw, nt, preferred_element_type=F32)
                 + bias_ref[hh, qb, :, 0:nk])
        s_ctx = lax.dot_general(qm, kc, nt, preferred_element_type=F32)
        return [(s_win, jnp.where(masks[hh], vw, jnp.zeros_like(vw))), (s_ctx, vch[hh])]

    partial = {}

    def store(unit, o):
        qb, hh = unit
        if hh == 0:
            partial[qb] = o
        else:
            o_ref[qb * nq:(qb + 1) * nq, :] = (partial.pop(qb) + o).astype(BF16)

    _pipelined_units(units, scores, store)


def _na_attention(q, k, v, kc, vc, o_idx, rpb, *, n_batch, row0):
    blk0 = row0 // SEG
    n_pairs = q.shape[1] // LANES
    sc = kc.shape[2]
    windows = _na_windows()
    heads = LANES // HEAD_DIM
    nr = 2 * NA_ROWS - 1
    qkv = pl.BlockSpec((SEG, LANES), lambda p, b: (blk0 + b, p))
    cspec = pl.BlockSpec((None, None, sc, LANES), lambda p, b: (b, o_idx, 0, p))
    rspec = pl.BlockSpec((None, heads, nr, LANES), lambda p, b: (o_idx, p, 0, 0))
    return pl.pallas_call(
        functools.partial(_na_kernel, windows=windows),
        grid=(n_pairs, n_batch),
        in_specs=[qkv, qkv, qkv, cspec, cspec, rspec],
        out_specs=pl.BlockSpec((SEG, LANES), lambda p, b: (b, p)),
        out_shape=jax.ShapeDtypeStruct((n_batch * SEG, n_pairs * LANES), BF16),
        scratch_shapes=[pltpu.VMEM((heads, len(windows), NA_QROWS * GRID_W, NA_KROWS * GRID_W), F32),
                        pltpu.VMEM((heads, nr, GRID_W, LANES), F32)],
        compiler_params=_cparams(2),
        name="na_attention",
    )(q, k, v, kc, vc, rpb)


def _out_kernel(*refs, n_prompt_tiles, split_x, has_a):
    refs = list(refs)
    x = _pick(n_prompt_tiles, refs.pop(0), refs.pop(0)) if split_x else refs.pop(0)[...]
    gt_ref, g_ref, fsh_ref, fsc_ref, gf_ref, w_ref = refs[:6]
    a_ref = refs[6] if has_a else None
    bp_ref, bs_ref, o_ref, h_ref, wbf_ref = refs[6 + has_a:]

    @pl.when(pl.program_id(0) == 0)
    def _():
        wbf_ref[...] = w_ref[...].astype(BF16)

    b = _pick(n_prompt_tiles, bp_ref, bs_ref)
    if has_a:
        half = a_ref.shape[1]
        y = (jnp.dot(a_ref[...], wbf_ref[0:half, :], preferred_element_type=F32)
             + jnp.dot(b, wbf_ref[half:2 * half, :], preferred_element_type=F32))
    else:
        y = jnp.dot(b, wbf_ref[...], preferred_element_type=F32)
    x1 = _gated_residual(x, y, g_ref[...], gt_ref[...])
    o_ref[...] = x1
    h_ref[...] = _mod_norm(x1, gf_ref[...], fsh_ref[...], fsc_ref[...]).astype(BF16)


def _out_proj(xs, a, bp, bs, mods, layer, li, T, g_post, g_ffn_pre, w_out):
    tm = TM_PROJ
    n_prompt_tiles = T // 2 // tm
    split_x = len(xs) == 2
    has_a = a is not None
    row = lambda w: pl.BlockSpec((tm, w), lambda i: (i, 0))
    x_specs = _split_specs(tm, D, n_prompt_tiles) if split_x else [row(D)]
    a_specs = [row(a.shape[1])] if has_a else []
    return pl.pallas_call(
        functools.partial(_out_kernel, n_prompt_tiles=n_prompt_tiles, split_x=split_x, has_a=has_a),
        grid=(T // tm,),
        in_specs=x_specs + [_mod_spec(layer, 2, tm), _const_spec((1, D), layer),
                            _mod_spec(layer, 3, tm), _mod_spec(layer, 4, tm),
                            _const_spec((1, D), layer), _weight_spec((D, D), li)] + a_specs
                 + _split_specs(tm, bp.shape[1], n_prompt_tiles),
        out_specs=[row(D), row(D)],
        out_shape=[jax.ShapeDtypeStruct((T, D), F32), jax.ShapeDtypeStruct((T, D), BF16)],
        scratch_shapes=[pltpu.VMEM((D, D), BF16)],
        compiler_params=_cparams(1),
        name="out_proj",
    )(*xs, mods, g_post, mods, mods, g_ffn_pre, w_out, *([a] if has_a else []), bp, bs)


def _ffn_kernel(x_ref, h_ref, gt_ref, gpost_ref, win_ref, wout_ref, *rest,
                layer, n_prompt_tiles, split_out):
    out_refs = rest[:-9]
    wg_buf, wu_buf, wo_buf, wg_res, wu_res, wo_res, act_buf, acc_ref, sem = rest[-9:]
    nf, _, tf = wg_res.shape
    first_tile = pl.program_id(0) == 0

    def aligned(f):
        return f * tf if isinstance(f, int) else pl.multiple_of(f * tf, tf)

    def in_copies(f, slot):
        col = aligned(f)
        return (pltpu.make_async_copy(win_ref.at[layer, :, pl.ds(col, tf)], wg_buf.at[slot],
                                      sem.at[0, slot]),
                pltpu.make_async_copy(win_ref.at[layer, :, pl.ds(D_FF + col, tf)], wu_buf.at[slot],
                                      sem.at[1, slot]))

    def out_copy(f, slot):
        row = aligned(f)
        return pltpu.make_async_copy(wout_ref.at[layer, pl.ds(row, tf), :], wo_buf.at[slot],
                                     sem.at[2, slot])

    def fetch(f, slot):
        @pl.when(first_tile)
        def _():
            for c in in_copies(f, slot):
                c.wait()
            out_copy(f, slot).wait()

            nxt = min(f + 1, nf - 1) if isinstance(f, int) else jnp.minimum(f + 1, nf - 1)

            @pl.when(jnp.asarray(f + 1 < nf))
            def _():
                for c in in_copies(nxt, 1 - slot):
                    c.start()
                out_copy(nxt, 1 - slot).start()

            wg_res[f] = wg_buf[slot].astype(BF16)
            wu_res[f] = wu_buf[slot].astype(BF16)
            wo_res[f] = wo_buf[slot].astype(BF16)

    def hidden(f, slot):
        h = h_ref[...]
        g = jnp.dot(h, wg_res[f], preferred_element_type=F32)
        u = jnp.dot(h, wu_res[f], preferred_element_type=F32)
        act_buf[slot] = (g * jax.nn.sigmoid(g) * u).astype(BF16)

    def project(f, slot):
        acc_ref[...] += jnp.dot(act_buf[slot], wo_res[f], preferred_element_type=F32)

    def step(f, slot):
        fetch(f, slot)
        hidden(f, slot)
        project(f - 1, 1 - slot)

    @pl.when(first_tile)
    def _():
        for c in in_copies(0, 0):
            c.start()
        out_copy(0, 0).start()

    acc_ref[...] = jnp.zeros_like(acc_ref)
    fetch(0, 0)
    hidden(0, 0)

    assert nf % 2 == 1

    def pair(k, carry):
        step(2 * k + 1, 1)
        step(2 * k + 2, 0)
        return carry

    lax.fori_loop(0, (nf - 1) // 2, pair, 0)
    project(nf - 1, 0)

    def result():
        return _gated_residual(x_ref[...], acc_ref[...], gpost_ref[...], gt_ref[...])

    if split_out:
        i = pl.program_id(0)

        @pl.when(i < n_prompt_tiles)
        def _():
            out_refs[0][...] = result()

        @pl.when(i >= n_prompt_tiles)
        def _():
            out_refs[1][...] = result()
    else:
        out_refs[0][...] = result()


def _ffn(x, h, mods, layer, g_post, w_in, w_out, split_out):
    T = x.shape[0]
    tm, tf = TM_FFN, TF_FFN
    nf = D_FF // tf
    n_prompt_tiles = T // 2 // tm
    xrow = pl.BlockSpec((tm, D), lambda i: (i, 0))
    if split_out:
        out_specs = _split_specs(tm, D, n_prompt_tiles)
        out_shape = [jax.ShapeDtypeStruct((T // 2, D), F32)] * 2
    else:
        out_specs = [xrow]
        out_shape = [jax.ShapeDtypeStruct((T, D), F32)]
    hbm = pl.BlockSpec(memory_space=pl.ANY)
    return pl.pallas_call(
        functools.partial(_ffn_kernel, layer=layer, n_prompt_tiles=n_prompt_tiles,
                          split_out=split_out),
        grid=(T // tm,),
        in_specs=[xrow, xrow, _mod_spec(layer, 5, tm), _const_spec((1, D), layer), hbm, hbm],
        out_specs=out_specs,
        out_shape=out_shape,
        scratch_shapes=[pltpu.VMEM((2, D, tf), F32), pltpu.VMEM((2, D, tf), F32),
                        pltpu.VMEM((2, tf, D), F32), pltpu.VMEM((nf, D, tf), BF16),
                        pltpu.VMEM((nf, D, tf), BF16), pltpu.VMEM((nf, tf, D), BF16),
                        pltpu.VMEM((2, tm, tf), BF16), pltpu.VMEM((tm, D), F32),
                        pltpu.SemaphoreType.DMA((3, 2))],
        compiler_params=_cparams(1),
        name="ffn",
    )(x, h, mods, g_post, w_in, w_out)


def _rope_tables():
    t = jnp.arange(SEG)
    nf = HEAD_DIM // 4
    freqs = ROPE_BASE ** (-jnp.arange(nf, dtype=F32) / nf)

    def cs(pos):
        ang = pos.astype(F32)[:, None] * freqs[None, :]
        return jnp.cos(ang), jnp.sin(ang)

    cr, sr = cs(t // GRID_W)
    cc, sn = cs(t % GRID_W)
    cos = jnp.concatenate([cr, cr, cc, cc], axis=1)
    sin = jnp.concatenate([-sr, sr, -sn, sn], axis=1)
    reps = LANES // HEAD_DIM
    cos = jnp.tile(cos, (1, reps))
    sin = jnp.tile(sin, (1, reps))
    return (jnp.stack([jnp.ones_like(cos), cos]), jnp.stack([jnp.zeros_like(sin), sin]))


def _block_diag_ones(n):
    i = jnp.arange(n) // HEAD_DIM
    return (i[:, None] == i[None, :]).astype(BF16)


def kernel(x_prompt, x_sample, cache_attn_k, cache_attn_v, cache_na_k, cache_na_v, c, c_ctx,
           mod_w, mod_b, norm_mix_pre, norm_mix_post, norm_ffn_pre, norm_ffn_post,
           even_w_in, even_w_out, sgu_w, sgu_b, sgu_norm, q_norm, k_norm,
           odd_w_in, odd_w_out, na_rpb, ffn_w_in, ffn_w_out):
    nb_p, seq_p, _ = x_prompt.shape
    nb_s, seq_s, _ = x_sample.shape
    tp = nb_p * seq_p
    T = 2 * tp
    assert seq_s == SEG and tp == nb_s * seq_s and tp % SEG == 0

    cond = jnp.concatenate([jnp.broadcast_to(c_ctx[None, :], (tp // SEG, D)), c], axis=0)
    mods = _modulation(cond, mod_w, mod_b)

    cos, sin = _rope_tables()
    qw = B_HEADS * HEAD_DIM
    bdq = _block_diag_ones(qw)
    bdk = _block_diag_ones(LANES)
    past = cache_attn_k.shape[2]
    ctx_ak = cache_attn_k.reshape(nb_s, -1, past, B_KV_HEADS * HEAD_DIM)
    ctx_av = cache_attn_v.reshape(nb_s, -1, past, B_KV_HEADS * HEAD_DIM)
    ctx_nk = cache_na_k.reshape(nb_s, -1, past, C_HEADS * HEAD_DIM)
    ctx_nv = cache_na_v.reshape(nb_s, -1, past, C_HEADS * HEAD_DIM)
    g_mix_pre, g_mix_post, g_ffn_pre, g_ffn_post = (
        a.reshape(DEPTH, 1, D) for a in (norm_mix_pre, norm_mix_post, norm_ffn_pre, norm_ffn_post))
    n_even = even_w_in.shape[0]
    sgu_b3 = sgu_b.reshape(n_even, A_GROUPS, CHUNK, 1)
    sgu_n3 = sgu_norm.reshape(n_even, 1, A_WIDTH)
    qn3 = jnp.tile(q_norm, (1, B_HEADS)).reshape(n_even, 1, qw)
    kn3 = jnp.tile(k_norm, (1, B_KV_HEADS)).reshape(n_even, 1, LANES)
    rpb = jnp.pad(na_rpb, ((0, 0), (0, 0), (0, 0), (0, LANES - na_rpb.shape[3])))

    xs = (x_prompt.reshape(tp, D), x_sample.reshape(nb_s * seq_s, D))
    n_odd = odd_w_in.shape[0]
    attn_k, attn_v, na_cache = [], [], ()
    for l in range(DEPTH):
        if l % 2 == 0:
            e = l // 2
            a_out, q, kd, vd, kf, vf = _even_in(xs, mods, l, e, T, g_mix_pre, even_w_in, sgu_w,
                                                sgu_b3, sgu_n3, qn3, kn3, bdq, bdk, cos, sin)
            n_pairs = qw // LANES
            pairs_per_kv = n_pairs // B_KV_HEADS
            mix_p = _attention(q, kd, vd, n_batch=nb_p, seq=seq_p, row0=0, pairs_per_step=n_pairs,
                               pairs_per_kv=pairs_per_kv, batches_per_step=PROMPT_BATCHES_PER_STEP)
            mix_s = _attention(q, kd, vd, n_batch=nb_s, seq=seq_s, row0=tp,
                               pairs_per_step=pairs_per_kv, pairs_per_kv=pairs_per_kv,
                               ctx=(ctx_ak, ctx_av, e))
            attn_k.append(kf[:tp].reshape(nb_p, seq_p, B_KV_HEADS, HEAD_DIM))
            attn_v.append(vf[:tp].reshape(nb_p, seq_p, B_KV_HEADS, HEAD_DIM))
            w_out, li = even_w_out, e
        else:
            o = l // 2
            a_out = None
            q, k, v, *na_cache = _odd_in(xs[0], mods, l, o, n_odd, seq_p, g_mix_pre, odd_w_in,
                                         tuple(na_cache))
            mix_p = _attention(q, k, v, n_batch=nb_p, seq=seq_p, row0=0,
                               pairs_per_step=PAIRS_PER_STEP, pairs_per_kv=1,
                               batches_per_step=PROMPT_BATCHES_PER_STEP)
            mix_s = _na_attention(q, k, v, ctx_nk, ctx_nv, o, rpb, n_batch=nb_s, row0=tp)
            w_out, li = odd_w_out, o
        x, h = _out_proj(xs, a_out, mix_p, mix_s, mods, l, li, T, g_mix_post, g_ffn_pre, w_out)
        xs = tuple(_ffn(x, h, mods, l, g_ffn_post, ffn_w_in, ffn_w_out,
                        split_out=(l == DEPTH - 1)))

    y_prompt = xs[0].reshape(nb_p, seq_p, D)
    y_sample = xs[1].reshape(nb_s, seq_s, D)
    new_na_k, new_na_v = (a.reshape(nb_p, n_odd, seq_p, C_HEADS, HEAD_DIM) for a in na_cache)
    return (y_prompt, y_sample, jnp.stack(attn_k, axis=1), jnp.stack(attn_v, axis=1),
            new_na_k, new_na_v)
```

```python
import functools

import jax
import jax.numpy as jnp
from jax import lax
from jax.experimental import pallas as pl
from jax.experimental.pallas import tpu as pltpu

F32 = jnp.float32
BF16 = jnp.bfloat16

D = 1024
DEPTH = 4
HEAD_DIM = 64
GRID_W = 64
CHUNK = 128
A_WIDTH = D // 2
A_GROUPS = 4
B_HEADS = 8
B_KV_HEADS = 2
C_HEADS = 16
NA_ROWS = 8
NA_COLS = 16
D_FF = 2816
ROPE_BASE = 10000.0
EPS = 1e-6
NEG = -1e30
SEG = 1024
N_SEG = 8
LANES = 128
LOG2E = 1.4426950408889634
QSCALE = HEAD_DIM ** -0.5 * LOG2E

TM_PROJ = 512
TM_OUT = 1024
TM_FFN = 1024
TF_FFN = 256
TQ_ATTN = 256
PAIRS_PER_STEP = 4
PROMPT_BATCHES_PER_STEP = 4
NA_QROWS = 4
NA_BATCHES_PER_STEP = 2
NA_KROWS = 12
VMEM_LIMIT = 56 * 1024 * 1024


def _cparams(n_axes):
    return pltpu.CompilerParams(dimension_semantics=("arbitrary",) * n_axes,
                                vmem_limit_bytes=VMEM_LIMIT)


def _rms(x):
    return x * lax.rsqrt(jnp.mean(x * x, axis=-1, keepdims=True) + EPS)


def _mod_norm(x, g, shift, scale):
    return _rms(x) * (g * (1.0 + scale)) + shift


def _gated_residual(x, y, g, gate):
    return x + _rms(y) * (gate * g)


def _const_spec(shape, *lead):
    block = (None,) * len(lead) + tuple(shape)
    return pl.BlockSpec(block, lambda *_: tuple(lead) + (0,) * len(shape))


def _weight_spec(shape, *lead):
    block = (None,) * len(lead) + tuple(shape)
    return pl.BlockSpec(block, lambda *_: tuple(lead) + (0,) * len(shape),
                        pipeline_mode=pl.Buffered(1))


def _same_tile(i):
    return i


def _mod_spec(layer, j, tm, tile=_same_tile):
    return pl.BlockSpec((None, None, None, 1, D),
                        lambda i, *_: (layer, j, (tile(i) * tm) // SEG, 0, 0))


def _split_specs(tm, width, n_prompt_tiles, col=0, tile=_same_tile):
    return [pl.BlockSpec((tm, width),
                         lambda i, *_: (jnp.minimum(tile(i), n_prompt_tiles - 1), col)),
            pl.BlockSpec((tm, width),
                         lambda i, *_: (jnp.maximum(tile(i) - n_prompt_tiles, 0), col))]


def _pick(n_prompt_tiles, p_ref, s_ref, tile=None):
    tile = pl.program_id(0) if tile is None else tile
    return jnp.where(tile < n_prompt_tiles, p_ref[...], s_ref[...])


def _mod_kernel(cond_ref, w_ref, b_ref, o_ref):
    s = cond_ref[...]
    s = s * jax.nn.sigmoid(s)
    o_ref[...] = jnp.dot(s.astype(BF16), w_ref[...].astype(BF16),
                         preferred_element_type=F32) + b_ref[...]


def _modulation(cond, mod_w, mod_b):
    b = mod_b.reshape(DEPTH, 6, 1, D)
    out = pl.pallas_call(
        _mod_kernel,
        grid=(DEPTH, 6),
        in_specs=[pl.BlockSpec((N_SEG, D), lambda l, j: (0, 0)),
                  pl.BlockSpec((None, D, D), lambda l, j: (l, 0, j)),
                  pl.BlockSpec((None, None, 1, D), lambda l, j: (l, j, 0, 0))],
        out_specs=pl.BlockSpec((None, None, N_SEG, D), lambda l, j: (l, j, 0, 0)),
        out_shape=jax.ShapeDtypeStruct((DEPTH, 6, N_SEG, D), F32),
        compiler_params=_cparams(2),
        name="modulation",
    )(cond, mod_w, b)
    return out.reshape(DEPTH, 6, N_SEG, 1, D)


def _head_sumsq(y, bd_ref):
    sq = y * y
    hi = sq.astype(BF16)
    lo = (sq - hi.astype(F32)).astype(BF16)
    bd = bd_ref[...]
    return (jnp.dot(hi, bd, preferred_element_type=F32)
            + jnp.dot(lo, bd, preferred_element_type=F32))


def _rope(y, cos, sin):
    lane = lax.broadcasted_iota(jnp.int32, (1, LANES), 1)
    first = (lane & 16) == 0
    partner = jnp.where(first, pltpu.roll(y, LANES - 16, 1), pltpu.roll(y, 16, 1))
    return y * cos + partner * sin


def _even_in_kernel(*refs, n_tiles, n_prompt_tiles, split_x):
    refs = list(refs)
    i = pl.program_id(0)
    t_proj = jnp.minimum(i, n_tiles - 1)
    x = _pick(n_prompt_tiles, refs.pop(0), refs.pop(0), t_proj) if split_x else refs.pop(0)[...]
    (sh_ref, sc_ref, g_ref, w_ref, sguw_ref, sgub_ref, sgun_ref, qn_ref, kn_ref, bdq_ref, bdk_ref,
     cos_ref, sin_ref, a_ref, q_ref, kd_ref, vd_ref, kf_ref, vf_ref, wbf_ref, h_ref, z0_ref,
     z1_ref) = refs
    tm = a_ref.shape[0]
    qw = B_HEADS * HEAD_DIM
    kw = B_KV_HEADS * HEAD_DIM
    c_q = 2 * A_WIDTH
    c_k = c_q + qw
    n_chunks = tm // CHUNK
    gch = A_WIDTH // A_GROUPS
    lane = lax.broadcasted_iota(jnp.int32, (1, LANES), 1)
    low = lane < HEAD_DIM

    @pl.when(i == 0)
    def _():
        wbf_ref[...] = w_ref[...].astype(BF16)
        z1_ref[...] = jnp.zeros_like(z1_ref)

    h_ref[...] = _mod_norm(x, g_ref[...], sh_ref[...], sc_ref[...]).astype(BF16)

    def step(z_new_ref, z_ref):
        def project(c0, c1):
            z_new_ref[:, c0:c1] = jnp.dot(h_ref[...], wbf_ref[:, c0:c1], preferred_element_type=F32)

        project(0, A_WIDTH)
        u = jax.nn.gelu(z_ref[:, 0:A_WIDTH], approximate=True)
        project(A_WIDTH, c_q)
        v = jax.nn.gelu(z_ref[:, A_WIDTH:c_q], approximate=True)
        mu = jnp.mean(v, axis=-1, keepdims=True)
        vc = v - mu
        var = jnp.mean(vc * vc, axis=-1, keepdims=True)
        vn = (vc * lax.rsqrt(var + EPS) * sgun_ref[...]).astype(BF16)
        for g in range(A_GROUPS):
            rhs = jnp.concatenate([vn[n * CHUNK:(n + 1) * CHUNK, g * gch:(g + 1) * gch]
                                   for n in range(n_chunks)], axis=1)
            mixed = jnp.dot(sguw_ref[g].astype(BF16), rhs, preferred_element_type=F32)
            bias = sgub_ref[g]
            for n in range(n_chunks):
                blk = (mixed[:, n * gch:(n + 1) * gch] + bias) * u[n * CHUNK:(n + 1) * CHUNK,
                                                                  g * gch:(g + 1) * gch]
                a_ref[n * CHUNK:(n + 1) * CHUNK, g * gch:(g + 1) * gch] = blk.astype(BF16)

        project(c_q, c_k)
        q = z_ref[:, c_q:c_k]
        q = q * lax.rsqrt(_head_sumsq(q, bdq_ref) * (1.0 / HEAD_DIM) + EPS) * qn_ref[...]
        cos = cos_ref[...]
        sin = sin_ref[...]
        for j in range(qw // LANES):
            qj = _rope(q[:, j * LANES:(j + 1) * LANES], cos, sin) * QSCALE
            q_ref[:, j * LANES:(j + 1) * LANES] = qj.astype(BF16)

        project(c_k, c_k + 2 * kw)
        k = z_ref[:, c_k:c_k + kw]
        k = k * lax.rsqrt(_head_sumsq(k, bdk_ref) * (1.0 / HEAD_DIM) + EPS) * kn_ref[...]
        kf_ref[...] = k
        vals = z_ref[:, c_k + kw:c_k + 2 * kw]
        vf_ref[...] = vals
        for src, dst in ((_rope(k, cos, sin), kd_ref), (vals, vd_ref)):
            sw = pltpu.roll(src, HEAD_DIM, 1)
            dst[:, 0:LANES] = jnp.where(low, src, sw).astype(BF16)
            dst[:, LANES:2 * LANES] = jnp.where(low, sw, src).astype(BF16)

    @pl.when(i % 2 == 0)
    def _():
        step(z0_ref, z1_ref)

    @pl.when(i % 2 == 1)
    def _():
        step(z1_ref, z0_ref)


def _even_in(xs, mods, layer, e, T, g_pre, w_in, sgu_w, sgu_b, sgu_norm, q_norm, k_norm,
             bdq, bdk, cos, sin):
    tm = TM_PROJ
    n_in = w_in.shape[2]
    n_tiles = T // tm
    tiles_per_seg = SEG // tm
    n_prompt_tiles = n_tiles // 2
    split_x = len(xs) == 2
    proj = lambda i: jnp.minimum(i, n_tiles - 1)
    post = lambda i: jnp.maximum(i - 1, 0)
    tab_spec = pl.BlockSpec((None, tm, LANES),
                            lambda i: (post(i) // n_prompt_tiles, post(i) % tiles_per_seg, 0))
    row = lambda w, tile: pl.BlockSpec((tm, w), lambda i: (tile(i), 0))
    x_specs = (_split_specs(tm, D, n_prompt_tiles, tile=proj) if split_x else [row(D, proj)])
    qw = B_HEADS * HEAD_DIM
    return pl.pallas_call(
        functools.partial(_even_in_kernel, n_tiles=n_tiles, n_prompt_tiles=n_prompt_tiles,
                          split_x=split_x),
        grid=(n_tiles + 1,),
        in_specs=x_specs + [
            _mod_spec(layer, 0, tm, proj), _mod_spec(layer, 1, tm, proj),
            _const_spec((1, D), layer),
            _weight_spec((D, n_in), e), _const_spec((A_GROUPS, CHUNK, CHUNK), e),
            _const_spec((A_GROUPS, CHUNK, 1), e), _const_spec((1, A_WIDTH), e),
            _const_spec((1, qw), e), _const_spec((1, LANES), e),
            _const_spec((qw, qw)), _const_spec((LANES, LANES)), tab_spec, tab_spec],
        out_specs=[row(A_WIDTH, post), row(qw, post), row(2 * LANES, post), row(2 * LANES, post),
                   row(LANES, post), row(LANES, post)],
        out_shape=[jax.ShapeDtypeStruct((T, A_WIDTH), BF16),
                   jax.ShapeDtypeStruct((T, qw), BF16),
                   jax.ShapeDtypeStruct((T, 2 * LANES), BF16),
                   jax.ShapeDtypeStruct((T, 2 * LANES), BF16),
                   jax.ShapeDtypeStruct((T, LANES), F32),
                   jax.ShapeDtypeStruct((T, LANES), F32)],
        scratch_shapes=[pltpu.VMEM((D, n_in), BF16), pltpu.VMEM((tm, D), BF16),
                        pltpu.VMEM((tm, n_in), F32), pltpu.VMEM((tm, n_in), F32)],
        compiler_params=_cparams(1),
        name="even_in_proj",
    )(*xs, mods, mods, g_pre, w_in, sgu_w, sgu_b, sgu_norm, q_norm, k_norm, bdq, bdk, cos, sin)


def _odd_in_kernel(x_ref, sh_ref, sc_ref, g_ref, w_ref, *rest, n_prompt_tiles, slot, all_slots):
    q_ref, k_ref, v_ref, kf_ref, vf_ref, wbf_ref = rest[-6:]
    i = pl.program_id(0)
    hw = C_HEADS * HEAD_DIM

    @pl.when(i == 0)
    def _():
        for c in range(3):
            wbf_ref[:, c * hw:(c + 1) * hw] = w_ref[:, c * hw:(c + 1) * hw].astype(BF16)

    h = _mod_norm(x_ref[...], g_ref[...], sh_ref[...], sc_ref[...]).astype(BF16)
    q = jnp.dot(h, wbf_ref[:, 0:hw], preferred_element_type=F32)
    q_ref[...] = (q * QSCALE).astype(BF16)
    k = jnp.dot(h, wbf_ref[:, hw:2 * hw], preferred_element_type=F32)
    k_ref[...] = k.astype(BF16)
    v = jnp.dot(h, wbf_ref[:, 2 * hw:3 * hw], preferred_element_type=F32)
    v_ref[...] = v.astype(BF16)

    @pl.when(i < n_prompt_tiles)
    def _():
        for src, dst in ((k, kf_ref), (v, vf_ref)):
            if all_slots:
                val = src.reshape((dst.shape[0],) + dst.shape[2:])
                for s in range(dst.shape[1]):
                    dst[:, s] = val if s == slot else jnp.zeros_like(val)
            else:
                dst[...] = src.reshape(dst.shape)


def _odd_in(x, mods, layer, o, n_odd, seq_p, g_pre, w_in, caches):
    T = x.shape[0]
    tm = TM_PROJ
    hw = C_HEADS * HEAD_DIM
    n_prompt_tiles = T // 2 // tm
    bt = tm // seq_p
    row = pl.BlockSpec((tm, hw), lambda i: (i, 0))
    all_slots = not caches
    if all_slots:
        crow = pl.BlockSpec((bt, n_odd, seq_p, hw),
                            lambda i: (jnp.minimum(i, n_prompt_tiles - 1), 0, 0, 0))
    else:
        crow = pl.BlockSpec((bt, None, seq_p, hw),
                            lambda i: (jnp.minimum(i, n_prompt_tiles - 1), o, 0, 0))
    cshape = jax.ShapeDtypeStruct((T // 2 // seq_p, n_odd, seq_p, hw), F32)
    n_in = 5
    return pl.pallas_call(
        functools.partial(_odd_in_kernel, n_prompt_tiles=n_prompt_tiles, slot=o,
                          all_slots=all_slots),
        grid=(T // tm,),
        in_specs=[pl.BlockSpec((tm, D), lambda i: (i, 0)), _mod_spec(layer, 0, tm),
                  _mod_spec(layer, 1, tm), _const_spec((1, D), layer),
                  _weight_spec((D, 3 * hw), o)]
                 + [pl.BlockSpec(memory_space=pl.ANY)] * len(caches),
        out_specs=[row, row, row, crow, crow],
        out_shape=[jax.ShapeDtypeStruct((T, hw), BF16)] * 3 + [cshape, cshape],
        input_output_aliases={n_in + j: 3 + j for j in range(len(caches))},
        scratch_shapes=[pltpu.VMEM((D, 3 * hw), BF16)],
        compiler_params=_cparams(1),
        name="odd_in_proj",
    )(x, mods, mods, g_pre, w_in, *caches)


def _softmax(pieces):
    m = None
    for s, _ in pieces:
        ms = jnp.max(s, axis=-1, keepdims=True)
        m = ms if m is None else jnp.maximum(m, ms)
    den = None
    probs = []
    for s, _ in pieces:
        p = jnp.exp2(s - m)
        ls = jnp.sum(p, axis=-1, keepdims=True)
        den = ls if den is None else den + ls
        probs.append(p.astype(BF16))
    return probs, [val for _, val in pieces], den


def _weighted_values(probs, vals, den):
    acc = None
    for p, val in zip(probs, vals):
        o = jnp.dot(p, val, preferred_element_type=F32)
        acc = o if acc is None else acc + o
    return acc / den


def _pipelined_units(units, scores, store):
    n = len(units)
    pending_scores = scores(units[0])
    pending_probs = None
    for idx in range(n + 1):
        upcoming = scores(units[idx + 1]) if idx + 1 < n else None
        probs = _softmax(pending_scores) if idx < n else None
        if pending_probs is not None:
            store(units[idx - 1], _weighted_values(*pending_probs))
        pending_scores, pending_probs = upcoming, probs


def _attn_kernel(*refs, has_ctx, seq, tq, pairs_per_kv):
    if has_ctx:
        q_ref, k_ref, v_ref, kc_ref, vc_ref, o_ref = refs
    else:
        q_ref, k_ref, v_ref, o_ref = refs
    lane = lax.broadcasted_iota(jnp.int32, (1, LANES), 1)
    masks = [lane < HEAD_DIM, lane >= HEAD_DIM]
    if has_ctx:
        kv_head = pl.program_id(1)
        sel = jnp.where(masks[0], 0, 1) == kv_head
        kc = jnp.where(sel, kc_ref[...], 0.0)
        kc = (kc + pltpu.roll(kc, HEAD_DIM, 1)).astype(BF16)
        vc = jnp.where(sel, vc_ref[...], 0.0)
        vc = (vc + pltpu.roll(vc, HEAD_DIM, 1)).astype(BF16)
        vch = [jnp.where(mh, vc, jnp.zeros_like(vc)) for mh in masks]
    nt = (((1,), (1,)), ((), ()))
    groups = [(r0, j) for r0 in range(0, q_ref.shape[0], seq) for j in range(q_ref.shape[1] // LANES)]
    units = [(r0, j, t, hh) for r0, j in groups for t in range(r0 // tq, (r0 + seq) // tq)
             for hh in range(len(masks))]
    operands = {}

    def group_operands(r0, j):
        if (r0, j) not in operands:
            kcols = slice((j // pairs_per_kv) * LANES, (j // pairs_per_kv + 1) * LANES)
            v = v_ref[r0:r0 + seq, kcols]
            operands[r0, j] = (k_ref[r0:r0 + seq, kcols],
                               [jnp.where(mh, v, jnp.zeros_like(v)) for mh in masks])
        return operands[r0, j]

    def scores(unit):
        r0, j, t, hh = unit
        k, vh = group_operands(r0, j)
        q = q_ref[t * tq:(t + 1) * tq, j * LANES:(j + 1) * LANES]
        qm = jnp.where(masks[hh], q, jnp.zeros_like(q))
        pieces = [(lax.dot_general(qm, k, nt, preferred_element_type=F32), vh[hh])]
        if has_ctx:
            pieces.append((lax.dot_general(qm, kc, nt, preferred_element_type=F32), vch[hh]))
        return pieces

    partial = {}

    def store(unit, o):
        r0, j, t, hh = unit
        if hh == 0:
            partial[r0, j, t] = o
        else:
            out = partial.pop((r0, j, t)) + o
            o_ref[t * tq:(t + 1) * tq, j * LANES:(j + 1) * LANES] = out.astype(BF16)

    _pipelined_units(units, scores, store)


def _attention(q, k, v, *, n_batch, seq, row0, pairs_per_step, pairs_per_kv, batches_per_step=1,
               ctx=None):
    rows = batches_per_step * seq
    assert row0 % rows == 0 and n_batch % batches_per_step == 0
    blk0 = row0 // rows
    n_groups = q.shape[1] // (pairs_per_step * LANES)
    qw = pairs_per_step * LANES
    kw = qw // pairs_per_kv
    in_specs = [pl.BlockSpec((rows, qw), lambda b, g: (blk0 + b, g)),
                pl.BlockSpec((rows, kw), lambda b, g: (blk0 + b, g)),
                pl.BlockSpec((rows, kw), lambda b, g: (blk0 + b, g))]
    args = [q, k, v]
    if ctx is not None:
        assert kw == LANES and batches_per_step == 1
        kc, vc, e = ctx
        sc = kc.shape[2]
        cspec = pl.BlockSpec((None, None, sc, LANES), lambda b, g: (b, e, 0, 0))
        in_specs += [cspec, cspec]
        args += [kc, vc]
    return pl.pallas_call(
        functools.partial(_attn_kernel, has_ctx=ctx is not None, seq=seq, tq=min(TQ_ATTN, seq),
                          pairs_per_kv=pairs_per_kv),
        grid=(n_batch // batches_per_step, n_groups),
        in_specs=in_specs,
        out_specs=pl.BlockSpec((rows, qw), lambda b, g: (b, g)),
        out_shape=jax.ShapeDtypeStruct((n_batch * seq, q.shape[1]), BF16),
        compiler_params=_cparams(2),
        name="attention",
    )(*args)


def _na_row_start(r):
    rows = SEG // GRID_W
    return min(max(r - NA_ROWS // 2, 0), rows - NA_ROWS)


def _na_windows():
    rows = SEG // GRID_W
    windows = []
    for qb in range(rows // NA_QROWS):
        lo = _na_row_start(qb * NA_QROWS)
        hi = _na_row_start(qb * NA_QROWS + NA_QROWS - 1) + NA_ROWS
        n = hi - lo + (hi - lo) % 2
        ws = min(lo, rows - n)
        assert ws <= lo and hi <= ws + n <= rows and n <= NA_KROWS
        windows.append((ws, n))
    return windows


def _na_kernel(q_ref, k_ref, v_ref, kc_ref, vc_ref, r_ref, o_ref, bias_ref, tab_ref, *, windows):
    lane = lax.broadcasted_iota(jnp.int32, (1, LANES), 1)
    low = lane < HEAD_DIM

    @pl.when(pl.program_id(1) == 0)
    def _():
        qcol = lax.broadcasted_iota(jnp.int32, (GRID_W, LANES), 0)
        kcol = lax.broadcasted_iota(jnp.int32, (GRID_W, LANES), 1) & (GRID_W - 1)
        start = jnp.clip(qcol - NA_COLS // 2, 0, GRID_W - NA_COLS)
        inside = (kcol >= start) & (kcol < start + NA_COLS)
        for hh in range(LANES // HEAD_DIM):
            for d in range(2 * NA_ROWS - 1):
                base = jnp.broadcast_to(r_ref[hh, d:d + 1, :], (GRID_W, LANES))
                lo_t = pltpu.roll(base, LANES - (NA_COLS - 1), 1, stride=1, stride_axis=0)
                hi_t = pltpu.roll(base, GRID_W - (NA_COLS - 1), 1, stride=1, stride_axis=0)
                tab_ref[hh, d] = jnp.where(inside, jnp.where(low, lo_t, hi_t) * LOG2E, NEG)
        neg = jnp.full((GRID_W, LANES), NEG, F32)
        for hh in range(LANES // HEAD_DIM):
            for qb, (ws, nrows) in enumerate(windows):
                for i in range(NA_QROWS):
                    r = qb * NA_QROWS + i
                    rs = _na_row_start(r)
                    for jp in range(nrows // 2):
                        kr = ws + 2 * jp
                        ok = [rs <= kr + d < rs + NA_ROWS for d in (0, 1)]
                        if not any(ok):
                            blk = neg
                        else:
                            t0 = tab_ref[hh, kr - r + NA_ROWS - 1] if ok[0] else neg
                            t1 = tab_ref[hh, kr + 1 - r + NA_ROWS - 1] if ok[1] else neg
                            blk = jnp.where(low, t0, t1)
                        bias_ref[hh, qb, i * GRID_W:(i + 1) * GRID_W,
                                 jp * LANES:(jp + 1) * LANES] = blk

    masks = [low, jnp.logical_not(low)]
    n_batch = kc_ref.shape[0]
    kc = [kc_ref[bb].astype(BF16) for bb in range(n_batch)]
    vch = []
    for bb in range(n_batch):
        vc = vc_ref[bb].astype(BF16)
        vch.append([jnp.where(mh, vc, jnp.zeros_like(vc)) for mh in masks])
    nt = (((1,), (1,)), ((), ()))
    nq = NA_QROWS * GRID_W
    units = [(bb, qb, hh) for bb in range(n_batch) for qb in range(len(windows))
             for hh in range(len(masks))]

    def scores(unit):
        bb, qb, hh = unit
        ws, nrows = windows[qb]
        nk = nrows * GRID_W
        k0 = bb * SEG + ws * GRID_W
        q = q_ref[bb * SEG + qb * nq:bb * SEG + (qb + 1) * nq, :]
        kw = k_ref[k0:k0 + nk, :]
        vw = v_ref[k0:k0 + nk, :]
        qm = jnp.where(masks[hh], q, jnp.zeros_like(q))
        s_win = (lax.dot_general(qm, kw, nt, preferred_element_type=F32)
                 + bias_ref[hh, qb, :, 0:nk])
        s_ctx = lax.dot_general(qm, kc[bb], nt, preferred_element_type=F32)
        return [(s_win, jnp.where(masks[hh], vw, jnp.zeros_like(vw))), (s_ctx, vch[bb][hh])]

    partial = {}

    def store(unit, o):
        bb, qb, hh = unit
        if hh == 0:
            partial[bb, qb] = o
        else:
            r0 = bb * SEG + qb * nq
            o_ref[r0:r0 + nq, :] = (partial.pop((bb, qb)) + o).astype(BF16)

    _pipelined_units(units, scores, store)


def _na_attention(q, k, v, kc, vc, o_idx, rpb, *, n_batch, row0):
    bps = NA_BATCHES_PER_STEP
    rows = bps * SEG
    assert row0 % rows == 0 and n_batch % bps == 0
    blk0 = row0 // rows
    n_pairs = q.shape[1] // LANES
    sc = kc.shape[2]
    windows = _na_windows()
    heads = LANES // HEAD_DIM
    nr = 2 * NA_ROWS - 1
    qkv = pl.BlockSpec((rows, LANES), lambda p, b: (blk0 + b, p))
    cspec = pl.BlockSpec((bps, None, sc, LANES), lambda p, b: (b, o_idx, 0, p))
    rspec = pl.BlockSpec((None, heads, nr, LANES), lambda p, b: (o_idx, p, 0, 0))
    return pl.pallas_call(
        functools.partial(_na_kernel, windows=windows),
        grid=(n_pairs, n_batch // bps),
        in_specs=[qkv, qkv, qkv, cspec, cspec, rspec],
        out_specs=pl.BlockSpec((rows, LANES), lambda p, b: (b, p)),
        out_shape=jax.ShapeDtypeStruct((n_batch * SEG, n_pairs * LANES), BF16),
        scratch_shapes=[pltpu.VMEM((heads, len(windows), NA_QROWS * GRID_W, NA_KROWS * GRID_W), F32),
                        pltpu.VMEM((heads, nr, GRID_W, LANES), F32)],
        compiler_params=_cparams(2),
        name="na_attention",
    )(q, k, v, kc, vc, rpb)


def _out_kernel(*refs, n_prompt_tiles, split_x, has_a):
    refs = list(refs)
    x = _pick(n_prompt_tiles, refs.pop(0), refs.pop(0)) if split_x else refs.pop(0)[...]
    gt_ref, g_ref, fsh_ref, fsc_ref, gf_ref, w_ref = refs[:6]
    a_ref = refs[6] if has_a else None
    bp_ref, bs_ref, o_ref, h_ref, wbf_ref = refs[6 + has_a:]

    @pl.when(pl.program_id(0) == 0)
    def _():
        wbf_ref[...] = w_ref[...].astype(BF16)

    b = _pick(n_prompt_tiles, bp_ref, bs_ref)
    if has_a:
        half = a_ref.shape[1]
        y = (jnp.dot(a_ref[...], wbf_ref[0:half, :], preferred_element_type=F32)
             + jnp.dot(b, wbf_ref[half:2 * half, :], preferred_element_type=F32))
    else:
        y = jnp.dot(b, wbf_ref[...], preferred_element_type=F32)
    x1 = _gated_residual(x, y, g_ref[...], gt_ref[...])
    o_ref[...] = x1
    h_ref[...] = _mod_norm(x1, gf_ref[...], fsh_ref[...], fsc_ref[...]).astype(BF16)


def _out_proj(xs, a, bp, bs, mods, layer, li, T, g_post, g_ffn_pre, w_out):
    tm = TM_OUT
    n_prompt_tiles = T // 2 // tm
    split_x = len(xs) == 2
    has_a = a is not None
    row = lambda w: pl.BlockSpec((tm, w), lambda i: (i, 0))
    x_specs = _split_specs(tm, D, n_prompt_tiles) if split_x else [row(D)]
    a_specs = [row(a.shape[1])] if has_a else []
    return pl.pallas_call(
        functools.partial(_out_kernel, n_prompt_tiles=n_prompt_tiles, split_x=split_x, has_a=has_a),
        grid=(T // tm,),
        in_specs=x_specs + [_mod_spec(layer, 2, tm), _const_spec((1, D), layer),
                            _mod_spec(layer, 3, tm), _mod_spec(layer, 4, tm),
                            _const_spec((1, D), layer), _weight_spec((D, D), li)] + a_specs
                 + _split_specs(tm, bp.shape[1], n_prompt_tiles),
        out_specs=[row(D), row(D)],
        out_shape=[jax.ShapeDtypeStruct((T, D), F32), jax.ShapeDtypeStruct((T, D), BF16)],
        scratch_shapes=[pltpu.VMEM((D, D), BF16)],
        compiler_params=_cparams(1),
        name="out_proj",
    )(*xs, mods, g_post, mods, mods, g_ffn_pre, w_out, *([a] if has_a else []), bp, bs)


def _ffn_kernel(x_ref, h_ref, gt_ref, gpost_ref, win_ref, wout_ref, *rest,
                layer, n_prompt_tiles, split_out):
    out_refs = rest[:-9]
    wg_buf, wu_buf, wo_buf, wg_res, wu_res, wo_res, act_buf, acc_ref, sem = rest[-9:]
    nf, _, tf = wg_res.shape
    first_tile = pl.program_id(0) == 0

    def aligned(f):
        return f * tf if isinstance(f, int) else pl.multiple_of(f * tf, tf)

    def in_copies(f, slot):
        col = aligned(f)
        return (pltpu.make_async_copy(win_ref.at[layer, :, pl.ds(col, tf)], wg_buf.at[slot],
                                      sem.at[0, slot]),
                pltpu.make_async_copy(win_ref.at[layer, :, pl.ds(D_FF + col, tf)], wu_buf.at[slot],
                                      sem.at[1, slot]))

    def out_copy(f, slot):
        row = aligned(f)
        return pltpu.make_async_copy(wout_ref.at[layer, pl.ds(row, tf), :], wo_buf.at[slot],
                                     sem.at[2, slot])

    def fetch(f, slot):
        @pl.when(first_tile)
        def _():
            for c in in_copies(f, slot):
                c.wait()
            out_copy(f, slot).wait()

            nxt = min(f + 1, nf - 1) if isinstance(f, int) else jnp.minimum(f + 1, nf - 1)

            @pl.when(jnp.asarray(f + 1 < nf))
            def _():
                for c in in_copies(nxt, 1 - slot):
                    c.start()
                out_copy(nxt, 1 - slot).start()

            wg_res[f] = wg_buf[slot].astype(BF16)
            wu_res[f] = wu_buf[slot].astype(BF16)
            wo_res[f] = wo_buf[slot].astype(BF16)

    def hidden(f, slot):
        h = h_ref[...]
        g = jnp.dot(h, wg_res[f], preferred_element_type=F32)
        u = jnp.dot(h, wu_res[f], preferred_element_type=F32)
        act_buf[slot] = (g * jax.nn.sigmoid(g) * u).astype(BF16)

    def project(f, slot):
        acc_ref[...] += jnp.dot(act_buf[slot], wo_res[f], preferred_element_type=F32)

    def step(f, slot):
        fetch(f, slot)
        hidden(f, slot)
        project(f - 1, 1 - slot)

    @pl.when(first_tile)
    def _():
        for c in in_copies(0, 0):
            c.start()
        out_copy(0, 0).start()

    acc_ref[...] = jnp.zeros_like(acc_ref)
    fetch(0, 0)
    hidden(0, 0)

    assert nf % 2 == 1

    def pair(k, carry):
        step(2 * k + 1, 1)
        step(2 * k + 2, 0)
        return carry

    lax.fori_loop(0, (nf - 1) // 2, pair, 0)
    project(nf - 1, 0)

    def result():
        return _gated_residual(x_ref[...], acc_ref[...], gpost_ref[...], gt_ref[...])

    if split_out:
        i = pl.program_id(0)

        @pl.when(i < n_prompt_tiles)
        def _():
            out_refs[0][...] = result()

        @pl.when(i >= n_prompt_tiles)
        def _():
            out_refs[1][...] = result()
    else:
        out_refs[0][...] = result()


def _ffn(x, h, mods, layer, g_post, w_in, w_out, split_out):
    T = x.shape[0]
    tm, tf = TM_FFN, TF_FFN
    nf = D_FF // tf
    n_prompt_tiles = T // 2 // tm
    xrow = pl.BlockSpec((tm, D), lambda i: (i, 0))
    if split_out:
        out_specs = _split_specs(tm, D, n_prompt_tiles)
        out_shape = [jax.ShapeDtypeStruct((T // 2, D), F32)] * 2
    else:
        out_specs = [xrow]
        out_shape = [jax.ShapeDtypeStruct((T, D), F32)]
    hbm = pl.BlockSpec(memory_space=pl.ANY)
    return pl.pallas_call(
        functools.partial(_ffn_kernel, layer=layer, n_prompt_tiles=n_prompt_tiles,
                          split_out=split_out),
        grid=(T // tm,),
        in_specs=[xrow, xrow, _mod_spec(layer, 5, tm), _const_spec((1, D), layer), hbm, hbm],
        out_specs=out_specs,
        out_shape=out_shape,
        scratch_shapes=[pltpu.VMEM((2, D, tf), F32), pltpu.VMEM((2, D, tf), F32),
                        pltpu.VMEM((2, tf, D), F32), pltpu.VMEM((nf, D, tf), BF16),
                        pltpu.VMEM((nf, D, tf), BF16), pltpu.VMEM((nf, tf, D), BF16),
                        pltpu.VMEM((2, tm, tf), BF16), pltpu.VMEM((tm, D), F32),
                        pltpu.SemaphoreType.DMA((3, 2))],
        compiler_params=_cparams(1),
        name="ffn",
    )(x, h, mods, g_post, w_in, w_out)


def _rope_tables():
    t = jnp.arange(SEG)
    nf = HEAD_DIM // 4
    freqs = ROPE_BASE ** (-jnp.arange(nf, dtype=F32) / nf)

    def cs(pos):
        ang = pos.astype(F32)[:, None] * freqs[None, :]
        return jnp.cos(ang), jnp.sin(ang)

    cr, sr = cs(t // GRID_W)
    cc, sn = cs(t % GRID_W)
    cos = jnp.concatenate([cr, cr, cc, cc], axis=1)
    sin = jnp.concatenate([-sr, sr, -sn, sn], axis=1)
    reps = LANES // HEAD_DIM
    cos = jnp.tile(cos, (1, reps))
    sin = jnp.tile(sin, (1, reps))
    return (jnp.stack([jnp.ones_like(cos), cos]), jnp.stack([jnp.zeros_like(sin), sin]))


def _block_diag_ones(n):
    i = jnp.arange(n) // HEAD_DIM
    return (i[:, None] == i[None, :]).astype(BF16)


def kernel(x_prompt, x_sample, cache_attn_k, cache_attn_v, cache_na_k, cache_na_v, c, c_ctx,
           mod_w, mod_b, norm_mix_pre, norm_mix_post, norm_ffn_pre, norm_ffn_post,
           even_w_in, even_w_out, sgu_w, sgu_b, sgu_norm, q_norm, k_norm,
           odd_w_in, odd_w_out, na_rpb, ffn_w_in, ffn_w_out):
    nb_p, seq_p, _ = x_prompt.shape
    nb_s, seq_s, _ = x_sample.shape
    tp = nb_p * seq_p
    T = 2 * tp
    assert seq_s == SEG and tp == nb_s * seq_s and tp % SEG == 0

    cond = jnp.concatenate([jnp.broadcast_to(c_ctx[None, :], (tp // SEG, D)), c], axis=0)
    mods = _modulation(cond, mod_w, mod_b)

    cos, sin = _rope_tables()
    qw = B_HEADS * HEAD_DIM
    bdq = _block_diag_ones(qw)
    bdk = _block_diag_ones(LANES)
    past = cache_attn_k.shape[2]
    ctx_ak = cache_attn_k.reshape(nb_s, -1, past, B_KV_HEADS * HEAD_DIM)
    ctx_av = cache_attn_v.reshape(nb_s, -1, past, B_KV_HEADS * HEAD_DIM)
    ctx_nk = cache_na_k.reshape(nb_s, -1, past, C_HEADS * HEAD_DIM)
    ctx_nv = cache_na_v.reshape(nb_s, -1, past, C_HEADS * HEAD_DIM)
    g_mix_pre, g_mix_post, g_ffn_pre, g_ffn_post = (
        a.reshape(DEPTH, 1, D) for a in (norm_mix_pre, norm_mix_post, norm_ffn_pre, norm_ffn_post))
    n_even = even_w_in.shape[0]
    sgu_b3 = sgu_b.reshape(n_even, A_GROUPS, CHUNK, 1)
    sgu_n3 = sgu_norm.reshape(n_even, 1, A_WIDTH)
    qn3 = jnp.tile(q_norm, (1, B_HEADS)).reshape(n_even, 1, qw)
    kn3 = jnp.tile(k_norm, (1, B_KV_HEADS)).reshape(n_even, 1, LANES)
    rpb = jnp.pad(na_rpb, ((0, 0), (0, 0), (0, 0), (0, LANES - na_rpb.shape[3])))

    xs = (x_prompt.reshape(tp, D), x_sample.reshape(nb_s * seq_s, D))
    n_odd = odd_w_in.shape[0]
    attn_k, attn_v, na_cache = [], [], ()
    for l in range(DEPTH):
        if l % 2 == 0:
            e = l // 2
            a_out, q, kd, vd, kf, vf = _even_in(xs, mods, l, e, T, g_mix_pre, even_w_in, sgu_w,
                                                sgu_b3, sgu_n3, qn3, kn3, bdq, bdk, cos, sin)
            n_pairs = qw // LANES
            pairs_per_kv = n_pairs // B_KV_HEADS
            mix_p = _attention(q, kd, vd, n_batch=nb_p, seq=seq_p, row0=0, pairs_per_step=n_pairs,
                               pairs_per_kv=pairs_per_kv, batches_per_step=PROMPT_BATCHES_PER_STEP)
            mix_s = _attention(q, kd, vd, n_batch=nb_s, seq=seq_s, row0=tp,
                               pairs_per_step=pairs_per_kv, pairs_per_kv=pairs_per_kv,
                               ctx=(ctx_ak, ctx_av, e))
            attn_k.append(kf[:tp].reshape(nb_p, seq_p, B_KV_HEADS, HEAD_DIM))
            attn_v.append(vf[:tp].reshape(nb_p, seq_p, B_KV_HEADS, HEAD_DIM))
            w_out, li = even_w_out, e
        else:
            o = l // 2
            a_out = None
            q, k, v, *na_cache = _odd_in(xs[0], mods, l, o, n_odd, seq_p, g_mix_pre, odd_w_in,
                                         tuple(na_cache))
            mix_p = _attention(q, k, v, n_batch=nb_p, seq=seq_p, row0=0,
                               pairs_per_step=PAIRS_PER_STEP, pairs_per_kv=1,
                               batches_per_step=PROMPT_BATCHES_PER_STEP)
            mix_s = _na_attention(q, k, v, ctx_nk, ctx_nv, o, rpb, n_batch=nb_s, row0=tp)
            w_out, li = odd_w_out, o
        x, h = _out_proj(xs, a_out, mix_p, mix_s, mods, l, li, T, g_mix_post, g_ffn_pre, w_out)
        xs = tuple(_ffn(x, h, mods, l, g_ffn_post, ffn_w_in, ffn_w_out,
                        split_out=(l == DEPTH - 1)))

    y_prompt = xs[0].reshape(nb_p, seq_p, D)
    y_sample = xs[1].reshape(nb_s, seq_s, D)
    new_na_k, new_na_v = (a.reshape(nb_p, n_odd, seq_p, C_HEADS, HEAD_DIM) for a in na_cache)
    return (y_prompt, y_sample, jnp.stack(attn_k, axis=1), jnp.stack(attn_v, axis=1),
            new_na_k, new_na_v)
```

```python
import functools

import jax
import jax.numpy as jnp
from jax import lax
from jax.experimental import pallas as pl
from jax.experimental.pallas import tpu as pltpu

F32 = jnp.float32
BF16 = jnp.bfloat16

D = 1024
DEPTH = 4
HEAD_DIM = 64
GRID_W = 64
CHUNK = 128
A_WIDTH = D // 2
A_GROUPS = 4
B_HEADS = 8
B_KV_HEADS = 2
C_HEADS = 16
NA_ROWS = 8
NA_COLS = 16
D_FF = 2816
ROPE_BASE = 10000.0
EPS = 1e-6
NEG = -1e30
SEG = 1024
N_SEG = 8
LANES = 128
LOG2E = 1.4426950408889634
QSCALE = HEAD_DIM ** -0.5 * LOG2E

TM_PROJ = 512
TM_OUT = 1024
TM_FFN = 1024
TF_FFN = 256
TQ_ATTN = 256
PAIRS_PER_STEP = 4
PROMPT_BATCHES_PER_STEP = 4
NA_QROWS = 4
NA_BATCHES_PER_STEP = 2
NA_KROWS = 12
VMEM_LIMIT = 56 * 1024 * 1024
VMEM_LIMIT_FFN = 62 * 1024 * 1024


def _cparams(n_axes, vmem_limit=VMEM_LIMIT):
    return pltpu.CompilerParams(dimension_semantics=("arbitrary",) * n_axes,
                                vmem_limit_bytes=vmem_limit)


def _rms(x):
    return x * lax.rsqrt(jnp.mean(x * x, axis=-1, keepdims=True) + EPS)


def _mod_norm(x, g, shift, scale):
    return _rms(x) * (g * (1.0 + scale)) + shift


def _gated_residual(x, y, g, gate):
    return x + _rms(y) * (gate * g)


def _const_spec(shape, *lead):
    block = (None,) * len(lead) + tuple(shape)
    return pl.BlockSpec(block, lambda *_: tuple(lead) + (0,) * len(shape))


def _weight_spec(shape, *lead):
    block = (None,) * len(lead) + tuple(shape)
    return pl.BlockSpec(block, lambda *_: tuple(lead) + (0,) * len(shape),
                        pipeline_mode=pl.Buffered(1))


def _same_tile(i):
    return i


def _mod_spec(layer, j, tm, tile=_same_tile):
    return pl.BlockSpec((None, None, None, 1, D),
                        lambda i, *_: (layer, j, (tile(i) * tm) // SEG, 0, 0))


def _split_specs(tm, width, n_prompt_tiles, col=0, tile=_same_tile):
    return [pl.BlockSpec((tm, width),
                         lambda i, *_: (jnp.minimum(tile(i), n_prompt_tiles - 1), col)),
            pl.BlockSpec((tm, width),
                         lambda i, *_: (jnp.maximum(tile(i) - n_prompt_tiles, 0), col))]


def _pick(n_prompt_tiles, p_ref, s_ref, tile=None):
    tile = pl.program_id(0) if tile is None else tile
    return jnp.where(tile < n_prompt_tiles, p_ref[...], s_ref[...])


def _mod_kernel(cond_ref, w_ref, b_ref, o_ref):
    s = cond_ref[...]
    s = s * jax.nn.sigmoid(s)
    o_ref[...] = jnp.dot(s.astype(BF16), w_ref[...].astype(BF16),
                         preferred_element_type=F32) + b_ref[...]


def _modulation(cond, mod_w, mod_b):
    b = mod_b.reshape(DEPTH, 6, 1, D)
    out = pl.pallas_call(
        _mod_kernel,
        grid=(DEPTH, 6),
        in_specs=[pl.BlockSpec((N_SEG, D), lambda l, j: (0, 0)),
                  pl.BlockSpec((None, D, D), lambda l, j: (l, 0, j)),
                  pl.BlockSpec((None, None, 1, D), lambda l, j: (l, j, 0, 0))],
        out_specs=pl.BlockSpec((None, None, N_SEG, D), lambda l, j: (l, j, 0, 0)),
        out_shape=jax.ShapeDtypeStruct((DEPTH, 6, N_SEG, D), F32),
        compiler_params=_cparams(2),
        name="modulation",
    )(cond, mod_w, b)
    return out.reshape(DEPTH, 6, N_SEG, 1, D)


def _head_sumsq(y, bd_ref):
    sq = y * y
    hi = sq.astype(BF16)
    lo = (sq - hi.astype(F32)).astype(BF16)
    bd = bd_ref[...]
    return (jnp.dot(hi, bd, preferred_element_type=F32)
            + jnp.dot(lo, bd, preferred_element_type=F32))


def _rope(y, cos, sin):
    lane = lax.broadcasted_iota(jnp.int32, (1, LANES), 1)
    first = (lane & 16) == 0
    partner = jnp.where(first, pltpu.roll(y, LANES - 16, 1), pltpu.roll(y, 16, 1))
    return y * cos + partner * sin


def _even_in_kernel(*refs, n_tiles, n_prompt_tiles, split_x):
    refs = list(refs)
    i = pl.program_id(0)
    t_proj = jnp.minimum(i, n_tiles - 1)
    x = _pick(n_prompt_tiles, refs.pop(0), refs.pop(0), t_proj) if split_x else refs.pop(0)[...]
    (sh_ref, sc_ref, g_ref, w_ref, sguw_ref, sgub_ref, sgun_ref, qn_ref, kn_ref, bdq_ref, bdk_ref,
     cos_ref, sin_ref, a_ref, q_ref, kd_ref, vd_ref, kf_ref, vf_ref, wbf_ref, h_ref, z0_ref,
     z1_ref) = refs
    tm = a_ref.shape[0]
    qw = B_HEADS * HEAD_DIM
    kw = B_KV_HEADS * HEAD_DIM
    c_q = 2 * A_WIDTH
    c_k = c_q + qw
    n_chunks = tm // CHUNK
    gch = A_WIDTH // A_GROUPS
    lane = lax.broadcasted_iota(jnp.int32, (1, LANES), 1)
    low = lane < HEAD_DIM

    @pl.when(i == 0)
    def _():
        wbf_ref[...] = w_ref[...].astype(BF16)
        z1_ref[...] = jnp.zeros_like(z1_ref)

    h_ref[...] = _mod_norm(x, g_ref[...], sh_ref[...], sc_ref[...]).astype(BF16)

    def step(z_new_ref, z_ref):
        def project(c0, c1):
            z_new_ref[:, c0:c1] = jnp.dot(h_ref[...], wbf_ref[:, c0:c1], preferred_element_type=F32)

        project(0, A_WIDTH)
        u = jax.nn.gelu(z_ref[:, 0:A_WIDTH], approximate=True)
        project(A_WIDTH, c_q)
        v = jax.nn.gelu(z_ref[:, A_WIDTH:c_q], approximate=True)
        mu = jnp.mean(v, axis=-1, keepdims=True)
        vc = v - mu
        var = jnp.mean(vc * vc, axis=-1, keepdims=True)
        vn = (vc * lax.rsqrt(var + EPS) * sgun_ref[...]).astype(BF16)
        for g in range(A_GROUPS):
            rhs = jnp.concatenate([vn[n * CHUNK:(n + 1) * CHUNK, g * gch:(g + 1) * gch]
                                   for n in range(n_chunks)], axis=1)
            mixed = jnp.dot(sguw_ref[g].astype(BF16), rhs, preferred_element_type=F32)
            bias = sgub_ref[g]
            for n in range(n_chunks):
                blk = (mixed[:, n * gch:(n + 1) * gch] + bias) * u[n * CHUNK:(n + 1) * CHUNK,
                                                                  g * gch:(g + 1) * gch]
                a_ref[n * CHUNK:(n + 1) * CHUNK, g * gch:(g + 1) * gch] = blk.astype(BF16)

        project(c_q, c_k)
        q = z_ref[:, c_q:c_k]
        q = q * lax.rsqrt(_head_sumsq(q, bdq_ref) * (1.0 / HEAD_DIM) + EPS) * qn_ref[...]
        cos = cos_ref[...]
        sin = sin_ref[...]
        for j in range(qw // LANES):
            qj = _rope(q[:, j * LANES:(j + 1) * LANES], cos, sin) * QSCALE
            q_ref[:, j * LANES:(j + 1) * LANES] = qj.astype(BF16)

        project(c_k, c_k + 2 * kw)
        k = z_ref[:, c_k:c_k + kw]
        k = k * lax.rsqrt(_head_sumsq(k, bdk_ref) * (1.0 / HEAD_DIM) + EPS) * kn_ref[...]
        kf_ref[...] = k
        vals = z_ref[:, c_k + kw:c_k + 2 * kw]
        vf_ref[...] = vals
        for src, dst in ((_rope(k, cos, sin), kd_ref), (vals, vd_ref)):
            sw = pltpu.roll(src, HEAD_DIM, 1)
            dst[:, 0:LANES] = jnp.where(low, src, sw).astype(BF16)
            dst[:, LANES:2 * LANES] = jnp.where(low, sw, src).astype(BF16)

    @pl.when(i % 2 == 0)
    def _():
        step(z0_ref, z1_ref)

    @pl.when(i % 2 == 1)
    def _():
        step(z1_ref, z0_ref)


def _even_in(xs, mods, layer, e, T, g_pre, w_in, sgu_w, sgu_b, sgu_norm, q_norm, k_norm,
             bdq, bdk, cos, sin):
    tm = TM_PROJ
    n_in = w_in.shape[2]
    n_tiles = T // tm
    tiles_per_seg = SEG // tm
    n_prompt_tiles = n_tiles // 2
    split_x = len(xs) == 2
    proj = lambda i: jnp.minimum(i, n_tiles - 1)
    post = lambda i: jnp.maximum(i - 1, 0)
    tab_spec = pl.BlockSpec((None, tm, LANES),
                            lambda i: (post(i) // n_prompt_tiles, post(i) % tiles_per_seg, 0))
    row = lambda w, tile: pl.BlockSpec((tm, w), lambda i: (tile(i), 0))
    x_specs = (_split_specs(tm, D, n_prompt_tiles, tile=proj) if split_x else [row(D, proj)])
    qw = B_HEADS * HEAD_DIM
    return pl.pallas_call(
        functools.partial(_even_in_kernel, n_tiles=n_tiles, n_prompt_tiles=n_prompt_tiles,
                          split_x=split_x),
        grid=(n_tiles + 1,),
        in_specs=x_specs + [
            _mod_spec(layer, 0, tm, proj), _mod_spec(layer, 1, tm, proj),
            _const_spec((1, D), layer),
            _weight_spec((D, n_in), e), _const_spec((A_GROUPS, CHUNK, CHUNK), e),
            _const_spec((A_GROUPS, CHUNK, 1), e), _const_spec((1, A_WIDTH), e),
            _const_spec((1, qw), e), _const_spec((1, LANES), e),
            _const_spec((qw, qw)), _const_spec((LANES, LANES)), tab_spec, tab_spec],
        out_specs=[row(A_WIDTH, post), row(qw, post), row(2 * LANES, post), row(2 * LANES, post),
                   row(LANES, post), row(LANES, post)],
        out_shape=[jax.ShapeDtypeStruct((T, A_WIDTH), BF16),
                   jax.ShapeDtypeStruct((T, qw), BF16),
                   jax.ShapeDtypeStruct((T, 2 * LANES), BF16),
                   jax.ShapeDtypeStruct((T, 2 * LANES), BF16),
                   jax.ShapeDtypeStruct((T, LANES), F32),
                   jax.ShapeDtypeStruct((T, LANES), F32)],
        scratch_shapes=[pltpu.VMEM((D, n_in), BF16), pltpu.VMEM((tm, D), BF16),
                        pltpu.VMEM((tm, n_in), F32), pltpu.VMEM((tm, n_in), F32)],
        compiler_params=_cparams(1),
        name="even_in_proj",
    )(*xs, mods, mods, g_pre, w_in, sgu_w, sgu_b, sgu_norm, q_norm, k_norm, bdq, bdk, cos, sin)


def _odd_in_kernel(x_ref, sh_ref, sc_ref, g_ref, w_ref, *rest, n_prompt_tiles, slot, all_slots):
    q_ref, k_ref, v_ref, kf_ref, vf_ref, wbf_ref = rest[-6:]
    i = pl.program_id(0)
    hw = C_HEADS * HEAD_DIM

    @pl.when(i == 0)
    def _():
        for c in range(3):
            wbf_ref[:, c * hw:(c + 1) * hw] = w_ref[:, c * hw:(c + 1) * hw].astype(BF16)

    h = _mod_norm(x_ref[...], g_ref[...], sh_ref[...], sc_ref[...]).astype(BF16)
    q = jnp.dot(h, wbf_ref[:, 0:hw], preferred_element_type=F32)
    q_ref[...] = (q * QSCALE).astype(BF16)
    k = jnp.dot(h, wbf_ref[:, hw:2 * hw], preferred_element_type=F32)
    k_ref[...] = k.astype(BF16)
    v = jnp.dot(h, wbf_ref[:, 2 * hw:3 * hw], preferred_element_type=F32)
    v_ref[...] = v.astype(BF16)

    @pl.when(i < n_prompt_tiles)
    def _():
        for src, dst in ((k, kf_ref), (v, vf_ref)):
            if all_slots:
                val = src.reshape((dst.shape[0],) + dst.shape[2:])
                for s in range(dst.shape[1]):
                    dst[:, s] = val if s == slot else jnp.zeros_like(val)
            else:
                dst[...] = src.reshape(dst.shape)


def _odd_in(x, mods, layer, o, n_odd, seq_p, g_pre, w_in, caches):
    T = x.shape[0]
    tm = TM_PROJ
    hw = C_HEADS * HEAD_DIM
    n_prompt_tiles = T // 2 // tm
    bt = tm // seq_p
    row = pl.BlockSpec((tm, hw), lambda i: (i, 0))
    all_slots = not caches
    if all_slots:
        crow = pl.BlockSpec((bt, n_odd, seq_p, hw),
                            lambda i: (jnp.minimum(i, n_prompt_tiles - 1), 0, 0, 0))
    else:
        crow = pl.BlockSpec((bt, None, seq_p, hw),
                            lambda i: (jnp.minimum(i, n_prompt_tiles - 1), o, 0, 0))
    cshape = jax.ShapeDtypeStruct((T // 2 // seq_p, n_odd, seq_p, hw), F32)
    n_in = 5
    return pl.pallas_call(
        functools.partial(_odd_in_kernel, n_prompt_tiles=n_prompt_tiles, slot=o,
                          all_slots=all_slots),
        grid=(T // tm,),
        in_specs=[pl.BlockSpec((tm, D), lambda i: (i, 0)), _mod_spec(layer, 0, tm),
                  _mod_spec(layer, 1, tm), _const_spec((1, D), layer),
                  _weight_spec((D, 3 * hw), o)]
                 + [pl.BlockSpec(memory_space=pl.ANY)] * len(caches),
        out_specs=[row, row, row, crow, crow],
        out_shape=[jax.ShapeDtypeStruct((T, hw), BF16)] * 3 + [cshape, cshape],
        input_output_aliases={n_in + j: 3 + j for j in range(len(caches))},
        scratch_shapes=[pltpu.VMEM((D, 3 * hw), BF16)],
        compiler_params=_cparams(1),
        name="odd_in_proj",
    )(x, mods, mods, g_pre, w_in, *caches)


def _softmax(pieces):
    m = None
    for s, _ in pieces:
        ms = jnp.max(s, axis=-1, keepdims=True)
        m = ms if m is None else jnp.maximum(m, ms)
    den = None
    probs = []
    for s, _ in pieces:
        p = jnp.exp2(s - m)
        ls = jnp.sum(p, axis=-1, keepdims=True)
        den = ls if den is None else den + ls
        probs.append(p.astype(BF16))
    return probs, [val for _, val in pieces], den


def _weighted_values(probs, vals, den):
    acc = None
    for p, val in zip(probs, vals):
        o = jnp.dot(p, val, preferred_element_type=F32)
        acc = o if acc is None else acc + o
    return acc / den


def _pipelined_units(units, scores, store):
    n = len(units)
    pending_scores = scores(units[0])
    pending_probs = None
    for idx in range(n + 1):
        upcoming = scores(units[idx + 1]) if idx + 1 < n else None
        probs = _softmax(pending_scores) if idx < n else None
        if pending_probs is not None:
            store(units[idx - 1], _weighted_values(*pending_probs))
        pending_scores, pending_probs = upcoming, probs


def _attn_kernel(*refs, has_ctx, seq, tq, pairs_per_kv):
    if has_ctx:
        q_ref, k_ref, v_ref, kc_ref, vc_ref, o_ref = refs
    else:
        q_ref, k_ref, v_ref, o_ref = refs
    lane = lax.broadcasted_iota(jnp.int32, (1, LANES), 1)
    masks = [lane < HEAD_DIM, lane >= HEAD_DIM]
    if has_ctx:
        kv_head = pl.program_id(1)
        sel = jnp.where(masks[0], 0, 1) == kv_head
        kc = jnp.where(sel, kc_ref[...], 0.0)
        kc = (kc + pltpu.roll(kc, HEAD_DIM, 1)).astype(BF16)
        vc = jnp.where(sel, vc_ref[...], 0.0)
        vc = (vc + pltpu.roll(vc, HEAD_DIM, 1)).astype(BF16)
        vch = [jnp.where(mh, vc, jnp.zeros_like(vc)) for mh in masks]
    nt = (((1,), (1,)), ((), ()))
    groups = [(r0, j) for r0 in range(0, q_ref.shape[0], seq) for j in range(q_ref.shape[1] // LANES)]
    units = [(r0, j, t, hh) for r0, j in groups for t in range(r0 // tq, (r0 + seq) // tq)
             for hh in range(len(masks))]
    operands = {}

    def group_operands(r0, j):
        if (r0, j) not in operands:
            kcols = slice((j // pairs_per_kv) * LANES, (j // pairs_per_kv + 1) * LANES)
            v = v_ref[r0:r0 + seq, kcols]
            operands[r0, j] = (k_ref[r0:r0 + seq, kcols],
                               [jnp.where(mh, v, jnp.zeros_like(v)) for mh in masks])
        return operands[r0, j]

    def scores(unit):
        r0, j, t, hh = unit
        k, vh = group_operands(r0, j)
        q = q_ref[t * tq:(t + 1) * tq, j * LANES:(j + 1) * LANES]
        qm = jnp.where(masks[hh], q, jnp.zeros_like(q))
        pieces = [(lax.dot_general(qm, k, nt, preferred_element_type=F32), vh[hh])]
        if has_ctx:
            pieces.append((lax.dot_general(qm, kc, nt, preferred_element_type=F32), vch[hh]))
        return pieces

    partial = {}

    def store(unit, o):
        r0, j, t, hh = unit
        if hh == 0:
            partial[r0, j, t] = o
        else:
            out = partial.pop((r0, j, t)) + o
            o_ref[t * tq:(t + 1) * tq, j * LANES:(j + 1) * LANES] = out.astype(BF16)

    _pipelined_units(units, scores, store)


def _attention(q, k, v, *, n_batch, seq, row0, pairs_per_step, pairs_per_kv, batches_per_step=1,
               ctx=None):
    rows = batches_per_step * seq
    assert row0 % rows == 0 and n_batch % batches_per_step == 0
    blk0 = row0 // rows
    n_groups = q.shape[1] // (pairs_per_step * LANES)
    qw = pairs_per_step * LANES
    kw = qw // pairs_per_kv
    in_specs = [pl.BlockSpec((rows, qw), lambda b, g: (blk0 + b, g)),
                pl.BlockSpec((rows, kw), lambda b, g: (blk0 + b, g)),
                pl.BlockSpec((rows, kw), lambda b, g: (blk0 + b, g))]
    args = [q, k, v]
    if ctx is not None:
        assert kw == LANES and batches_per_step == 1
        kc, vc, e = ctx
        sc = kc.shape[2]
        cspec = pl.BlockSpec((None, None, sc, LANES), lambda b, g: (b, e, 0, 0))
        in_specs += [cspec, cspec]
        args += [kc, vc]
    return pl.pallas_call(
        functools.partial(_attn_kernel, has_ctx=ctx is not None, seq=seq, tq=min(TQ_ATTN, seq),
                          pairs_per_kv=pairs_per_kv),
        grid=(n_batch // batches_per_step, n_groups),
        in_specs=in_specs,
        out_specs=pl.BlockSpec((rows, qw), lambda b, g: (b, g)),
        out_shape=jax.ShapeDtypeStruct((n_batch * seq, q.shape[1]), BF16),
        compiler_params=_cparams(2),
        name="attention",
    )(*args)


def _na_row_start(r):
    rows = SEG // GRID_W
    return min(max(r - NA_ROWS // 2, 0), rows - NA_ROWS)


def _na_windows():
    rows = SEG // GRID_W
    windows = []
    for qb in range(rows // NA_QROWS):
        lo = _na_row_start(qb * NA_QROWS)
        hi = _na_row_start(qb * NA_QROWS + NA_QROWS - 1) + NA_ROWS
        n = hi - lo + (hi - lo) % 2
        ws = min(lo, rows - n)
        assert ws <= lo and hi <= ws + n <= rows and n <= NA_KROWS
        windows.append((ws, n))
    return windows


def _na_kernel(q_ref, k_ref, v_ref, kc_ref, vc_ref, r_ref, o_ref, bias_ref, tab_ref, *, windows):
    lane = lax.broadcasted_iota(jnp.int32, (1, LANES), 1)
    low = lane < HEAD_DIM

    @pl.when(pl.program_id(1) == 0)
    def _():
        qcol = lax.broadcasted_iota(jnp.int32, (GRID_W, LANES), 0)
        kcol = lax.broadcasted_iota(jnp.int32, (GRID_W, LANES), 1) & (GRID_W - 1)
        start = jnp.clip(qcol - NA_COLS // 2, 0, GRID_W - NA_COLS)
        inside = (kcol >= start) & (kcol < start + NA_COLS)
        for hh in range(LANES // HEAD_DIM):
            for d in range(2 * NA_ROWS - 1):
                base = jnp.broadcast_to(r_ref[hh, d:d + 1, :], (GRID_W, LANES))
                lo_t = pltpu.roll(base, LANES - (NA_COLS - 1), 1, stride=1, stride_axis=0)
                hi_t = pltpu.roll(base, GRID_W - (NA_COLS - 1), 1, stride=1, stride_axis=0)
                tab_ref[hh, d] = jnp.where(inside, jnp.where(low, lo_t, hi_t) * LOG2E, NEG)
        neg = jnp.full((GRID_W, LANES), NEG, F32)
        for hh in range(LANES // HEAD_DIM):
            for qb, (ws, nrows) in enumerate(windows):
                for i in range(NA_QROWS):
                    r = qb * NA_QROWS + i
                    rs = _na_row_start(r)
                    for jp in range(nrows // 2):
                        kr = ws + 2 * jp
                        ok = [rs <= kr + d < rs + NA_ROWS for d in (0, 1)]
                        if not any(ok):
                            blk = neg
                        else:
                            t0 = tab_ref[hh, kr - r + NA_ROWS - 1] if ok[0] else neg
                            t1 = tab_ref[hh, kr + 1 - r + NA_ROWS - 1] if ok[1] else neg
                            blk = jnp.where(low, t0, t1)
                        bias_ref[hh, qb, i * GRID_W:(i + 1) * GRID_W,
                                 jp * LANES:(jp + 1) * LANES] = blk

    masks = [low, jnp.logical_not(low)]
    n_batch = kc_ref.shape[0]
    kc = [kc_ref[bb].astype(BF16) for bb in range(n_batch)]
    vch = []
    for bb in range(n_batch):
        vc = vc_ref[bb].astype(BF16)
        vch.append([jnp.where(mh, vc, jnp.zeros_like(vc)) for mh in masks])
    nt = (((1,), (1,)), ((), ()))
    nq = NA_QROWS * GRID_W
    units = [(bb, qb, hh) for bb in range(n_batch) for qb in range(len(windows))
             for hh in range(len(masks))]

    def scores(unit):
        bb, qb, hh = unit
        ws, nrows = windows[qb]
        nk = nrows * GRID_W
        k0 = bb * SEG + ws * GRID_W
        q = q_ref[bb * SEG + qb * nq:bb * SEG + (qb + 1) * nq, :]
        kw = k_ref[k0:k0 + nk, :]
        vw = v_ref[k0:k0 + nk, :]
        qm = jnp.where(masks[hh], q, jnp.zeros_like(q))
        s_win = (lax.dot_general(qm, kw, nt, preferred_element_type=F32)
                 + bias_ref[hh, qb, :, 0:nk])
        s_ctx = lax.dot_general(qm, kc[bb], nt, preferred_element_type=F32)
        return [(s_win, jnp.where(masks[hh], vw, jnp.zeros_like(vw))), (s_ctx, vch[bb][hh])]

    partial = {}

    def store(unit, o):
        bb, qb, hh = unit
        if hh == 0:
            partial[bb, qb] = o
        else:
            r0 = bb * SEG + qb * nq
            o_ref[r0:r0 + nq, :] = (partial.pop((bb, qb)) + o).astype(BF16)

    _pipelined_units(units, scores, store)


def _na_attention(q, k, v, kc, vc, o_idx, rpb, *, n_batch, row0):
    bps = NA_BATCHES_PER_STEP
    rows = bps * SEG
    assert row0 % rows == 0 and n_batch % bps == 0
    blk0 = row0 // rows
    n_pairs = q.shape[1] // LANES
    sc = kc.shape[2]
    windows = _na_windows()
    heads = LANES // HEAD_DIM
    nr = 2 * NA_ROWS - 1
    qkv = pl.BlockSpec((rows, LANES), lambda p, b: (blk0 + b, p))
    cspec = pl.BlockSpec((bps, None, sc, LANES), lambda p, b: (b, o_idx, 0, p))
    rspec = pl.BlockSpec((None, heads, nr, LANES), lambda p, b: (o_idx, p, 0, 0))
    return pl.pallas_call(
        functools.partial(_na_kernel, windows=windows),
        grid=(n_pairs, n_batch // bps),
        in_specs=[qkv, qkv, qkv, cspec, cspec, rspec],
        out_specs=pl.BlockSpec((rows, LANES), lambda p, b: (b, p)),
        out_shape=jax.ShapeDtypeStruct((n_batch * SEG, n_pairs * LANES), BF16),
        scratch_shapes=[pltpu.VMEM((heads, len(windows), NA_QROWS * GRID_W, NA_KROWS * GRID_W), F32),
                        pltpu.VMEM((heads, nr, GRID_W, LANES), F32)],
        compiler_params=_cparams(2),
        name="na_attention",
    )(q, k, v, kc, vc, rpb)


def _out_kernel(*refs, n_prompt_tiles, split_x, has_a):
    refs = list(refs)
    x = _pick(n_prompt_tiles, refs.pop(0), refs.pop(0)) if split_x else refs.pop(0)[...]
    gt_ref, g_ref, fsh_ref, fsc_ref, gf_ref, w_ref = refs[:6]
    a_ref = refs[6] if has_a else None
    bp_ref, bs_ref, o_ref, h_ref, wbf_ref = refs[6 + has_a:]

    @pl.when(pl.program_id(0) == 0)
    def _():
        wbf_ref[...] = w_ref[...].astype(BF16)

    b = _pick(n_prompt_tiles, bp_ref, bs_ref)
    if has_a:
        half = a_ref.shape[1]
        y = (jnp.dot(a_ref[...], wbf_ref[0:half, :], preferred_element_type=F32)
             + jnp.dot(b, wbf_ref[half:2 * half, :], preferred_element_type=F32))
    else:
        y = jnp.dot(b, wbf_ref[...], preferred_element_type=F32)
    x1 = _gated_residual(x, y, g_ref[...], gt_ref[...])
    o_ref[...] = x1
    h_ref[...] = _mod_norm(x1, gf_ref[...], fsh_ref[...], fsc_ref[...]).astype(BF16)


def _out_proj(xs, a, bp, bs, mods, layer, li, T, g_post, g_ffn_pre, w_out):
    tm = TM_OUT
    n_prompt_tiles = T // 2 // tm
    split_x = len(xs) == 2
    has_a = a is not None
    row = lambda w: pl.BlockSpec((tm, w), lambda i: (i, 0))
    x_specs = _split_specs(tm, D, n_prompt_tiles) if split_x else [row(D)]
    a_specs = [row(a.shape[1])] if has_a else []
    return pl.pallas_call(
        functools.partial(_out_kernel, n_prompt_tiles=n_prompt_tiles, split_x=split_x, has_a=has_a),
        grid=(T // tm,),
        in_specs=x_specs + [_mod_spec(layer, 2, tm), _const_spec((1, D), layer),
                            _mod_spec(layer, 3, tm), _mod_spec(layer, 4, tm),
                            _const_spec((1, D), layer), _weight_spec((D, D), li)] + a_specs
                 + _split_specs(tm, bp.shape[1], n_prompt_tiles),
        out_specs=[row(D), row(D)],
        out_shape=[jax.ShapeDtypeStruct((T, D), F32), jax.ShapeDtypeStruct((T, D), BF16)],
        scratch_shapes=[pltpu.VMEM((D, D), BF16)],
        compiler_params=_cparams(1),
        name="out_proj",
    )(*xs, mods, g_post, mods, mods, g_ffn_pre, w_out, *([a] if has_a else []), bp, bs)


def _ffn_kernel(x_ref, h_ref, gt_ref, gpost_ref, win_ref, wout_ref, *rest,
                layer, n_prompt_tiles, split_out):
    out_refs = rest[:-9]
    wg_buf, wu_buf, wo_buf, wg_res, wu_res, wo_res, act_buf, acc_ref, sem = rest[-9:]
    nf, _, tf = wg_res.shape
    first_tile = pl.program_id(0) == 0

    def aligned(f):
        return f * tf if isinstance(f, int) else pl.multiple_of(f * tf, tf)

    def in_copies(f, slot):
        col = aligned(f)
        return (pltpu.make_async_copy(win_ref.at[layer, :, pl.ds(col, tf)], wg_buf.at[slot],
                                      sem.at[0, slot]),
                pltpu.make_async_copy(win_ref.at[layer, :, pl.ds(D_FF + col, tf)], wu_buf.at[slot],
                                      sem.at[1, slot]))

    def out_copy(f, slot):
        row = aligned(f)
        return pltpu.make_async_copy(wout_ref.at[layer, pl.ds(row, tf), :], wo_buf.at[slot],
                                     sem.at[2, slot])

    def fetch(f, slot):
        for c in in_copies(f, slot):
            c.wait()
        out_copy(f, slot).wait()

        nxt = min(f + 1, nf - 1) if isinstance(f, int) else jnp.minimum(f + 1, nf - 1)

        @pl.when(jnp.asarray(f + 1 < nf))
        def _():
            for c in in_copies(nxt, 1 - slot):
                c.start()
            out_copy(nxt, 1 - slot).start()

        wg_res[f] = wg_buf[slot].astype(BF16)
        wu_res[f] = wu_buf[slot].astype(BF16)
        wo_res[f] = wo_buf[slot].astype(BF16)

    def hidden(f, slot):
        h = h_ref[...]
        g = jnp.dot(h, wg_res[f], preferred_element_type=F32)
        u = jnp.dot(h, wu_res[f], preferred_element_type=F32)
        act_buf[slot] = (g * jax.nn.sigmoid(g) * u).astype(BF16)

    def project(f, slot):
        acc_ref[...] += jnp.dot(act_buf[slot], wo_res[f], preferred_element_type=F32)

    assert nf % 2 == 1

    def chunks(streaming):
        acc_ref[...] = jnp.zeros_like(acc_ref)
        if streaming:
            fetch(0, 0)
        hidden(0, 0)

        def pair(k, carry):
            for f, slot in ((2 * k + 1, 1), (2 * k + 2, 0)):
                if streaming:
                    fetch(f, slot)
                hidden(f, slot)
                project(f - 1, 1 - slot)
            return carry

        lax.fori_loop(0, (nf - 1) // 2, pair, 0)
        project(nf - 1, 0)

    @pl.when(first_tile)
    def _():
        for c in in_copies(0, 0):
            c.start()
        out_copy(0, 0).start()
        chunks(streaming=True)

    @pl.when(jnp.logical_not(first_tile))
    def _():
        chunks(streaming=False)

    def result():
        return _gated_residual(x_ref[...], acc_ref[...], gpost_ref[...], gt_ref[...])

    if split_out:
        i = pl.program_id(0)

        @pl.when(i < n_prompt_tiles)
        def _():
            out_refs[0][...] = result()

        @pl.when(i >= n_prompt_tiles)
        def _():
            out_refs[1][...] = result()
    else:
        out_refs[0][...] = result()


def _ffn(x, h, mods, layer, g_post, w_in, w_out, split_out):
    T = x.shape[0]
    tm, tf = TM_FFN, TF_FFN
    nf = D_FF // tf
    n_prompt_tiles = T // 2 // tm
    xrow = pl.BlockSpec((tm, D), lambda i: (i, 0))
    if split_out:
        out_specs = _split_specs(tm, D, n_prompt_tiles)
        out_shape = [jax.ShapeDtypeStruct((T // 2, D), F32)] * 2
    else:
        out_specs = [xrow]
        out_shape = [jax.ShapeDtypeStruct((T, D), F32)]
    hbm = pl.BlockSpec(memory_space=pl.ANY)
    return pl.pallas_call(
        functools.partial(_ffn_kernel, layer=layer, n_prompt_tiles=n_prompt_tiles,
                          split_out=split_out),
        grid=(T // tm,),
        in_specs=[xrow, xrow, _mod_spec(layer, 5, tm), _const_spec((1, D), layer), hbm, hbm],
        out_specs=out_specs,
        out_shape=out_shape,
        scratch_shapes=[pltpu.VMEM((2, D, tf), F32), pltpu.VMEM((2, D, tf), F32),
                        pltpu.VMEM((2, tf, D), F32), pltpu.VMEM((nf, D, tf), BF16),
                        pltpu.VMEM((nf, D, tf), BF16), pltpu.VMEM((nf, tf, D), BF16),
                        pltpu.VMEM((2, tm, tf), BF16), pltpu.VMEM((tm, D), F32),
                        pltpu.SemaphoreType.DMA((3, 2))],
        compiler_params=_cparams(1, VMEM_LIMIT_FFN),
        name="ffn",
    )(x, h, mods, g_post, w_in, w_out)


def _rope_tables():
    t = jnp.arange(SEG)
    nf = HEAD_DIM // 4
    freqs = ROPE_BASE ** (-jnp.arange(nf, dtype=F32) / nf)

    def cs(pos):
        ang = pos.astype(F32)[:, None] * freqs[None, :]
        return jnp.cos(ang), jnp.sin(ang)

    cr, sr = cs(t // GRID_W)
    cc, sn = cs(t % GRID_W)
    cos = jnp.concatenate([cr, cr, cc, cc], axis=1)
    sin = jnp.concatenate([-sr, sr, -sn, sn], axis=1)
    reps = LANES // HEAD_DIM
    cos = jnp.tile(cos, (1, reps))
    sin = jnp.tile(sin, (1, reps))
    return (jnp.stack([jnp.ones_like(cos), cos]), jnp.stack([jnp.zeros_like(sin), sin]))


def _block_diag_ones(n):
    i = jnp.arange(n) // HEAD_DIM
    return (i[:, None] == i[None, :]).astype(BF16)


def kernel(x_prompt, x_sample, cache_attn_k, cache_attn_v, cache_na_k, cache_na_v, c, c_ctx,
           mod_w, mod_b, norm_mix_pre, norm_mix_post, norm_ffn_pre, norm_ffn_post,
           even_w_in, even_w_out, sgu_w, sgu_b, sgu_norm, q_norm, k_norm,
           odd_w_in, odd_w_out, na_rpb, ffn_w_in, ffn_w_out):
    nb_p, seq_p, _ = x_prompt.shape
    nb_s, seq_s, _ = x_sample.shape
    tp = nb_p * seq_p
    T = 2 * tp
    assert seq_s == SEG and tp == nb_s * seq_s and tp % SEG == 0

    cond = jnp.concatenate([jnp.broadcast_to(c_ctx[None, :], (tp // SEG, D)), c], axis=0)
    mods = _modulation(cond, mod_w, mod_b)

    cos, sin = _rope_tables()
    qw = B_HEADS * HEAD_DIM
    bdq = _block_diag_ones(qw)
    bdk = _block_diag_ones(LANES)
    past = cache_attn_k.shape[2]
    ctx_ak = cache_attn_k.reshape(nb_s, -1, past, B_KV_HEADS * HEAD_DIM)
    ctx_av = cache_attn_v.reshape(nb_s, -1, past, B_KV_HEADS * HEAD_DIM)
    ctx_nk = cache_na_k.reshape(nb_s, -1, past, C_HEADS * HEAD_DIM)
    ctx_nv = cache_na_v.reshape(nb_s, -1, past, C_HEADS * HEAD_DIM)
    g_mix_pre, g_mix_post, g_ffn_pre, g_ffn_post = (
        a.reshape(DEPTH, 1, D) for a in (norm_mix_pre, norm_mix_post, norm_ffn_pre, norm_ffn_post))
    n_even = even_w_in.shape[0]
    sgu_b3 = sgu_b.reshape(n_even, A_GROUPS, CHUNK, 1)
    sgu_n3 = sgu_norm.reshape(n_even, 1, A_WIDTH)
    qn3 = jnp.tile(q_norm, (1, B_HEADS)).reshape(n_even, 1, qw)
    kn3 = jnp.tile(k_norm, (1, B_KV_HEADS)).reshape(n_even, 1, LANES)
    rpb = jnp.pad(na_rpb, ((0, 0), (0, 0), (0, 0), (0, LANES - na_rpb.shape[3])))

    xs = (x_prompt.reshape(tp, D), x_sample.reshape(nb_s * seq_s, D))
    n_odd = odd_w_in.shape[0]
    attn_k, attn_v, na_cache = [], [], ()
    for l in range(DEPTH):
        if l % 2 == 0:
            e = l // 2
            a_out, q, kd, vd, kf, vf = _even_in(xs, mods, l, e, T, g_mix_pre, even_w_in, sgu_w,
                                                sgu_b3, sgu_n3, qn3, kn3, bdq, bdk, cos, sin)
            n_pairs = qw // LANES
            pairs_per_kv = n_pairs // B_KV_HEADS
            mix_p = _attention(q, kd, vd, n_batch=nb_p, seq=seq_p, row0=0, pairs_per_step=n_pairs,
                               pairs_per_kv=pairs_per_kv, batches_per_step=PROMPT_BATCHES_PER_STEP)
            mix_s = _attention(q, kd, vd, n_batch=nb_s, seq=seq_s, row0=tp,
                               pairs_per_step=pairs_per_kv, pairs_per_kv=pairs_per_kv,
                               ctx=(ctx_ak, ctx_av, e))
            attn_k.append(kf[:tp].reshape(nb_p, seq_p, B_KV_HEADS, HEAD_DIM))
            attn_v.append(vf[:tp].reshape(nb_p, seq_p, B_KV_HEADS, HEAD_DIM))
            w_out, li = even_w_out, e
        else:
            o = l // 2
            a_out = None
            q, k, v, *na_cache = _odd_in(xs[0], mods, l, o, n_odd, seq_p, g_mix_pre, odd_w_in,
                                         tuple(na_cache))
            mix_p = _attention(q, k, v, n_batch=nb_p, seq=seq_p, row0=0,
                               pairs_per_step=PAIRS_PER_STEP, pairs_per_kv=1,
                               batches_per_step=PROMPT_BATCHES_PER_STEP)
            mix_s = _na_attention(q, k, v, ctx_nk, ctx_nv, o, rpb, n_batch=nb_s, row0=tp)
            w_out, li = odd_w_out, o
        x, h = _out_proj(xs, a_out, mix_p, mix_s, mods, l, li, T, g_mix_post, g_ffn_pre, w_out)
        xs = tuple(_ffn(x, h, mods, l, g_ffn_post, ffn_w_in, ffn_w_out,
                        split_out=(l == DEPTH - 1)))

    y_prompt = xs[0].reshape(nb_p, seq_p, D)
    y_sample = xs[1].reshape(nb_s, seq_s, D)
    new_na_k, new_na_v = (a.reshape(nb_p, n_odd, seq_p, C_HEADS, HEAD_DIM) for a in na_cache)
    return (y_prompt, y_sample, jnp.stack(attn_k, axis=1), jnp.stack(attn_v, axis=1),
            new_na_k, new_na_v)
```

```python
import functools

import jax
import jax.numpy as jnp
from jax import lax
from jax.experimental import pallas as pl
from jax.experimental.pallas import tpu as pltpu

F32 = jnp.float32
BF16 = jnp.bfloat16

D = 1024
DEPTH = 4
HEAD_DIM = 64
GRID_W = 64
CHUNK = 128
A_WIDTH = D // 2
A_GROUPS = 4
B_HEADS = 8
B_KV_HEADS = 2
C_HEADS = 16
NA_ROWS = 8
NA_COLS = 16
D_FF = 2816
ROPE_BASE = 10000.0
EPS = 1e-6
NEG = -1e30
SEG = 1024
N_SEG = 8
LANES = 128
LOG2E = 1.4426950408889634
QSCALE = HEAD_DIM ** -0.5 * LOG2E

TM_PROJ = 512
TM_OUT = 1024
TM_FFN = 1024
TF_FFN = 256
IN_COPY_SPLIT = 4
TQ_ATTN = 256
PAIRS_PER_STEP = 4
PROMPT_BATCHES_PER_STEP = 4
NA_QROWS = 4
NA_BATCHES_PER_STEP = 2
NA_KROWS = 12
VMEM_LIMIT = 56 * 1024 * 1024
VMEM_LIMIT_FFN = 62 * 1024 * 1024


def _cparams(n_axes, vmem_limit=VMEM_LIMIT):
    return pltpu.CompilerParams(dimension_semantics=("arbitrary",) * n_axes,
                                vmem_limit_bytes=vmem_limit)


def _rms(x):
    return x * lax.rsqrt(jnp.mean(x * x, axis=-1, keepdims=True) + EPS)


def _mod_norm(x, g, shift, scale):
    return _rms(x) * (g * (1.0 + scale)) + shift


def _gated_residual(x, y, g, gate):
    return x + _rms(y) * (gate * g)


def _const_spec(shape, *lead):
    block = (None,) * len(lead) + tuple(shape)
    return pl.BlockSpec(block, lambda *_: tuple(lead) + (0,) * len(shape))


def _weight_spec(shape, *lead):
    block = (None,) * len(lead) + tuple(shape)
    return pl.BlockSpec(block, lambda *_: tuple(lead) + (0,) * len(shape),
                        pipeline_mode=pl.Buffered(1))


def _same_tile(i):
    return i


def _mod_spec(layer, j, tm, tile=_same_tile):
    return pl.BlockSpec((None, None, None, 1, D),
                        lambda i, *_: (layer, j, (tile(i) * tm) // SEG, 0, 0))


def _split_specs(tm, width, n_prompt_tiles, col=0, tile=_same_tile):
    return [pl.BlockSpec((tm, width),
                         lambda i, *_: (jnp.minimum(tile(i), n_prompt_tiles - 1), col)),
            pl.BlockSpec((tm, width),
                         lambda i, *_: (jnp.maximum(tile(i) - n_prompt_tiles, 0), col))]


def _pick(n_prompt_tiles, p_ref, s_ref, tile=None):
    tile = pl.program_id(0) if tile is None else tile
    return jnp.where(tile < n_prompt_tiles, p_ref[...], s_ref[...])


def _mod_kernel(cond_ref, w_ref, b_ref, o_ref):
    s = cond_ref[...]
    s = s * jax.nn.sigmoid(s)
    o_ref[...] = jnp.dot(s.astype(BF16), w_ref[...].astype(BF16),
                         preferred_element_type=F32) + b_ref[...]


def _modulation(cond, mod_w, mod_b):
    b = mod_b.reshape(DEPTH, 6, 1, D)
    out = pl.pallas_call(
        _mod_kernel,
        grid=(DEPTH, 6),
        in_specs=[pl.BlockSpec((N_SEG, D), lambda l, j: (0, 0)),
                  pl.BlockSpec((None, D, D), lambda l, j: (l, 0, j)),
                  pl.BlockSpec((None, None, 1, D), lambda l, j: (l, j, 0, 0))],
        out_specs=pl.BlockSpec((None, None, N_SEG, D), lambda l, j: (l, j, 0, 0)),
        out_shape=jax.ShapeDtypeStruct((DEPTH, 6, N_SEG, D), F32),
        compiler_params=_cparams(2),
        name="modulation",
    )(cond, mod_w, b)
    return out.reshape(DEPTH, 6, N_SEG, 1, D)


def _head_sumsq(y, bd_ref):
    sq = y * y
    hi = sq.astype(BF16)
    lo = (sq - hi.astype(F32)).astype(BF16)
    bd = bd_ref[...]
    return (jnp.dot(hi, bd, preferred_element_type=F32)
            + jnp.dot(lo, bd, preferred_element_type=F32))


def _rope(y, cos, sin):
    lane = lax.broadcasted_iota(jnp.int32, (1, LANES), 1)
    first = (lane & 16) == 0
    partner = jnp.where(first, pltpu.roll(y, LANES - 16, 1), pltpu.roll(y, 16, 1))
    return y * cos + partner * sin


def _even_in_kernel(*refs, n_tiles, n_prompt_tiles, split_x):
    refs = list(refs)
    i = pl.program_id(0)
    t_proj = jnp.minimum(i, n_tiles - 1)
    x = _pick(n_prompt_tiles, refs.pop(0), refs.pop(0), t_proj) if split_x else refs.pop(0)[...]
    (sh_ref, sc_ref, g_ref, w_ref, sguw_ref, sgub_ref, sgun_ref, qn_ref, kn_ref, bdq_ref, bdk_ref,
     cos_ref, sin_ref, a_ref, q_ref, kd_ref, vd_ref, kf_ref, vf_ref, wbf_ref, h_ref, z0_ref,
     z1_ref) = refs
    tm = a_ref.shape[0]
    qw = B_HEADS * HEAD_DIM
    kw = B_KV_HEADS * HEAD_DIM
    c_q = 2 * A_WIDTH
    c_k = c_q + qw
    n_chunks = tm // CHUNK
    gch = A_WIDTH // A_GROUPS
    lane = lax.broadcasted_iota(jnp.int32, (1, LANES), 1)
    low = lane < HEAD_DIM

    @pl.when(i == 0)
    def _():
        wbf_ref[...] = w_ref[...].astype(BF16)
        z1_ref[...] = jnp.zeros_like(z1_ref)

    h_ref[...] = _mod_norm(x, g_ref[...], sh_ref[...], sc_ref[...]).astype(BF16)

    def step(z_new_ref, z_ref):
        def project(c0, c1):
            z_new_ref[:, c0:c1] = jnp.dot(h_ref[...], wbf_ref[:, c0:c1], preferred_element_type=F32)

        project(0, A_WIDTH)
        u = jax.nn.gelu(z_ref[:, 0:A_WIDTH], approximate=True)
        project(A_WIDTH, c_q)
        v = jax.nn.gelu(z_ref[:, A_WIDTH:c_q], approximate=True)
        mu = jnp.mean(v, axis=-1, keepdims=True)
        vc = v - mu
        var = jnp.mean(vc * vc, axis=-1, keepdims=True)
        vn = (vc * lax.rsqrt(var + EPS) * sgun_ref[...]).astype(BF16)
        for g in range(A_GROUPS):
            rhs = jnp.concatenate([vn[n * CHUNK:(n + 1) * CHUNK, g * gch:(g + 1) * gch]
                                   for n in range(n_chunks)], axis=1)
            mixed = jnp.dot(sguw_ref[g].astype(BF16), rhs, preferred_element_type=F32)
            bias = sgub_ref[g]
            for n in range(n_chunks):
                blk = (mixed[:, n * gch:(n + 1) * gch] + bias) * u[n * CHUNK:(n + 1) * CHUNK,
                                                                  g * gch:(g + 1) * gch]
                a_ref[n * CHUNK:(n + 1) * CHUNK, g * gch:(g + 1) * gch] = blk.astype(BF16)

        project(c_q, c_k)
        q = z_ref[:, c_q:c_k]
        q = q * lax.rsqrt(_head_sumsq(q, bdq_ref) * (1.0 / HEAD_DIM) + EPS) * qn_ref[...]
        cos = cos_ref[...]
        sin = sin_ref[...]
        for j in range(qw // LANES):
            qj = _rope(q[:, j * LANES:(j + 1) * LANES], cos, sin) * QSCALE
            q_ref[:, j * LANES:(j + 1) * LANES] = qj.astype(BF16)

        project(c_k, c_k + 2 * kw)
        k = z_ref[:, c_k:c_k + kw]
        k = k * lax.rsqrt(_head_sumsq(k, bdk_ref) * (1.0 / HEAD_DIM) + EPS) * kn_ref[...]
        kf_ref[...] = k
        vals = z_ref[:, c_k + kw:c_k + 2 * kw]
        vf_ref[...] = vals
        for src, dst in ((_rope(k, cos, sin), kd_ref), (vals, vd_ref)):
            sw = pltpu.roll(src, HEAD_DIM, 1)
            dst[:, 0:LANES] = jnp.where(low, src, sw).astype(BF16)
            dst[:, LANES:2 * LANES] = jnp.where(low, sw, src).astype(BF16)

    @pl.when(i % 2 == 0)
    def _():
        step(z0_ref, z1_ref)

    @pl.when(i % 2 == 1)
    def _():
        step(z1_ref, z0_ref)


def _even_in(xs, mods, layer, e, T, g_pre, w_in, sgu_w, sgu_b, sgu_norm, q_norm, k_norm,
             bdq, bdk, cos, sin):
    tm = TM_PROJ
    n_in = w_in.shape[2]
    n_tiles = T // tm
    tiles_per_seg = SEG // tm
    n_prompt_tiles = n_tiles // 2
    split_x = len(xs) == 2
    proj = lambda i: jnp.minimum(i, n_tiles - 1)
    post = lambda i: jnp.maximum(i - 1, 0)
    tab_spec = pl.BlockSpec((None, tm, LANES),
                            lambda i: (post(i) // n_prompt_tiles, post(i) % tiles_per_seg, 0))
    row = lambda w, tile: pl.BlockSpec((tm, w), lambda i: (tile(i), 0))
    x_specs = (_split_specs(tm, D, n_prompt_tiles, tile=proj) if split_x else [row(D, proj)])
    qw = B_HEADS * HEAD_DIM
    return pl.pallas_call(
        functools.partial(_even_in_kernel, n_tiles=n_tiles, n_prompt_tiles=n_prompt_tiles,
                          split_x=split_x),
        grid=(n_tiles + 1,),
        in_specs=x_specs + [
            _mod_spec(layer, 0, tm, proj), _mod_spec(layer, 1, tm, proj),
            _const_spec((1, D), layer),
            _weight_spec((D, n_in), e), _const_spec((A_GROUPS, CHUNK, CHUNK), e),
            _const_spec((A_GROUPS, CHUNK, 1), e), _const_spec((1, A_WIDTH), e),
            _const_spec((1, qw), e), _const_spec((1, LANES), e),
            _const_spec((qw, qw)), _const_spec((LANES, LANES)), tab_spec, tab_spec],
        out_specs=[row(A_WIDTH, post), row(qw, post), row(2 * LANES, post), row(2 * LANES, post),
                   row(LANES, post), row(LANES, post)],
        out_shape=[jax.ShapeDtypeStruct((T, A_WIDTH), BF16),
                   jax.ShapeDtypeStruct((T, qw), BF16),
                   jax.ShapeDtypeStruct((T, 2 * LANES), BF16),
                   jax.ShapeDtypeStruct((T, 2 * LANES), BF16),
                   jax.ShapeDtypeStruct((T, LANES), F32),
                   jax.ShapeDtypeStruct((T, LANES), F32)],
        scratch_shapes=[pltpu.VMEM((D, n_in), BF16), pltpu.VMEM((tm, D), BF16),
                        pltpu.VMEM((tm, n_in), F32), pltpu.VMEM((tm, n_in), F32)],
        compiler_params=_cparams(1),
        name="even_in_proj",
    )(*xs, mods, mods, g_pre, w_in, sgu_w, sgu_b, sgu_norm, q_norm, k_norm, bdq, bdk, cos, sin)


def _odd_in_kernel(x_ref, sh_ref, sc_ref, g_ref, w_ref, *rest, n_prompt_tiles, slot, all_slots):
    q_ref, k_ref, v_ref, kf_ref, vf_ref, wbf_ref = rest[-6:]
    i = pl.program_id(0)
    hw = C_HEADS * HEAD_DIM

    @pl.when(i == 0)
    def _():
        for c in range(3):
            wbf_ref[:, c * hw:(c + 1) * hw] = w_ref[:, c * hw:(c + 1) * hw].astype(BF16)

    h = _mod_norm(x_ref[...], g_ref[...], sh_ref[...], sc_ref[...]).astype(BF16)
    q = jnp.dot(h, wbf_ref[:, 0:hw], preferred_element_type=F32)
    q_ref[...] = (q * QSCALE).astype(BF16)
    k = jnp.dot(h, wbf_ref[:, hw:2 * hw], preferred_element_type=F32)
    k_ref[...] = k.astype(BF16)
    v = jnp.dot(h, wbf_ref[:, 2 * hw:3 * hw], preferred_element_type=F32)
    v_ref[...] = v.astype(BF16)

    @pl.when(i < n_prompt_tiles)
    def _():
        for src, dst in ((k, kf_ref), (v, vf_ref)):
            if all_slots:
                val = src.reshape((dst.shape[0],) + dst.shape[2:])
                for s in range(dst.shape[1]):
                    dst[:, s] = val if s == slot else jnp.zeros_like(val)
            else:
                dst[...] = src.reshape(dst.shape)


def _odd_in(x, mods, layer, o, n_odd, seq_p, g_pre, w_in, caches):
    T = x.shape[0]
    tm = TM_PROJ
    hw = C_HEADS * HEAD_DIM
    n_prompt_tiles = T // 2 // tm
    bt = tm // seq_p
    row = pl.BlockSpec((tm, hw), lambda i: (i, 0))
    all_slots = not caches
    if all_slots:
        crow = pl.BlockSpec((bt, n_odd, seq_p, hw),
                            lambda i: (jnp.minimum(i, n_prompt_tiles - 1), 0, 0, 0))
    else:
        crow = pl.BlockSpec((bt, None, seq_p, hw),
                            lambda i: (jnp.minimum(i, n_prompt_tiles - 1), o, 0, 0))
    cshape = jax.ShapeDtypeStruct((T // 2 // seq_p, n_odd, seq_p, hw), F32)
    n_in = 5
    return pl.pallas_call(
        functools.partial(_odd_in_kernel, n_prompt_tiles=n_prompt_tiles, slot=o,
                          all_slots=all_slots),
        grid=(T // tm,),
        in_specs=[pl.BlockSpec((tm, D), lambda i: (i, 0)), _mod_spec(layer, 0, tm),
                  _mod_spec(layer, 1, tm), _const_spec((1, D), layer),
                  _weight_spec((D, 3 * hw), o)]
                 + [pl.BlockSpec(memory_space=pl.ANY)] * len(caches),
        out_specs=[row, row, row, crow, crow],
        out_shape=[jax.ShapeDtypeStruct((T, hw), BF16)] * 3 + [cshape, cshape],
        input_output_aliases={n_in + j: 3 + j for j in range(len(caches))},
        scratch_shapes=[pltpu.VMEM((D, 3 * hw), BF16)],
        compiler_params=_cparams(1),
        name="odd_in_proj",
    )(x, mods, mods, g_pre, w_in, *caches)


def _softmax(pieces):
    m = None
    for s, _ in pieces:
        ms = jnp.max(s, axis=-1, keepdims=True)
        m = ms if m is None else jnp.maximum(m, ms)
    den = None
    probs = []
    for s, _ in pieces:
        p = jnp.exp2(s - m)
        ls = jnp.sum(p, axis=-1, keepdims=True)
        den = ls if den is None else den + ls
        probs.append(p.astype(BF16))
    return probs, [val for _, val in pieces], den


def _weighted_values(probs, vals, den):
    acc = None
    for p, val in zip(probs, vals):
        o = jnp.dot(p, val, preferred_element_type=F32)
        acc = o if acc is None else acc + o
    return acc / den


def _pipelined_units(units, scores, store):
    n = len(units)
    pending_scores = scores(units[0])
    pending_probs = None
    for idx in range(n + 1):
        upcoming = scores(units[idx + 1]) if idx + 1 < n else None
        probs = _softmax(pending_scores) if idx < n else None
        if pending_probs is not None:
            store(units[idx - 1], _weighted_values(*pending_probs))
        pending_scores, pending_probs = upcoming, probs


def _attn_kernel(*refs, has_ctx, seq, tq, pairs_per_kv):
    if has_ctx:
        q_ref, k_ref, v_ref, kc_ref, vc_ref, o_ref = refs
    else:
        q_ref, k_ref, v_ref, o_ref = refs
    lane = lax.broadcasted_iota(jnp.int32, (1, LANES), 1)
    masks = [lane < HEAD_DIM, lane >= HEAD_DIM]
    if has_ctx:
        kv_head = pl.program_id(1)
        sel = jnp.where(masks[0], 0, 1) == kv_head
        kc = jnp.where(sel, kc_ref[...], 0.0)
        kc = (kc + pltpu.roll(kc, HEAD_DIM, 1)).astype(BF16)
        vc = jnp.where(sel, vc_ref[...], 0.0)
        vc = (vc + pltpu.roll(vc, HEAD_DIM, 1)).astype(BF16)
        vch = [jnp.where(mh, vc, jnp.zeros_like(vc)) for mh in masks]
    nt = (((1,), (1,)), ((), ()))
    groups = [(r0, j) for r0 in range(0, q_ref.shape[0], seq) for j in range(q_ref.shape[1] // LANES)]
    units = [(r0, j, t, hh) for r0, j in groups for t in range(r0 // tq, (r0 + seq) // tq)
             for hh in range(len(masks))]
    operands = {}

    def group_operands(r0, j):
        if (r0, j) not in operands:
            kcols = slice((j // pairs_per_kv) * LANES, (j // pairs_per_kv + 1) * LANES)
            v = v_ref[r0:r0 + seq, kcols]
            operands[r0, j] = (k_ref[r0:r0 + seq, kcols],
                               [jnp.where(mh, v, jnp.zeros_like(v)) for mh in masks])
        return operands[r0, j]

    def scores(unit):
        r0, j, t, hh = unit
        k, vh = group_operands(r0, j)
        q = q_ref[t * tq:(t + 1) * tq, j * LANES:(j + 1) * LANES]
        qm = jnp.where(masks[hh], q, jnp.zeros_like(q))
        pieces = [(lax.dot_general(qm, k, nt, preferred_element_type=F32), vh[hh])]
        if has_ctx:
            pieces.append((lax.dot_general(qm, kc, nt, preferred_element_type=F32), vch[hh]))
        return pieces

    partial = {}

    def store(unit, o):
        r0, j, t, hh = unit
        if hh == 0:
            partial[r0, j, t] = o
        else:
            out = partial.pop((r0, j, t)) + o
            o_ref[t * tq:(t + 1) * tq, j * LANES:(j + 1) * LANES] = out.astype(BF16)

    _pipelined_units(units, scores, store)


def _attention(q, k, v, *, n_batch, seq, row0, pairs_per_step, pairs_per_kv, batches_per_step=1,
               ctx=None):
    rows = batches_per_step * seq
    assert row0 % rows == 0 and n_batch % batches_per_step == 0
    blk0 = row0 // rows
    n_groups = q.shape[1] // (pairs_per_step * LANES)
    qw = pairs_per_step * LANES
    kw = qw // pairs_per_kv
    in_specs = [pl.BlockSpec((rows, qw), lambda b, g: (blk0 + b, g)),
                pl.BlockSpec((rows, kw), lambda b, g: (blk0 + b, g)),
                pl.BlockSpec((rows, kw), lambda b, g: (blk0 + b, g))]
    args = [q, k, v]
    if ctx is not None:
        assert kw == LANES and batches_per_step == 1
        kc, vc, e = ctx
        sc = kc.shape[2]
        cspec = pl.BlockSpec((None, None, sc, LANES), lambda b, g: (b, e, 0, 0))
        in_specs += [cspec, cspec]
        args += [kc, vc]
    return pl.pallas_call(
        functools.partial(_attn_kernel, has_ctx=ctx is not None, seq=seq, tq=min(TQ_ATTN, seq),
                          pairs_per_kv=pairs_per_kv),
        grid=(n_batch // batches_per_step, n_groups),
        in_specs=in_specs,
        out_specs=pl.BlockSpec((rows, qw), lambda b, g: (b, g)),
        out_shape=jax.ShapeDtypeStruct((n_batch * seq, q.shape[1]), BF16),
        compiler_params=_cparams(2),
        name="attention",
    )(*args)


def _na_row_start(r):
    rows = SEG // GRID_W
    return min(max(r - NA_ROWS // 2, 0), rows - NA_ROWS)


def _na_windows():
    rows = SEG // GRID_W
    windows = []
    for qb in range(rows // NA_QROWS):
        lo = _na_row_start(qb * NA_QROWS)
        hi = _na_row_start(qb * NA_QROWS + NA_QROWS - 1) + NA_ROWS
        n = hi - lo + (hi - lo) % 2
        ws = min(lo, rows - n)
        assert ws <= lo and hi <= ws + n <= rows and n <= NA_KROWS
        windows.append((ws, n))
    return windows


def _na_kernel(q_ref, k_ref, v_ref, kc_ref, vc_ref, r_ref, o_ref, bias_ref, tab_ref, *, windows):
    lane = lax.broadcasted_iota(jnp.int32, (1, LANES), 1)
    low = lane < HEAD_DIM

    @pl.when(pl.program_id(1) == 0)
    def _():
        qcol = lax.broadcasted_iota(jnp.int32, (GRID_W, LANES), 0)
        kcol = lax.broadcasted_iota(jnp.int32, (GRID_W, LANES), 1) & (GRID_W - 1)
        start = jnp.clip(qcol - NA_COLS // 2, 0, GRID_W - NA_COLS)
        inside = (kcol >= start) & (kcol < start + NA_COLS)
        for hh in range(LANES // HEAD_DIM):
            for d in range(2 * NA_ROWS - 1):
                base = jnp.broadcast_to(r_ref[hh, d:d + 1, :], (GRID_W, LANES))
                lo_t = pltpu.roll(base, LANES - (NA_COLS - 1), 1, stride=1, stride_axis=0)
                hi_t = pltpu.roll(base, GRID_W - (NA_COLS - 1), 1, stride=1, stride_axis=0)
                tab_ref[hh, d] = jnp.where(inside, jnp.where(low, lo_t, hi_t) * LOG2E, NEG)
        neg = jnp.full((GRID_W, LANES), NEG, F32)
        for hh in range(LANES // HEAD_DIM):
            for qb, (ws, nrows) in enumerate(windows):
                for i in range(NA_QROWS):
                    r = qb * NA_QROWS + i
                    rs = _na_row_start(r)
                    for jp in range(nrows // 2):
                        kr = ws + 2 * jp
                        ok = [rs <= kr + d < rs + NA_ROWS for d in (0, 1)]
                        if not any(ok):
                            blk = neg
                        else:
                            t0 = tab_ref[hh, kr - r + NA_ROWS - 1] if ok[0] else neg
                            t1 = tab_ref[hh, kr + 1 - r + NA_ROWS - 1] if ok[1] else neg
                            blk = jnp.where(low, t0, t1)
                        bias_ref[hh, qb, i * GRID_W:(i + 1) * GRID_W,
                                 jp * LANES:(jp + 1) * LANES] = blk

    masks = [low, jnp.logical_not(low)]
    n_batch = kc_ref.shape[0]
    kc = [kc_ref[bb].astype(BF16) for bb in range(n_batch)]
    vch = []
    for bb in range(n_batch):
        vc = vc_ref[bb].astype(BF16)
        vch.append([jnp.where(mh, vc, jnp.zeros_like(vc)) for mh in masks])
    nt = (((1,), (1,)), ((), ()))
    nq = NA_QROWS * GRID_W
    units = [(bb, qb, hh) for bb in range(n_batch) for qb in range(len(windows))
             for hh in range(len(masks))]

    def scores(unit):
        bb, qb, hh = unit
        ws, nrows = windows[qb]
        nk = nrows * GRID_W
        k0 = bb * SEG + ws * GRID_W
        q = q_ref[bb * SEG + qb * nq:bb * SEG + (qb + 1) * nq, :]
        kw = k_ref[k0:k0 + nk, :]
        vw = v_ref[k0:k0 + nk, :]
        qm = jnp.where(masks[hh], q, jnp.zeros_like(q))
        s_win = (lax.dot_general(qm, kw, nt, preferred_element_type=F32)
                 + bias_ref[hh, qb, :, 0:nk])
        s_ctx = lax.dot_general(qm, kc[bb], nt, preferred_element_type=F32)
        return [(s_win, jnp.where(masks[hh], vw, jnp.zeros_like(vw))), (s_ctx, vch[bb][hh])]

    partial = {}

    def store(unit, o):
        bb, qb, hh = unit
        if hh == 0:
            partial[bb, qb] = o
        else:
            r0 = bb * SEG + qb * nq
            o_ref[r0:r0 + nq, :] = (partial.pop((bb, qb)) + o).astype(BF16)

    _pipelined_units(units, scores, store)


def _na_attention(q, k, v, kc, vc, o_idx, rpb, *, n_batch, row0):
    bps = NA_BATCHES_PER_STEP
    rows = bps * SEG
    assert row0 % rows == 0 and n_batch % bps == 0
    blk0 = row0 // rows
    n_pairs = q.shape[1] // LANES
    sc = kc.shape[2]
    windows = _na_windows()
    heads = LANES // HEAD_DIM
    nr = 2 * NA_ROWS - 1
    qkv = pl.BlockSpec((rows, LANES), lambda p, b: (blk0 + b, p))
    cspec = pl.BlockSpec((bps, None, sc, LANES), lambda p, b: (b, o_idx, 0, p))
    rspec = pl.BlockSpec((None, heads, nr, LANES), lambda p, b: (o_idx, p, 0, 0))
    return pl.pallas_call(
        functools.partial(_na_kernel, windows=windows),
        grid=(n_pairs, n_batch // bps),
        in_specs=[qkv, qkv, qkv, cspec, cspec, rspec],
        out_specs=pl.BlockSpec((rows, LANES), lambda p, b: (b, p)),
        out_shape=jax.ShapeDtypeStruct((n_batch * SEG, n_pairs * LANES), BF16),
        scratch_shapes=[pltpu.VMEM((heads, len(windows), NA_QROWS * GRID_W, NA_KROWS * GRID_W), F32),
                        pltpu.VMEM((heads, nr, GRID_W, LANES), F32)],
        compiler_params=_cparams(2),
        name="na_attention",
    )(q, k, v, kc, vc, rpb)


def _out_kernel(*refs, n_prompt_tiles, split_x, has_a):
    refs = list(refs)
    x = _pick(n_prompt_tiles, refs.pop(0), refs.pop(0)) if split_x else refs.pop(0)[...]
    gt_ref, g_ref, fsh_ref, fsc_ref, gf_ref, w_ref = refs[:6]
    a_ref = refs[6] if has_a else None
    bp_ref, bs_ref, o_ref, h_ref, wbf_ref = refs[6 + has_a:]

    @pl.when(pl.program_id(0) == 0)
    def _():
        wbf_ref[...] = w_ref[...].astype(BF16)

    b = _pick(n_prompt_tiles, bp_ref, bs_ref)
    if has_a:
        half = a_ref.shape[1]
        y = (jnp.dot(a_ref[...], wbf_ref[0:half, :], preferred_element_type=F32)
             + jnp.dot(b, wbf_ref[half:2 * half, :], preferred_element_type=F32))
    else:
        y = jnp.dot(b, wbf_ref[...], preferred_element_type=F32)
    x1 = _gated_residual(x, y, g_ref[...], gt_ref[...])
    o_ref[...] = x1
    h_ref[...] = _mod_norm(x1, gf_ref[...], fsh_ref[...], fsc_ref[...]).astype(BF16)


def _out_proj(xs, a, bp, bs, mods, layer, li, T, g_post, g_ffn_pre, w_out):
    tm = TM_OUT
    n_prompt_tiles = T // 2 // tm
    split_x = len(xs) == 2
    has_a = a is not None
    row = lambda w: pl.BlockSpec((tm, w), lambda i: (i, 0))
    x_specs = _split_specs(tm, D, n_prompt_tiles) if split_x else [row(D)]
    a_specs = [row(a.shape[1])] if has_a else []
    return pl.pallas_call(
        functools.partial(_out_kernel, n_prompt_tiles=n_prompt_tiles, split_x=split_x, has_a=has_a),
        grid=(T // tm,),
        in_specs=x_specs + [_mod_spec(layer, 2, tm), _const_spec((1, D), layer),
                            _mod_spec(layer, 3, tm), _mod_spec(layer, 4, tm),
                            _const_spec((1, D), layer), _weight_spec((D, D), li)] + a_specs
                 + _split_specs(tm, bp.shape[1], n_prompt_tiles),
        out_specs=[row(D), row(D)],
        out_shape=[jax.ShapeDtypeStruct((T, D), F32), jax.ShapeDtypeStruct((T, D), BF16)],
        scratch_shapes=[pltpu.VMEM((D, D), BF16)],
        compiler_params=_cparams(1),
        name="out_proj",
    )(*xs, mods, g_post, mods, mods, g_ffn_pre, w_out, *([a] if has_a else []), bp, bs)


def _ffn_kernel(x_ref, h_ref, gt_ref, gpost_ref, win_ref, wout_ref, *rest,
                layer, n_prompt_tiles, split_out):
    out_refs = rest[:-9]
    wg_buf, wu_buf, wo_buf, wg_res, wu_res, wo_res, act_buf, acc_ref, sem = rest[-9:]
    nf, _, tf = wg_res.shape
    first_tile = pl.program_id(0) == 0

    def aligned(f):
        return f * tf if isinstance(f, int) else pl.multiple_of(f * tf, tf)

    def in_copies(f, slot):
        col = aligned(f)
        copies = []
        for n, (buf, c0) in enumerate(((wg_buf, col), (wu_buf, D_FF + col))):
            for r in range(IN_COPY_SPLIT):
                rows = pl.ds(r * (D // IN_COPY_SPLIT), D // IN_COPY_SPLIT)
                copies.append(pltpu.make_async_copy(win_ref.at[layer, rows, pl.ds(c0, tf)],
                                                    buf.at[slot, rows, :],
                                                    sem.at[n * IN_COPY_SPLIT + r, slot]))
        return copies

    def out_copy(f, slot):
        row = aligned(f)
        return pltpu.make_async_copy(wout_ref.at[layer, pl.ds(row, tf), :], wo_buf.at[slot],
                                     sem.at[2 * IN_COPY_SPLIT, slot])

    def fetch(f, slot):
        for c in in_copies(f, slot):
            c.wait()
        out_copy(f, slot).wait()

        nxt = min(f + 1, nf - 1) if isinstance(f, int) else jnp.minimum(f + 1, nf - 1)

        @pl.when(jnp.asarray(f + 1 < nf))
        def _():
            for c in in_copies(nxt, 1 - slot):
                c.start()
            out_copy(nxt, 1 - slot).start()

        wg_res[f] = wg_buf[slot].astype(BF16)
        wu_res[f] = wu_buf[slot].astype(BF16)
        wo_res[f] = wo_buf[slot].astype(BF16)

    def hidden(f, slot):
        h = h_ref[...]
        g = jnp.dot(h, wg_res[f], preferred_element_type=F32)
        u = jnp.dot(h, wu_res[f], preferred_element_type=F32)
        act_buf[slot] = (g * jax.nn.sigmoid(g) * u).astype(BF16)

    def project(f, slot):
        acc_ref[...] += jnp.dot(act_buf[slot], wo_res[f], preferred_element_type=F32)

    assert nf % 2 == 1

    def chunks(streaming):
        acc_ref[...] = jnp.zeros_like(acc_ref)
        if streaming:
            fetch(0, 0)
        hidden(0, 0)

        def pair(k, carry):
            for f, slot in ((2 * k + 1, 1), (2 * k + 2, 0)):
                if streaming:
                    fetch(f, slot)
                hidden(f, slot)
                project(f - 1, 1 - slot)
            return carry

        lax.fori_loop(0, (nf - 1) // 2, pair, 0)
        project(nf - 1, 0)

    @pl.when(first_tile)
    def _():
        for c in in_copies(0, 0):
            c.start()
        out_copy(0, 0).start()
        chunks(streaming=True)

    @pl.when(jnp.logical_not(first_tile))
    def _():
        chunks(streaming=False)

    def result():
        return _gated_residual(x_ref[...], acc_ref[...], gpost_ref[...], gt_ref[...])

    if split_out:
        i = pl.program_id(0)

        @pl.when(i < n_prompt_tiles)
        def _():
            out_refs[0][...] = result()

        @pl.when(i >= n_prompt_tiles)
        def _():
            out_refs[1][...] = result()
    else:
        out_refs[0][...] = result()


def _ffn(x, h, mods, layer, g_post, w_in, w_out, split_out):
    T = x.shape[0]
    tm, tf = TM_FFN, TF_FFN
    nf = D_FF // tf
    n_prompt_tiles = T // 2 // tm
    xrow = pl.BlockSpec((tm, D), lambda i: (i, 0))
    if split_out:
        out_specs = _split_specs(tm, D, n_prompt_tiles)
        out_shape = [jax.ShapeDtypeStruct((T // 2, D), F32)] * 2
    else:
        out_specs = [xrow]
        out_shape = [jax.ShapeDtypeStruct((T, D), F32)]
    hbm = pl.BlockSpec(memory_space=pl.ANY)
    return pl.pallas_call(
        functools.partial(_ffn_kernel, layer=layer, n_prompt_tiles=n_prompt_tiles,
                          split_out=split_out),
        grid=(T // tm,),
        in_specs=[xrow, xrow, _mod_spec(layer, 5, tm), _const_spec((1, D), layer), hbm, hbm],
        out_specs=out_specs,
        out_shape=out_shape,
        scratch_shapes=[pltpu.VMEM((2, D, tf), F32), pltpu.VMEM((2, D, tf), F32),
                        pltpu.VMEM((2, tf, D), F32), pltpu.VMEM((nf, D, tf), BF16),
                        pltpu.VMEM((nf, D, tf), BF16), pltpu.VMEM((nf, tf, D), BF16),
                        pltpu.VMEM((2, tm, tf), BF16), pltpu.VMEM((tm, D), F32),
                        pltpu.SemaphoreType.DMA((2 * IN_COPY_SPLIT + 1, 2))],
        compiler_params=_cparams(1, VMEM_LIMIT_FFN),
        name="ffn",
    )(x, h, mods, g_post, w_in, w_out)


def _rope_tables():
    t = jnp.arange(SEG)
    nf = HEAD_DIM // 4
    freqs = ROPE_BASE ** (-jnp.arange(nf, dtype=F32) / nf)

    def cs(pos):
        ang = pos.astype(F32)[:, None] * freqs[None, :]
        return jnp.cos(ang), jnp.sin(ang)

    cr, sr = cs(t // GRID_W)
    cc, sn = cs(t % GRID_W)
    cos = jnp.concatenate([cr, cr, cc, cc], axis=1)
    sin = jnp.concatenate([-sr, sr, -sn, sn], axis=1)
    reps = LANES // HEAD_DIM
    cos = jnp.tile(cos, (1, reps))
    sin = jnp.tile(sin, (1, reps))
    return (jnp.stack([jnp.ones_like(cos), cos]), jnp.stack([jnp.zeros_like(sin), sin]))


def _block_diag_ones(n):
    i = jnp.arange(n) // HEAD_DIM
    return (i[:, None] == i[None, :]).astype(BF16)


def kernel(x_prompt, x_sample, cache_attn_k, cache_attn_v, cache_na_k, cache_na_v, c, c_ctx,
           mod_w, mod_b, norm_mix_pre, norm_mix_post, norm_ffn_pre, norm_ffn_post,
           even_w_in, even_w_out, sgu_w, sgu_b, sgu_norm, q_norm, k_norm,
           odd_w_in, odd_w_out, na_rpb, ffn_w_in, ffn_w_out):
    nb_p, seq_p, _ = x_prompt.shape
    nb_s, seq_s, _ = x_sample.shape
    tp = nb_p * seq_p
    T = 2 * tp
    assert seq_s == SEG and tp == nb_s * seq_s and tp % SEG == 0

    cond = jnp.concatenate([jnp.broadcast_to(c_ctx[None, :], (tp // SEG, D)), c], axis=0)
    mods = _modulation(cond, mod_w, mod_b)

    cos, sin = _rope_tables()
    qw = B_HEADS * HEAD_DIM
    bdq = _block_diag_ones(qw)
    bdk = _block_diag_ones(LANES)
    past = cache_attn_k.shape[2]
    ctx_ak = cache_attn_k.reshape(nb_s, -1, past, B_KV_HEADS * HEAD_DIM)
    ctx_av = cache_attn_v.reshape(nb_s, -1, past, B_KV_HEADS * HEAD_DIM)
    ctx_nk = cache_na_k.reshape(nb_s, -1, past, C_HEADS * HEAD_DIM)
    ctx_nv = cache_na_v.reshape(nb_s, -1, past, C_HEADS * HEAD_DIM)
    g_mix_pre, g_mix_post, g_ffn_pre, g_ffn_post = (
        a.reshape(DEPTH, 1, D) for a in (norm_mix_pre, norm_mix_post, norm_ffn_pre, norm_ffn_post))
    n_even = even_w_in.shape[0]
    sgu_b3 = sgu_b.reshape(n_even, A_GROUPS, CHUNK, 1)
    sgu_n3 = sgu_norm.reshape(n_even, 1, A_WIDTH)
    qn3 = jnp.tile(q_norm, (1, B_HEADS)).reshape(n_even, 1, qw)
    kn3 = jnp.tile(k_norm, (1, B_KV_HEADS)).reshape(n_even, 1, LANES)
    rpb = jnp.pad(na_rpb, ((0, 0), (0, 0), (0, 0), (0, LANES - na_rpb.shape[3])))

    xs = (x_prompt.reshape(tp, D), x_sample.reshape(nb_s * seq_s, D))
    n_odd = odd_w_in.shape[0]
    attn_k, attn_v, na_cache = [], [], ()
    for l in range(DEPTH):
        if l % 2 == 0:
            e = l // 2
            a_out, q, kd, vd, kf, vf = _even_in(xs, mods, l, e, T, g_mix_pre, even_w_in, sgu_w,
                                                sgu_b3, sgu_n3, qn3, kn3, bdq, bdk, cos, sin)
            n_pairs = qw // LANES
            pairs_per_kv = n_pairs // B_KV_HEADS
            mix_p = _attention(q, kd, vd, n_batch=nb_p, seq=seq_p, row0=0, pairs_per_step=n_pairs,
                               pairs_per_kv=pairs_per_kv, batches_per_step=PROMPT_BATCHES_PER_STEP)
            mix_s = _attention(q, kd, vd, n_batch=nb_s, seq=seq_s, row0=tp,
                               pairs_per_step=pairs_per_kv, pairs_per_kv=pairs_per_kv,
                               ctx=(ctx_ak, ctx_av, e))
            attn_k.append(kf[:tp].reshape(nb_p, seq_p, B_KV_HEADS, HEAD_DIM))
            attn_v.append(vf[:tp].reshape(nb_p, seq_p, B_KV_HEADS, HEAD_DIM))
            w_out, li = even_w_out, e
        else:
            o = l // 2
            a_out = None
            q, k, v, *na_cache = _odd_in(xs[0], mods, l, o, n_odd, seq_p, g_mix_pre, odd_w_in,
                                         tuple(na_cache))
            mix_p = _attention(q, k, v, n_batch=nb_p, seq=seq_p, row0=0,
                               pairs_per_step=PAIRS_PER_STEP, pairs_per_kv=1,
                               batches_per_step=PROMPT_BATCHES_PER_STEP)
            mix_s = _na_attention(q, k, v, ctx_nk, ctx_nv, o, rpb, n_batch=nb_s, row0=tp)
            w_out, li = odd_w_out, o
        x, h = _out_proj(xs, a_out, mix_p, mix_s, mods, l, li, T, g_mix_post, g_ffn_pre, w_out)
        xs = tuple(_ffn(x, h, mods, l, g_ffn_post, ffn_w_in, ffn_w_out,
                        split_out=(l == DEPTH - 1)))

    y_prompt = xs[0].reshape(nb_p, seq_p, D)
    y_sample = xs[1].reshape(nb_s, seq_s, D)
    new_na_k, new_na_v = (a.reshape(nb_p, n_odd, seq_p, C_HEADS, HEAD_DIM) for a in na_cache)
    return (y_prompt, y_sample, jnp.stack(attn_k, axis=1), jnp.stack(attn_v, axis=1),
            new_na_k, new_na_v)
```

```python
import functools

import jax
import jax.numpy as jnp
from jax import lax
from jax.experimental import pallas as pl
from jax.experimental.pallas import tpu as pltpu

F32 = jnp.float32
BF16 = jnp.bfloat16

D = 1024
DEPTH = 4
HEAD_DIM = 64
GRID_W = 64
CHUNK = 128
A_WIDTH = D // 2
A_GROUPS = 4
B_HEADS = 8
B_KV_HEADS = 2
C_HEADS = 16
NA_ROWS = 8
NA_COLS = 16
D_FF = 2816
ROPE_BASE = 10000.0
EPS = 1e-6
NEG = -1e30
SEG = 1024
N_SEG = 8
LANES = 128
LOG2E = 1.4426950408889634
QSCALE = HEAD_DIM ** -0.5 * LOG2E

TM_PROJ = 512
TM_OUT = 1024
OUT_ROWS = 256
TM_FFN = 1024
TF_FFN = 256
IN_COPY_SPLIT = 4
TQ_ATTN = 256
PAIRS_PER_STEP = 4
PROMPT_BATCHES_PER_STEP = 4
NA_QROWS = 4
NA_BATCHES_PER_STEP = 2
NA_KROWS = 12
VMEM_LIMIT = 56 * 1024 * 1024
VMEM_LIMIT_FFN = 62 * 1024 * 1024


def _cparams(n_axes, vmem_limit=VMEM_LIMIT):
    return pltpu.CompilerParams(dimension_semantics=("arbitrary",) * n_axes,
                                vmem_limit_bytes=vmem_limit)


def _rms(x):
    return x * lax.rsqrt(jnp.mean(x * x, axis=-1, keepdims=True) + EPS)


def _mod_norm(x, g, shift, scale):
    return _rms(x) * (g * (1.0 + scale)) + shift


def _gated_residual(x, y, g, gate):
    return x + _rms(y) * (gate * g)


def _const_spec(shape, *lead):
    block = (None,) * len(lead) + tuple(shape)
    return pl.BlockSpec(block, lambda *_: tuple(lead) + (0,) * len(shape))


def _weight_spec(shape, *lead):
    block = (None,) * len(lead) + tuple(shape)
    return pl.BlockSpec(block, lambda *_: tuple(lead) + (0,) * len(shape),
                        pipeline_mode=pl.Buffered(1))


def _same_tile(i):
    return i


def _mod_spec(layer, j, tm, tile=_same_tile):
    return pl.BlockSpec((None, None, None, 1, D),
                        lambda i, *_: (layer, j, (tile(i) * tm) // SEG, 0, 0))


def _split_specs(tm, width, n_prompt_tiles, col=0, tile=_same_tile):
    return [pl.BlockSpec((tm, width),
                         lambda i, *_: (jnp.minimum(tile(i), n_prompt_tiles - 1), col)),
            pl.BlockSpec((tm, width),
                         lambda i, *_: (jnp.maximum(tile(i) - n_prompt_tiles, 0), col))]


def _pick(n_prompt_tiles, p_ref, s_ref, tile=None):
    tile = pl.program_id(0) if tile is None else tile
    return jnp.where(tile < n_prompt_tiles, p_ref[...], s_ref[...])


def _mod_kernel(cond_ref, w_ref, b_ref, o_ref):
    s = cond_ref[...]
    s = s * jax.nn.sigmoid(s)
    o_ref[...] = jnp.dot(s.astype(BF16), w_ref[...].astype(BF16),
                         preferred_element_type=F32) + b_ref[...]


def _modulation(cond, mod_w, mod_b):
    b = mod_b.reshape(DEPTH, 6, 1, D)
    out = pl.pallas_call(
        _mod_kernel,
        grid=(DEPTH, 6),
        in_specs=[pl.BlockSpec((N_SEG, D), lambda l, j: (0, 0)),
                  pl.BlockSpec((None, D, D), lambda l, j: (l, 0, j)),
                  pl.BlockSpec((None, None, 1, D), lambda l, j: (l, j, 0, 0))],
        out_specs=pl.BlockSpec((None, None, N_SEG, D), lambda l, j: (l, j, 0, 0)),
        out_shape=jax.ShapeDtypeStruct((DEPTH, 6, N_SEG, D), F32),
        compiler_params=_cparams(2),
        name="modulation",
    )(cond, mod_w, b)
    return out.reshape(DEPTH, 6, N_SEG, 1, D)


def _head_sumsq(y, bd_ref):
    sq = y * y
    hi = sq.astype(BF16)
    lo = (sq - hi.astype(F32)).astype(BF16)
    bd = bd_ref[...]
    return (jnp.dot(hi, bd, preferred_element_type=F32)
            + jnp.dot(lo, bd, preferred_element_type=F32))


def _rope(y, cos, sin):
    lane = lax.broadcasted_iota(jnp.int32, (1, LANES), 1)
    first = (lane & 16) == 0
    partner = jnp.where(first, pltpu.roll(y, LANES - 16, 1), pltpu.roll(y, 16, 1))
    return y * cos + partner * sin


def _even_in_kernel(*refs, n_tiles, n_prompt_tiles, split_x):
    refs = list(refs)
    i = pl.program_id(0)
    t_proj = jnp.minimum(i, n_tiles - 1)
    x = _pick(n_prompt_tiles, refs.pop(0), refs.pop(0), t_proj) if split_x else refs.pop(0)[...]
    (sh_ref, sc_ref, g_ref, w_ref, sguw_ref, sgub_ref, sgun_ref, qn_ref, kn_ref, bdq_ref, bdk_ref,
     cos_ref, sin_ref, a_ref, q_ref, kd_ref, vd_ref, kf_ref, vf_ref, wbf_ref, h_ref, z0_ref,
     z1_ref) = refs
    tm = a_ref.shape[0]
    qw = B_HEADS * HEAD_DIM
    kw = B_KV_HEADS * HEAD_DIM
    c_q = 2 * A_WIDTH
    c_k = c_q + qw
    n_chunks = tm // CHUNK
    gch = A_WIDTH // A_GROUPS
    lane = lax.broadcasted_iota(jnp.int32, (1, LANES), 1)
    low = lane < HEAD_DIM

    @pl.when(i == 0)
    def _():
        wbf_ref[...] = w_ref[...].astype(BF16)
        z1_ref[...] = jnp.zeros_like(z1_ref)

    h_ref[...] = _mod_norm(x, g_ref[...], sh_ref[...], sc_ref[...]).astype(BF16)

    def step(z_new_ref, z_ref):
        def project(c0, c1):
            z_new_ref[:, c0:c1] = jnp.dot(h_ref[...], wbf_ref[:, c0:c1], preferred_element_type=F32)

        project(0, A_WIDTH)
        u = jax.nn.gelu(z_ref[:, 0:A_WIDTH], approximate=True)
        project(A_WIDTH, c_q)
        v = jax.nn.gelu(z_ref[:, A_WIDTH:c_q], approximate=True)
        mu = jnp.mean(v, axis=-1, keepdims=True)
        vc = v - mu
        var = jnp.mean(vc * vc, axis=-1, keepdims=True)
        vn = (vc * lax.rsqrt(var + EPS) * sgun_ref[...]).astype(BF16)
        for g in range(A_GROUPS):
            rhs = jnp.concatenate([vn[n * CHUNK:(n + 1) * CHUNK, g * gch:(g + 1) * gch]
                                   for n in range(n_chunks)], axis=1)
            mixed = jnp.dot(sguw_ref[g].astype(BF16), rhs, preferred_element_type=F32)
            bias = sgub_ref[g]
            for n in range(n_chunks):
                blk = (mixed[:, n * gch:(n + 1) * gch] + bias) * u[n * CHUNK:(n + 1) * CHUNK,
                                                                  g * gch:(g + 1) * gch]
                a_ref[n * CHUNK:(n + 1) * CHUNK, g * gch:(g + 1) * gch] = blk.astype(BF16)

        project(c_q, c_k)
        q = z_ref[:, c_q:c_k]
        q = q * lax.rsqrt(_head_sumsq(q, bdq_ref) * (1.0 / HEAD_DIM) + EPS) * qn_ref[...]
        cos = cos_ref[...]
        sin = sin_ref[...]
        for j in range(qw // LANES):
            qj = _rope(q[:, j * LANES:(j + 1) * LANES], cos, sin) * QSCALE
            q_ref[:, j * LANES:(j + 1) * LANES] = qj.astype(BF16)

        project(c_k, c_k + 2 * kw)
        k = z_ref[:, c_k:c_k + kw]
        k = k * lax.rsqrt(_head_sumsq(k, bdk_ref) * (1.0 / HEAD_DIM) + EPS) * kn_ref[...]
        kf_ref[...] = k
        vals = z_ref[:, c_k + kw:c_k + 2 * kw]
        vf_ref[...] = vals
        for src, dst in ((_rope(k, cos, sin), kd_ref), (vals, vd_ref)):
            sw = pltpu.roll(src, HEAD_DIM, 1)
            dst[:, 0:LANES] = jnp.where(low, src, sw).astype(BF16)
            dst[:, LANES:2 * LANES] = jnp.where(low, sw, src).astype(BF16)

    @pl.when(i % 2 == 0)
    def _():
        step(z0_ref, z1_ref)

    @pl.when(i % 2 == 1)
    def _():
        step(z1_ref, z0_ref)


def _even_in(xs, mods, layer, e, T, g_pre, w_in, sgu_w, sgu_b, sgu_norm, q_norm, k_norm,
             bdq, bdk, cos, sin):
    tm = TM_PROJ
    n_in = w_in.shape[2]
    n_tiles = T // tm
    tiles_per_seg = SEG // tm
    n_prompt_tiles = n_tiles // 2
    split_x = len(xs) == 2
    proj = lambda i: jnp.minimum(i, n_tiles - 1)
    post = lambda i: jnp.maximum(i - 1, 0)
    tab_spec = pl.BlockSpec((None, tm, LANES),
                            lambda i: (post(i) // n_prompt_tiles, post(i) % tiles_per_seg, 0))
    row = lambda w, tile: pl.BlockSpec((tm, w), lambda i: (tile(i), 0))
    x_specs = (_split_specs(tm, D, n_prompt_tiles, tile=proj) if split_x else [row(D, proj)])
    qw = B_HEADS * HEAD_DIM
    return pl.pallas_call(
        functools.partial(_even_in_kernel, n_tiles=n_tiles, n_prompt_tiles=n_prompt_tiles,
                          split_x=split_x),
        grid=(n_tiles + 1,),
        in_specs=x_specs + [
            _mod_spec(layer, 0, tm, proj), _mod_spec(layer, 1, tm, proj),
            _const_spec((1, D), layer),
            _weight_spec((D, n_in), e), _const_spec((A_GROUPS, CHUNK, CHUNK), e),
            _const_spec((A_GROUPS, CHUNK, 1), e), _const_spec((1, A_WIDTH), e),
            _const_spec((1, qw), e), _const_spec((1, LANES), e),
            _const_spec((qw, qw)), _const_spec((LANES, LANES)), tab_spec, tab_spec],
        out_specs=[row(A_WIDTH, post), row(qw, post), row(2 * LANES, post), row(2 * LANES, post),
                   row(LANES, post), row(LANES, post)],
        out_shape=[jax.ShapeDtypeStruct((T, A_WIDTH), BF16),
                   jax.ShapeDtypeStruct((T, qw), BF16),
                   jax.ShapeDtypeStruct((T, 2 * LANES), BF16),
                   jax.ShapeDtypeStruct((T, 2 * LANES), BF16),
                   jax.ShapeDtypeStruct((T, LANES), F32),
                   jax.ShapeDtypeStruct((T, LANES), F32)],
        scratch_shapes=[pltpu.VMEM((D, n_in), BF16), pltpu.VMEM((tm, D), BF16),
                        pltpu.VMEM((tm, n_in), F32), pltpu.VMEM((tm, n_in), F32)],
        compiler_params=_cparams(1),
        name="even_in_proj",
    )(*xs, mods, mods, g_pre, w_in, sgu_w, sgu_b, sgu_norm, q_norm, k_norm, bdq, bdk, cos, sin)


def _odd_in_kernel(x_ref, sh_ref, sc_ref, g_ref, w_ref, *rest, n_prompt_tiles, slot, all_slots):
    q_ref, k_ref, v_ref, kf_ref, vf_ref, wbf_ref = rest[-6:]
    i = pl.program_id(0)
    hw = C_HEADS * HEAD_DIM

    @pl.when(i == 0)
    def _():
        for c in range(3):
            wbf_ref[:, c * hw:(c + 1) * hw] = w_ref[:, c * hw:(c + 1) * hw].astype(BF16)

    h = _mod_norm(x_ref[...], g_ref[...], sh_ref[...], sc_ref[...]).astype(BF16)
    q = jnp.dot(h, wbf_ref[:, 0:hw], preferred_element_type=F32)
    q_ref[...] = (q * QSCALE).astype(BF16)
    k = jnp.dot(h, wbf_ref[:, hw:2 * hw], preferred_element_type=F32)
    k_ref[...] = k.astype(BF16)
    v = jnp.dot(h, wbf_ref[:, 2 * hw:3 * hw], preferred_element_type=F32)
    v_ref[...] = v.astype(BF16)

    @pl.when(i < n_prompt_tiles)
    def _():
        for src, dst in ((k, kf_ref), (v, vf_ref)):
            if all_slots:
                val = src.reshape((dst.shape[0],) + dst.shape[2:])
                for s in range(dst.shape[1]):
                    dst[:, s] = val if s == slot else jnp.zeros_like(val)
            else:
                dst[...] = src.reshape(dst.shape)


def _odd_in(x, mods, layer, o, n_odd, seq_p, g_pre, w_in, caches):
    T = x.shape[0]
    tm = TM_PROJ
    hw = C_HEADS * HEAD_DIM
    n_prompt_tiles = T // 2 // tm
    bt = tm // seq_p
    row = pl.BlockSpec((tm, hw), lambda i: (i, 0))
    all_slots = not caches
    if all_slots:
        crow = pl.BlockSpec((bt, n_odd, seq_p, hw),
                            lambda i: (jnp.minimum(i, n_prompt_tiles - 1), 0, 0, 0))
    else:
        crow = pl.BlockSpec((bt, None, seq_p, hw),
                            lambda i: (jnp.minimum(i, n_prompt_tiles - 1), o, 0, 0))
    cshape = jax.ShapeDtypeStruct((T // 2 // seq_p, n_odd, seq_p, hw), F32)
    n_in = 5
    return pl.pallas_call(
        functools.partial(_odd_in_kernel, n_prompt_tiles=n_prompt_tiles, slot=o,
                          all_slots=all_slots),
        grid=(T // tm,),
        in_specs=[pl.BlockSpec((tm, D), lambda i: (i, 0)), _mod_spec(layer, 0, tm),
                  _mod_spec(layer, 1, tm), _const_spec((1, D), layer),
                  _weight_spec((D, 3 * hw), o)]
                 + [pl.BlockSpec(memory_space=pl.ANY)] * len(caches),
        out_specs=[row, row, row, crow, crow],
        out_shape=[jax.ShapeDtypeStruct((T, hw), BF16)] * 3 + [cshape, cshape],
        input_output_aliases={n_in + j: 3 + j for j in range(len(caches))},
        scratch_shapes=[pltpu.VMEM((D, 3 * hw), BF16)],
        compiler_params=_cparams(1),
        name="odd_in_proj",
    )(x, mods, mods, g_pre, w_in, *caches)


def _softmax(pieces):
    m = None
    for s, _ in pieces:
        ms = jnp.max(s, axis=-1, keepdims=True)
        m = ms if m is None else jnp.maximum(m, ms)
    den = None
    probs = []
    for s, _ in pieces:
        p = jnp.exp2(s - m)
        ls = jnp.sum(p, axis=-1, keepdims=True)
        den = ls if den is None else den + ls
        probs.append(p.astype(BF16))
    return probs, [val for _, val in pieces], den


def _weighted_values(probs, vals, den):
    acc = None
    for p, val in zip(probs, vals):
        o = jnp.dot(p, val, preferred_element_type=F32)
        acc = o if acc is None else acc + o
    return acc / den


def _pipelined_units(units, scores, store):
    n = len(units)
    pending_scores = scores(units[0])
    pending_probs = None
    for idx in range(n + 1):
        upcoming = scores(units[idx + 1]) if idx + 1 < n else None
        probs = _softmax(pending_scores) if idx < n else None
        if pending_probs is not None:
            store(units[idx - 1], _weighted_values(*pending_probs))
        pending_scores, pending_probs = upcoming, probs


def _attn_kernel(*refs, has_ctx, seq, tq, pairs_per_kv):
    if has_ctx:
        q_ref, k_ref, v_ref, kc_ref, vc_ref, o_ref = refs
    else:
        q_ref, k_ref, v_ref, o_ref = refs
    lane = lax.broadcasted_iota(jnp.int32, (1, LANES), 1)
    masks = [lane < HEAD_DIM, lane >= HEAD_DIM]
    if has_ctx:
        kv_head = pl.program_id(1)
        sel = jnp.where(masks[0], 0, 1) == kv_head
        kc = jnp.where(sel, kc_ref[...], 0.0)
        kc = (kc + pltpu.roll(kc, HEAD_DIM, 1)).astype(BF16)
        vc = jnp.where(sel, vc_ref[...], 0.0)
        vc = (vc + pltpu.roll(vc, HEAD_DIM, 1)).astype(BF16)
        vch = [jnp.where(mh, vc, jnp.zeros_like(vc)) for mh in masks]
    nt = (((1,), (1,)), ((), ()))
    groups = [(r0, j) for r0 in range(0, q_ref.shape[0], seq) for j in range(q_ref.shape[1] // LANES)]
    units = [(r0, j, t, hh) for r0, j in groups for t in range(r0 // tq, (r0 + seq) // tq)
             for hh in range(len(masks))]
    operands = {}

    def group_operands(r0, j):
        if (r0, j) not in operands:
            kcols = slice((j // pairs_per_kv) * LANES, (j // pairs_per_kv + 1) * LANES)
            v = v_ref[r0:r0 + seq, kcols]
            operands[r0, j] = (k_ref[r0:r0 + seq, kcols],
                               [jnp.where(mh, v, jnp.zeros_like(v)) for mh in masks])
        return operands[r0, j]

    def scores(unit):
        r0, j, t, hh = unit
        k, vh = group_operands(r0, j)
        q = q_ref[t * tq:(t + 1) * tq, j * LANES:(j + 1) * LANES]
        qm = jnp.where(masks[hh], q, jnp.zeros_like(q))
        pieces = [(lax.dot_general(qm, k, nt, preferred_element_type=F32), vh[hh])]
        if has_ctx:
            pieces.append((lax.dot_general(qm, kc, nt, preferred_element_type=F32), vch[hh]))
        return pieces

    partial = {}

    def store(unit, o):
        r0, j, t, hh = unit
        if hh == 0:
            partial[r0, j, t] = o
        else:
            out = partial.pop((r0, j, t)) + o
            o_ref[t * tq:(t + 1) * tq, j * LANES:(j + 1) * LANES] = out.astype(BF16)

    _pipelined_units(units, scores, store)


def _attention(q, k, v, *, n_batch, seq, row0, pairs_per_step, pairs_per_kv, batches_per_step=1,
               ctx=None):
    rows = batches_per_step * seq
    assert row0 % rows == 0 and n_batch % batches_per_step == 0
    blk0 = row0 // rows
    n_groups = q.shape[1] // (pairs_per_step * LANES)
    qw = pairs_per_step * LANES
    kw = qw // pairs_per_kv
    in_specs = [pl.BlockSpec((rows, qw), lambda b, g: (blk0 + b, g)),
                pl.BlockSpec((rows, kw), lambda b, g: (blk0 + b, g)),
                pl.BlockSpec((rows, kw), lambda b, g: (blk0 + b, g))]
    args = [q, k, v]
    if ctx is not None:
        assert kw == LANES and batches_per_step == 1
        kc, vc, e = ctx
        sc = kc.shape[2]
        cspec = pl.BlockSpec((None, None, sc, LANES), lambda b, g: (b, e, 0, 0))
        in_specs += [cspec, cspec]
        args += [kc, vc]
    return pl.pallas_call(
        functools.partial(_attn_kernel, has_ctx=ctx is not None, seq=seq, tq=min(TQ_ATTN, seq),
                          pairs_per_kv=pairs_per_kv),
        grid=(n_batch // batches_per_step, n_groups),
        in_specs=in_specs,
        out_specs=pl.BlockSpec((rows, qw), lambda b, g: (b, g)),
        out_shape=jax.ShapeDtypeStruct((n_batch * seq, q.shape[1]), BF16),
        compiler_params=_cparams(2),
        name="attention",
    )(*args)


def _na_row_start(r):
    rows = SEG // GRID_W
    return min(max(r - NA_ROWS // 2, 0), rows - NA_ROWS)


def _na_windows():
    rows = SEG // GRID_W
    windows = []
    for qb in range(rows // NA_QROWS):
        lo = _na_row_start(qb * NA_QROWS)
        hi = _na_row_start(qb * NA_QROWS + NA_QROWS - 1) + NA_ROWS
        n = hi - lo + (hi - lo) % 2
        ws = min(lo, rows - n)
        assert ws <= lo and hi <= ws + n <= rows and n <= NA_KROWS
        windows.append((ws, n))
    return windows


def _na_kernel(q_ref, k_ref, v_ref, kc_ref, vc_ref, r_ref, o_ref, bias_ref, tab_ref, *, windows):
    lane = lax.broadcasted_iota(jnp.int32, (1, LANES), 1)
    low = lane < HEAD_DIM

    @pl.when(pl.program_id(1) == 0)
    def _():
        qcol = lax.broadcasted_iota(jnp.int32, (GRID_W, LANES), 0)
        kcol = lax.broadcasted_iota(jnp.int32, (GRID_W, LANES), 1) & (GRID_W - 1)
        start = jnp.clip(qcol - NA_COLS // 2, 0, GRID_W - NA_COLS)
        inside = (kcol >= start) & (kcol < start + NA_COLS)
        for hh in range(LANES // HEAD_DIM):
            for d in range(2 * NA_ROWS - 1):
                base = jnp.broadcast_to(r_ref[hh, d:d + 1, :], (GRID_W, LANES))
                lo_t = pltpu.roll(base, LANES - (NA_COLS - 1), 1, stride=1, stride_axis=0)
                hi_t = pltpu.roll(base, GRID_W - (NA_COLS - 1), 1, stride=1, stride_axis=0)
                tab_ref[hh, d] = jnp.where(inside, jnp.where(low, lo_t, hi_t) * LOG2E, NEG)
        neg = jnp.full((GRID_W, LANES), NEG, F32)
        for hh in range(LANES // HEAD_DIM):
            for qb, (ws, nrows) in enumerate(windows):
                for i in range(NA_QROWS):
                    r = qb * NA_QROWS + i
                    rs = _na_row_start(r)
                    for jp in range(nrows // 2):
                        kr = ws + 2 * jp
                        ok = [rs <= kr + d < rs + NA_ROWS for d in (0, 1)]
                        if not any(ok):
                            blk = neg
                        else:
                            t0 = tab_ref[hh, kr - r + NA_ROWS - 1] if ok[0] else neg
                            t1 = tab_ref[hh, kr + 1 - r + NA_ROWS - 1] if ok[1] else neg
                            blk = jnp.where(low, t0, t1)
                        bias_ref[hh, qb, i * GRID_W:(i + 1) * GRID_W,
                                 jp * LANES:(jp + 1) * LANES] = blk

    masks = [low, jnp.logical_not(low)]
    n_batch = kc_ref.shape[0]
    kc = [kc_ref[bb].astype(BF16) for bb in range(n_batch)]
    vch = []
    for bb in range(n_batch):
        vc = vc_ref[bb].astype(BF16)
        vch.append([jnp.where(mh, vc, jnp.zeros_like(vc)) for mh in masks])
    nt = (((1,), (1,)), ((), ()))
    nq = NA_QROWS * GRID_W
    units = [(bb, qb, hh) for bb in range(n_batch) for qb in range(len(windows))
             for hh in range(len(masks))]

    def scores(unit):
        bb, qb, hh = unit
        ws, nrows = windows[qb]
        nk = nrows * GRID_W
        k0 = bb * SEG + ws * GRID_W
        q = q_ref[bb * SEG + qb * nq:bb * SEG + (qb + 1) * nq, :]
        kw = k_ref[k0:k0 + nk, :]
        vw = v_ref[k0:k0 + nk, :]
        qm = jnp.where(masks[hh], q, jnp.zeros_like(q))
        s_win = (lax.dot_general(qm, kw, nt, preferred_element_type=F32)
                 + bias_ref[hh, qb, :, 0:nk])
        s_ctx = lax.dot_general(qm, kc[bb], nt, preferred_element_type=F32)
        return [(s_win, jnp.where(masks[hh], vw, jnp.zeros_like(vw))), (s_ctx, vch[bb][hh])]

    partial = {}

    def store(unit, o):
        bb, qb, hh = unit
        if hh == 0:
            partial[bb, qb] = o
        else:
            r0 = bb * SEG + qb * nq
            o_ref[r0:r0 + nq, :] = (partial.pop((bb, qb)) + o).astype(BF16)

    _pipelined_units(units, scores, store)


def _na_attention(q, k, v, kc, vc, o_idx, rpb, *, n_batch, row0):
    bps = NA_BATCHES_PER_STEP
    rows = bps * SEG
    assert row0 % rows == 0 and n_batch % bps == 0
    blk0 = row0 // rows
    n_pairs = q.shape[1] // LANES
    sc = kc.shape[2]
    windows = _na_windows()
    heads = LANES // HEAD_DIM
    nr = 2 * NA_ROWS - 1
    qkv = pl.BlockSpec((rows, LANES), lambda p, b: (blk0 + b, p))
    cspec = pl.BlockSpec((bps, None, sc, LANES), lambda p, b: (b, o_idx, 0, p))
    rspec = pl.BlockSpec((None, heads, nr, LANES), lambda p, b: (o_idx, p, 0, 0))
    return pl.pallas_call(
        functools.partial(_na_kernel, windows=windows),
        grid=(n_pairs, n_batch // bps),
        in_specs=[qkv, qkv, qkv, cspec, cspec, rspec],
        out_specs=pl.BlockSpec((rows, LANES), lambda p, b: (b, p)),
        out_shape=jax.ShapeDtypeStruct((n_batch * SEG, n_pairs * LANES), BF16),
        scratch_shapes=[pltpu.VMEM((heads, len(windows), NA_QROWS * GRID_W, NA_KROWS * GRID_W), F32),
                        pltpu.VMEM((heads, nr, GRID_W, LANES), F32)],
        compiler_params=_cparams(2),
        name="na_attention",
    )(q, k, v, kc, vc, rpb)


def _out_kernel(*refs, n_prompt_tiles, split_x, has_a):
    refs = list(refs)
    x_refs = [refs.pop(0), refs.pop(0)] if split_x else [refs.pop(0)]
    gt_ref, g_ref, fsh_ref, fsc_ref, gf_ref, w_ref = refs[:6]
    a_ref = refs[6] if has_a else None
    bp_ref, bs_ref, o_ref, h_ref, wbf_ref = refs[6 + has_a:]
    prompt = pl.program_id(0) < n_prompt_tiles

    @pl.when(pl.program_id(0) == 0)
    def _():
        wbf_ref[...] = w_ref[...].astype(BF16)

    def project(rows):
        b = jnp.where(prompt, bp_ref[rows, :], bs_ref[rows, :])
        if has_a:
            half = a_ref.shape[1]
            return (jnp.dot(a_ref[rows, :], wbf_ref[0:half, :], preferred_element_type=F32)
                    + jnp.dot(b, wbf_ref[half:2 * half, :], preferred_element_type=F32))
        return jnp.dot(b, wbf_ref[...], preferred_element_type=F32)

    def finish(rows, y):
        x = jnp.where(prompt, x_refs[0][rows, :], x_refs[1][rows, :]) if split_x else x_refs[0][rows, :]
        x1 = _gated_residual(x, y, g_ref[...], gt_ref[...])
        o_ref[rows, :] = x1
        h_ref[rows, :] = _mod_norm(x1, gf_ref[...], fsh_ref[...], fsc_ref[...]).astype(BF16)

    tm = o_ref.shape[0]
    blocks = [slice(r, r + OUT_ROWS) for r in range(0, tm, OUT_ROWS)]
    pending = project(blocks[0])
    for idx, rows in enumerate(blocks):
        upcoming = project(blocks[idx + 1]) if idx + 1 < len(blocks) else None
        finish(rows, pending)
        pending = upcoming


def _out_proj(xs, a, bp, bs, mods, layer, li, T, g_post, g_ffn_pre, w_out):
    tm = TM_OUT
    n_prompt_tiles = T // 2 // tm
    split_x = len(xs) == 2
    has_a = a is not None
    row = lambda w: pl.BlockSpec((tm, w), lambda i: (i, 0))
    x_specs = _split_specs(tm, D, n_prompt_tiles) if split_x else [row(D)]
    a_specs = [row(a.shape[1])] if has_a else []
    return pl.pallas_call(
        functools.partial(_out_kernel, n_prompt_tiles=n_prompt_tiles, split_x=split_x, has_a=has_a),
        grid=(T // tm,),
        in_specs=x_specs + [_mod_spec(layer, 2, tm), _const_spec((1, D), layer),
                            _mod_spec(layer, 3, tm), _mod_spec(layer, 4, tm),
                            _const_spec((1, D), layer), _weight_spec((D, D), li)] + a_specs
                 + _split_specs(tm, bp.shape[1], n_prompt_tiles),
        out_specs=[row(D), row(D)],
        out_shape=[jax.ShapeDtypeStruct((T, D), F32), jax.ShapeDtypeStruct((T, D), BF16)],
        scratch_shapes=[pltpu.VMEM((D, D), BF16)],
        compiler_params=_cparams(1),
        name="out_proj",
    )(*xs, mods, g_post, mods, mods, g_ffn_pre, w_out, *([a] if has_a else []), bp, bs)


def _ffn_kernel(x_ref, h_ref, gt_ref, gpost_ref, win_ref, wout_ref, *rest,
                layer, n_prompt_tiles, split_out):
    out_refs = rest[:-9]
    wg_buf, wu_buf, wo_buf, wg_res, wu_res, wo_res, act_buf, acc_ref, sem = rest[-9:]
    nf, _, tf = wg_res.shape
    first_tile = pl.program_id(0) == 0

    def aligned(f):
        return f * tf if isinstance(f, int) else pl.multiple_of(f * tf, tf)

    def in_copies(f, slot):
        col = aligned(f)
        copies = []
        for n, (buf, c0) in enumerate(((wg_buf, col), (wu_buf, D_FF + col))):
            for r in range(IN_COPY_SPLIT):
                rows = pl.ds(r * (D // IN_COPY_SPLIT), D // IN_COPY_SPLIT)
                copies.append(pltpu.make_async_copy(win_ref.at[layer, rows, pl.ds(c0, tf)],
                                                    buf.at[slot, rows, :],
                                                    sem.at[n * IN_COPY_SPLIT + r, slot]))
        return copies

    def out_copy(f, slot):
        row = aligned(f)
        return pltpu.make_async_copy(wout_ref.at[layer, pl.ds(row, tf), :], wo_buf.at[slot],
                                     sem.at[2 * IN_COPY_SPLIT, slot])

    def fetch(f, slot):
        for c in in_copies(f, slot):
            c.wait()
        out_copy(f, slot).wait()

        nxt = min(f + 1, nf - 1) if isinstance(f, int) else jnp.minimum(f + 1, nf - 1)

        @pl.when(jnp.asarray(f + 1 < nf))
        def _():
            for c in in_copies(nxt, 1 - slot):
                c.start()
            out_copy(nxt, 1 - slot).start()

        wg_res[f] = wg_buf[slot].astype(BF16)
        wu_res[f] = wu_buf[slot].astype(BF16)
        wo_res[f] = wo_buf[slot].astype(BF16)

    def hidden(f, slot):
        h = h_ref[...]
        g = jnp.dot(h, wg_res[f], preferred_element_type=F32)
        u = jnp.dot(h, wu_res[f], preferred_element_type=F32)
        act_buf[slot] = (g * jax.nn.sigmoid(g) * u).astype(BF16)

    def project(f, slot):
        acc_ref[...] += jnp.dot(act_buf[slot], wo_res[f], preferred_element_type=F32)

    assert nf % 2 == 1

    def chunks(streaming):
        acc_ref[...] = jnp.zeros_like(acc_ref)
        if streaming:
            fetch(0, 0)
        hidden(0, 0)

        def pair(k, carry):
            for f, slot in ((2 * k + 1, 1), (2 * k + 2, 0)):
                if streaming:
                    fetch(f, slot)
                hidden(f, slot)
                project(f - 1, 1 - slot)
            return carry

        lax.fori_loop(0, (nf - 1) // 2, pair, 0)
        project(nf - 1, 0)

    @pl.when(first_tile)
    def _():
        for c in in_copies(0, 0):
            c.start()
        out_copy(0, 0).start()
        chunks(streaming=True)

    @pl.when(jnp.logical_not(first_tile))
    def _():
        chunks(streaming=False)

    def result():
        return _gated_residual(x_ref[...], acc_ref[...], gpost_ref[...], gt_ref[...])

    if split_out:
        i = pl.program_id(0)

        @pl.when(i < n_prompt_tiles)
        def _():
            out_refs[0][...] = result()

        @pl.when(i >= n_prompt_tiles)
        def _():
            out_refs[1][...] = result()
    else:
        out_refs[0][...] = result()


def _ffn(x, h, mods, layer, g_post, w_in, w_out, split_out):
    T = x.shape[0]
    tm, tf = TM_FFN, TF_FFN
    nf = D_FF // tf
    n_prompt_tiles = T // 2 // tm
    xrow = pl.BlockSpec((tm, D), lambda i: (i, 0))
    if split_out:
        out_specs = _split_specs(tm, D, n_prompt_tiles)
        out_shape = [jax.ShapeDtypeStruct((T // 2, D), F32)] * 2
    else:
        out_specs = [xrow]
        out_shape = [jax.ShapeDtypeStruct((T, D), F32)]
    hbm = pl.BlockSpec(memory_space=pl.ANY)
    return pl.pallas_call(
        functools.partial(_ffn_kernel, layer=layer, n_prompt_tiles=n_prompt_tiles,
                          split_out=split_out),
        grid=(T // tm,),
        in_specs=[xrow, xrow, _mod_spec(layer, 5, tm), _const_spec((1, D), layer), hbm, hbm],
        out_specs=out_specs,
        out_shape=out_shape,
        scratch_shapes=[pltpu.VMEM((2, D, tf), F32), pltpu.VMEM((2, D, tf), F32),
                        pltpu.VMEM((2, tf, D), F32), pltpu.VMEM((nf, D, tf), BF16),
                        pltpu.VMEM((nf, D, tf), BF16), pltpu.VMEM((nf, tf, D), BF16),
                        pltpu.VMEM((2, tm, tf), BF16), pltpu.VMEM((tm, D), F32),
                        pltpu.SemaphoreType.DMA((2 * IN_COPY_SPLIT + 1, 2))],
        compiler_params=_cparams(1, VMEM_LIMIT_FFN),
        name="ffn",
    )(x, h, mods, g_post, w_in, w_out)


def _rope_tables():
    t = jnp.arange(SEG)
    nf = HEAD_DIM // 4
    freqs = ROPE_BASE ** (-jnp.arange(nf, dtype=F32) / nf)

    def cs(pos):
        ang = pos.astype(F32)[:, None] * freqs[None, :]
        return jnp.cos(ang), jnp.sin(ang)

    cr, sr = cs(t // GRID_W)
    cc, sn = cs(t % GRID_W)
    cos = jnp.concatenate([cr, cr, cc, cc], axis=1)
    sin = jnp.concatenate([-sr, sr, -sn, sn], axis=1)
    reps = LANES // HEAD_DIM
    cos = jnp.tile(cos, (1, reps))
    sin = jnp.tile(sin, (1, reps))
    return (jnp.stack([jnp.ones_like(cos), cos]), jnp.stack([jnp.zeros_like(sin), sin]))


def _block_diag_ones(n):
    i = jnp.arange(n) // HEAD_DIM
    return (i[:, None] == i[None, :]).astype(BF16)


def kernel(x_prompt, x_sample, cache_attn_k, cache_attn_v, cache_na_k, cache_na_v, c, c_ctx,
           mod_w, mod_b, norm_mix_pre, norm_mix_post, norm_ffn_pre, norm_ffn_post,
           even_w_in, even_w_out, sgu_w, sgu_b, sgu_norm, q_norm, k_norm,
           odd_w_in, odd_w_out, na_rpb, ffn_w_in, ffn_w_out):
    nb_p, seq_p, _ = x_prompt.shape
    nb_s, seq_s, _ = x_sample.shape
    tp = nb_p * seq_p
    T = 2 * tp
    assert seq_s == SEG and tp == nb_s * seq_s and tp % SEG == 0

    cond = jnp.concatenate([jnp.broadcast_to(c_ctx[None, :], (tp // SEG, D)), c], axis=0)
    mods = _modulation(cond, mod_w, mod_b)

    cos, sin = _rope_tables()
    qw = B_HEADS * HEAD_DIM
    bdq = _block_diag_ones(qw)
    bdk = _block_diag_ones(LANES)
    past = cache_attn_k.shape[2]
    ctx_ak = cache_attn_k.reshape(nb_s, -1, past, B_KV_HEADS * HEAD_DIM)
    ctx_av = cache_attn_v.reshape(nb_s, -1, past, B_KV_HEADS * HEAD_DIM)
    ctx_nk = cache_na_k.reshape(nb_s, -1, past, C_HEADS * HEAD_DIM)
    ctx_nv = cache_na_v.reshape(nb_s, -1, past, C_HEADS * HEAD_DIM)
    g_mix_pre, g_mix_post, g_ffn_pre, g_ffn_post = (
        a.reshape(DEPTH, 1, D) for a in (norm_mix_pre, norm_mix_post, norm_ffn_pre, norm_ffn_post))
    n_even = even_w_in.shape[0]
    sgu_b3 = sgu_b.reshape(n_even, A_GROUPS, CHUNK, 1)
    sgu_n3 = sgu_norm.reshape(n_even, 1, A_WIDTH)
    qn3 = jnp.tile(q_norm, (1, B_HEADS)).reshape(n_even, 1, qw)
    kn3 = jnp.tile(k_norm, (1, B_KV_HEADS)).reshape(n_even, 1, LANES)
    rpb = jnp.pad(na_rpb, ((0, 0), (0, 0), (0, 0), (0, LANES - na_rpb.shape[3])))

    xs = (x_prompt.reshape(tp, D), x_sample.reshape(nb_s * seq_s, D))
    n_odd = odd_w_in.shape[0]
    attn_k, attn_v, na_cache = [], [], ()
    for l in range(DEPTH):
        if l % 2 == 0:
            e = l // 2
            a_out, q, kd, vd, kf, vf = _even_in(xs, mods, l, e, T, g_mix_pre, even_w_in, sgu_w,
                                                sgu_b3, sgu_n3, qn3, kn3, bdq, bdk, cos, sin)
            n_pairs = qw // LANES
            pairs_per_kv = n_pairs // B_KV_HEADS
            mix_p = _attention(q, kd, vd, n_batch=nb_p, seq=seq_p, row0=0, pairs_per_step=n_pairs,
                               pairs_per_kv=pairs_per_kv, batches_per_step=PROMPT_BATCHES_PER_STEP)
            mix_s = _attention(q, kd, vd, n_batch=nb_s, seq=seq_s, row0=tp,
                               pairs_per_step=pairs_per_kv, pairs_per_kv=pairs_per_kv,
                               ctx=(ctx_ak, ctx_av, e))
            attn_k.append(kf[:tp].reshape(nb_p, seq_p, B_KV_HEADS, HEAD_DIM))
            attn_v.append(vf[:tp].reshape(nb_p, seq_p, B_KV_HEADS, HEAD_DIM))
            w_out, li = even_w_out, e
        else:
            o = l // 2
            a_out = None
            q, k, v, *na_cache = _odd_in(xs[0], mods, l, o, n_odd, seq_p, g_mix_pre, odd_w_in,
                                         tuple(na_cache))
            mix_p = _attention(q, k, v, n_batch=nb_p, seq=seq_p, row0=0,
                               pairs_per_step=PAIRS_PER_STEP, pairs_per_kv=1,
                               batches_per_step=PROMPT_BATCHES_PER_STEP)
            mix_s = _na_attention(q, k, v, ctx_nk, ctx_nv, o, rpb, n_batch=nb_s, row0=tp)
            w_out, li = odd_w_out, o
        x, h = _out_proj(xs, a_out, mix_p, mix_s, mods, l, li, T, g_mix_post, g_ffn_pre, w_out)
        xs = tuple(_ffn(x, h, mods, l, g_ffn_post, ffn_w_in, ffn_w_out,
                        split_out=(l == DEPTH - 1)))

    y_prompt = xs[0].reshape(nb_p, seq_p, D)
    y_sample = xs[1].reshape(nb_s, seq_s, D)
    new_na_k, new_na_v = (a.reshape(nb_p, n_odd, seq_p, C_HEADS, HEAD_DIM) for a in na_cache)
    return (y_prompt, y_sample, jnp.stack(attn_k, axis=1), jnp.stack(attn_v, axis=1),
            new_na_k, new_na_v)
```

```python
import functools

import jax
import jax.numpy as jnp
from jax import lax
from jax.experimental import pallas as pl
from jax.experimental.pallas import tpu as pltpu

F32 = jnp.float32
BF16 = jnp.bfloat16

D = 1024
DEPTH = 4
HEAD_DIM = 64
GRID_W = 64
CHUNK = 128
A_WIDTH = D // 2
A_GROUPS = 4
B_HEADS = 8
B_KV_HEADS = 2
C_HEADS = 16
NA_ROWS = 8
NA_COLS = 16
D_FF = 2816
ROPE_BASE = 10000.0
EPS = 1e-6
NEG = -1e30
SEG = 1024
N_SEG = 8
LANES = 128
LOG2E = 1.4426950408889634
QSCALE = HEAD_DIM ** -0.5 * LOG2E

TM_PROJ = 512
TM_OUT = 1024
OUT_ROWS = 256
TM_FFN = 1024
TF_FFN = 256
IN_COPY_SPLIT = 4
TQ_ATTN = 256
PAIRS_PER_STEP = 4
PROMPT_BATCHES_PER_STEP = 4
NA_QROWS = 4
NA_BATCHES_PER_STEP = 2
NA_KROWS = 12
VMEM_LIMIT = 56 * 1024 * 1024
VMEM_LIMIT_FFN = 62 * 1024 * 1024


def _cparams(n_axes, vmem_limit=VMEM_LIMIT):
    return pltpu.CompilerParams(dimension_semantics=("arbitrary",) * n_axes,
                                vmem_limit_bytes=vmem_limit)


def _rms(x):
    return x * lax.rsqrt(jnp.mean(x * x, axis=-1, keepdims=True) + EPS)


def _mod_norm(x, g, shift, scale):
    return _rms(x) * (g * (1.0 + scale)) + shift


def _gated_residual(x, y, g, gate):
    return x + _rms(y) * (gate * g)


def _const_spec(shape, *lead):
    block = (None,) * len(lead) + tuple(shape)
    return pl.BlockSpec(block, lambda *_: tuple(lead) + (0,) * len(shape))


def _weight_spec(shape, *lead):
    block = (None,) * len(lead) + tuple(shape)
    return pl.BlockSpec(block, lambda *_: tuple(lead) + (0,) * len(shape),
                        pipeline_mode=pl.Buffered(1))


def _same_tile(i):
    return i


def _mod_spec(layer, j, tm, tile=_same_tile):
    return pl.BlockSpec((None, None, None, 1, D),
                        lambda i, *_: (layer, j, (tile(i) * tm) // SEG, 0, 0))


def _split_specs(tm, width, n_prompt_tiles, col=0, tile=_same_tile):
    return [pl.BlockSpec((tm, width),
                         lambda i, *_: (jnp.minimum(tile(i), n_prompt_tiles - 1), col)),
            pl.BlockSpec((tm, width),
                         lambda i, *_: (jnp.maximum(tile(i) - n_prompt_tiles, 0), col))]


def _pick(n_prompt_tiles, p_ref, s_ref, tile=None):
    tile = pl.program_id(0) if tile is None else tile
    return jnp.where(tile < n_prompt_tiles, p_ref[...], s_ref[...])


def _mod_kernel(cond_ref, w_ref, b_ref, o_ref):
    s = cond_ref[...]
    s = s * jax.nn.sigmoid(s)
    o_ref[...] = jnp.dot(s.astype(BF16), w_ref[...].astype(BF16),
                         preferred_element_type=F32) + b_ref[...]


def _modulation(cond, mod_w, mod_b):
    b = mod_b.reshape(DEPTH, 6, 1, D)
    out = pl.pallas_call(
        _mod_kernel,
        grid=(DEPTH, 6),
        in_specs=[pl.BlockSpec((N_SEG, D), lambda l, j: (0, 0)),
                  pl.BlockSpec((None, D, D), lambda l, j: (l, 0, j)),
                  pl.BlockSpec((None, None, 1, D), lambda l, j: (l, j, 0, 0))],
        out_specs=pl.BlockSpec((None, None, N_SEG, D), lambda l, j: (l, j, 0, 0)),
        out_shape=jax.ShapeDtypeStruct((DEPTH, 6, N_SEG, D), F32),
        compiler_params=_cparams(2),
        name="modulation",
    )(cond, mod_w, b)
    return out.reshape(DEPTH, 6, N_SEG, 1, D)


def _head_sumsq(y, bd_ref):
    return jnp.dot((y * y).astype(BF16), bd_ref[...], preferred_element_type=F32)


def _gelu_tanh(x):
    c = -2.0 * (2.0 / jnp.pi) ** 0.5 * LOG2E
    t = x * (c + (c * 0.044715) * (x * x))
    return x / (1.0 + jnp.exp2(t))


def _rope(y, cos, sin):
    lane = lax.broadcasted_iota(jnp.int32, (1, LANES), 1)
    first = (lane & 16) == 0
    partner = jnp.where(first, pltpu.roll(y, LANES - 16, 1), pltpu.roll(y, 16, 1))
    return y * cos + partner * sin


def _even_in_kernel(*refs, n_tiles, n_prompt_tiles, split_x):
    refs = list(refs)
    i = pl.program_id(0)
    t_proj = jnp.minimum(i, n_tiles - 1)
    x = _pick(n_prompt_tiles, refs.pop(0), refs.pop(0), t_proj) if split_x else refs.pop(0)[...]
    (sh_ref, sc_ref, g_ref, w_ref, sguw_ref, sgub_ref, sgun_ref, qn_ref, kn_ref, bdq_ref, bdk_ref,
     cos_ref, sin_ref, a_ref, q_ref, kd_ref, vd_ref, kf_ref, vf_ref, wbf_ref, h_ref, z0_ref,
     z1_ref) = refs
    tm = a_ref.shape[0]
    qw = B_HEADS * HEAD_DIM
    kw = B_KV_HEADS * HEAD_DIM
    c_q = 2 * A_WIDTH
    c_k = c_q + qw
    n_chunks = tm // CHUNK
    gch = A_WIDTH // A_GROUPS
    lane = lax.broadcasted_iota(jnp.int32, (1, LANES), 1)
    low = lane < HEAD_DIM

    @pl.when(i == 0)
    def _():
        wbf_ref[...] = w_ref[...].astype(BF16)
        z1_ref[...] = jnp.zeros_like(z1_ref)

    h_ref[...] = _mod_norm(x, g_ref[...], sh_ref[...], sc_ref[...]).astype(BF16)

    def step(z_new_ref, z_ref):
        def project(c0, c1):
            z_new_ref[:, c0:c1] = jnp.dot(h_ref[...], wbf_ref[:, c0:c1], preferred_element_type=F32)

        project(0, A_WIDTH)
        u = _gelu_tanh(z_ref[:, 0:A_WIDTH])
        project(A_WIDTH, c_q)
        v = _gelu_tanh(z_ref[:, A_WIDTH:c_q])
        mu = jnp.mean(v, axis=-1, keepdims=True)
        vc = v - mu
        var = jnp.mean(vc * vc, axis=-1, keepdims=True)
        vn = (vc * lax.rsqrt(var + EPS) * sgun_ref[...]).astype(BF16)
        for g in range(A_GROUPS):
            rhs = jnp.concatenate([vn[n * CHUNK:(n + 1) * CHUNK, g * gch:(g + 1) * gch]
                                   for n in range(n_chunks)], axis=1)
            mixed = jnp.dot(sguw_ref[g].astype(BF16), rhs, preferred_element_type=F32)
            bias = sgub_ref[g]
            for n in range(n_chunks):
                blk = (mixed[:, n * gch:(n + 1) * gch] + bias) * u[n * CHUNK:(n + 1) * CHUNK,
                                                                  g * gch:(g + 1) * gch]
                a_ref[n * CHUNK:(n + 1) * CHUNK, g * gch:(g + 1) * gch] = blk.astype(BF16)

        project(c_q, c_k)
        q = z_ref[:, c_q:c_k]
        q = q * lax.rsqrt(_head_sumsq(q, bdq_ref) * (1.0 / HEAD_DIM) + EPS) * qn_ref[...]
        cos = cos_ref[...]
        sin = sin_ref[...]
        for j in range(qw // LANES):
            qj = _rope(q[:, j * LANES:(j + 1) * LANES], cos, sin) * QSCALE
            q_ref[:, j * LANES:(j + 1) * LANES] = qj.astype(BF16)

        project(c_k, c_k + 2 * kw)
        k = z_ref[:, c_k:c_k + kw]
        k = k * lax.rsqrt(_head_sumsq(k, bdk_ref) * (1.0 / HEAD_DIM) + EPS) * kn_ref[...]
        kf_ref[...] = k
        vals = z_ref[:, c_k + kw:c_k + 2 * kw]
        vf_ref[...] = vals
        for src, dst in ((_rope(k, cos, sin), kd_ref), (vals, vd_ref)):
            sw = pltpu.roll(src, HEAD_DIM, 1)
            dst[:, 0:LANES] = jnp.where(low, src, sw).astype(BF16)
            dst[:, LANES:2 * LANES] = jnp.where(low, sw, src).astype(BF16)

    @pl.when(i % 2 == 0)
    def _():
        step(z0_ref, z1_ref)

    @pl.when(i % 2 == 1)
    def _():
        step(z1_ref, z0_ref)


def _even_in(xs, mods, layer, e, T, g_pre, w_in, sgu_w, sgu_b, sgu_norm, q_norm, k_norm,
             bdq, bdk, cos, sin):
    tm = TM_PROJ
    n_in = w_in.shape[2]
    n_tiles = T // tm
    tiles_per_seg = SEG // tm
    n_prompt_tiles = n_tiles // 2
    split_x = len(xs) == 2
    proj = lambda i: jnp.minimum(i, n_tiles - 1)
    post = lambda i: jnp.maximum(i - 1, 0)
    tab_spec = pl.BlockSpec((None, tm, LANES),
                            lambda i: (post(i) // n_prompt_tiles, post(i) % tiles_per_seg, 0))
    row = lambda w, tile: pl.BlockSpec((tm, w), lambda i: (tile(i), 0))
    x_specs = (_split_specs(tm, D, n_prompt_tiles, tile=proj) if split_x else [row(D, proj)])
    qw = B_HEADS * HEAD_DIM
    return pl.pallas_call(
        functools.partial(_even_in_kernel, n_tiles=n_tiles, n_prompt_tiles=n_prompt_tiles,
                          split_x=split_x),
        grid=(n_tiles + 1,),
        in_specs=x_specs + [
            _mod_spec(layer, 0, tm, proj), _mod_spec(layer, 1, tm, proj),
            _const_spec((1, D), layer),
            _weight_spec((D, n_in), e), _const_spec((A_GROUPS, CHUNK, CHUNK), e),
            _const_spec((A_GROUPS, CHUNK, 1), e), _const_spec((1, A_WIDTH), e),
            _const_spec((1, qw), e), _const_spec((1, LANES), e),
            _const_spec((qw, qw)), _const_spec((LANES, LANES)), tab_spec, tab_spec],
        out_specs=[row(A_WIDTH, post), row(qw, post), row(2 * LANES, post), row(2 * LANES, post),
                   row(LANES, post), row(LANES, post)],
        out_shape=[jax.ShapeDtypeStruct((T, A_WIDTH), BF16),
                   jax.ShapeDtypeStruct((T, qw), BF16),
                   jax.ShapeDtypeStruct((T, 2 * LANES), BF16),
                   jax.ShapeDtypeStruct((T, 2 * LANES), BF16),
                   jax.ShapeDtypeStruct((T, LANES), F32),
                   jax.ShapeDtypeStruct((T, LANES), F32)],
        scratch_shapes=[pltpu.VMEM((D, n_in), BF16), pltpu.VMEM((tm, D), BF16),
                        pltpu.VMEM((tm, n_in), F32), pltpu.VMEM((tm, n_in), F32)],
        compiler_params=_cparams(1),
        name="even_in_proj",
    )(*xs, mods, mods, g_pre, w_in, sgu_w, sgu_b, sgu_norm, q_norm, k_norm, bdq, bdk, cos, sin)


def _odd_in_kernel(x_ref, sh_ref, sc_ref, g_ref, w_ref, *rest, n_prompt_tiles, slot, all_slots):
    q_ref, k_ref, v_ref, kf_ref, vf_ref, wbf_ref = rest[-6:]
    i = pl.program_id(0)
    hw = C_HEADS * HEAD_DIM

    @pl.when(i == 0)
    def _():
        for c in range(3):
            wbf_ref[:, c * hw:(c + 1) * hw] = w_ref[:, c * hw:(c + 1) * hw].astype(BF16)

    h = _mod_norm(x_ref[...], g_ref[...], sh_ref[...], sc_ref[...]).astype(BF16)
    q = jnp.dot(h, wbf_ref[:, 0:hw], preferred_element_type=F32)
    q_ref[...] = (q * QSCALE).astype(BF16)
    k = jnp.dot(h, wbf_ref[:, hw:2 * hw], preferred_element_type=F32)
    k_ref[...] = k.astype(BF16)
    v = jnp.dot(h, wbf_ref[:, 2 * hw:3 * hw], preferred_element_type=F32)
    v_ref[...] = v.astype(BF16)

    @pl.when(i < n_prompt_tiles)
    def _():
        for src, dst in ((k, kf_ref), (v, vf_ref)):
            if all_slots:
                val = src.reshape((dst.shape[0],) + dst.shape[2:])
                for s in range(dst.shape[1]):
                    dst[:, s] = val if s == slot else jnp.zeros_like(val)
            else:
                dst[...] = src.reshape(dst.shape)


def _odd_in(x, mods, layer, o, n_odd, seq_p, g_pre, w_in, caches):
    T = x.shape[0]
    tm = TM_PROJ
    hw = C_HEADS * HEAD_DIM
    n_prompt_tiles = T // 2 // tm
    bt = tm // seq_p
    row = pl.BlockSpec((tm, hw), lambda i: (i, 0))
    all_slots = not caches
    if all_slots:
        crow = pl.BlockSpec((bt, n_odd, seq_p, hw),
                            lambda i: (jnp.minimum(i, n_prompt_tiles - 1), 0, 0, 0))
    else:
        crow = pl.BlockSpec((bt, None, seq_p, hw),
                            lambda i: (jnp.minimum(i, n_prompt_tiles - 1), o, 0, 0))
    cshape = jax.ShapeDtypeStruct((T // 2 // seq_p, n_odd, seq_p, hw), F32)
    n_in = 5
    return pl.pallas_call(
        functools.partial(_odd_in_kernel, n_prompt_tiles=n_prompt_tiles, slot=o,
                          all_slots=all_slots),
        grid=(T // tm,),
        in_specs=[pl.BlockSpec((tm, D), lambda i: (i, 0)), _mod_spec(layer, 0, tm),
                  _mod_spec(layer, 1, tm), _const_spec((1, D), layer),
                  _weight_spec((D, 3 * hw), o)]
                 + [pl.BlockSpec(memory_space=pl.ANY)] * len(caches),
        out_specs=[row, row, row, crow, crow],
        out_shape=[jax.ShapeDtypeStruct((T, hw), BF16)] * 3 + [cshape, cshape],
        input_output_aliases={n_in + j: 3 + j for j in range(len(caches))},
        scratch_shapes=[pltpu.VMEM((D, 3 * hw), BF16)],
        compiler_params=_cparams(1),
        name="odd_in_proj",
    )(x, mods, mods, g_pre, w_in, *caches)


def _softmax(pieces):
    m = None
    for s, _ in pieces:
        ms = jnp.max(s, axis=-1, keepdims=True)
        m = ms if m is None else jnp.maximum(m, ms)
    den = None
    probs = []
    for s, _ in pieces:
        p = jnp.exp2(s - m)
        ls = jnp.sum(p, axis=-1, keepdims=True)
        den = ls if den is None else den + ls
        probs.append(p.astype(BF16))
    return probs, [val for _, val in pieces], den


def _weighted_values(probs, vals, den):
    acc = None
    for p, val in zip(probs, vals):
        o = jnp.dot(p, val, preferred_element_type=F32)
        acc = o if acc is None else acc + o
    return acc / den


def _pipelined_units(units, scores, store):
    n = len(units)
    pending_scores = scores(units[0])
    pending_probs = None
    for idx in range(n + 1):
        upcoming = scores(units[idx + 1]) if idx + 1 < n else None
        probs = _softmax(pending_scores) if idx < n else None
        if pending_probs is not None:
            store(units[idx - 1], _weighted_values(*pending_probs))
        pending_scores, pending_probs = upcoming, probs


def _attn_kernel(*refs, has_ctx, seq, tq, pairs_per_kv):
    if has_ctx:
        q_ref, k_ref, v_ref, kc_ref, vc_ref, o_ref = refs
    else:
        q_ref, k_ref, v_ref, o_ref = refs
    lane = lax.broadcasted_iota(jnp.int32, (1, LANES), 1)
    masks = [lane < HEAD_DIM, lane >= HEAD_DIM]
    if has_ctx:
        kv_head = pl.program_id(1)
        sel = jnp.where(masks[0], 0, 1) == kv_head
        kc = jnp.where(sel, kc_ref[...], 0.0)
        kc = (kc + pltpu.roll(kc, HEAD_DIM, 1)).astype(BF16)
        vc = jnp.where(sel, vc_ref[...], 0.0)
        vc = (vc + pltpu.roll(vc, HEAD_DIM, 1)).astype(BF16)
        vch = [jnp.where(mh, vc, jnp.zeros_like(vc)) for mh in masks]
    nt = (((1,), (1,)), ((), ()))
    groups = [(r0, j) for r0 in range(0, q_ref.shape[0], seq) for j in range(q_ref.shape[1] // LANES)]
    units = [(r0, j, t, hh) for r0, j in groups for t in range(r0 // tq, (r0 + seq) // tq)
             for hh in range(len(masks))]
    operands = {}

    def group_operands(r0, j):
        if (r0, j) not in operands:
            kcols = slice((j // pairs_per_kv) * LANES, (j // pairs_per_kv + 1) * LANES)
            v = v_ref[r0:r0 + seq, kcols]
            operands[r0, j] = (k_ref[r0:r0 + seq, kcols],
                               [jnp.where(mh, v, jnp.zeros_like(v)) for mh in masks])
        return operands[r0, j]

    def scores(unit):
        r0, j, t, hh = unit
        k, vh = group_operands(r0, j)
        q = q_ref[t * tq:(t + 1) * tq, j * LANES:(j + 1) * LANES]
        qm = jnp.where(masks[hh], q, jnp.zeros_like(q))
        pieces = [(lax.dot_general(qm, k, nt, preferred_element_type=F32), vh[hh])]
        if has_ctx:
            pieces.append((lax.dot_general(qm, kc, nt, preferred_element_type=F32), vch[hh]))
        return pieces

    partial = {}

    def store(unit, o):
        r0, j, t, hh = unit
        if hh == 0:
            partial[r0, j, t] = o
        else:
            out = partial.pop((r0, j, t)) + o
            o_ref[t * tq:(t + 1) * tq, j * LANES:(j + 1) * LANES] = out.astype(BF16)

    _pipelined_units(units, scores, store)


def _attention(q, k, v, *, n_batch, seq, row0, pairs_per_step, pairs_per_kv, batches_per_step=1,
               ctx=None):
    rows = batches_per_step * seq
    assert row0 % rows == 0 and n_batch % batches_per_step == 0
    blk0 = row0 // rows
    n_groups = q.shape[1] // (pairs_per_step * LANES)
    qw = pairs_per_step * LANES
    kw = qw // pairs_per_kv
    in_specs = [pl.BlockSpec((rows, qw), lambda b, g: (blk0 + b, g)),
                pl.BlockSpec((rows, kw), lambda b, g: (blk0 + b, g)),
                pl.BlockSpec((rows, kw), lambda b, g: (blk0 + b, g))]
    args = [q, k, v]
    if ctx is not None:
        assert kw == LANES and batches_per_step == 1
        kc, vc, e = ctx
        sc = kc.shape[2]
        cspec = pl.BlockSpec((None, None, sc, LANES), lambda b, g: (b, e, 0, 0))
        in_specs += [cspec, cspec]
        args += [kc, vc]
    return pl.pallas_call(
        functools.partial(_attn_kernel, has_ctx=ctx is not None, seq=seq, tq=min(TQ_ATTN, seq),
                          pairs_per_kv=pairs_per_kv),
        grid=(n_batch // batches_per_step, n_groups),
        in_specs=in_specs,
        out_specs=pl.BlockSpec((rows, qw), lambda b, g: (b, g)),
        out_shape=jax.ShapeDtypeStruct((n_batch * seq, q.shape[1]), BF16),
        compiler_params=_cparams(2),
        name="attention",
    )(*args)


def _na_row_start(r):
    rows = SEG // GRID_W
    return min(max(r - NA_ROWS // 2, 0), rows - NA_ROWS)


def _na_windows():
    rows = SEG // GRID_W
    windows = []
    for qb in range(rows // NA_QROWS):
        lo = _na_row_start(qb * NA_QROWS)
        hi = _na_row_start(qb * NA_QROWS + NA_QROWS - 1) + NA_ROWS
        n = hi - lo + (hi - lo) % 2
        ws = min(lo, rows - n)
        assert ws <= lo and hi <= ws + n <= rows and n <= NA_KROWS
        windows.append((ws, n))
    return windows


def _na_kernel(q_ref, k_ref, v_ref, kc_ref, vc_ref, r_ref, o_ref, bias_ref, tab_ref, *, windows):
    lane = lax.broadcasted_iota(jnp.int32, (1, LANES), 1)
    low = lane < HEAD_DIM

    @pl.when(pl.program_id(1) == 0)
    def _():
        qcol = lax.broadcasted_iota(jnp.int32, (GRID_W, LANES), 0)
        kcol = lax.broadcasted_iota(jnp.int32, (GRID_W, LANES), 1) & (GRID_W - 1)
        start = jnp.clip(qcol - NA_COLS // 2, 0, GRID_W - NA_COLS)
        inside = (kcol >= start) & (kcol < start + NA_COLS)
        for hh in range(LANES // HEAD_DIM):
            for d in range(2 * NA_ROWS - 1):
                base = jnp.broadcast_to(r_ref[hh, d:d + 1, :], (GRID_W, LANES))
                lo_t = pltpu.roll(base, LANES - (NA_COLS - 1), 1, stride=1, stride_axis=0)
                hi_t = pltpu.roll(base, GRID_W - (NA_COLS - 1), 1, stride=1, stride_axis=0)
                tab_ref[hh, d] = jnp.where(inside, jnp.where(low, lo_t, hi_t) * LOG2E, NEG)
        neg = jnp.full((GRID_W, LANES), NEG, F32)
        for hh in range(LANES // HEAD_DIM):
            for qb, (ws, nrows) in enumerate(windows):
                for i in range(NA_QROWS):
                    r = qb * NA_QROWS + i
                    rs = _na_row_start(r)
                    for jp in range(nrows // 2):
                        kr = ws + 2 * jp
                        ok = [rs <= kr + d < rs + NA_ROWS for d in (0, 1)]
                        if not any(ok):
                            blk = neg
                        else:
                            t0 = tab_ref[hh, kr - r + NA_ROWS - 1] if ok[0] else neg
                            t1 = tab_ref[hh, kr + 1 - r + NA_ROWS - 1] if ok[1] else neg
                            blk = jnp.where(low, t0, t1)
                        bias_ref[hh, qb, i * GRID_W:(i + 1) * GRID_W,
                                 jp * LANES:(jp + 1) * LANES] = blk

    masks = [low, jnp.logical_not(low)]
    n_batch = kc_ref.shape[0]
    kc = [kc_ref[bb].astype(BF16) for bb in range(n_batch)]
    vch = []
    for bb in range(n_batch):
        vc = vc_ref[bb].astype(BF16)
        vch.append([jnp.where(mh, vc, jnp.zeros_like(vc)) for mh in masks])
    nt = (((1,), (1,)), ((), ()))
    nq = NA_QROWS * GRID_W
    units = [(bb, qb, hh) for bb in range(n_batch) for qb in range(len(windows))
             for hh in range(len(masks))]

    def scores(unit):
        bb, qb, hh = unit
        ws, nrows = windows[qb]
        nk = nrows * GRID_W
        k0 = bb * SEG + ws * GRID_W
        q = q_ref[bb * SEG + qb * nq:bb * SEG + (qb + 1) * nq, :]
        kw = k_ref[k0:k0 + nk, :]
        vw = v_ref[k0:k0 + nk, :]
        qm = jnp.where(masks[hh], q, jnp.zeros_like(q))
        s_win = (lax.dot_general(qm, kw, nt, preferred_element_type=F32)
                 + bias_ref[hh, qb, :, 0:nk])
        s_ctx = lax.dot_general(qm, kc[bb], nt, preferred_element_type=F32)
        return [(s_win, jnp.where(masks[hh], vw, jnp.zeros_like(vw))), (s_ctx, vch[bb][hh])]

    partial = {}

    def store(unit, o):
        bb, qb, hh = unit
        if hh == 0:
            partial[bb, qb] = o
        else:
            r0 = bb * SEG + qb * nq
            o_ref[r0:r0 + nq, :] = (partial.pop((bb, qb)) + o).astype(BF16)

    _pipelined_units(units, scores, store)


def _na_attention(q, k, v, kc, vc, o_idx, rpb, *, n_batch, row0):
    bps = NA_BATCHES_PER_STEP
    rows = bps * SEG
    assert row0 % rows == 0 and n_batch % bps == 0
    blk0 = row0 // rows
    n_pairs = q.shape[1] // LANES
    sc = kc.shape[2]
    windows = _na_windows()
    heads = LANES // HEAD_DIM
    nr = 2 * NA_ROWS - 1
    qkv = pl.BlockSpec((rows, LANES), lambda p, b: (blk0 + b, p))
    cspec = pl.BlockSpec((bps, None, sc, LANES), lambda p, b: (b, o_idx, 0, p))
    rspec = pl.BlockSpec((None, heads, nr, LANES), lambda p, b: (o_idx, p, 0, 0))
    return pl.pallas_call(
        functools.partial(_na_kernel, windows=windows),
        grid=(n_pairs, n_batch // bps),
        in_specs=[qkv, qkv, qkv, cspec, cspec, rspec],
        out_specs=pl.BlockSpec((rows, LANES), lambda p, b: (b, p)),
        out_shape=jax.ShapeDtypeStruct((n_batch * SEG, n_pairs * LANES), BF16),
        scratch_shapes=[pltpu.VMEM((heads, len(windows), NA_QROWS * GRID_W, NA_KROWS * GRID_W), F32),
                        pltpu.VMEM((heads, nr, GRID_W, LANES), F32)],
        compiler_params=_cparams(2),
        name="na_attention",
    )(q, k, v, kc, vc, rpb)


def _out_kernel(*refs, n_prompt_tiles, split_x, has_a):
    refs = list(refs)
    x_refs = [refs.pop(0), refs.pop(0)] if split_x else [refs.pop(0)]
    gt_ref, g_ref, fsh_ref, fsc_ref, gf_ref, w_ref = refs[:6]
    a_ref = refs[6] if has_a else None
    bp_ref, bs_ref, o_ref, h_ref, wbf_ref = refs[6 + has_a:]
    prompt = pl.program_id(0) < n_prompt_tiles

    @pl.when(pl.program_id(0) == 0)
    def _():
        wbf_ref[...] = w_ref[...].astype(BF16)

    def project(rows):
        b = jnp.where(prompt, bp_ref[rows, :], bs_ref[rows, :])
        if has_a:
            half = a_ref.shape[1]
            return (jnp.dot(a_ref[rows, :], wbf_ref[0:half, :], preferred_element_type=F32)
                    + jnp.dot(b, wbf_ref[half:2 * half, :], preferred_element_type=F32))
        return jnp.dot(b, wbf_ref[...], preferred_element_type=F32)

    def finish(rows, y):
        x = jnp.where(prompt, x_refs[0][rows, :], x_refs[1][rows, :]) if split_x else x_refs[0][rows, :]
        x1 = _gated_residual(x, y, g_ref[...], gt_ref[...])
        o_ref[rows, :] = x1
        h_ref[rows, :] = _mod_norm(x1, gf_ref[...], fsh_ref[...], fsc_ref[...]).astype(BF16)

    tm = o_ref.shape[0]
    blocks = [slice(r, r + OUT_ROWS) for r in range(0, tm, OUT_ROWS)]
    pending = project(blocks[0])
    for idx, rows in enumerate(blocks):
        upcoming = project(blocks[idx + 1]) if idx + 1 < len(blocks) else None
        finish(rows, pending)
        pending = upcoming


def _out_proj(xs, a, bp, bs, mods, layer, li, T, g_post, g_ffn_pre, w_out):
    tm = TM_OUT
    n_prompt_tiles = T // 2 // tm
    split_x = len(xs) == 2
    has_a = a is not None
    row = lambda w: pl.BlockSpec((tm, w), lambda i: (i, 0))
    x_specs = _split_specs(tm, D, n_prompt_tiles) if split_x else [row(D)]
    a_specs = [row(a.shape[1])] if has_a else []
    return pl.pallas_call(
        functools.partial(_out_kernel, n_prompt_tiles=n_prompt_tiles, split_x=split_x, has_a=has_a),
        grid=(T // tm,),
        in_specs=x_specs + [_mod_spec(layer, 2, tm), _const_spec((1, D), layer),
                            _mod_spec(layer, 3, tm), _mod_spec(layer, 4, tm),
                            _const_spec((1, D), layer), _weight_spec((D, D), li)] + a_specs
                 + _split_specs(tm, bp.shape[1], n_prompt_tiles),
        out_specs=[row(D), row(D)],
        out_shape=[jax.ShapeDtypeStruct((T, D), F32), jax.ShapeDtypeStruct((T, D), BF16)],
        scratch_shapes=[pltpu.VMEM((D, D), BF16)],
        compiler_params=_cparams(1),
        name="out_proj",
    )(*xs, mods, g_post, mods, mods, g_ffn_pre, w_out, *([a] if has_a else []), bp, bs)


def _ffn_kernel(x_ref, h_ref, gt_ref, gpost_ref, win_ref, wout_ref, *rest,
                layer, n_prompt_tiles, split_out):
    out_refs = rest[:-9]
    wg_buf, wu_buf, wo_buf, wg_res, wu_res, wo_res, act_buf, acc_ref, sem = rest[-9:]
    nf, _, tf = wg_res.shape
    first_tile = pl.program_id(0) == 0

    def aligned(f):
        return f * tf if isinstance(f, int) else pl.multiple_of(f * tf, tf)

    def in_copies(f, slot):
        col = aligned(f)
        copies = []
        for n, (buf, c0) in enumerate(((wg_buf, col), (wu_buf, D_FF + col))):
            for r in range(IN_COPY_SPLIT):
                rows = pl.ds(r * (D // IN_COPY_SPLIT), D // IN_COPY_SPLIT)
                copies.append(pltpu.make_async_copy(win_ref.at[layer, rows, pl.ds(c0, tf)],
                                                    buf.at[slot, rows, :],
                                                    sem.at[n * IN_COPY_SPLIT + r, slot]))
        return copies

    def out_copy(f, slot):
        row = aligned(f)
        return pltpu.make_async_copy(wout_ref.at[layer, pl.ds(row, tf), :], wo_buf.at[slot],
                                     sem.at[2 * IN_COPY_SPLIT, slot])

    def fetch(f, slot):
        for c in in_copies(f, slot):
            c.wait()
        out_copy(f, slot).wait()

        nxt = min(f + 1, nf - 1) if isinstance(f, int) else jnp.minimum(f + 1, nf - 1)

        @pl.when(jnp.asarray(f + 1 < nf))
        def _():
            for c in in_copies(nxt, 1 - slot):
                c.start()
            out_copy(nxt, 1 - slot).start()

        wg_res[f] = wg_buf[slot].astype(BF16)
        wu_res[f] = wu_buf[slot].astype(BF16)
        wo_res[f] = wo_buf[slot].astype(BF16)

    def hidden(f, slot):
        h = h_ref[...]
        g = jnp.dot(h, wg_res[f], preferred_element_type=F32)
        u = jnp.dot(h, wu_res[f], preferred_element_type=F32)
        act_buf[slot] = (g * jax.nn.sigmoid(g) * u).astype(BF16)

    def project(f, slot):
        acc_ref[...] += jnp.dot(act_buf[slot], wo_res[f], preferred_element_type=F32)

    assert nf % 2 == 1

    def chunks(streaming):
        acc_ref[...] = jnp.zeros_like(acc_ref)
        if streaming:
            fetch(0, 0)
        hidden(0, 0)

        def pair(k, carry):
            for f, slot in ((2 * k + 1, 1), (2 * k + 2, 0)):
                if streaming:
                    fetch(f, slot)
                hidden(f, slot)
                project(f - 1, 1 - slot)
            return carry

        lax.fori_loop(0, (nf - 1) // 2, pair, 0)
        project(nf - 1, 0)

    @pl.when(first_tile)
    def _():
        for c in in_copies(0, 0):
            c.start()
        out_copy(0, 0).start()
        chunks(streaming=True)

    @pl.when(jnp.logical_not(first_tile))
    def _():
        chunks(streaming=False)

    def result():
        return _gated_residual(x_ref[...], acc_ref[...], gpost_ref[...], gt_ref[...])

    if split_out:
        i = pl.program_id(0)

        @pl.when(i < n_prompt_tiles)
        def _():
            out_refs[0][...] = result()

        @pl.when(i >= n_prompt_tiles)
        def _():
            out_refs[1][...] = result()
    else:
        out_refs[0][...] = result()


def _ffn(x, h, mods, layer, g_post, w_in, w_out, split_out):
    T = x.shape[0]
    tm, tf = TM_FFN, TF_FFN
    nf = D_FF // tf
    n_prompt_tiles = T // 2 // tm
    xrow = pl.BlockSpec((tm, D), lambda i: (i, 0))
    if split_out:
        out_specs = _split_specs(tm, D, n_prompt_tiles)
        out_shape = [jax.ShapeDtypeStruct((T // 2, D), F32)] * 2
    else:
        out_specs = [xrow]
        out_shape = [jax.ShapeDtypeStruct((T, D), F32)]
    hbm = pl.BlockSpec(memory_space=pl.ANY)
    return pl.pallas_call(
        functools.partial(_ffn_kernel, layer=layer, n_prompt_tiles=n_prompt_tiles,
                          split_out=split_out),
        grid=(T // tm,),
        in_specs=[xrow, xrow, _mod_spec(layer, 5, tm), _const_spec((1, D), layer), hbm, hbm],
        out_specs=out_specs,
        out_shape=out_shape,
        scratch_shapes=[pltpu.VMEM((2, D, tf), F32), pltpu.VMEM((2, D, tf), F32),
                        pltpu.VMEM((2, tf, D), F32), pltpu.VMEM((nf, D, tf), BF16),
                        pltpu.VMEM((nf, D, tf), BF16), pltpu.VMEM((nf, tf, D), BF16),
                        pltpu.VMEM((2, tm, tf), BF16), pltpu.VMEM((tm, D), F32),
                        pltpu.SemaphoreType.DMA((2 * IN_COPY_SPLIT + 1, 2))],
        compiler_params=_cparams(1, VMEM_LIMIT_FFN),
        name="ffn",
    )(x, h, mods, g_post, w_in, w_out)


def _rope_tables():
    t = jnp.arange(SEG)
    nf = HEAD_DIM // 4
    freqs = ROPE_BASE ** (-jnp.arange(nf, dtype=F32) / nf)

    def cs(pos):
        ang = pos.astype(F32)[:, None] * freqs[None, :]
        return jnp.cos(ang), jnp.sin(ang)

    cr, sr = cs(t // GRID_W)
    cc, sn = cs(t % GRID_W)
    cos = jnp.concatenate([cr, cr, cc, cc], axis=1)
    sin = jnp.concatenate([-sr, sr, -sn, sn], axis=1)
    reps = LANES // HEAD_DIM
    cos = jnp.tile(cos, (1, reps))
    sin = jnp.tile(sin, (1, reps))
    return (jnp.stack([jnp.ones_like(cos), cos]), jnp.stack([jnp.zeros_like(sin), sin]))


def _block_diag_ones(n):
    i = jnp.arange(n) // HEAD_DIM
    return (i[:, None] == i[None, :]).astype(BF16)


def kernel(x_prompt, x_sample, cache_attn_k, cache_attn_v, cache_na_k, cache_na_v, c, c_ctx,
           mod_w, mod_b, norm_mix_pre, norm_mix_post, norm_ffn_pre, norm_ffn_post,
           even_w_in, even_w_out, sgu_w, sgu_b, sgu_norm, q_norm, k_norm,
           odd_w_in, odd_w_out, na_rpb, ffn_w_in, ffn_w_out):
    nb_p, seq_p, _ = x_prompt.shape
    nb_s, seq_s, _ = x_sample.shape
    tp = nb_p * seq_p
    T = 2 * tp
    assert seq_s == SEG and tp == nb_s * seq_s and tp % SEG == 0

    cond = jnp.concatenate([jnp.broadcast_to(c_ctx[None, :], (tp // SEG, D)), c], axis=0)
    mods = _modulation(cond, mod_w, mod_b)

    cos, sin = _rope_tables()
    qw = B_HEADS * HEAD_DIM
    bdq = _block_diag_ones(qw)
    bdk = _block_diag_ones(LANES)
    past = cache_attn_k.shape[2]
    ctx_ak = cache_attn_k.reshape(nb_s, -1, past, B_KV_HEADS * HEAD_DIM)
    ctx_av = cache_attn_v.reshape(nb_s, -1, past, B_KV_HEADS * HEAD_DIM)
    ctx_nk = cache_na_k.reshape(nb_s, -1, past, C_HEADS * HEAD_DIM)
    ctx_nv = cache_na_v.reshape(nb_s, -1, past, C_HEADS * HEAD_DIM)
    g_mix_pre, g_mix_post, g_ffn_pre, g_ffn_post = (
        a.reshape(DEPTH, 1, D) for a in (norm_mix_pre, norm_mix_post, norm_ffn_pre, norm_ffn_post))
    n_even = even_w_in.shape[0]
    sgu_b3 = sgu_b.reshape(n_even, A_GROUPS, CHUNK, 1)
    sgu_n3 = sgu_norm.reshape(n_even, 1, A_WIDTH)
    qn3 = jnp.tile(q_norm, (1, B_HEADS)).reshape(n_even, 1, qw)
    kn3 = jnp.tile(k_norm, (1, B_KV_HEADS)).reshape(n_even, 1, LANES)
    rpb = jnp.pad(na_rpb, ((0, 0), (0, 0), (0, 0), (0, LANES - na_rpb.shape[3])))

    xs = (x_prompt.reshape(tp, D), x_sample.reshape(nb_s * seq_s, D))
    n_odd = odd_w_in.shape[0]
    attn_k, attn_v, na_cache = [], [], ()
    for l in range(DEPTH):
        if l % 2 == 0:
            e = l // 2
            a_out, q, kd, vd, kf, vf = _even_in(xs, mods, l, e, T, g_mix_pre, even_w_in, sgu_w,
                                                sgu_b3, sgu_n3, qn3, kn3, bdq, bdk, cos, sin)
            n_pairs = qw // LANES
            pairs_per_kv = n_pairs // B_KV_HEADS
            mix_p = _attention(q, kd, vd, n_batch=nb_p, seq=seq_p, row0=0, pairs_per_step=n_pairs,
                               pairs_per_kv=pairs_per_kv, batches_per_step=PROMPT_BATCHES_PER_STEP)
            mix_s = _attention(q, kd, vd, n_batch=nb_s, seq=seq_s, row0=tp,
                               pairs_per_step=pairs_per_kv, pairs_per_kv=pairs_per_kv,
                               ctx=(ctx_ak, ctx_av, e))
            attn_k.append(kf[:tp].reshape(nb_p, seq_p, B_KV_HEADS, HEAD_DIM))
            attn_v.append(vf[:tp].reshape(nb_p, seq_p, B_KV_HEADS, HEAD_DIM))
            w_out, li = even_w_out, e
        else:
            o = l // 2
            a_out = None
            q, k, v, *na_cache = _odd_in(xs[0], mods, l, o, n_odd, seq_p, g_mix_pre, odd_w_in,
                                         tuple(na_cache))
            mix_p = _attention(q, k, v, n_batch=nb_p, seq=seq_p, row0=0,
                               pairs_per_step=PAIRS_PER_STEP, pairs_per_kv=1,
                               batches_per_step=PROMPT_BATCHES_PER_STEP)
            mix_s = _na_attention(q, k, v, ctx_nk, ctx_nv, o, rpb, n_batch=nb_s, row0=tp)
            w_out, li = odd_w_out, o
        x, h = _out_proj(xs, a_out, mix_p, mix_s, mods, l, li, T, g_mix_post, g_ffn_pre, w_out)
        xs = tuple(_ffn(x, h, mods, l, g_ffn_post, ffn_w_in, ffn_w_out,
                        split_out=(l == DEPTH - 1)))

    y_prompt = xs[0].reshape(nb_p, seq_p, D)
    y_sample = xs[1].reshape(nb_s, seq_s, D)
    new_na_k, new_na_v = (a.reshape(nb_p, n_odd, seq_p, C_HEADS, HEAD_DIM) for a in na_cache)
    return (y_prompt, y_sample, jnp.stack(attn_k, axis=1), jnp.stack(attn_v, axis=1),
            new_na_k, new_na_v)
```

```python
import functools

import jax
import jax.numpy as jnp
from jax import lax
from jax.experimental import pallas as pl
from jax.experimental.pallas import tpu as pltpu

F32 = jnp.float32
BF16 = jnp.bfloat16

D = 1024
DEPTH = 4
HEAD_DIM = 64
GRID_W = 64
CHUNK = 128
A_WIDTH = D // 2
A_GROUPS = 4
B_HEADS = 8
B_KV_HEADS = 2
C_HEADS = 16
NA_ROWS = 8
NA_COLS = 16
D_FF = 2816
ROPE_BASE = 10000.0
EPS = 1e-6
NEG = -1e30
SEG = 1024
N_SEG = 8
LANES = 128
LOG2E = 1.4426950408889634
QSCALE = HEAD_DIM ** -0.5 * LOG2E

MOD_CHUNKS = 3
TM_PROJ = 512
TM_OUT = 1024
OUT_ROWS = 256
TM_FFN = 1024
TF_FFN = 256
TQ_ATTN = 256
PAIRS_PER_STEP = 4
PROMPT_BATCHES_PER_STEP = 4
NA_QROWS = 4
NA_BATCHES_PER_STEP = 2
NA_KROWS = 12
VMEM_LIMIT = 56 * 1024 * 1024
VMEM_LIMIT_FFN = 62 * 1024 * 1024


def _cparams(n_axes, vmem_limit=VMEM_LIMIT):
    return pltpu.CompilerParams(dimension_semantics=("arbitrary",) * n_axes,
                                vmem_limit_bytes=vmem_limit)


def _rms(x):
    return x * lax.rsqrt(jnp.mean(x * x, axis=-1, keepdims=True) + EPS)


def _mod_norm(x, g, shift, scale):
    return _rms(x) * (g * (1.0 + scale)) + shift


def _gated_residual(x, y, g, gate):
    return x + _rms(y) * (gate * g)


def _const_spec(shape, *lead):
    block = (None,) * len(lead) + tuple(shape)
    return pl.BlockSpec(block, lambda *_: tuple(lead) + (0,) * len(shape))


def _weight_spec(shape, *lead):
    block = (None,) * len(lead) + tuple(shape)
    return pl.BlockSpec(block, lambda *_: tuple(lead) + (0,) * len(shape),
                        pipeline_mode=pl.Buffered(1))


def _same_tile(i):
    return i


def _mod_spec(layer, j, tm, tile=_same_tile):
    return pl.BlockSpec((None, None, None, 1, D),
                        lambda i, *_: (layer, j, (tile(i) * tm) // SEG, 0, 0))


def _split_specs(tm, width, n_prompt_tiles, tile=_same_tile):
    return [pl.BlockSpec((tm, width),
                         lambda i, *_: (jnp.minimum(tile(i), n_prompt_tiles - 1), 0)),
            pl.BlockSpec((tm, width),
                         lambda i, *_: (jnp.maximum(tile(i) - n_prompt_tiles, 0), 0))]


def _pick(n_prompt_tiles, p_ref, s_ref, tile=None):
    tile = pl.program_id(0) if tile is None else tile
    return jnp.where(tile < n_prompt_tiles, p_ref[...], s_ref[...])


def _mod_kernel(cond_ref, w_ref, b_ref, o_ref):
    s = cond_ref[...]
    s = (s * jax.nn.sigmoid(s)).astype(BF16)
    for c in range(o_ref.shape[0]):
        o_ref[c] = jnp.dot(s, w_ref[:, c * D:(c + 1) * D].astype(BF16),
                           preferred_element_type=F32) + b_ref[c]


def _modulation(cond, mod_w, mod_b):
    n_vec = mod_w.shape[2] // D
    b = mod_b.reshape(DEPTH, n_vec, 1, D)
    out = pl.pallas_call(
        _mod_kernel,
        grid=(DEPTH, n_vec // MOD_CHUNKS),
        in_specs=[pl.BlockSpec((N_SEG, D), lambda l, j: (0, 0)),
                  pl.BlockSpec((None, D, MOD_CHUNKS * D), lambda l, j: (l, 0, j)),
                  pl.BlockSpec((None, MOD_CHUNKS, 1, D), lambda l, j: (l, j, 0, 0))],
        out_specs=pl.BlockSpec((None, MOD_CHUNKS, N_SEG, D), lambda l, j: (l, j, 0, 0)),
        out_shape=jax.ShapeDtypeStruct((DEPTH, n_vec, N_SEG, D), F32),
        compiler_params=_cparams(2),
        name="modulation",
    )(cond, mod_w, b)
    return out.reshape(DEPTH, n_vec, N_SEG, 1, D)


def _head_sumsq(y, bd_ref):
    return jnp.dot((y * y).astype(BF16), bd_ref[...], preferred_element_type=F32)


def _gelu_tanh(x):
    c = -2.0 * (2.0 / jnp.pi) ** 0.5 * LOG2E
    t = x * (c + (c * 0.044715) * (x * x))
    return x / (1.0 + jnp.exp2(t))


def _rope(y, cos, sin):
    lane = lax.broadcasted_iota(jnp.int32, (1, LANES), 1)
    first = (lane & 16) == 0
    partner = jnp.where(first, pltpu.roll(y, LANES - 16, 1), pltpu.roll(y, 16, 1))
    return y * cos + partner * sin


def _even_in_kernel(*refs, n_tiles, n_prompt_tiles, split_x):
    refs = list(refs)
    i = pl.program_id(0)
    t_proj = jnp.minimum(i, n_tiles - 1)
    x = _pick(n_prompt_tiles, refs.pop(0), refs.pop(0), t_proj) if split_x else refs.pop(0)[...]
    (sh_ref, sc_ref, g_ref, w_ref, sguw_ref, sgub_ref, sgun_ref, qn_ref, kn_ref, bdq_ref, bdk_ref,
     cos_ref, sin_ref, a_ref, q_ref, kd_ref, vd_ref, kf_ref, vf_ref, wbf_ref, h_ref, z0_ref,
     z1_ref) = refs
    tm = a_ref.shape[0]
    qw = B_HEADS * HEAD_DIM
    kw = B_KV_HEADS * HEAD_DIM
    c_q = 2 * A_WIDTH
    c_k = c_q + qw
    n_chunks = tm // CHUNK
    gch = A_WIDTH // A_GROUPS
    lane = lax.broadcasted_iota(jnp.int32, (1, LANES), 1)
    low = lane < HEAD_DIM

    @pl.when(i == 0)
    def _():
        wbf_ref[...] = w_ref[...].astype(BF16)
        z1_ref[...] = jnp.zeros_like(z1_ref)

    h_ref[...] = _mod_norm(x, g_ref[...], sh_ref[...], sc_ref[...]).astype(BF16)

    def step(z_new_ref, z_ref):
        def project(c0, c1):
            z_new_ref[:, c0:c1] = jnp.dot(h_ref[...], wbf_ref[:, c0:c1], preferred_element_type=F32)

        project(0, A_WIDTH)
        u = _gelu_tanh(z_ref[:, 0:A_WIDTH])
        project(A_WIDTH, c_q)
        v = _gelu_tanh(z_ref[:, A_WIDTH:c_q])
        mu = jnp.mean(v, axis=-1, keepdims=True)
        vc = v - mu
        var = jnp.mean(vc * vc, axis=-1, keepdims=True)
        vn = (vc * lax.rsqrt(var + EPS) * sgun_ref[...]).astype(BF16)
        for g in range(A_GROUPS):
            rhs = jnp.concatenate([vn[n * CHUNK:(n + 1) * CHUNK, g * gch:(g + 1) * gch]
                                   for n in range(n_chunks)], axis=1)
            mixed = jnp.dot(sguw_ref[g].astype(BF16), rhs, preferred_element_type=F32)
            bias = sgub_ref[g]
            for n in range(n_chunks):
                blk = (mixed[:, n * gch:(n + 1) * gch] + bias) * u[n * CHUNK:(n + 1) * CHUNK,
                                                                  g * gch:(g + 1) * gch]
                a_ref[n * CHUNK:(n + 1) * CHUNK, g * gch:(g + 1) * gch] = blk.astype(BF16)

        project(c_q, c_k)
        q = z_ref[:, c_q:c_k]
        q = q * lax.rsqrt(_head_sumsq(q, bdq_ref) * (1.0 / HEAD_DIM) + EPS) * qn_ref[...]
        cos = cos_ref[...]
        sin = sin_ref[...]
        for j in range(qw // LANES):
            qj = _rope(q[:, j * LANES:(j + 1) * LANES], cos, sin) * QSCALE
            q_ref[:, j * LANES:(j + 1) * LANES] = qj.astype(BF16)

        project(c_k, c_k + 2 * kw)
        k = z_ref[:, c_k:c_k + kw]
        k = k * lax.rsqrt(_head_sumsq(k, bdk_ref) * (1.0 / HEAD_DIM) + EPS) * kn_ref[...]
        kf_ref[...] = k
        vals = z_ref[:, c_k + kw:c_k + 2 * kw]
        vf_ref[...] = vals
        for src, dst in ((_rope(k, cos, sin), kd_ref), (vals, vd_ref)):
            sw = pltpu.roll(src, HEAD_DIM, 1)
            dst[:, 0:LANES] = jnp.where(low, src, sw).astype(BF16)
            dst[:, LANES:2 * LANES] = jnp.where(low, sw, src).astype(BF16)

    @pl.when(i % 2 == 0)
    def _():
        step(z0_ref, z1_ref)

    @pl.when(i % 2 == 1)
    def _():
        step(z1_ref, z0_ref)


def _even_in(xs, mods, layer, e, T, g_pre, w_in, sgu_w, sgu_b, sgu_norm, q_norm, k_norm,
             bdq, bdk, cos, sin):
    tm = TM_PROJ
    n_in = w_in.shape[2]
    n_tiles = T // tm
    tiles_per_seg = SEG // tm
    n_prompt_tiles = n_tiles // 2
    split_x = len(xs) == 2
    proj = lambda i: jnp.minimum(i, n_tiles - 1)
    post = lambda i: jnp.maximum(i - 1, 0)
    tab_spec = pl.BlockSpec((None, tm, LANES),
                            lambda i: (post(i) // n_prompt_tiles, post(i) % tiles_per_seg, 0))
    row = lambda w, tile: pl.BlockSpec((tm, w), lambda i: (tile(i), 0))
    x_specs = (_split_specs(tm, D, n_prompt_tiles, tile=proj) if split_x else [row(D, proj)])
    qw = B_HEADS * HEAD_DIM
    return pl.pallas_call(
        functools.partial(_even_in_kernel, n_tiles=n_tiles, n_prompt_tiles=n_prompt_tiles,
                          split_x=split_x),
        grid=(n_tiles + 1,),
        in_specs=x_specs + [
            _mod_spec(layer, 0, tm, proj), _mod_spec(layer, 1, tm, proj),
            _const_spec((1, D), layer),
            _weight_spec((D, n_in), e), _const_spec((A_GROUPS, CHUNK, CHUNK), e),
            _const_spec((A_GROUPS, CHUNK, 1), e), _const_spec((1, A_WIDTH), e),
            _const_spec((1, qw), e), _const_spec((1, LANES), e),
            _const_spec((qw, qw)), _const_spec((LANES, LANES)), tab_spec, tab_spec],
        out_specs=[row(A_WIDTH, post), row(qw, post), row(2 * LANES, post), row(2 * LANES, post),
                   row(LANES, post), row(LANES, post)],
        out_shape=[jax.ShapeDtypeStruct((T, A_WIDTH), BF16),
                   jax.ShapeDtypeStruct((T, qw), BF16),
                   jax.ShapeDtypeStruct((T, 2 * LANES), BF16),
                   jax.ShapeDtypeStruct((T, 2 * LANES), BF16),
                   jax.ShapeDtypeStruct((T, LANES), F32),
                   jax.ShapeDtypeStruct((T, LANES), F32)],
        scratch_shapes=[pltpu.VMEM((D, n_in), BF16), pltpu.VMEM((tm, D), BF16),
                        pltpu.VMEM((tm, n_in), F32), pltpu.VMEM((tm, n_in), F32)],
        compiler_params=_cparams(1),
        name="even_in_proj",
    )(*xs, mods, mods, g_pre, w_in, sgu_w, sgu_b, sgu_norm, q_norm, k_norm, bdq, bdk, cos, sin)


def _odd_in_kernel(x_ref, sh_ref, sc_ref, g_ref, w_ref, *rest, n_prompt_tiles, slot, all_slots):
    q_ref, k_ref, v_ref, kf_ref, vf_ref, wbf_ref = rest[-6:]
    i = pl.program_id(0)
    hw = C_HEADS * HEAD_DIM

    @pl.when(i == 0)
    def _():
        for c in range(3):
            wbf_ref[:, c * hw:(c + 1) * hw] = w_ref[:, c * hw:(c + 1) * hw].astype(BF16)

    h = _mod_norm(x_ref[...], g_ref[...], sh_ref[...], sc_ref[...]).astype(BF16)
    q = jnp.dot(h, wbf_ref[:, 0:hw], preferred_element_type=F32)
    q_ref[...] = (q * QSCALE).astype(BF16)
    k = jnp.dot(h, wbf_ref[:, hw:2 * hw], preferred_element_type=F32)
    k_ref[...] = k.astype(BF16)
    v = jnp.dot(h, wbf_ref[:, 2 * hw:3 * hw], preferred_element_type=F32)
    v_ref[...] = v.astype(BF16)

    @pl.when(i < n_prompt_tiles)
    def _():
        for src, dst in ((k, kf_ref), (v, vf_ref)):
            if all_slots:
                val = src.reshape((dst.shape[0],) + dst.shape[2:])
                for s in range(dst.shape[1]):
                    dst[:, s] = val if s == slot else jnp.zeros_like(val)
            else:
                dst[...] = src.reshape(dst.shape)


def _odd_in(x, mods, layer, o, n_odd, seq_p, g_pre, w_in, caches):
    T = x.shape[0]
    tm = TM_PROJ
    hw = C_HEADS * HEAD_DIM
    n_prompt_tiles = T // 2 // tm
    bt = tm // seq_p
    row = pl.BlockSpec((tm, hw), lambda i: (i, 0))
    all_slots = not caches
    if all_slots:
        crow = pl.BlockSpec((bt, n_odd, seq_p, hw),
                            lambda i: (jnp.minimum(i, n_prompt_tiles - 1), 0, 0, 0))
    else:
        crow = pl.BlockSpec((bt, None, seq_p, hw),
                            lambda i: (jnp.minimum(i, n_prompt_tiles - 1), o, 0, 0))
    cshape = jax.ShapeDtypeStruct((T // 2 // seq_p, n_odd, seq_p, hw), F32)
    n_in = 5
    return pl.pallas_call(
        functools.partial(_odd_in_kernel, n_prompt_tiles=n_prompt_tiles, slot=o,
                          all_slots=all_slots),
        grid=(T // tm,),
        in_specs=[pl.BlockSpec((tm, D), lambda i: (i, 0)), _mod_spec(layer, 0, tm),
                  _mod_spec(layer, 1, tm), _const_spec((1, D), layer),
                  _weight_spec((D, 3 * hw), o)]
                 + [pl.BlockSpec(memory_space=pl.ANY)] * len(caches),
        out_specs=[row, row, row, crow, crow],
        out_shape=[jax.ShapeDtypeStruct((T, hw), BF16)] * 3 + [cshape, cshape],
        input_output_aliases={n_in + j: 3 + j for j in range(len(caches))},
        scratch_shapes=[pltpu.VMEM((D, 3 * hw), BF16)],
        compiler_params=_cparams(1),
        name="odd_in_proj",
    )(x, mods, mods, g_pre, w_in, *caches)


def _softmax(pieces):
    m = None
    for s, _ in pieces:
        ms = jnp.max(s, axis=-1, keepdims=True)
        m = ms if m is None else jnp.maximum(m, ms)
    den = None
    probs = []
    for s, _ in pieces:
        p = jnp.exp2(s - m)
        ls = jnp.sum(p, axis=-1, keepdims=True)
        den = ls if den is None else den + ls
        probs.append(p.astype(BF16))
    return probs, [val for _, val in pieces], den


def _weighted_values(probs, vals, den):
    acc = None
    for p, val in zip(probs, vals):
        o = jnp.dot(p, val, preferred_element_type=F32)
        acc = o if acc is None else acc + o
    return acc / den


def _pipelined_units(units, scores, store):
    n = len(units)
    pending_scores = scores(units[0])
    pending_probs = None
    for idx in range(n + 1):
        upcoming = scores(units[idx + 1]) if idx + 1 < n else None
        probs = _softmax(pending_scores) if idx < n else None
        if pending_probs is not None:
            store(units[idx - 1], _weighted_values(*pending_probs))
        pending_scores, pending_probs = upcoming, probs


def _attn_kernel(*refs, has_ctx, seq, tq, pairs_per_kv):
    if has_ctx:
        q_ref, k_ref, v_ref, kc_ref, vc_ref, o_ref = refs
    else:
        q_ref, k_ref, v_ref, o_ref = refs
    lane = lax.broadcasted_iota(jnp.int32, (1, LANES), 1)
    masks = [lane < HEAD_DIM, lane >= HEAD_DIM]
    if has_ctx:
        kv_head = pl.program_id(1)
        sel = jnp.where(masks[0], 0, 1) == kv_head
        kc = jnp.where(sel, kc_ref[...], 0.0)
        kc = (kc + pltpu.roll(kc, HEAD_DIM, 1)).astype(BF16)
        vc = jnp.where(sel, vc_ref[...], 0.0)
        vc = (vc + pltpu.roll(vc, HEAD_DIM, 1)).astype(BF16)
        vch = [jnp.where(mh, vc, jnp.zeros_like(vc)) for mh in masks]
    nt = (((1,), (1,)), ((), ()))
    groups = [(r0, j) for r0 in range(0, q_ref.shape[0], seq) for j in range(q_ref.shape[1] // LANES)]
    units = [(r0, j, t, hh) for r0, j in groups for t in range(r0 // tq, (r0 + seq) // tq)
             for hh in range(len(masks))]
    operands = {}

    def group_operands(r0, j):
        if (r0, j) not in operands:
            kcols = slice((j // pairs_per_kv) * LANES, (j // pairs_per_kv + 1) * LANES)
            v = v_ref[r0:r0 + seq, kcols]
            operands[r0, j] = (k_ref[r0:r0 + seq, kcols],
                               [jnp.where(mh, v, jnp.zeros_like(v)) for mh in masks])
        return operands[r0, j]

    def scores(unit):
        r0, j, t, hh = unit
        k, vh = group_operands(r0, j)
        q = q_ref[t * tq:(t + 1) * tq, j * LANES:(j + 1) * LANES]
        qm = jnp.where(masks[hh], q, jnp.zeros_like(q))
        pieces = [(lax.dot_general(qm, k, nt, preferred_element_type=F32), vh[hh])]
        if has_ctx:
            pieces.append((lax.dot_general(qm, kc, nt, preferred_element_type=F32), vch[hh]))
        return pieces

    partial = {}

    def store(unit, o):
        r0, j, t, hh = unit
        if hh == 0:
            partial[r0, j, t] = o
        else:
            out = partial.pop((r0, j, t)) + o
            o_ref[t * tq:(t + 1) * tq, j * LANES:(j + 1) * LANES] = out.astype(BF16)

    _pipelined_units(units, scores, store)


def _attention(q, k, v, *, n_batch, seq, row0, pairs_per_step, pairs_per_kv, batches_per_step=1,
               ctx=None):
    rows = batches_per_step * seq
    assert row0 % rows == 0 and n_batch % batches_per_step == 0
    blk0 = row0 // rows
    n_groups = q.shape[1] // (pairs_per_step * LANES)
    qw = pairs_per_step * LANES
    kw = qw // pairs_per_kv
    in_specs = [pl.BlockSpec((rows, qw), lambda b, g: (blk0 + b, g)),
                pl.BlockSpec((rows, kw), lambda b, g: (blk0 + b, g)),
                pl.BlockSpec((rows, kw), lambda b, g: (blk0 + b, g))]
    args = [q, k, v]
    if ctx is not None:
        assert kw == LANES and batches_per_step == 1
        kc, vc, e = ctx
        sc = kc.shape[2]
        cspec = pl.BlockSpec((None, None, sc, LANES), lambda b, g: (b, e, 0, 0))
        in_specs += [cspec, cspec]
        args += [kc, vc]
    return pl.pallas_call(
        functools.partial(_attn_kernel, has_ctx=ctx is not None, seq=seq, tq=min(TQ_ATTN, seq),
                          pairs_per_kv=pairs_per_kv),
        grid=(n_batch // batches_per_step, n_groups),
        in_specs=in_specs,
        out_specs=pl.BlockSpec((rows, qw), lambda b, g: (b, g)),
        out_shape=jax.ShapeDtypeStruct((n_batch * seq, q.shape[1]), BF16),
        compiler_params=_cparams(2),
        name="attention",
    )(*args)


def _na_row_start(r):
    rows = SEG // GRID_W
    return min(max(r - NA_ROWS // 2, 0), rows - NA_ROWS)


def _na_windows():
    rows = SEG // GRID_W
    windows = []
    for qb in range(rows // NA_QROWS):
        lo = _na_row_start(qb * NA_QROWS)
        hi = _na_row_start(qb * NA_QROWS + NA_QROWS - 1) + NA_ROWS
        n = hi - lo + (hi - lo) % 2
        ws = min(lo, rows - n)
        assert ws <= lo and hi <= ws + n <= rows and n <= NA_KROWS
        windows.append((ws, n))
    return windows


def _na_kernel(q_ref, k_ref, v_ref, kc_ref, vc_ref, r_ref, o_ref, bias_ref, tab_ref, *, windows):
    lane = lax.broadcasted_iota(jnp.int32, (1, LANES), 1)
    low = lane < HEAD_DIM

    @pl.when(pl.program_id(1) == 0)
    def _():
        qcol = lax.broadcasted_iota(jnp.int32, (GRID_W, LANES), 0)
        kcol = lax.broadcasted_iota(jnp.int32, (GRID_W, LANES), 1) & (GRID_W - 1)
        start = jnp.clip(qcol - NA_COLS // 2, 0, GRID_W - NA_COLS)
        inside = (kcol >= start) & (kcol < start + NA_COLS)
        for hh in range(LANES // HEAD_DIM):
            for d in range(2 * NA_ROWS - 1):
                base = jnp.broadcast_to(r_ref[hh, d:d + 1, :], (GRID_W, LANES))
                lo_t = pltpu.roll(base, LANES - (NA_COLS - 1), 1, stride=1, stride_axis=0)
                hi_t = pltpu.roll(base, GRID_W - (NA_COLS - 1), 1, stride=1, stride_axis=0)
                tab_ref[hh, d] = jnp.where(inside, jnp.where(low, lo_t, hi_t) * LOG2E, NEG)
        neg = jnp.full((GRID_W, LANES), NEG, F32)
        for hh in range(LANES // HEAD_DIM):
            for qb, (ws, nrows) in enumerate(windows):
                for i in range(NA_QROWS):
                    r = qb * NA_QROWS + i
                    rs = _na_row_start(r)
                    for jp in range(nrows // 2):
                        kr = ws + 2 * jp
                        ok = [rs <= kr + d < rs + NA_ROWS for d in (0, 1)]
                        if not any(ok):
                            blk = neg
                        else:
                            t0 = tab_ref[hh, kr - r + NA_ROWS - 1] if ok[0] else neg
                            t1 = tab_ref[hh, kr + 1 - r + NA_ROWS - 1] if ok[1] else neg
                            blk = jnp.where(low, t0, t1)
                        bias_ref[hh, qb, i * GRID_W:(i + 1) * GRID_W,
                                 jp * LANES:(jp + 1) * LANES] = blk

    masks = [low, jnp.logical_not(low)]
    n_batch = kc_ref.shape[0]
    kc = [kc_ref[bb].astype(BF16) for bb in range(n_batch)]
    vch = []
    for bb in range(n_batch):
        vc = vc_ref[bb].astype(BF16)
        vch.append([jnp.where(mh, vc, jnp.zeros_like(vc)) for mh in masks])
    nt = (((1,), (1,)), ((), ()))
    nq = NA_QROWS * GRID_W
    units = [(bb, qb, hh) for bb in range(n_batch) for qb in range(len(windows))
             for hh in range(len(masks))]

    def scores(unit):
        bb, qb, hh = unit
        ws, nrows = windows[qb]
        nk = nrows * GRID_W
        k0 = bb * SEG + ws * GRID_W
        q = q_ref[bb * SEG + qb * nq:bb * SEG + (qb + 1) * nq, :]
        kw = k_ref[k0:k0 + nk, :]
        vw = v_ref[k0:k0 + nk, :]
        qm = jnp.where(masks[hh], q, jnp.zeros_like(q))
        s_win = (lax.dot_general(qm, kw, nt, preferred_element_type=F32)
                 + bias_ref[hh, qb, :, 0:nk])
        s_ctx = lax.dot_general(qm, kc[bb], nt, preferred_element_type=F32)
        return [(s_win, jnp.where(masks[hh], vw, jnp.zeros_like(vw))), (s_ctx, vch[bb][hh])]

    partial = {}

    def store(unit, o):
        bb, qb, hh = unit
        if hh == 0:
            partial[bb, qb] = o
        else:
            r0 = bb * SEG + qb * nq
            o_ref[r0:r0 + nq, :] = (partial.pop((bb, qb)) + o).astype(BF16)

    _pipelined_units(units, scores, store)


def _na_attention(q, k, v, kc, vc, o_idx, rpb, *, n_batch, row0):
    bps = NA_BATCHES_PER_STEP
    rows = bps * SEG
    assert row0 % rows == 0 and n_batch % bps == 0
    blk0 = row0 // rows
    n_pairs = q.shape[1] // LANES
    sc = kc.shape[2]
    windows = _na_windows()
    heads = LANES // HEAD_DIM
    nr = 2 * NA_ROWS - 1
    qkv = pl.BlockSpec((rows, LANES), lambda p, b: (blk0 + b, p))
    cspec = pl.BlockSpec((bps, None, sc, LANES), lambda p, b: (b, o_idx, 0, p))
    rspec = pl.BlockSpec((None, heads, nr, LANES), lambda p, b: (o_idx, p, 0, 0))
    return pl.pallas_call(
        functools.partial(_na_kernel, windows=windows),
        grid=(n_pairs, n_batch // bps),
        in_specs=[qkv, qkv, qkv, cspec, cspec, rspec],
        out_specs=pl.BlockSpec((rows, LANES), lambda p, b: (b, p)),
        out_shape=jax.ShapeDtypeStruct((n_batch * SEG, n_pairs * LANES), BF16),
        scratch_shapes=[pltpu.VMEM((heads, len(windows), NA_QROWS * GRID_W, NA_KROWS * GRID_W), F32),
                        pltpu.VMEM((heads, nr, GRID_W, LANES), F32)],
        compiler_params=_cparams(2),
        name="na_attention",
    )(q, k, v, kc, vc, rpb)


def _out_kernel(*refs, n_prompt_tiles, split_x, has_a):
    refs = list(refs)
    x_refs = [refs.pop(0), refs.pop(0)] if split_x else [refs.pop(0)]
    gt_ref, g_ref, fsh_ref, fsc_ref, gf_ref, w_ref = refs[:6]
    a_ref = refs[6] if has_a else None
    bp_ref, bs_ref, o_ref, h_ref, wbf_ref = refs[6 + has_a:]
    prompt = pl.program_id(0) < n_prompt_tiles

    @pl.when(pl.program_id(0) == 0)
    def _():
        wbf_ref[...] = w_ref[...].astype(BF16)

    def project(rows):
        b = jnp.where(prompt, bp_ref[rows, :], bs_ref[rows, :])
        if has_a:
            half = a_ref.shape[1]
            return (jnp.dot(a_ref[rows, :], wbf_ref[0:half, :], preferred_element_type=F32)
                    + jnp.dot(b, wbf_ref[half:2 * half, :], preferred_element_type=F32))
        return jnp.dot(b, wbf_ref[...], preferred_element_type=F32)

    def finish(rows, y):
        x = jnp.where(prompt, x_refs[0][rows, :], x_refs[1][rows, :]) if split_x else x_refs[0][rows, :]
        x1 = _gated_residual(x, y, g_ref[...], gt_ref[...])
        o_ref[rows, :] = x1
        h_ref[rows, :] = _mod_norm(x1, gf_ref[...], fsh_ref[...], fsc_ref[...]).astype(BF16)

    tm = o_ref.shape[0]
    blocks = [slice(r, r + OUT_ROWS) for r in range(0, tm, OUT_ROWS)]
    pending = project(blocks[0])
    for idx, rows in enumerate(blocks):
        upcoming = project(blocks[idx + 1]) if idx + 1 < len(blocks) else None
        finish(rows, pending)
        pending = upcoming


def _out_proj(xs, a, bp, bs, mods, layer, li, T, g_post, g_ffn_pre, w_out):
    tm = TM_OUT
    n_prompt_tiles = T // 2 // tm
    split_x = len(xs) == 2
    has_a = a is not None
    row = lambda w: pl.BlockSpec((tm, w), lambda i: (i, 0))
    x_specs = _split_specs(tm, D, n_prompt_tiles) if split_x else [row(D)]
    a_specs = [row(a.shape[1])] if has_a else []
    return pl.pallas_call(
        functools.partial(_out_kernel, n_prompt_tiles=n_prompt_tiles, split_x=split_x, has_a=has_a),
        grid=(T // tm,),
        in_specs=x_specs + [_mod_spec(layer, 2, tm), _const_spec((1, D), layer),
                            _mod_spec(layer, 3, tm), _mod_spec(layer, 4, tm),
                            _const_spec((1, D), layer), _weight_spec((D, D), li)] + a_specs
                 + _split_specs(tm, bp.shape[1], n_prompt_tiles),
        out_specs=[row(D), row(D)],
        out_shape=[jax.ShapeDtypeStruct((T, D), F32), jax.ShapeDtypeStruct((T, D), BF16)],
        scratch_shapes=[pltpu.VMEM((D, D), BF16)],
        compiler_params=_cparams(1),
        name="out_proj",
    )(*xs, mods, g_post, mods, mods, g_ffn_pre, w_out, *([a] if has_a else []), bp, bs)


def _ffn_kernel(x_ref, h_ref, gt_ref, gpost_ref, win_ref, wout_ref, *rest,
                layer, n_prompt_tiles, split_out):
    out_refs = rest[:-9]
    wg_buf, wu_buf, wo_buf, wg_res, wu_res, wo_res, act_buf, acc_ref, sem = rest[-9:]
    nf, _, tf = wg_res.shape
    first_tile = pl.program_id(0) == 0

    def aligned(f):
        return f * tf if isinstance(f, int) else pl.multiple_of(f * tf, tf)

    def in_copies(f, slot):
        col = aligned(f)
        return (pltpu.make_async_copy(win_ref.at[layer, :, pl.ds(col, tf)], wg_buf.at[slot],
                                      sem.at[0, slot]),
                pltpu.make_async_copy(win_ref.at[layer, :, pl.ds(D_FF + col, tf)], wu_buf.at[slot],
                                      sem.at[1, slot]))

    def out_copy(f, slot):
        row = aligned(f)
        return pltpu.make_async_copy(wout_ref.at[layer, pl.ds(row, tf), :], wo_buf.at[slot],
                                     sem.at[2, slot])

    def fetch(f, slot):
        for c in in_copies(f, slot):
            c.wait()
        out_copy(f, slot).wait()

        nxt = min(f + 1, nf - 1) if isinstance(f, int) else jnp.minimum(f + 1, nf - 1)

        @pl.when(jnp.asarray(f + 1 < nf))
        def _():
            for c in in_copies(nxt, 1 - slot):
                c.start()
            out_copy(nxt, 1 - slot).start()

        wg_res[f] = wg_buf[slot].astype(BF16)
        wu_res[f] = wu_buf[slot].astype(BF16)
        wo_res[f] = wo_buf[slot].astype(BF16)

    def hidden(f, slot):
        h = h_ref[...]
        g = jnp.dot(h, wg_res[f], preferred_element_type=F32)
        u = jnp.dot(h, wu_res[f], preferred_element_type=F32)
        act_buf[slot] = (g * jax.nn.sigmoid(g) * u).astype(BF16)

    def project(f, slot):
        acc_ref[...] += jnp.dot(act_buf[slot], wo_res[f], preferred_element_type=F32)

    assert nf % 2 == 1

    def chunks(streaming):
        acc_ref[...] = jnp.zeros_like(acc_ref)
        if streaming:
            fetch(0, 0)
        hidden(0, 0)

        def pair(k, carry):
            for f, slot in ((2 * k + 1, 1), (2 * k + 2, 0)):
                if streaming:
                    fetch(f, slot)
                hidden(f, slot)
                project(f - 1, 1 - slot)
            return carry

        lax.fori_loop(0, (nf - 1) // 2, pair, 0)
        project(nf - 1, 0)

    @pl.when(first_tile)
    def _():
        for c in in_copies(0, 0):
            c.start()
        out_copy(0, 0).start()
        chunks(streaming=True)

    @pl.when(jnp.logical_not(first_tile))
    def _():
        chunks(streaming=False)

    def result():
        return _gated_residual(x_ref[...], acc_ref[...], gpost_ref[...], gt_ref[...])

    if split_out:
        i = pl.program_id(0)

        @pl.when(i < n_prompt_tiles)
        def _():
            out_refs[0][...] = result()

        @pl.when(i >= n_prompt_tiles)
        def _():
            out_refs[1][...] = result()
    else:
        out_refs[0][...] = result()


def _ffn(x, h, mods, layer, g_post, w_in, w_out, split_out):
    T = x.shape[0]
    tm, tf = TM_FFN, TF_FFN
    nf = D_FF // tf
    n_prompt_tiles = T // 2 // tm
    xrow = pl.BlockSpec((tm, D), lambda i: (i, 0))
    if split_out:
        out_specs = _split_specs(tm, D, n_prompt_tiles)
        out_shape = [jax.ShapeDtypeStruct((T // 2, D), F32)] * 2
    else:
        out_specs = [xrow]
        out_shape = [jax.ShapeDtypeStruct((T, D), F32)]
    hbm = pl.BlockSpec(memory_space=pl.ANY)
    return pl.pallas_call(
        functools.partial(_ffn_kernel, layer=layer, n_prompt_tiles=n_prompt_tiles,
                          split_out=split_out),
        grid=(T // tm,),
        in_specs=[xrow, xrow, _mod_spec(layer, 5, tm), _const_spec((1, D), layer), hbm, hbm],
        out_specs=out_specs,
        out_shape=out_shape,
        scratch_shapes=[pltpu.VMEM((2, D, tf), F32), pltpu.VMEM((2, D, tf), F32),
                        pltpu.VMEM((2, tf, D), F32), pltpu.VMEM((nf, D, tf), BF16),
                        pltpu.VMEM((nf, D, tf), BF16), pltpu.VMEM((nf, tf, D), BF16),
                        pltpu.VMEM((2, tm, tf), BF16), pltpu.VMEM((tm, D), F32),
                        pltpu.SemaphoreType.DMA((3, 2))],
        compiler_params=_cparams(1, VMEM_LIMIT_FFN),
        name="ffn",
    )(x, h, mods, g_post, w_in, w_out)


def _rope_tables():
    t = jnp.arange(SEG)
    nf = HEAD_DIM // 4
    freqs = ROPE_BASE ** (-jnp.arange(nf, dtype=F32) / nf)

    def cs(pos):
        ang = pos.astype(F32)[:, None] * freqs[None, :]
        return jnp.cos(ang), jnp.sin(ang)

    cr, sr = cs(t // GRID_W)
    cc, sn = cs(t % GRID_W)
    cos = jnp.concatenate([cr, cr, cc, cc], axis=1)
    sin = jnp.concatenate([-sr, sr, -sn, sn], axis=1)
    reps = LANES // HEAD_DIM
    cos = jnp.tile(cos, (1, reps))
    sin = jnp.tile(sin, (1, reps))
    return (jnp.stack([jnp.ones_like(cos), cos]), jnp.stack([jnp.zeros_like(sin), sin]))


def _block_diag_ones(n):
    i = jnp.arange(n) // HEAD_DIM
    return (i[:, None] == i[None, :]).astype(BF16)


def kernel(x_prompt, x_sample, cache_attn_k, cache_attn_v, cache_na_k, cache_na_v, c, c_ctx,
           mod_w, mod_b, norm_mix_pre, norm_mix_post, norm_ffn_pre, norm_ffn_post,
           even_w_in, even_w_out, sgu_w, sgu_b, sgu_norm, q_norm, k_norm,
           odd_w_in, odd_w_out, na_rpb, ffn_w_in, ffn_w_out):
    nb_p, seq_p, _ = x_prompt.shape
    nb_s, seq_s, _ = x_sample.shape
    tp = nb_p * seq_p
    T = 2 * tp
    assert seq_s == SEG and tp == nb_s * seq_s and tp % SEG == 0

    cond = jnp.concatenate([jnp.broadcast_to(c_ctx[None, :], (tp // SEG, D)), c], axis=0)
    mods = _modulation(cond, mod_w, mod_b)

    cos, sin = _rope_tables()
    qw = B_HEADS * HEAD_DIM
    bdq = _block_diag_ones(qw)
    bdk = _block_diag_ones(LANES)
    past = cache_attn_k.shape[2]
    ctx_ak = cache_attn_k.reshape(nb_s, -1, past, B_KV_HEADS * HEAD_DIM)
    ctx_av = cache_attn_v.reshape(nb_s, -1, past, B_KV_HEADS * HEAD_DIM)
    ctx_nk = cache_na_k.reshape(nb_s, -1, past, C_HEADS * HEAD_DIM)
    ctx_nv = cache_na_v.reshape(nb_s, -1, past, C_HEADS * HEAD_DIM)
    g_mix_pre, g_mix_post, g_ffn_pre, g_ffn_post = (
        a.reshape(DEPTH, 1, D) for a in (norm_mix_pre, norm_mix_post, norm_ffn_pre, norm_ffn_post))
    n_even = even_w_in.shape[0]
    sgu_b3 = sgu_b.reshape(n_even, A_GROUPS, CHUNK, 1)
    sgu_n3 = sgu_norm.reshape(n_even, 1, A_WIDTH)
    qn3 = jnp.tile(q_norm, (1, B_HEADS)).reshape(n_even, 1, qw)
    kn3 = jnp.tile(k_norm, (1, B_KV_HEADS)).reshape(n_even, 1, LANES)
    rpb = jnp.pad(na_rpb, ((0, 0), (0, 0), (0, 0), (0, LANES - na_rpb.shape[3])))

    xs = (x_prompt.reshape(tp, D), x_sample.reshape(nb_s * seq_s, D))
    n_odd = odd_w_in.shape[0]
    attn_k, attn_v, na_cache = [], [], ()
    for l in range(DEPTH):
        if l % 2 == 0:
            e = l // 2
            a_out, q, kd, vd, kf, vf = _even_in(xs, mods, l, e, T, g_mix_pre, even_w_in, sgu_w,
                                                sgu_b3, sgu_n3, qn3, kn3, bdq, bdk, cos, sin)
            n_pairs = qw // LANES
            pairs_per_kv = n_pairs // B_KV_HEADS
            mix_p = _attention(q, kd, vd, n_batch=nb_p, seq=seq_p, row0=0, pairs_per_step=n_pairs,
                               pairs_per_kv=pairs_per_kv, batches_per_step=PROMPT_BATCHES_PER_STEP)
            mix_s = _attention(q, kd, vd, n_batch=nb_s, seq=seq_s, row0=tp,
                               pairs_per_step=pairs_per_kv, pairs_per_kv=pairs_per_kv,
                               ctx=(ctx_ak, ctx_av, e))
            attn_k.append(kf[:tp].reshape(nb_p, seq_p, B_KV_HEADS, HEAD_DIM))
            attn_v.append(vf[:tp].reshape(nb_p, seq_p, B_KV_HEADS, HEAD_DIM))
            w_out, li = even_w_out, e
        else:
            o = l // 2
            a_out = None
            q, k, v, *na_cache = _odd_in(xs[0], mods, l, o, n_odd, seq_p, g_mix_pre, odd_w_in,
                                         tuple(na_cache))
            mix_p = _attention(q, k, v, n_batch=nb_p, seq=seq_p, row0=0,
                               pairs_per_step=PAIRS_PER_STEP, pairs_per_kv=1,
                               batches_per_step=PROMPT_BATCHES_PER_STEP)
            mix_s = _na_attention(q, k, v, ctx_nk, ctx_nv, o, rpb, n_batch=nb_s, row0=tp)
            w_out, li = odd_w_out, o
        x, h = _out_proj(xs, a_out, mix_p, mix_s, mods, l, li, T, g_mix_post, g_ffn_pre, w_out)
        xs = tuple(_ffn(x, h, mods, l, g_ffn_post, ffn_w_in, ffn_w_out,
                        split_out=(l == DEPTH - 1)))

    y_prompt = xs[0].reshape(nb_p, seq_p, D)
    y_sample = xs[1].reshape(nb_s, seq_s, D)
    new_na_k, new_na_v = (a.reshape(nb_p, n_odd, seq_p, C_HEADS, HEAD_DIM) for a in na_cache)
    return (y_prompt, y_sample, jnp.stack(attn_k, axis=1), jnp.stack(attn_v, axis=1),
            new_na_k, new_na_v)
```

```python
import functools

import jax
import jax.numpy as jnp
from jax import lax
from jax.experimental import pallas as pl
from jax.experimental.pallas import tpu as pltpu

F32 = jnp.float32
BF16 = jnp.bfloat16

D = 1024
DEPTH = 4
HEAD_DIM = 64
GRID_W = 64
CHUNK = 128
A_WIDTH = D // 2
A_GROUPS = 4
B_HEADS = 8
B_KV_HEADS = 2
C_HEADS = 16
NA_ROWS = 8
NA_COLS = 16
D_FF = 2816
ROPE_BASE = 10000.0
EPS = 1e-6
NEG = -1e30
SEG = 1024
N_SEG = 8
LANES = 128
LOG2E = 1.4426950408889634
QSCALE = HEAD_DIM ** -0.5 * LOG2E

MOD_CHUNKS = 3
TM_PROJ = 512
TM_OUT = 1024
OUT_ROWS = 256
TM_FFN = 1024
TF_FFN = 256
TQ_ATTN = 256
PAIRS_PER_STEP = 4
PROMPT_BATCHES_PER_STEP = 4
NA_QROWS = 4
NA_BATCHES_PER_STEP = 4
NA_KROWS = 12
VMEM_LIMIT = 56 * 1024 * 1024
VMEM_LIMIT_FFN = 62 * 1024 * 1024


def _cparams(n_axes, vmem_limit=VMEM_LIMIT):
    return pltpu.CompilerParams(dimension_semantics=("arbitrary",) * n_axes,
                                vmem_limit_bytes=vmem_limit)


def _rms(x):
    return x * lax.rsqrt(jnp.mean(x * x, axis=-1, keepdims=True) + EPS)


def _mod_norm(x, g, shift, scale):
    return _rms(x) * (g * (1.0 + scale)) + shift


def _gated_residual(x, y, g, gate):
    return x + _rms(y) * (gate * g)


def _const_spec(shape, *lead):
    block = (None,) * len(lead) + tuple(shape)
    return pl.BlockSpec(block, lambda *_: tuple(lead) + (0,) * len(shape))


def _weight_spec(shape, *lead):
    block = (None,) * len(lead) + tuple(shape)
    return pl.BlockSpec(block, lambda *_: tuple(lead) + (0,) * len(shape),
                        pipeline_mode=pl.Buffered(1))


def _same_tile(i):
    return i


def _mod_spec(layer, j, tm, tile=_same_tile):
    return pl.BlockSpec((None, None, None, 1, D),
                        lambda i, *_: (layer, j, (tile(i) * tm) // SEG, 0, 0))


def _split_specs(tm, width, n_prompt_tiles, tile=_same_tile):
    return [pl.BlockSpec((tm, width),
                         lambda i, *_: (jnp.minimum(tile(i), n_prompt_tiles - 1), 0)),
            pl.BlockSpec((tm, width),
                         lambda i, *_: (jnp.maximum(tile(i) - n_prompt_tiles, 0), 0))]


def _pick(n_prompt_tiles, p_ref, s_ref, tile=None):
    tile = pl.program_id(0) if tile is None else tile
    return jnp.where(tile < n_prompt_tiles, p_ref[...], s_ref[...])


def _mod_kernel(cond_ref, w_ref, b_ref, o_ref):
    s = cond_ref[...]
    s = (s * jax.nn.sigmoid(s)).astype(BF16)
    for c in range(o_ref.shape[0]):
        o_ref[c] = jnp.dot(s, w_ref[:, c * D:(c + 1) * D].astype(BF16),
                           preferred_element_type=F32) + b_ref[c]


def _modulation(cond, mod_w, mod_b):
    n_vec = mod_w.shape[2] // D
    b = mod_b.reshape(DEPTH, n_vec, 1, D)
    out = pl.pallas_call(
        _mod_kernel,
        grid=(DEPTH, n_vec // MOD_CHUNKS),
        in_specs=[pl.BlockSpec((N_SEG, D), lambda l, j: (0, 0)),
                  pl.BlockSpec((None, D, MOD_CHUNKS * D), lambda l, j: (l, 0, j)),
                  pl.BlockSpec((None, MOD_CHUNKS, 1, D), lambda l, j: (l, j, 0, 0))],
        out_specs=pl.BlockSpec((None, MOD_CHUNKS, N_SEG, D), lambda l, j: (l, j, 0, 0)),
        out_shape=jax.ShapeDtypeStruct((DEPTH, n_vec, N_SEG, D), F32),
        compiler_params=_cparams(2),
        name="modulation",
    )(cond, mod_w, b)
    return out.reshape(DEPTH, n_vec, N_SEG, 1, D)


def _head_sumsq(y, bd_ref):
    return jnp.dot((y * y).astype(BF16), bd_ref[...], preferred_element_type=F32)


def _gelu_tanh(x):
    c = -2.0 * (2.0 / jnp.pi) ** 0.5 * LOG2E
    t = x * (c + (c * 0.044715) * (x * x))
    return x / (1.0 + jnp.exp2(t))


def _rope(y, cos, sin):
    lane = lax.broadcasted_iota(jnp.int32, (1, LANES), 1)
    first = (lane & 16) == 0
    partner = jnp.where(first, pltpu.roll(y, LANES - 16, 1), pltpu.roll(y, 16, 1))
    return y * cos + partner * sin


def _even_in_kernel(*refs, n_tiles, n_prompt_tiles, split_x):
    refs = list(refs)
    i = pl.program_id(0)
    t_proj = jnp.minimum(i, n_tiles - 1)
    x = _pick(n_prompt_tiles, refs.pop(0), refs.pop(0), t_proj) if split_x else refs.pop(0)[...]
    (sh_ref, sc_ref, g_ref, w_ref, sguw_ref, sgub_ref, sgun_ref, qn_ref, kn_ref, bdq_ref, bdk_ref,
     cos_ref, sin_ref, a_ref, q_ref, kd_ref, vd_ref, kf_ref, vf_ref, wbf_ref, h_ref, z0_ref,
     z1_ref) = refs
    tm = a_ref.shape[0]
    qw = B_HEADS * HEAD_DIM
    kw = B_KV_HEADS * HEAD_DIM
    c_q = 2 * A_WIDTH
    c_k = c_q + qw
    n_chunks = tm // CHUNK
    gch = A_WIDTH // A_GROUPS
    lane = lax.broadcasted_iota(jnp.int32, (1, LANES), 1)
    low = lane < HEAD_DIM

    @pl.when(i == 0)
    def _():
        wbf_ref[...] = w_ref[...].astype(BF16)
        z1_ref[...] = jnp.zeros_like(z1_ref)

    h_ref[...] = _mod_norm(x, g_ref[...], sh_ref[...], sc_ref[...]).astype(BF16)

    def step(z_new_ref, z_ref):
        def project(c0, c1):
            z_new_ref[:, c0:c1] = jnp.dot(h_ref[...], wbf_ref[:, c0:c1], preferred_element_type=F32)

        project(0, A_WIDTH)
        u = _gelu_tanh(z_ref[:, 0:A_WIDTH])
        project(A_WIDTH, c_q)
        v = _gelu_tanh(z_ref[:, A_WIDTH:c_q])
        mu = jnp.mean(v, axis=-1, keepdims=True)
        vc = v - mu
        var = jnp.mean(vc * vc, axis=-1, keepdims=True)
        vn = (vc * lax.rsqrt(var + EPS) * sgun_ref[...]).astype(BF16)
        for g in range(A_GROUPS):
            rhs = jnp.concatenate([vn[n * CHUNK:(n + 1) * CHUNK, g * gch:(g + 1) * gch]
                                   for n in range(n_chunks)], axis=1)
            mixed = jnp.dot(sguw_ref[g].astype(BF16), rhs, preferred_element_type=F32)
            bias = sgub_ref[g]
            for n in range(n_chunks):
                blk = (mixed[:, n * gch:(n + 1) * gch] + bias) * u[n * CHUNK:(n + 1) * CHUNK,
                                                                  g * gch:(g + 1) * gch]
                a_ref[n * CHUNK:(n + 1) * CHUNK, g * gch:(g + 1) * gch] = blk.astype(BF16)

        project(c_q, c_k)
        q = z_ref[:, c_q:c_k]
        q = q * lax.rsqrt(_head_sumsq(q, bdq_ref) * (1.0 / HEAD_DIM) + EPS) * qn_ref[...]
        cos = cos_ref[...]
        sin = sin_ref[...]
        for j in range(qw // LANES):
            qj = _rope(q[:, j * LANES:(j + 1) * LANES], cos, sin) * QSCALE
            q_ref[:, j * LANES:(j + 1) * LANES] = qj.astype(BF16)

        project(c_k, c_k + 2 * kw)
        k = z_ref[:, c_k:c_k + kw]
        k = k * lax.rsqrt(_head_sumsq(k, bdk_ref) * (1.0 / HEAD_DIM) + EPS) * kn_ref[...]
        kf_ref[...] = k
        vals = z_ref[:, c_k + kw:c_k + 2 * kw]
        vf_ref[...] = vals
        for src, dst in ((_rope(k, cos, sin), kd_ref), (vals, vd_ref)):
            sw = pltpu.roll(src, HEAD_DIM, 1)
            dst[:, 0:LANES] = jnp.where(low, src, sw).astype(BF16)
            dst[:, LANES:2 * LANES] = jnp.where(low, sw, src).astype(BF16)

    @pl.when(i % 2 == 0)
    def _():
        step(z0_ref, z1_ref)

    @pl.when(i % 2 == 1)
    def _():
        step(z1_ref, z0_ref)


def _even_in(xs, mods, layer, e, T, g_pre, w_in, sgu_w, sgu_b, sgu_norm, q_norm, k_norm,
             bdq, bdk, cos, sin):
    tm = TM_PROJ
    n_in = w_in.shape[2]
    n_tiles = T // tm
    tiles_per_seg = SEG // tm
    n_prompt_tiles = n_tiles // 2
    split_x = len(xs) == 2
    proj = lambda i: jnp.minimum(i, n_tiles - 1)
    post = lambda i: jnp.maximum(i - 1, 0)
    tab_spec = pl.BlockSpec((None, tm, LANES),
                            lambda i: (post(i) // n_prompt_tiles, post(i) % tiles_per_seg, 0))
    row = lambda w, tile: pl.BlockSpec((tm, w), lambda i: (tile(i), 0))
    x_specs = (_split_specs(tm, D, n_prompt_tiles, tile=proj) if split_x else [row(D, proj)])
    qw = B_HEADS * HEAD_DIM
    return pl.pallas_call(
        functools.partial(_even_in_kernel, n_tiles=n_tiles, n_prompt_tiles=n_prompt_tiles,
                          split_x=split_x),
        grid=(n_tiles + 1,),
        in_specs=x_specs + [
            _mod_spec(layer, 0, tm, proj), _mod_spec(layer, 1, tm, proj),
            _const_spec((1, D), layer),
            _weight_spec((D, n_in), e), _const_spec((A_GROUPS, CHUNK, CHUNK), e),
            _const_spec((A_GROUPS, CHUNK, 1), e), _const_spec((1, A_WIDTH), e),
            _const_spec((1, qw), e), _const_spec((1, LANES), e),
            _const_spec((qw, qw)), _const_spec((LANES, LANES)), tab_spec, tab_spec],
        out_specs=[row(A_WIDTH, post), row(qw, post), row(2 * LANES, post), row(2 * LANES, post),
                   row(LANES, post), row(LANES, post)],
        out_shape=[jax.ShapeDtypeStruct((T, A_WIDTH), BF16),
                   jax.ShapeDtypeStruct((T, qw), BF16),
                   jax.ShapeDtypeStruct((T, 2 * LANES), BF16),
                   jax.ShapeDtypeStruct((T, 2 * LANES), BF16),
                   jax.ShapeDtypeStruct((T, LANES), F32),
                   jax.ShapeDtypeStruct((T, LANES), F32)],
        scratch_shapes=[pltpu.VMEM((D, n_in), BF16), pltpu.VMEM((tm, D), BF16),
                        pltpu.VMEM((tm, n_in), F32), pltpu.VMEM((tm, n_in), F32)],
        compiler_params=_cparams(1),
        name="even_in_proj",
    )(*xs, mods, mods, g_pre, w_in, sgu_w, sgu_b, sgu_norm, q_norm, k_norm, bdq, bdk, cos, sin)


def _odd_in_kernel(x_ref, sh_ref, sc_ref, g_ref, w_ref, *rest, n_prompt_tiles, slot, all_slots):
    q_ref, k_ref, v_ref, kf_ref, vf_ref, wbf_ref = rest[-6:]
    i = pl.program_id(0)
    hw = C_HEADS * HEAD_DIM

    @pl.when(i == 0)
    def _():
        for c in range(3):
            wbf_ref[:, c * hw:(c + 1) * hw] = w_ref[:, c * hw:(c + 1) * hw].astype(BF16)

    h = _mod_norm(x_ref[...], g_ref[...], sh_ref[...], sc_ref[...]).astype(BF16)
    q = jnp.dot(h, wbf_ref[:, 0:hw], preferred_element_type=F32)
    q_ref[...] = (q * QSCALE).astype(BF16)
    k = jnp.dot(h, wbf_ref[:, hw:2 * hw], preferred_element_type=F32)
    k_ref[...] = k.astype(BF16)
    v = jnp.dot(h, wbf_ref[:, 2 * hw:3 * hw], preferred_element_type=F32)
    v_ref[...] = v.astype(BF16)

    @pl.when(i < n_prompt_tiles)
    def _():
        for src, dst in ((k, kf_ref), (v, vf_ref)):
            if all_slots:
                val = src.reshape((dst.shape[0],) + dst.shape[2:])
                for s in range(dst.shape[1]):
                    dst[:, s] = val if s == slot else jnp.zeros_like(val)
            else:
                dst[...] = src.reshape(dst.shape)


def _odd_in(x, mods, layer, o, n_odd, seq_p, g_pre, w_in, caches):
    T = x.shape[0]
    tm = TM_PROJ
    hw = C_HEADS * HEAD_DIM
    n_prompt_tiles = T // 2 // tm
    bt = tm // seq_p
    row = pl.BlockSpec((tm, hw), lambda i: (i, 0))
    all_slots = not caches
    if all_slots:
        crow = pl.BlockSpec((bt, n_odd, seq_p, hw),
                            lambda i: (jnp.minimum(i, n_prompt_tiles - 1), 0, 0, 0))
    else:
        crow = pl.BlockSpec((bt, None, seq_p, hw),
                            lambda i: (jnp.minimum(i, n_prompt_tiles - 1), o, 0, 0))
    cshape = jax.ShapeDtypeStruct((T // 2 // seq_p, n_odd, seq_p, hw), F32)
    n_in = 5
    return pl.pallas_call(
        functools.partial(_odd_in_kernel, n_prompt_tiles=n_prompt_tiles, slot=o,
                          all_slots=all_slots),
        grid=(T // tm,),
        in_specs=[pl.BlockSpec((tm, D), lambda i: (i, 0)), _mod_spec(layer, 0, tm),
                  _mod_spec(layer, 1, tm), _const_spec((1, D), layer),
                  _weight_spec((D, 3 * hw), o)]
                 + [pl.BlockSpec(memory_space=pl.ANY)] * len(caches),
        out_specs=[row, row, row, crow, crow],
        out_shape=[jax.ShapeDtypeStruct((T, hw), BF16)] * 3 + [cshape, cshape],
        input_output_aliases={n_in + j: 3 + j for j in range(len(caches))},
        scratch_shapes=[pltpu.VMEM((D, 3 * hw), BF16)],
        compiler_params=_cparams(1),
        name="odd_in_proj",
    )(x, mods, mods, g_pre, w_in, *caches)


def _softmax(pieces):
    m = None
    for s, _ in pieces:
        ms = jnp.max(s, axis=-1, keepdims=True)
        m = ms if m is None else jnp.maximum(m, ms)
    den = None
    probs = []
    for s, _ in pieces:
        p = jnp.exp2(s - m)
        ls = jnp.sum(p, axis=-1, keepdims=True)
        den = ls if den is None else den + ls
        probs.append(p.astype(BF16))
    return probs, [val for _, val in pieces], den


def _weighted_values(probs, vals, den):
    acc = None
    for p, val in zip(probs, vals):
        o = jnp.dot(p, val, preferred_element_type=F32)
        acc = o if acc is None else acc + o
    return acc / den


def _pipelined_units(units, scores, store):
    n = len(units)
    pending_scores = scores(units[0])
    pending_probs = None
    for idx in range(n + 1):
        upcoming = scores(units[idx + 1]) if idx + 1 < n else None
        probs = _softmax(pending_scores) if idx < n else None
        if pending_probs is not None:
            store(units[idx - 1], _weighted_values(*pending_probs))
        pending_scores, pending_probs = upcoming, probs


def _attn_kernel(*refs, has_ctx, seq, tq, pairs_per_kv):
    if has_ctx:
        q_ref, k_ref, v_ref, kc_ref, vc_ref, o_ref = refs
    else:
        q_ref, k_ref, v_ref, o_ref = refs
    lane = lax.broadcasted_iota(jnp.int32, (1, LANES), 1)
    masks = [lane < HEAD_DIM, lane >= HEAD_DIM]
    if has_ctx:
        kv_head = pl.program_id(1)
        sel = jnp.where(masks[0], 0, 1) == kv_head
        kc = jnp.where(sel, kc_ref[...], 0.0)
        kc = (kc + pltpu.roll(kc, HEAD_DIM, 1)).astype(BF16)
        vc = jnp.where(sel, vc_ref[...], 0.0)
        vc = (vc + pltpu.roll(vc, HEAD_DIM, 1)).astype(BF16)
        vch = [jnp.where(mh, vc, jnp.zeros_like(vc)) for mh in masks]
    nt = (((1,), (1,)), ((), ()))
    groups = [(r0, j) for r0 in range(0, q_ref.shape[0], seq) for j in range(q_ref.shape[1] // LANES)]
    units = [(r0, j, t, hh) for r0, j in groups for t in range(r0 // tq, (r0 + seq) // tq)
             for hh in range(len(masks))]
    operands = {}

    def group_operands(r0, j):
        if (r0, j) not in operands:
            kcols = slice((j // pairs_per_kv) * LANES, (j // pairs_per_kv + 1) * LANES)
            v = v_ref[r0:r0 + seq, kcols]
            operands[r0, j] = (k_ref[r0:r0 + seq, kcols],
                               [jnp.where(mh, v, jnp.zeros_like(v)) for mh in masks])
        return operands[r0, j]

    def scores(unit):
        r0, j, t, hh = unit
        k, vh = group_operands(r0, j)
        q = q_ref[t * tq:(t + 1) * tq, j * LANES:(j + 1) * LANES]
        qm = jnp.where(masks[hh], q, jnp.zeros_like(q))
        pieces = [(lax.dot_general(qm, k, nt, preferred_element_type=F32), vh[hh])]
        if has_ctx:
            pieces.append((lax.dot_general(qm, kc, nt, preferred_element_type=F32), vch[hh]))
        return pieces

    partial = {}

    def store(unit, o):
        r0, j, t, hh = unit
        if hh == 0:
            partial[r0, j, t] = o
        else:
            out = partial.pop((r0, j, t)) + o
            o_ref[t * tq:(t + 1) * tq, j * LANES:(j + 1) * LANES] = out.astype(BF16)

    _pipelined_units(units, scores, store)


def _attention(q, k, v, *, n_batch, seq, row0, pairs_per_step, pairs_per_kv, batches_per_step=1,
               ctx=None):
    rows = batches_per_step * seq
    assert row0 % rows == 0 and n_batch % batches_per_step == 0
    blk0 = row0 // rows
    n_groups = q.shape[1] // (pairs_per_step * LANES)
    qw = pairs_per_step * LANES
    kw = qw // pairs_per_kv
    in_specs = [pl.BlockSpec((rows, qw), lambda b, g: (blk0 + b, g)),
                pl.BlockSpec((rows, kw), lambda b, g: (blk0 + b, g)),
                pl.BlockSpec((rows, kw), lambda b, g: (blk0 + b, g))]
    args = [q, k, v]
    if ctx is not None:
        assert kw == LANES and batches_per_step == 1
        kc, vc, e = ctx
        sc = kc.shape[2]
        cspec = pl.BlockSpec((None, None, sc, LANES), lambda b, g: (b, e, 0, 0))
        in_specs += [cspec, cspec]
        args += [kc, vc]
    return pl.pallas_call(
        functools.partial(_attn_kernel, has_ctx=ctx is not None, seq=seq, tq=min(TQ_ATTN, seq),
                          pairs_per_kv=pairs_per_kv),
        grid=(n_batch // batches_per_step, n_groups),
        in_specs=in_specs,
        out_specs=pl.BlockSpec((rows, qw), lambda b, g: (b, g)),
        out_shape=jax.ShapeDtypeStruct((n_batch * seq, q.shape[1]), BF16),
        compiler_params=_cparams(2),
        name="attention",
    )(*args)


def _na_row_start(r):
    rows = SEG // GRID_W
    return min(max(r - NA_ROWS // 2, 0), rows - NA_ROWS)


def _na_windows():
    rows = SEG // GRID_W
    windows = []
    for qb in range(rows // NA_QROWS):
        lo = _na_row_start(qb * NA_QROWS)
        hi = _na_row_start(qb * NA_QROWS + NA_QROWS - 1) + NA_ROWS
        n = hi - lo + (hi - lo) % 2
        ws = min(lo, rows - n)
        assert ws <= lo and hi <= ws + n <= rows and n <= NA_KROWS
        windows.append((ws, n))
    return windows


def _na_kernel(q_ref, k_ref, v_ref, kc_ref, vc_ref, r_ref, o_ref, bias_ref, tab_ref, *, windows):
    lane = lax.broadcasted_iota(jnp.int32, (1, LANES), 1)
    low = lane < HEAD_DIM

    @pl.when(pl.program_id(1) == 0)
    def _():
        qcol = lax.broadcasted_iota(jnp.int32, (GRID_W, LANES), 0)
        kcol = lax.broadcasted_iota(jnp.int32, (GRID_W, LANES), 1) & (GRID_W - 1)
        start = jnp.clip(qcol - NA_COLS // 2, 0, GRID_W - NA_COLS)
        inside = (kcol >= start) & (kcol < start + NA_COLS)
        for hh in range(LANES // HEAD_DIM):
            for d in range(2 * NA_ROWS - 1):
                base = jnp.broadcast_to(r_ref[hh, d:d + 1, :], (GRID_W, LANES))
                lo_t = pltpu.roll(base, LANES - (NA_COLS - 1), 1, stride=1, stride_axis=0)
                hi_t = pltpu.roll(base, GRID_W - (NA_COLS - 1), 1, stride=1, stride_axis=0)
                tab_ref[hh, d] = jnp.where(inside, jnp.where(low, lo_t, hi_t) * LOG2E, NEG)
        neg = jnp.full((GRID_W, LANES), NEG, F32)
        for hh in range(LANES // HEAD_DIM):
            for qb, (ws, nrows) in enumerate(windows):
                for i in range(NA_QROWS):
                    r = qb * NA_QROWS + i
                    rs = _na_row_start(r)
                    for jp in range(nrows // 2):
                        kr = ws + 2 * jp
                        ok = [rs <= kr + d < rs + NA_ROWS for d in (0, 1)]
                        if not any(ok):
                            blk = neg
                        else:
                            t0 = tab_ref[hh, kr - r + NA_ROWS - 1] if ok[0] else neg
                            t1 = tab_ref[hh, kr + 1 - r + NA_ROWS - 1] if ok[1] else neg
                            blk = jnp.where(low, t0, t1)
                        bias_ref[hh, qb, i * GRID_W:(i + 1) * GRID_W,
                                 jp * LANES:(jp + 1) * LANES] = blk

    masks = [low, jnp.logical_not(low)]
    n_batch = kc_ref.shape[0]
    kc = [kc_ref[bb].astype(BF16) for bb in range(n_batch)]
    vch = []
    for bb in range(n_batch):
        vc = vc_ref[bb].astype(BF16)
        vch.append([jnp.where(mh, vc, jnp.zeros_like(vc)) for mh in masks])
    nt = (((1,), (1,)), ((), ()))
    nq = NA_QROWS * GRID_W
    units = [(bb, qb, hh) for bb in range(n_batch) for qb in range(len(windows))
             for hh in range(len(masks))]

    def scores(unit):
        bb, qb, hh = unit
        ws, nrows = windows[qb]
        nk = nrows * GRID_W
        k0 = bb * SEG + ws * GRID_W
        q = q_ref[bb * SEG + qb * nq:bb * SEG + (qb + 1) * nq, :]
        kw = k_ref[k0:k0 + nk, :]
        vw = v_ref[k0:k0 + nk, :]
        qm = jnp.where(masks[hh], q, jnp.zeros_like(q))
        s_win = (lax.dot_general(qm, kw, nt, preferred_element_type=F32)
                 + bias_ref[hh, qb, :, 0:nk])
        s_ctx = lax.dot_general(qm, kc[bb], nt, preferred_element_type=F32)
        return [(s_win, jnp.where(masks[hh], vw, jnp.zeros_like(vw))), (s_ctx, vch[bb][hh])]

    partial = {}

    def store(unit, o):
        bb, qb, hh = unit
        if hh == 0:
            partial[bb, qb] = o
        else:
            r0 = bb * SEG + qb * nq
            o_ref[r0:r0 + nq, :] = (partial.pop((bb, qb)) + o).astype(BF16)

    _pipelined_units(units, scores, store)


def _na_attention(q, k, v, kc, vc, o_idx, rpb, *, n_batch, row0):
    bps = NA_BATCHES_PER_STEP
    rows = bps * SEG
    assert row0 % rows == 0 and n_batch % bps == 0
    blk0 = row0 // rows
    n_pairs = q.shape[1] // LANES
    sc = kc.shape[2]
    windows = _na_windows()
    heads = LANES // HEAD_DIM
    nr = 2 * NA_ROWS - 1
    qkv = pl.BlockSpec((rows, LANES), lambda p, b: (blk0 + b, p))
    cspec = pl.BlockSpec((bps, None, sc, LANES), lambda p, b: (b, o_idx, 0, p))
    rspec = pl.BlockSpec((None, heads, nr, LANES), lambda p, b: (o_idx, p, 0, 0))
    return pl.pallas_call(
        functools.partial(_na_kernel, windows=windows),
        grid=(n_pairs, n_batch // bps),
        in_specs=[qkv, qkv, qkv, cspec, cspec, rspec],
        out_specs=pl.BlockSpec((rows, LANES), lambda p, b: (b, p)),
        out_shape=jax.ShapeDtypeStruct((n_batch * SEG, n_pairs * LANES), BF16),
        scratch_shapes=[pltpu.VMEM((heads, len(windows), NA_QROWS * GRID_W, NA_KROWS * GRID_W), F32),
                        pltpu.VMEM((heads, nr, GRID_W, LANES), F32)],
        compiler_params=_cparams(2),
        name="na_attention",
    )(q, k, v, kc, vc, rpb)


def _out_kernel(*refs, n_prompt_tiles, split_x, has_a):
    refs = list(refs)
    x_refs = [refs.pop(0), refs.pop(0)] if split_x else [refs.pop(0)]
    gt_ref, g_ref, fsh_ref, fsc_ref, gf_ref, w_ref = refs[:6]
    a_ref = refs[6] if has_a else None
    bp_ref, bs_ref, o_ref, h_ref, wbf_ref = refs[6 + has_a:]
    prompt = pl.program_id(0) < n_prompt_tiles

    @pl.when(pl.program_id(0) == 0)
    def _():
        wbf_ref[...] = w_ref[...].astype(BF16)

    def project(rows):
        b = jnp.where(prompt, bp_ref[rows, :], bs_ref[rows, :])
        if has_a:
            half = a_ref.shape[1]
            return (jnp.dot(a_ref[rows, :], wbf_ref[0:half, :], preferred_element_type=F32)
                    + jnp.dot(b, wbf_ref[half:2 * half, :], preferred_element_type=F32))
        return jnp.dot(b, wbf_ref[...], preferred_element_type=F32)

    def finish(rows, y):
        x = jnp.where(prompt, x_refs[0][rows, :], x_refs[1][rows, :]) if split_x else x_refs[0][rows, :]
        x1 = _gated_residual(x, y, g_ref[...], gt_ref[...])
        o_ref[rows, :] = x1
        h_ref[rows, :] = _mod_norm(x1, gf_ref[...], fsh_ref[...], fsc_ref[...]).astype(BF16)

    tm = o_ref.shape[0]
    blocks = [slice(r, r + OUT_ROWS) for r in range(0, tm, OUT_ROWS)]
    pending = project(blocks[0])
    for idx, rows in enumerate(blocks):
        upcoming = project(blocks[idx + 1]) if idx + 1 < len(blocks) else None
        finish(rows, pending)
        pending = upcoming


def _out_proj(xs, a, bp, bs, mods, layer, li, T, g_post, g_ffn_pre, w_out):
    tm = TM_OUT
    n_prompt_tiles = T // 2 // tm
    split_x = len(xs) == 2
    has_a = a is not None
    row = lambda w: pl.BlockSpec((tm, w), lambda i: (i, 0))
    x_specs = _split_specs(tm, D, n_prompt_tiles) if split_x else [row(D)]
    a_specs = [row(a.shape[1])] if has_a else []
    return pl.pallas_call(
        functools.partial(_out_kernel, n_prompt_tiles=n_prompt_tiles, split_x=split_x, has_a=has_a),
        grid=(T // tm,),
        in_specs=x_specs + [_mod_spec(layer, 2, tm), _const_spec((1, D), layer),
                            _mod_spec(layer, 3, tm), _mod_spec(layer, 4, tm),
                            _const_spec((1, D), layer), _weight_spec((D, D), li)] + a_specs
                 + _split_specs(tm, bp.shape[1], n_prompt_tiles),
        out_specs=[row(D), row(D)],
        out_shape=[jax.ShapeDtypeStruct((T, D), F32), jax.ShapeDtypeStruct((T, D), BF16)],
        scratch_shapes=[pltpu.VMEM((D, D), BF16)],
        compiler_params=_cparams(1),
        name="out_proj",
    )(*xs, mods, g_post, mods, mods, g_ffn_pre, w_out, *([a] if has_a else []), bp, bs)


def _ffn_kernel(x_ref, h_ref, gt_ref, gpost_ref, win_ref, wout_ref, *rest,
                layer, n_prompt_tiles, split_out):
    out_refs = rest[:-9]
    wg_buf, wu_buf, wo_buf, wg_res, wu_res, wo_res, act_buf, acc_ref, sem = rest[-9:]
    nf, _, tf = wg_res.shape
    first_tile = pl.program_id(0) == 0

    def aligned(f):
        return f * tf if isinstance(f, int) else pl.multiple_of(f * tf, tf)

    def in_copies(f, slot):
        col = aligned(f)
        return (pltpu.make_async_copy(win_ref.at[layer, :, pl.ds(col, tf)], wg_buf.at[slot],
                                      sem.at[0, slot]),
                pltpu.make_async_copy(win_ref.at[layer, :, pl.ds(D_FF + col, tf)], wu_buf.at[slot],
                                      sem.at[1, slot]))

    def out_copy(f, slot):
        row = aligned(f)
        return pltpu.make_async_copy(wout_ref.at[layer, pl.ds(row, tf), :], wo_buf.at[slot],
                                     sem.at[2, slot])

    def fetch(f, slot):
        for c in in_copies(f, slot):
            c.wait()
        out_copy(f, slot).wait()

        nxt = min(f + 1, nf - 1) if isinstance(f, int) else jnp.minimum(f + 1, nf - 1)

        @pl.when(jnp.asarray(f + 1 < nf))
        def _():
            for c in in_copies(nxt, 1 - slot):
                c.start()
            out_copy(nxt, 1 - slot).start()

        wg_res[f] = wg_buf[slot].astype(BF16)
        wu_res[f] = wu_buf[slot].astype(BF16)
        wo_res[f] = wo_buf[slot].astype(BF16)

    def hidden(f, slot):
        h = h_ref[...]
        g = jnp.dot(h, wg_res[f], preferred_element_type=F32)
        u = jnp.dot(h, wu_res[f], preferred_element_type=F32)
        act_buf[slot] = (g * jax.nn.sigmoid(g) * u).astype(BF16)

    def project(f, slot):
        acc_ref[...] += jnp.dot(act_buf[slot], wo_res[f], preferred_element_type=F32)

    assert nf % 2 == 1

    def chunks(streaming):
        acc_ref[...] = jnp.zeros_like(acc_ref)
        if streaming:
            fetch(0, 0)
        hidden(0, 0)

        def pair(k, carry):
            for f, slot in ((2 * k + 1, 1), (2 * k + 2, 0)):
                if streaming:
                    fetch(f, slot)
                hidden(f, slot)
                project(f - 1, 1 - slot)
            return carry

        lax.fori_loop(0, (nf - 1) // 2, pair, 0)
        project(nf - 1, 0)

    @pl.when(first_tile)
    def _():
        for c in in_copies(0, 0):
            c.start()
        out_copy(0, 0).start()
        chunks(streaming=True)

    @pl.when(jnp.logical_not(first_tile))
    def _():
        chunks(streaming=False)

    def result():
        return _gated_residual(x_ref[...], acc_ref[...], gpost_ref[...], gt_ref[...])

    if split_out:
        i = pl.program_id(0)

        @pl.when(i < n_prompt_tiles)
        def _():
            out_refs[0][...] = result()

        @pl.when(i >= n_prompt_tiles)
        def _():
            out_refs[1][...] = result()
    else:
        out_refs[0][...] = result()


def _ffn(x, h, mods, layer, g_post, w_in, w_out, split_out):
    T = x.shape[0]
    tm, tf = TM_FFN, TF_FFN
    nf = D_FF // tf
    n_prompt_tiles = T // 2 // tm
    xrow = pl.BlockSpec((tm, D), lambda i: (i, 0))
    if split_out:
        out_specs = _split_specs(tm, D, n_prompt_tiles)
        out_shape = [jax.ShapeDtypeStruct((T // 2, D), F32)] * 2
    else:
        out_specs = [xrow]
        out_shape = [jax.ShapeDtypeStruct((T, D), F32)]
    hbm = pl.BlockSpec(memory_space=pl.ANY)
    return pl.pallas_call(
        functools.partial(_ffn_kernel, layer=layer, n_prompt_tiles=n_prompt_tiles,
                          split_out=split_out),
        grid=(T // tm,),
        in_specs=[xrow, xrow, _mod_spec(layer, 5, tm), _const_spec((1, D), layer), hbm, hbm],
        out_specs=out_specs,
        out_shape=out_shape,
        scratch_shapes=[pltpu.VMEM((2, D, tf), F32), pltpu.VMEM((2, D, tf), F32),
                        pltpu.VMEM((2, tf, D), F32), pltpu.VMEM((nf, D, tf), BF16),
                        pltpu.VMEM((nf, D, tf), BF16), pltpu.VMEM((nf, tf, D), BF16),
                        pltpu.VMEM((2, tm, tf), BF16), pltpu.VMEM((tm, D), F32),
                        pltpu.SemaphoreType.DMA((3, 2))],
        compiler_params=_cparams(1, VMEM_LIMIT_FFN),
        name="ffn",
    )(x, h, mods, g_post, w_in, w_out)


def _rope_tables():
    t = jnp.arange(SEG)
    nf = HEAD_DIM // 4
    freqs = ROPE_BASE ** (-jnp.arange(nf, dtype=F32) / nf)

    def cs(pos):
        ang = pos.astype(F32)[:, None] * freqs[None, :]
        return jnp.cos(ang), jnp.sin(ang)

    cr, sr = cs(t // GRID_W)
    cc, sn = cs(t % GRID_W)
    cos = jnp.concatenate([cr, cr, cc, cc], axis=1)
    sin = jnp.concatenate([-sr, sr, -sn, sn], axis=1)
    reps = LANES // HEAD_DIM
    cos = jnp.tile(cos, (1, reps))
    sin = jnp.tile(sin, (1, reps))
    return (jnp.stack([jnp.ones_like(cos), cos]), jnp.stack([jnp.zeros_like(sin), sin]))


def _block_diag_ones(n):
    i = jnp.arange(n) // HEAD_DIM
    return (i[:, None] == i[None, :]).astype(BF16)


def kernel(x_prompt, x_sample, cache_attn_k, cache_attn_v, cache_na_k, cache_na_v, c, c_ctx,
           mod_w, mod_b, norm_mix_pre, norm_mix_post, norm_ffn_pre, norm_ffn_post,
           even_w_in, even_w_out, sgu_w, sgu_b, sgu_norm, q_norm, k_norm,
           odd_w_in, odd_w_out, na_rpb, ffn_w_in, ffn_w_out):
    nb_p, seq_p, _ = x_prompt.shape
    nb_s, seq_s, _ = x_sample.shape
    tp = nb_p * seq_p
    T = 2 * tp
    assert seq_s == SEG and tp == nb_s * seq_s and tp % SEG == 0

    cond = jnp.concatenate([jnp.broadcast_to(c_ctx[None, :], (tp // SEG, D)), c], axis=0)
    mods = _modulation(cond, mod_w, mod_b)

    cos, sin = _rope_tables()
    qw = B_HEADS * HEAD_DIM
    bdq = _block_diag_ones(qw)
    bdk = _block_diag_ones(LANES)
    past = cache_attn_k.shape[2]
    ctx_ak = cache_attn_k.reshape(nb_s, -1, past, B_KV_HEADS * HEAD_DIM)
    ctx_av = cache_attn_v.reshape(nb_s, -1, past, B_KV_HEADS * HEAD_DIM)
    ctx_nk = cache_na_k.reshape(nb_s, -1, past, C_HEADS * HEAD_DIM)
    ctx_nv = cache_na_v.reshape(nb_s, -1, past, C_HEADS * HEAD_DIM)
    g_mix_pre, g_mix_post, g_ffn_pre, g_ffn_post = (
        a.reshape(DEPTH, 1, D) for a in (norm_mix_pre, norm_mix_post, norm_ffn_pre, norm_ffn_post))
    n_even = even_w_in.shape[0]
    sgu_b3 = sgu_b.reshape(n_even, A_GROUPS, CHUNK, 1)
    sgu_n3 = sgu_norm.reshape(n_even, 1, A_WIDTH)
    qn3 = jnp.tile(q_norm, (1, B_HEADS)).reshape(n_even, 1, qw)
    kn3 = jnp.tile(k_norm, (1, B_KV_HEADS)).reshape(n_even, 1, LANES)
    rpb = jnp.pad(na_rpb, ((0, 0), (0, 0), (0, 0), (0, LANES - na_rpb.shape[3])))

    xs = (x_prompt.reshape(tp, D), x_sample.reshape(nb_s * seq_s, D))
    n_odd = odd_w_in.shape[0]
    attn_k, attn_v, na_cache = [], [], ()
    for l in range(DEPTH):
        if l % 2 == 0:
            e = l // 2
            a_out, q, kd, vd, kf, vf = _even_in(xs, mods, l, e, T, g_mix_pre, even_w_in, sgu_w,
                                                sgu_b3, sgu_n3, qn3, kn3, bdq, bdk, cos, sin)
            n_pairs = qw // LANES
            pairs_per_kv = n_pairs // B_KV_HEADS
            mix_p = _attention(q, kd, vd, n_batch=nb_p, seq=seq_p, row0=0, pairs_per_step=n_pairs,
                               pairs_per_kv=pairs_per_kv, batches_per_step=PROMPT_BATCHES_PER_STEP)
            mix_s = _attention(q, kd, vd, n_batch=nb_s, seq=seq_s, row0=tp,
                               pairs_per_step=pairs_per_kv, pairs_per_kv=pairs_per_kv,
                               ctx=(ctx_ak, ctx_av, e))
            attn_k.append(kf[:tp].reshape(nb_p, seq_p, B_KV_HEADS, HEAD_DIM))
            attn_v.append(vf[:tp].reshape(nb_p, seq_p, B_KV_HEADS, HEAD_DIM))
            w_out, li = even_w_out, e
        else:
            o = l // 2
            a_out = None
            q, k, v, *na_cache = _odd_in(xs[0], mods, l, o, n_odd, seq_p, g_mix_pre, odd_w_in,
                                         tuple(na_cache))
            mix_p = _attention(q, k, v, n_batch=nb_p, seq=seq_p, row0=0,
                               pairs_per_step=PAIRS_PER_STEP, pairs_per_kv=1,
                               batches_per_step=PROMPT_BATCHES_PER_STEP)
            mix_s = _na_attention(q, k, v, ctx_nk, ctx_nv, o, rpb, n_batch=nb_s, row0=tp)
            w_out, li = odd_w_out, o
        x, h = _out_proj(xs, a_out, mix_p, mix_s, mods, l, li, T, g_mix_post, g_ffn_pre, w_out)
        xs = tuple(_ffn(x, h, mods, l, g_ffn_post, ffn_w_in, ffn_w_out,
                        split_out=(l == DEPTH - 1)))

    y_prompt = xs[0].reshape(nb_p, seq_p, D)
    y_sample = xs[1].reshape(nb_s, seq_s, D)
    new_na_k, new_na_v = (a.reshape(nb_p, n_odd, seq_p, C_HEADS, HEAD_DIM) for a in na_cache)
    return (y_prompt, y_sample, jnp.stack(attn_k, axis=1), jnp.stack(attn_v, axis=1),
            new_na_k, new_na_v)
```
